```python
import jax, jax.numpy as jnp
from jax import lax
import numpy as np

D_MODEL = 1024
BATCH = 16
SEQ = 4096
DEPTH = 2
DEC_BATCH = 16
DEC_SEQ = 32
PAST_LEN = 2048

CHUNK = 64
N_AB_LAYERS = (DEPTH + 1) // 2
N_C_LAYERS = DEPTH // 2
A_WIDTH = D_MODEL // 2
A_HEAD = 128
A_HEADS = A_WIDTH // A_HEAD
B_WIDTH = D_MODEL // 2
B_HEADS = 4
B_KEY_WIDTH = B_WIDTH // 2
B_DK = B_KEY_WIDTH // B_HEADS
B_DV = B_WIDTH // B_HEADS
GLA_GATE_RANK = 16
GLA_GATE_NORMALIZER = 16.0
AB_SPLITS = (A_WIDTH, A_WIDTH, A_WIDTH, A_WIDTH, B_KEY_WIDTH, B_KEY_WIDTH, B_WIDTH, GLA_GATE_RANK, B_WIDTH)
AB_IN_WIDTH = 4 * A_WIDTH + 2 * B_KEY_WIDTH + 2 * B_WIDTH + GLA_GATE_RANK
AB_OUT_WIDTH = A_WIDTH + B_WIDTH
C_HEAD = 64
C_HEADS = D_MODEL // C_HEAD
C_DECAY_RANK = 64
C_AAA_RANK = 64
C_GATE_RANK = 128
C_GN_EPS = 64e-5
PEER_HEADS = 8
PEER_N_KEYS = 128
PEER_N_EXPERTS = PEER_N_KEYS * PEER_N_KEYS
PEER_TOPK = 16
PEER_QUERY = 256
PEER_SUBKEY = PEER_QUERY // 2
PEER_BLOCK = 256
NORM_EPS = 1e-6

kernel_name = 'hgrn2_gla_rwkv7_peer_stream_step'


def rms_norm(x, g):
    xf = x.astype(jnp.float32)
    y = xf * lax.rsqrt(jnp.mean(xf * xf, axis=-1, keepdims=True) + NORM_EPS)
    return (y * g.astype(jnp.float32)).astype(x.dtype)


def chunked_gated_recurrence(q, k, v, log_a, s0):
    bsz, T, H, _ = q.shape
    V = v.shape[-1]
    c = min(CHUNK, T)
    n = -(-T // c)
    pad = n * c - T

    def blocks(t):
        t = jnp.pad(t, ((0, 0), (0, pad), (0, 0), (0, 0)))
        return jnp.moveaxis(t.reshape(bsz, n, c, H, t.shape[-1]), 1, 0)

    qs, ks, vs, gs = blocks(q), blocks(k), blocks(v), blocks(log_a.astype(jnp.float32))
    causal = jnp.tril(jnp.ones((c, c), dtype=bool))[None, :, :, None, None]

    def step(S, inp):
        qc, kc, vc, gc = inp
        b = jnp.cumsum(gc, axis=1)
        diff = jnp.where(causal, b[:, :, None] - b[:, None, :], -jnp.inf)
        attn = jnp.einsum('bthk,bshk,btshk->bhts', qc, kc, jnp.exp(diff))
        o = (jnp.einsum('bhts,bshv->bthv', attn, vc)
             + jnp.einsum('bthk,bhkv->bthv', qc * jnp.exp(b), S))
        b_last = b[:, -1]
        S = (jnp.exp(b_last)[..., None] * S
             + jnp.einsum('bshk,bshv->bhkv', kc * jnp.exp(b_last[:, None] - b), vc))
        return S, o

    S, o = lax.scan(step, s0.astype(jnp.float32), (qs, ks, vs, gs))
    o = jnp.moveaxis(o, 0, 1).reshape(bsz, n * c, H, V)[:, :T]
    return o, S


def hgrn2_gla_mixer(h, s_hgrn, s_gla, w_in, lb, hgrn_norm_g, gla_gate_w2, gla_gate_b, gla_norm_g, w_out):
    bsz, T, _ = h.shape
    z = h @ w_in
    offs = [int(o) for o in np.cumsum(AB_SPLITS)[:-1]]
    a_q, a_f, a_i, a_gate, b_q, b_k, b_v, b_lr, b_gate = jnp.split(z, offs, axis=-1)

    def heads(t, H):
        return t.reshape(bsz, T, H, -1)

    f = lb + (1.0 - lb) * jax.nn.sigmoid(a_f.astype(jnp.float32))
    o_a, s_a = chunked_gated_recurrence(heads(jax.nn.silu(a_q), A_HEADS), heads(1.0 - f, A_HEADS),
                                        heads(a_i, A_HEADS), heads(jnp.log(f), A_HEADS), s_hgrn)
    o_a = rms_norm(o_a, hgrn_norm_g).reshape(bsz, T, A_WIDTH) * jax.nn.silu(a_gate)

    log_g = jax.nn.log_sigmoid((b_lr @ gla_gate_w2 + gla_gate_b).astype(jnp.float32)) / GLA_GATE_NORMALIZER
    o_b, s_b = chunked_gated_recurrence(heads(b_q * (B_DK ** -0.5), B_HEADS), heads(b_k, B_HEADS),
                                        heads(b_v, B_HEADS), heads(log_g, B_HEADS), s_gla)
    o_b = rms_norm(o_b, gla_norm_g).reshape(bsz, T, B_WIDTH) * jax.nn.silu(b_gate)

    y = jnp.concatenate([o_a, o_b], axis=-1).astype(h.dtype) @ w_out
    return y, s_a, s_b


def rwkv7_recurrence(r, w, k, v, kk, a, s0):
    def tm(t):
        return jnp.moveaxis(t.astype(jnp.float32), 1, 0)

    def step(S, inp):
        r_t, w_t, k_t, v_t, kk_t, a_t = inp
        sa = jnp.einsum('bhij,bhj->bhi', S, kk_t)
        S = (S * w_t[:, :, None, :] - sa[..., None] * (kk_t * a_t)[:, :, None, :]
             + v_t[..., None] * k_t[:, :, None, :])
        return S, jnp.einsum('bhij,bhj->bhi', S, r_t)

    S, o = lax.scan(step, s0.astype(jnp.float32), (tm(r), tm(w), tm(k), tm(v), tm(kk), tm(a)))
    return jnp.moveaxis(o, 0, 1), S


def rwkv7_mixer(h, s_wkv, x_last, mu, w_rkv, w_w1, w_w2, w0, a_w1, a_w2, a0, g_w1, g_w2,
                k_k, k_a, r_k, ln_g, ln_b, w_out):
    bsz, T, D = h.shape
    x_prev = jnp.concatenate([x_last[:, None].astype(h.dtype), h[:, :-1]], axis=1)
    dx = x_prev - h

    def mix(i):
        return h + dx * mu[i]

    r = mix(0) @ w_rkv[0]
    k = mix(1) @ w_rkv[1]
    v = mix(2) @ w_rkv[2]
    w = -jax.nn.softplus(-(w0 + jnp.tanh(mix(3) @ w_w1) @ w_w2).astype(jnp.float32)) - 0.5
    decay = jnp.exp(-jnp.exp(w))
    a = jax.nn.sigmoid((a0 + (mix(4) @ a_w1) @ a_w2).astype(jnp.float32))
    g = jax.nn.sigmoid(mix(5) @ g_w1) @ g_w2

    def hd(t):
        return t.reshape(bsz, T, C_HEADS, C_HEAD)

    kk = hd((k * k_k).astype(jnp.float32))
    kk = kk * lax.rsqrt(jnp.sum(kk * kk, axis=-1, keepdims=True) + 1e-12)
    k = k * (1.0 + (a - 1.0) * k_a)
    rh, kh, vh = hd(r), hd(k), hd(v)
    o, s_new = rwkv7_recurrence(rh, hd(decay), kh, vh, kk, hd(a), s_wkv)
    mean = jnp.mean(o, axis=-1, keepdims=True)
    var = jnp.mean(jnp.square(o - mean), axis=-1, keepdims=True)
    o = ((o - mean) * lax.rsqrt(var + C_GN_EPS)).reshape(bsz, T, D) * ln_g + ln_b
    bonus = jnp.sum(rh * kh * r_k, axis=-1, keepdims=True) * vh
    o = o + bonus.reshape(bsz, T, D)
    y = (o * g).astype(h.dtype) @ w_out
    return y, s_new, h[:, -1]


def peer_ffn(h, w_q, sub_keys, u_tab, v_tab):
    bsz, T, D = h.shape
    n = bsz * T
    blk = min(PEER_BLOCK, n)
    nb = -(-n // blk)
    xt = jnp.pad(h.reshape(n, D), ((0, nb * blk - n), (0, 0))).reshape(nb, blk, D)

    def block(xb):
        q = (xb @ w_q).reshape(blk, PEER_HEADS, 2, PEER_SUBKEY)
        s = jnp.einsum('thpd,hpnd->thpn', q, sub_keys).astype(jnp.float32)
        sv, si = lax.top_k(s, PEER_TOPK)
        cand = (sv[:, :, 0, :, None] + sv[:, :, 1, None, :]).reshape(blk, PEER_HEADS, PEER_TOPK * PEER_TOPK)
        cidx = (si[:, :, 0, :, None] * PEER_N_KEYS + si[:, :, 1, None, :]).reshape(blk, PEER_HEADS, PEER_TOPK * PEER_TOPK)
        cs, ci = lax.top_k(cand, PEER_TOPK)
        eidx = jnp.take_along_axis(cidx, ci, axis=-1)
        gate = jax.nn.softmax(cs, axis=-1)
        hid = jnp.einsum('td,thkd->thk', xb, u_tab[eidx]).astype(jnp.float32)
        act = (jax.nn.gelu(hid, approximate=False) * gate).astype(xb.dtype)
        return jnp.einsum('thk,thkd->td', act, v_tab[eidx])

    out = lax.map(block, xt).reshape(nb * blk, D)[:n]
    return out.reshape(bsz, T, D)


def setup_inputs(seed: int = 0) -> dict:
    key = jax.random.key(seed)
    ks = iter(jax.random.split(key, 48))

    def nrm(shape, scale):
        return jax.random.normal(next(ks), shape, jnp.float32) * scale

    def gain(shape):
        return 1.0 + nrm(shape, 0.02)

    D = D_MODEL
    return {
        'x_prompt': nrm((BATCH, SEQ, D), 1.0),
        'x_sample': nrm((DEC_BATCH, DEC_SEQ, D), 1.0),
        'state_hgrn': nrm((N_AB_LAYERS, DEC_BATCH, A_HEADS, A_HEAD, A_HEAD), 0.1),
        'state_gla': nrm((N_AB_LAYERS, DEC_BATCH, B_HEADS, B_DK, B_DV), 0.1),
        'state_rwkv': nrm((N_C_LAYERS, DEC_BATCH, C_HEADS, C_HEAD, C_HEAD), 0.1),
        'state_shift': nrm((N_C_LAYERS, DEC_BATCH, D), 1.0),
        'w_in_ab': nrm((N_AB_LAYERS, D, AB_IN_WIDTH), D ** -0.5),
        'hgrn_lower_bounds': nrm((DEPTH + 1, A_WIDTH), 0.1),
        'hgrn_norm_g': gain((N_AB_LAYERS, A_HEAD)),
        'gla_gate_w2': nrm((N_AB_LAYERS, GLA_GATE_RANK, B_KEY_WIDTH), GLA_GATE_RANK ** -0.5),
        'gla_gate_b': nrm((N_AB_LAYERS, B_KEY_WIDTH), 0.01),
        'gla_norm_g': gain((N_AB_LAYERS, B_DV)),
        'w_out_ab': nrm((N_AB_LAYERS, AB_OUT_WIDTH, D), AB_OUT_WIDTH ** -0.5),
        'rwkv_mu': jax.random.uniform(next(ks), (N_C_LAYERS, 6, D), jnp.float32),
        'rwkv_w_rkv': nrm((N_C_LAYERS, 3, D, D), D ** -0.5),
        'rwkv_w_w1': nrm((N_C_LAYERS, D, C_DECAY_RANK), D ** -0.5),
        'rwkv_w_w2': nrm((N_C_LAYERS, C_DECAY_RANK, D), 0.2),
        'rwkv_w0': -2.0 + nrm((N_C_LAYERS, D), 0.5),
        'rwkv_a_w1': nrm((N_C_LAYERS, D, C_AAA_RANK), D ** -0.5),
        'rwkv_a_w2': nrm((N_C_LAYERS, C_AAA_RANK, D), 0.2),
        'rwkv_a0': nrm((N_C_LAYERS, D), 0.1),
        'rwkv_g_w1': nrm((N_C_LAYERS, D, C_GATE_RANK), D ** -0.5),
        'rwkv_g_w2': nrm((N_C_LAYERS, C_GATE_RANK, D), C_GATE_RANK ** -0.5),
        'rwkv_k_k': 0.85 + nrm((N_C_LAYERS, D), 0.02),
        'rwkv_k_a': gain((N_C_LAYERS, D)),
        'rwkv_r_k': nrm((N_C_LAYERS, C_HEADS, C_HEAD), 0.1),
        'rwkv_ln_g': gain((N_C_LAYERS, D)),
        'rwkv_ln_b': nrm((N_C_LAYERS, D), 0.01),
        'w_out_c': nrm((N_C_LAYERS, D, D), D ** -0.5),
        'norm1_g': gain((DEPTH, D)),
        'norm2_g': gain((DEPTH, D)),
        'final_g': gain((D,)),
        'peer_w_q': nrm((DEPTH, D, PEER_HEADS * PEER_QUERY), D ** -0.5),
        'peer_sub_keys': nrm((DEPTH, PEER_HEADS, 2, PEER_N_KEYS, PEER_SUBKEY), PEER_SUBKEY ** -0.5),
        'peer_u': nrm((DEPTH, PEER_N_EXPERTS, D), D ** -0.5),
        'peer_v': nrm((DEPTH, PEER_N_EXPERTS, D), 0.1),
    }


def reference(x_prompt, x_sample, state_hgrn, state_gla, state_rwkv, state_shift,
              w_in_ab, hgrn_lower_bounds, hgrn_norm_g, gla_gate_w2, gla_gate_b, gla_norm_g, w_out_ab,
              rwkv_mu, rwkv_w_rkv, rwkv_w_w1, rwkv_w_w2, rwkv_w0, rwkv_a_w1, rwkv_a_w2, rwkv_a0,
              rwkv_g_w1, rwkv_g_w2, rwkv_k_k, rwkv_k_a, rwkv_r_k, rwkv_ln_g, rwkv_ln_b, w_out_c,
              norm1_g, norm2_g, final_g, peer_w_q, peer_sub_keys, peer_u, peer_v):
    lbs = jnp.cumsum(jax.nn.softmax(hgrn_lower_bounds.astype(jnp.float32), axis=0), axis=0)

    def trunk(x, st_h, st_g, st_r, st_s):
        out_h, out_g, out_r, out_s = [], [], [], []
        for l in range(DEPTH):
            j = l // 2
            hn = rms_norm(x, norm1_g[l])
            if l % 2 == 0:
                y, sh, sg = hgrn2_gla_mixer(hn, st_h[j], st_g[j], w_in_ab[j], lbs[l], hgrn_norm_g[j],
                                            gla_gate_w2[j], gla_gate_b[j], gla_norm_g[j], w_out_ab[j])
                out_h.append(sh)
                out_g.append(sg)
            else:
                y, sr, ss = rwkv7_mixer(hn, st_r[j], st_s[j], rwkv_mu[j], rwkv_w_rkv[j], rwkv_w_w1[j],
                                        rwkv_w_w2[j], rwkv_w0[j], rwkv_a_w1[j], rwkv_a_w2[j], rwkv_a0[j],
                                        rwkv_g_w1[j], rwkv_g_w2[j], rwkv_k_k[j], rwkv_k_a[j], rwkv_r_k[j],
                                        rwkv_ln_g[j], rwkv_ln_b[j], w_out_c[j])
                out_r.append(sr)
                out_s.append(ss)
            x = x + y.astype(x.dtype)
            x = x + peer_ffn(rms_norm(x, norm2_g[l]), peer_w_q[l], peer_sub_keys[l],
                             peer_u[l], peer_v[l]).astype(x.dtype)
        return rms_norm(x, final_g), jnp.stack(out_h), jnp.stack(out_g), jnp.stack(out_r), jnp.stack(out_s)

    bp = x_prompt.shape[0]
    z_h = jnp.zeros((N_AB_LAYERS, bp, A_HEADS, A_HEAD, A_HEAD), jnp.float32)
    z_g = jnp.zeros((N_AB_LAYERS, bp, B_HEADS, B_DK, B_DV), jnp.float32)
    z_r = jnp.zeros((N_C_LAYERS, bp, C_HEADS, C_HEAD, C_HEAD), jnp.float32)
    z_s = jnp.zeros((N_C_LAYERS, bp, D_MODEL), x_prompt.dtype)
    y_prompt, p_hgrn, p_gla, p_rwkv, p_shift = trunk(x_prompt, z_h, z_g, z_r, z_s)
    y_sample, s_hgrn, s_gla, s_rwkv, s_shift = trunk(x_sample, state_hgrn, state_gla, state_rwkv, state_shift)
    return (y_prompt, y_sample, p_hgrn, p_gla, p_rwkv, p_shift, s_hgrn, s_gla, s_rwkv, s_shift)
```

```python
import functools

import jax
import jax.numpy as jnp
from jax import lax
from jax.experimental import pallas as pl
from jax.experimental.pallas import tpu as pltpu
from jax.experimental.pallas import tpu_sc as plsc

F32 = jnp.float32
BF16 = jnp.bfloat16
I32 = jnp.int32
HI = lax.Precision.HIGHEST

D_MODEL = 1024
NORM_EPS = 1e-6
LANES = 128
SUBLANES = 8
VMEM_LIMIT = 56 * 1024 * 1024

A_WIDTH = 512
A_HEADS = 4
B_WIDTH = 512
B_HEADS = 4
B_DK = 64
GLA_RANK = 16
GLA_NORMALIZER = 16.0
Z_WIDTH = 3712
C_HEAD = 64
C_HEADS = 16
C_GN_EPS = 64e-5
P_HEADS = 8
P_KEYS = 128
P_TOPK = 16
P_PICKS = P_HEADS * P_TOPK
SC_CORES = 2
SC_SUBCORES = 16
SC_WORKERS = SC_CORES * SC_SUBCORES
SC_LANES = 16
SC_CHUNK = 32
SC_TOK_BLOCK = 8


def _cparams(sem):
    return pltpu.CompilerParams(dimension_semantics=sem, vmem_limit_bytes=VMEM_LIMIT)


def _rms(x, g):
    ms = jnp.mean(x * x, axis=-1, keepdims=True)
    return x * lax.rsqrt(ms + NORM_EPS) * g


def _dot(a, b, precision=None):
    return jnp.dot(a, b, preferred_element_type=F32, precision=precision)


def _dot_nt(a, b, precision=None):
    return lax.dot_general(a, b, (((1,), (1,)), ((), ())), preferred_element_type=F32, precision=precision)


def _dot_tn(a, b, precision=None):
    return lax.dot_general(a, b, (((0,), (0,)), ((), ())), preferred_element_type=F32, precision=precision)


def _tri(n, strict):
    r = lax.broadcasted_iota(I32, (n, n), 0)
    c = lax.broadcasted_iota(I32, (n, n), 1)
    return (c < r) if strict else (c <= r)


def _cumsum_rows(g):
    return _dot(_tri(g.shape[0], False).astype(F32), g, precision=HI)


def _lane_mask(width, lo, hi):
    l = lax.broadcasted_iota(I32, (1, width), 1)
    return (l >= lo) & (l < hi)


def _sigmoid(x):
    return 1.0 / (1.0 + jnp.exp(-x))


def _silu(x):
    return x * _sigmoid(x)


def _norm_proj_kernel(x_ref, g_ref, w_ref, o_ref):
    hn = _rms(x_ref[...], g_ref[...])
    o_ref[...] = _dot(hn.astype(BF16), w_ref[...])


def norm_proj(x, g, w_bf16, tm):
    n, d = x.shape
    f = w_bf16.shape[1]
    return pl.pallas_call(
        _norm_proj_kernel,
        grid=(n // tm,),
        in_specs=[pl.BlockSpec((tm, d), lambda i: (i, 0)),
                  pl.BlockSpec((1, d), lambda i: (0, 0)),
                  pl.BlockSpec((d, f), lambda i: (0, 0))],
        out_specs=pl.BlockSpec((tm, f), lambda i: (i, 0)),
        out_shape=jax.ShapeDtypeStruct((n, f), F32),
        compiler_params=_cparams(("parallel",)),
        name="norm_proj",
    )(x, g.reshape(1, d), w_bf16)


def _out_proj2_kernel(x_ref, a_ref, b_ref, wa_ref, wb_ref, o_ref):
    y = _dot(a_ref[...].astype(BF16), wa_ref[...]) + _dot(b_ref[...].astype(BF16), wb_ref[...])
    o_ref[...] = x_ref[...] + y


def out_proj2(x, a, b, wa, wb, tm):
    n, d = x.shape
    ka, kb = a.shape[1], b.shape[1]
    return pl.pallas_call(
        _out_proj2_kernel,
        grid=(n // tm,),
        in_specs=[pl.BlockSpec((tm, d), lambda i: (i, 0)),
                  pl.BlockSpec((tm, ka), lambda i: (i, 0)),
                  pl.BlockSpec((tm, kb), lambda i: (i, 0)),
                  pl.BlockSpec((ka, d), lambda i: (0, 0)),
                  pl.BlockSpec((kb, d), lambda i: (0, 0))],
        out_specs=pl.BlockSpec((tm, d), lambda i: (i, 0)),
        out_shape=jax.ShapeDtypeStruct((n, d), F32),
        compiler_params=_cparams(("parallel",)),
        name="out_proj2",
    )(x, a, b, wa, wb)


def _out_proj1_kernel(x_ref, a_ref, wa_ref, o_ref):
    o_ref[...] = x_ref[...] + _dot(a_ref[...].astype(BF16), wa_ref[...])


def out_proj1(x, a, wa, tm):
    n, d = x.shape
    ka = a.shape[1]
    return pl.pallas_call(
        _out_proj1_kernel,
        grid=(n // tm,),
        in_specs=[pl.BlockSpec((tm, d), lambda i: (i, 0)),
                  pl.BlockSpec((tm, ka), lambda i: (i, 0)),
                  pl.BlockSpec((ka, d), lambda i: (0, 0))],
        out_specs=pl.BlockSpec((tm, d), lambda i: (i, 0)),
        out_shape=jax.ShapeDtypeStruct((n, d), F32),
        compiler_params=_cparams(("parallel",)),
        name="out_proj1",
    )(x, a, wa)


def _final_norm_kernel(x_ref, g_ref, o_ref):
    o_ref[...] = _rms(x_ref[...], g_ref[...])


def final_norm(x, g, tm):
    n, d = x.shape
    return pl.pallas_call(
        _final_norm_kernel,
        grid=(n // tm,),
        in_specs=[pl.BlockSpec((tm, d), lambda i: (i, 0)), pl.BlockSpec((1, d), lambda i: (0, 0))],
        out_specs=pl.BlockSpec((tm, d), lambda i: (i, 0)),
        out_shape=jax.ShapeDtypeStruct((n, d), F32),
        compiler_params=_cparams(("parallel",)),
        name="final_norm",
    )(x, g.reshape(1, d))


def _intra_chunk(q, k, b, heads):
    c = q.shape[0]
    nb = c // SUBLANES
    row = lax.broadcasted_iota(I32, (SUBLANES, 1), 0)
    qb = [q[SUBLANES * i:SUBLANES * (i + 1)] for i in range(nb)]
    bb = [b[SUBLANES * i:SUBLANES * (i + 1)] for i in range(nb)]
    outs = [[None] * nb for _ in heads]
    for s in range(c):
        rb0 = s // SUBLANES
        ks = k[s:s + 1, :]
        bs = b[s:s + 1, :]
        for rb in range(rb0, nb):
            p = qb[rb] * (ks * jnp.exp(bb[rb] - bs))
            for hi, (mask, v) in enumerate(heads):
                pm = p if mask is None else jnp.where(mask, p, 0.0)
                col = jnp.sum(pm, axis=-1, keepdims=True)
                if rb == rb0:
                    col = jnp.where(row + SUBLANES * rb >= s, col, 0.0)
                term = col * v[s:s + 1, :]
                outs[hi][rb] = term if outs[hi][rb] is None else outs[hi][rb] + term
    return [jnp.concatenate(o, axis=0) for o in outs]


def _gated_chunk(q, k, b, heads, st):
    intra = _intra_chunk(q, k, b, heads)
    qe = q * jnp.exp(b)
    b_last = b[-1:, :]
    kh = k * jnp.exp(b_last - b)
    outs = []
    for (mask, v), oi in zip(heads, intra):
        qm = qe if mask is None else jnp.where(mask, qe, 0.0)
        outs.append(oi + _dot_nt(qm, st, precision=HI))
    upd = _dot_tn(heads[0][1], kh, precision=HI)
    if len(heads) == 2:
        upd = jnp.where(heads[0][0], upd, _dot_tn(heads[1][1], kh, precision=HI))
    st = st * jnp.exp(b_last) + upd
    return outs, st


def _head_rms(o, g):
    ms = jnp.mean(o * o, axis=-1, keepdims=True)
    return o * lax.rsqrt(ms + NORM_EPS) * g


def _hgrn_kernel(zq_ref, zf_ref, zi_ref, zg_ref, lb_ref, ng_ref, s0_ref, o_ref, s_ref, st_scr, *, chunk):
    t = pl.program_id(2)

    @pl.when(t == 0)
    def _():
        st_scr[...] = s0_ref[0, 0]

    lb = lb_ref[0]
    nchunks = zq_ref.shape[1] // chunk

    def body(ci, carry):
        sl = pl.ds(pl.multiple_of(ci * chunk, chunk), chunk)
        f = lb + (1.0 - lb) * _sigmoid(zf_ref[0, sl, :])
        q = _silu(zq_ref[0, sl, :])
        b = _cumsum_rows(jnp.log(f))
        (o,), st = _gated_chunk(q, 1.0 - f, b, [(None, zi_ref[0, sl, :])], st_scr[...])
        st_scr[...] = st
        o_ref[0, sl, :] = _head_rms(o, ng_ref[...]) * _silu(zg_ref[0, sl, :])
        return carry

    lax.fori_loop(0, nchunks, body, 0)

    @pl.when(t == pl.num_programs(2) - 1)
    def _():
        s_ref[0, 0] = st_scr[...]


def hgrn_recurrence(z, lb, norm_g, s0_t, tb, chunk):
    bsz, t, _ = z.shape
    zspec = lambda off: pl.BlockSpec((1, tb, LANES), lambda b, h, i: (b, i, h + off))
    return pl.pallas_call(
        functools.partial(_hgrn_kernel, chunk=chunk),
        grid=(bsz, A_HEADS, t // tb),
        in_specs=[zspec(0), zspec(4), zspec(8), zspec(12),
                  pl.BlockSpec((1, 1, LANES), lambda b, h, i: (h, 0, 0)),
                  pl.BlockSpec((1, LANES), lambda b, h, i: (0, 0)),
                  pl.BlockSpec((1, 1, LANES, LANES), lambda b, h, i: (b, h, 0, 0))],
        out_specs=[pl.BlockSpec((1, tb, LANES), lambda b, h, i: (b, i, h)),
                   pl.BlockSpec((1, 1, LANES, LANES), lambda b, h, i: (b, h, 0, 0))],
        out_shape=[jax.ShapeDtypeStruct((bsz, t, A_WIDTH), F32),
                   jax.ShapeDtypeStruct((bsz, A_HEADS, LANES, LANES), F32)],
        scratch_shapes=[pltpu.VMEM((LANES, LANES), F32)],
        compiler_params=_cparams(("parallel", "parallel", "arbitrary")),
        name="hgrn_recurrence",
    )(z, z, z, z, lb.reshape(A_HEADS, 1, LANES), norm_g.reshape(1, LANES), s0_t)


def _gla_kernel(zq_ref, zk_ref, zv_ref, zg_ref, zlr_ref, w2_ref, gb_ref, ng_ref, s0_ref, o_ref, s_ref, st_scr,
                *, chunk):
    t = pl.program_id(2)

    @pl.when(t == 0)
    def _():
        st_scr[...] = s0_ref[0, 0]

    nchunks = zq_ref.shape[1] // chunk
    m0 = _lane_mask(LANES, 0, B_DK)
    m1 = _lane_mask(LANES, B_DK, LANES)

    def body(ci, carry):
        sl = pl.ds(pl.multiple_of(ci * chunk, chunk), chunk)
        pre = _dot(zlr_ref[0, sl, :], w2_ref[...], precision=HI) + gb_ref[...]
        log_g = (jnp.minimum(pre, 0.0) - jnp.log(1.0 + jnp.exp(-jnp.abs(pre)))) * (1.0 / GLA_NORMALIZER)
        q = zq_ref[0, sl, :] * (B_DK ** -0.5)
        b = _cumsum_rows(log_g)
        v = zv_ref[0, sl, :]
        (o0, o1), st = _gated_chunk(q, zk_ref[0, sl, :], b,
                                    [(m0, v[:, :LANES]), (m1, v[:, LANES:])], st_scr[...])
        st_scr[...] = st
        gate = _silu(zg_ref[0, sl, :])
        o_ref[0, sl, 0:LANES] = _head_rms(o0, ng_ref[...]) * gate[:, :LANES]
        o_ref[0, sl, LANES:2 * LANES] = _head_rms(o1, ng_ref[...]) * gate[:, LANES:]
        return carry

    lax.fori_loop(0, nchunks, body, 0)

    @pl.when(t == pl.num_programs(2) - 1)
    def _():
        s_ref[0, 0] = st_scr[...]


def gla_recurrence(z, w2pad, gate_b, norm_g, s0_t, tb, chunk):
    bsz, t, _ = z.shape
    npairs = B_HEADS // 2
    return pl.pallas_call(
        functools.partial(_gla_kernel, chunk=chunk),
        grid=(bsz, npairs, t // tb),
        in_specs=[pl.BlockSpec((1, tb, LANES), lambda b, p, i: (b, i, 16 + p)),
                  pl.BlockSpec((1, tb, LANES), lambda b, p, i: (b, i, 18 + p)),
                  pl.BlockSpec((1, tb, 2 * LANES), lambda b, p, i: (b, i, 10 + p)),
                  pl.BlockSpec((1, tb, 2 * LANES), lambda b, p, i: (b, i, 12 + p)),
                  pl.BlockSpec((1, tb, LANES), lambda b, p, i: (b, i, 28)),
                  pl.BlockSpec((LANES, LANES), lambda b, p, i: (0, p)),
                  pl.BlockSpec((1, LANES), lambda b, p, i: (0, p)),
                  pl.BlockSpec((1, LANES), lambda b, p, i: (0, 0)),
                  pl.BlockSpec((1, 1, LANES, LANES), lambda b, p, i: (b, p, 0, 0))],
        out_specs=[pl.BlockSpec((1, tb, 2 * LANES), lambda b, p, i: (b, i, p)),
                   pl.BlockSpec((1, 1, LANES, LANES), lambda b, p, i: (b, p, 0, 0))],
        out_shape=[jax.ShapeDtypeStruct((bsz, t, B_WIDTH), F32),
                   jax.ShapeDtypeStruct((bsz, npairs, LANES, LANES), F32)],
        scratch_shapes=[pltpu.VMEM((LANES, LANES), F32)],
        compiler_params=_cparams(("parallel", "parallel", "arbitrary")),
        name="gla_recurrence",
    )(z, z, z, z, z, w2pad, gate_b.reshape(1, 2 * LANES), norm_g.reshape(1, LANES), s0_t)


def _rwkv_proj_kernel(x_ref, xp_ref, xl_ref, g1_ref, mu_ref, wr_ref, wk_ref, wv_ref, ww1_ref, ww2_ref, w0_ref,
                      aw1_ref, aw2_ref, a0_ref, gw1_ref, gw2_ref, kk_ref, ka_ref,
                      r_out, k_out, v_out, lw_out, a_out, kk_out, g_out, hl_out, *, tiles_per_seq):
    i = pl.program_id(0)
    g1 = g1_ref[...]
    hn = _rms(x_ref[...], g1)
    tm = hn.shape[0]
    prev = _rms(xp_ref[...], g1)[SUBLANES - 1:SUBLANES, :]
    prev = jnp.where(i % tiles_per_seq == 0, xl_ref[0], prev)
    row = lax.broadcasted_iota(I32, (tm, 1), 0)
    xprev = jnp.where(row == 0, prev, pltpu.roll(hn, 1, axis=0))
    dx = xprev - hn

    def mix(j):
        return hn + dx * mu_ref[j:j + 1, :]

    r = _dot(mix(0).astype(BF16), wr_ref[...])
    k = _dot(mix(1).astype(BF16), wk_ref[...])
    v = _dot(mix(2).astype(BF16), wv_ref[...])
    wl = _dot(jnp.tanh(_dot(mix(3), ww1_ref[...], precision=HI)), ww2_ref[...], precision=HI)
    z = w0_ref[...] + wl
    wpre = -(jnp.maximum(-z, 0.0) + jnp.log(1.0 + jnp.exp(-jnp.abs(z)))) - 0.5
    al = _dot(_dot(mix(4), aw1_ref[...], precision=HI), aw2_ref[...], precision=HI)
    a = _sigmoid(a0_ref[...] + al)
    gg = _dot(_sigmoid(_dot(mix(5).astype(BF16), gw1_ref[...])).astype(BF16), gw2_ref[...])
    r_out[...] = r
    k_out[...] = k * (1.0 + (a - 1.0) * ka_ref[...])
    v_out[...] = v
    lw_out[...] = -jnp.exp(wpre)
    a_out[...] = a
    kk_out[...] = k * kk_ref[...]
    g_out[...] = gg
    hl_out[0] = hn[tm - 1:tm, :]


def rwkv_proj(x, x_last, seq_len, g1, mu, wr, wk, wv, ww1, ww2, w0, aw1, aw2, a0, gw1, gw2, k_k, k_a, tm):
    n, d = x.shape
    tiles_per_seq = seq_len // tm
    row = lambda a: a.reshape(1, d)
    full = lambda a: pl.BlockSpec(a.shape, lambda i: (0,) * a.ndim)
    tile = pl.BlockSpec((tm, d), lambda i: (i, 0))
    blocks8 = tm // SUBLANES
    args = (x, x, x_last.reshape(-1, 1, d), row(g1), mu, wr, wk, wv, ww1, ww2, row(w0), aw1, aw2, row(a0),
            gw1, gw2, row(k_k), row(k_a))
    in_specs = [tile,
                pl.BlockSpec((SUBLANES, d), lambda i: (jnp.maximum(i * blocks8 - 1, 0), 0)),
                pl.BlockSpec((1, 1, d), lambda i: (i // tiles_per_seq, 0, 0))]
    in_specs += [full(a) for a in args[3:]]
    outs = pl.pallas_call(
        functools.partial(_rwkv_proj_kernel, tiles_per_seq=tiles_per_seq),
        grid=(n // tm,),
        in_specs=in_specs,
        out_specs=[tile] * 7 + [pl.BlockSpec((1, 1, d), lambda i: (i, 0, 0))],
        out_shape=[jax.ShapeDtypeStruct((n, d), F32)] * 7 + [jax.ShapeDtypeStruct((n // tm, 1, d), F32)],
        compiler_params=_cparams(("parallel",)),
        name="rwkv_proj",
    )(*args)
    return outs


def _unit_lower_inverse(nmat):
    c = nmat.shape[0]
    eye = (lax.broadcasted_iota(I32, (c, c), 0) == lax.broadcasted_iota(I32, (c, c), 1)).astype(F32)
    t = eye + nmat
    p = nmat
    span = 2
    while span < c:
        p = _dot(p, p, precision=HI)
        t = t + _dot(t, p, precision=HI)
        span *= 2
    return t


def _pair_sum(x, m0):
    s0 = jnp.sum(jnp.where(m0, x, 0.0), axis=-1, keepdims=True)
    s1 = jnp.sum(jnp.where(m0, 0.0, x), axis=-1, keepdims=True)
    return jnp.where(m0, s0, s1)


def _rwkv_kernel(r_ref, k_ref, v_ref, lw_ref, a_ref, kk_ref, g_ref, rk_ref, lng_ref, lnb_ref, s0_ref,
                 o_ref, s_ref, mt_scr, *, chunk):
    t = pl.program_id(2)

    @pl.when(t == 0)
    def _():
        mt_scr[...] = s0_ref[0, 0]

    nchunks = r_ref.shape[1] // chunk
    m0 = _lane_mask(LANES, 0, C_HEAD)
    rowi = lax.broadcasted_iota(I32, (LANES, LANES), 0)
    coli = lax.broadcasted_iota(I32, (LANES, LANES), 1)
    blockdiag = (rowi < C_HEAD) == (coli < C_HEAD)
    strict = _tri(chunk, True)
    incl = _tri(chunk, False)

    def body(ci, carry):
        sl = pl.ds(pl.multiple_of(ci * chunk, chunk), chunk)
        r = r_ref[0, sl, :]
        k = k_ref[0, sl, :]
        v = v_ref[0, sl, :]
        lw = lw_ref[0, sl, :]
        a = a_ref[0, sl, :]
        kkr = kk_ref[0, sl, :]
        kk = kkr * lax.rsqrt(_pair_sum(kkr * kkr, m0) + 1e-12)
        al = a * kk
        gam = _cumsum_rows(lw)
        e_neg = jnp.exp(-gam)
        kap_t = kk * jnp.exp(gam - lw)
        al_t = al * e_neg
        k_t = k * e_neg
        r_t = r * jnp.exp(gam)
        mt = mt_scr[...]

        def per_head(fn):
            x0 = fn(m0)
            x1 = fn(jnp.logical_not(m0))
            return jnp.where(m0, x0, x1)

        def masked(x, m):
            return jnp.where(m, x, 0.0)

        def solve(m):
            kap_m = masked(kap_t, m)
            a_al = jnp.where(strict, _dot_nt(kap_m, al_t, precision=HI), 0.0)
            a_k = jnp.where(strict, _dot_nt(kap_m, k_t, precision=HI), 0.0)
            tinv = _unit_lower_inverse(-a_al)
            rhs = _dot_nt(kap_m, mt, precision=HI) + _dot(a_k, v, precision=HI)
            return _dot(tinv, rhs, precision=HI)

        u = per_head(solve)

        def outp(m):
            r_m = masked(r_t, m)
            b_al = jnp.where(incl, _dot_nt(r_m, al_t, precision=HI), 0.0)
            b_k = jnp.where(incl, _dot_nt(r_m, k_t, precision=HI), 0.0)
            return _dot_nt(r_m, mt, precision=HI) + _dot(b_k, v, precision=HI) - _dot(b_al, u, precision=HI)

        o = per_head(outp)
        g_last = gam[-1:, :]
        e_end = jnp.exp(g_last - gam)
        upd = _dot_tn(v, k * e_end, precision=HI) - _dot_tn(u, al * e_end, precision=HI)
        mt_scr[...] = mt * jnp.exp(g_last) + jnp.where(blockdiag, upd, 0.0)

        mean = _pair_sum(o, m0) * (1.0 / C_HEAD)
        cen = o - mean
        var = _pair_sum(cen * cen, m0) * (1.0 / C_HEAD)
        on = cen * lax.rsqrt(var + C_GN_EPS) * lng_ref[...] + lnb_ref[...]
        bonus = _pair_sum(r * k * rk_ref[...], m0) * v
        o_ref[0, sl, :] = (on + bonus) * g_ref[0, sl, :]
        return carry

    lax.fori_loop(0, nchunks, body, 0)

    @pl.when(t == pl.num_programs(2) - 1)
    def _():
        s_ref[0, 0] = mt_scr[...]


def rwkv_recurrence(r, k, v, lw, a, kk, g, r_k, ln_g, ln_b, s0_bd, tb, chunk):
    bsz, t, d = r.shape
    npairs = C_HEADS // 2
    seq = pl.BlockSpec((1, tb, LANES), lambda b, p, i: (b, i, p))
    vec = pl.BlockSpec((1, LANES), lambda b, p, i: (0, p))
    st = pl.BlockSpec((1, 1, LANES, LANES), lambda b, p, i: (b, p, 0, 0))
    return pl.pallas_call(
        functools.partial(_rwkv_kernel, chunk=chunk),
        grid=(bsz, npairs, t // tb),
        in_specs=[seq] * 7 + [vec, vec, vec, st],
        out_specs=[seq, st],
        out_shape=[jax.ShapeDtypeStruct((bsz, t, d), F32),
                   jax.ShapeDtypeStruct((bsz, npairs, LANES, LANES), F32)],
        scratch_shapes=[pltpu.VMEM((LANES, LANES), F32)],
        compiler_params=_cparams(("parallel", "parallel", "arbitrary")),
        name="rwkv_recurrence",
    )(r, k, v, lw, a, kk, g, r_k.reshape(1, d), ln_g.reshape(1, d), ln_b.reshape(1, d), s0_bd)


NEG_INF = float("-inf")


def _top16_rows(s):
    n = s.shape[0]
    key = lax.broadcasted_iota(I32, s.shape, 0)
    vals, idxs = [], []
    for _ in range(P_TOPK):
        m = jnp.max(s, axis=0, keepdims=True)
        am = jnp.min(jnp.where(s == m, key, n), axis=0, keepdims=True)
        vals.append(m)
        idxs.append(am)
        s = jnp.where(key == am, NEG_INF, s)
    return vals, idxs


def _top16_pairs(v0, i0, v1, i1):
    a0 = jnp.concatenate(v0[0:8], axis=0)
    a1 = jnp.concatenate(v0[8:16], axis=0)
    b0 = jnp.concatenate(v1[0:8], axis=0)
    b1 = jnp.concatenate(v1[8:16], axis=0)
    ia0 = jnp.concatenate(i0[0:8], axis=0) * P_KEYS
    ia1 = jnp.concatenate(i0[8:16], axis=0) * P_KEYS
    ib0 = jnp.concatenate(i1[0:8], axis=0)
    ib1 = jnp.concatenate(i1[8:16], axis=0)
    row = lax.broadcasted_iota(I32, (SUBLANES, 1), 0)
    slabs = []

    def add(val, eid, keep):
        slabs.append((val if keep is None else jnp.where(keep, val, NEG_INF), eid))

    add(v0[0] + b0, ia0[0:1] + ib0, None)
    add(v0[0] + b1, ia0[0:1] + ib1, None)
    add(v0[1] + b0, ia0[1:2] + ib0, None)
    add(v0[2] + b0, ia0[2:3] + ib0, row < 5)
    add(v0[3] + b0, ia0[3:4] + ib0, row < 4)
    add(a0 + v1[0], ia0 + ib0[0:1], row >= 4)
    add(a1 + v1[0], ia1 + ib0[0:1], None)
    add(a0 + v1[1], ia0 + ib0[1:2], row >= 4)
    add(a0 + v1[2], ia0 + ib0[2:3], row == 4)

    big = P_KEYS * P_KEYS
    out_v, out_e = [], []
    for _ in range(P_TOPK):
        m = slabs[0][0]
        for val, _e in slabs[1:]:
            m = jnp.maximum(m, val)
        m = jnp.max(m, axis=0, keepdims=True)
        e = None
        for val, eid in slabs:
            c = jnp.where(val == m, eid, big)
            e = c if e is None else jnp.minimum(e, c)
        e = jnp.min(e, axis=0, keepdims=True)
        out_v.append(m)
        out_e.append(e)
        slabs = [(jnp.where(eid == e, NEG_INF, val), eid) for val, eid in slabs]
    return out_v, out_e


def _peer_select_kernel(x_ref, g_ref, wq_ref, keys_ref, xn_out, eid_out, gate_out):
    hn = _rms(x_ref[...], g_ref[...])
    xn_out[...] = hn
    q = _dot(hn.astype(BF16), wq_ref[...])
    tm = q.shape[0]
    for lt in range(tm // LANES):
        rows = slice(lt * LANES, (lt + 1) * LANES)
        e_rows, g_rows = [], []
        for h in range(P_HEADS):
            tops = []
            for p in range(2):
                hp = 2 * h + p
                s = _dot_nt(keys_ref[hp], q[rows, hp * LANES:(hp + 1) * LANES], precision=HI)
                tops.append(_top16_rows(s))
            cs, ce = _top16_pairs(tops[0][0], tops[0][1], tops[1][0], tops[1][1])
            ex = [jnp.exp(c - cs[0]) for c in cs]
            tot = ex[0]
            for e in ex[1:]:
                tot = tot + e
            inv = 1.0 / tot
            e_rows += ce
            g_rows += [e * inv for e in ex]
        eid_out[rows, :] = jnp.concatenate(e_rows, axis=0).T
        gate_out[rows, :] = jnp.concatenate(g_rows, axis=0).T


def peer_select(x, g, wq_bf16, keys, tm):
    n, d = x.shape
    return pl.pallas_call(
        _peer_select_kernel,
        grid=(n // tm,),
        in_specs=[pl.BlockSpec((tm, d), lambda i: (i, 0)),
                  pl.BlockSpec((1, d), lambda i: (0, 0)),
                  pl.BlockSpec(wq_bf16.shape, lambda i: (0, 0)),
                  pl.BlockSpec(keys.shape, lambda i: (0, 0, 0))],
        out_specs=[pl.BlockSpec((tm, d), lambda i: (i, 0)),
                   pl.BlockSpec((tm, P_PICKS), lambda i: (i, 0)),
                   pl.BlockSpec((tm, P_PICKS), lambda i: (i, 0))],
        out_shape=[jax.ShapeDtypeStruct((n, d), F32),
                   jax.ShapeDtypeStruct((n, P_PICKS), I32),
                   jax.ShapeDtypeStruct((n, P_PICKS), F32)],
        compiler_params=_cparams(("parallel",)),
        name="peer_select",
    )(x, g.reshape(1, d), wq_bf16, keys)


def _peer_act_kernel(h_ref, g_ref, o_ref):
    h = h_ref[...]
    o_ref[...] = 0.5 * h * (1.0 + lax.erf(h * (2.0 ** -0.5))) * g_ref[...]


def peer_act(hid, gate, tm):
    n, p = hid.shape
    spec = pl.BlockSpec((tm, p), lambda i: (i, 0))
    return pl.pallas_call(
        _peer_act_kernel, grid=(n // tm,), in_specs=[spec, spec], out_specs=spec,
        out_shape=jax.ShapeDtypeStruct((n, p), F32),
        compiler_params=_cparams(("parallel",)), name="peer_act",
    )(hid, gate)


_CHUNKS_PER_TOK = P_PICKS // SC_CHUNK
_VREGS_PER_ROW = D_MODEL // SC_LANES


def _sc_worker():
    return lax.axis_index("s") * SC_CORES + lax.axis_index("c")


def _sc_hid_body(tab_hbm, idx_hbm, x_hbm, hid_hbm, idx_v, x_v, hid_v, rows_a, rows_b, sem_a, sem_b, *, tok_per_w):
    wid = _sc_worker()
    nblk = tok_per_w // SC_TOK_BLOCK
    nch = SC_TOK_BLOCK * _CHUNKS_PER_TOK
    lane = lax.iota(I32, SC_LANES)

    def dots(rows, x_row, res_ref, col0):
        for half in range(SC_CHUNK // SC_LANES):
            def one(rr, res):
                r = half * SC_LANES + rr
                acc = rows[r, pl.ds(0, SC_LANES)] * x_v[x_row, pl.ds(0, SC_LANES)]
                for j in range(1, _VREGS_PER_ROW):
                    acc = acc + rows[r, pl.ds(j * SC_LANES, SC_LANES)] * x_v[x_row, pl.ds(j * SC_LANES, SC_LANES)]
                return jnp.where(lane == rr, jnp.sum(acc), res)
            res = lax.fori_loop(0, SC_LANES, one, jnp.zeros((SC_LANES,), F32))
            res_ref[x_row, pl.ds(col0 + half * SC_LANES, SC_LANES)] = res

    def block(bi, carry):
        tok0 = wid * tok_per_w + bi * SC_TOK_BLOCK
        pltpu.sync_copy(idx_hbm.at[pl.ds(tok0 * _CHUNKS_PER_TOK, nch)], idx_v)
        pltpu.sync_copy(x_hbm.at[pl.ds(tok0, SC_TOK_BLOCK)], x_v)
        pltpu.async_copy(tab_hbm.at[idx_v.at[0]], rows_a, sem_a)

        def pair(j, c):
            ca = 2 * j
            pltpu.async_copy(tab_hbm.at[idx_v.at[ca + 1]], rows_b, sem_b)
            pltpu.make_async_copy(tab_hbm.at[idx_v.at[0]], rows_a, sem_a).wait()
            dots(rows_a, ca // _CHUNKS_PER_TOK, hid_v, (ca % _CHUNKS_PER_TOK) * SC_CHUNK)

            @pl.when(j < nch // 2 - 1)
            def _():
                pltpu.async_copy(tab_hbm.at[idx_v.at[ca + 2]], rows_a, sem_a)
            pltpu.make_async_copy(tab_hbm.at[idx_v.at[0]], rows_b, sem_b).wait()
            cb = ca + 1
            dots(rows_b, cb // _CHUNKS_PER_TOK, hid_v, (cb % _CHUNKS_PER_TOK) * SC_CHUNK)
            return c
        lax.fori_loop(0, nch // 2, pair, 0)
        pltpu.sync_copy(hid_v, hid_hbm.at[pl.ds(tok0, SC_TOK_BLOCK)])
        return carry

    lax.fori_loop(0, nblk, block, 0)


def sc_expert_hidden(table, eidx, xn):
    n = xn.shape[0]
    tok_per_w = n // SC_WORKERS
    mesh = plsc.VectorSubcoreMesh(core_axis_name="c", subcore_axis_name="s")
    nch = SC_TOK_BLOCK * _CHUNKS_PER_TOK
    k = pl.kernel(
        functools.partial(_sc_hid_body, tok_per_w=tok_per_w), mesh=mesh,
        out_type=jax.ShapeDtypeStruct((n, P_PICKS), F32),
        scratch_types=[pltpu.VMEM((nch, SC_CHUNK), I32),
                       pltpu.VMEM((SC_TOK_BLOCK, D_MODEL), F32),
                       pltpu.VMEM((SC_TOK_BLOCK, P_PICKS), F32),
                       pltpu.VMEM((SC_CHUNK, D_MODEL), F32),
                       pltpu.VMEM((SC_CHUNK, D_MODEL), F32),
                       pltpu.SemaphoreType.DMA, pltpu.SemaphoreType.DMA],
        compiler_params=pltpu.CompilerParams(needs_layout_passes=False),
        name="sc_expert_hidden",
    )
    return k(table, eidx.reshape(n * _CHUNKS_PER_TOK, SC_CHUNK), xn)


def _sc_out_body(tab_hbm, idx_hbm, act_hbm, x_hbm, y_hbm, idx_v, act_v, y_v, rows_a, rows_b, sem_a, sem_b,
                 *, tok_per_w):
    wid = _sc_worker()
    nblk = tok_per_w // SC_TOK_BLOCK
    nch = SC_TOK_BLOCK * _CHUNKS_PER_TOK
    half_v = _VREGS_PER_ROW // 2

    def accum(rows, tok, col0):
        for hv in range(2):
            base = hv * half_v * SC_LANES
            acc0 = tuple(y_v[tok, pl.ds(base + j * SC_LANES, SC_LANES)] for j in range(half_v))

            def one(r, acc):
                aidx = jnp.full((SC_LANES,), col0, I32) + r
                w = plsc.load_gather(act_v, [jnp.full((SC_LANES,), tok, I32), aidx])
                return tuple(acc[j] + w * rows[r, pl.ds(base + j * SC_LANES, SC_LANES)] for j in range(half_v))
            acc = lax.fori_loop(0, SC_CHUNK, one, acc0)
            for j in range(half_v):
                y_v[tok, pl.ds(base + j * SC_LANES, SC_LANES)] = acc[j]

    def block(bi, carry):
        tok0 = wid * tok_per_w + bi * SC_TOK_BLOCK
        pltpu.sync_copy(idx_hbm.at[pl.ds(tok0 * _CHUNKS_PER_TOK, nch)], idx_v)
        pltpu.sync_copy(act_hbm.at[pl.ds(tok0, SC_TOK_BLOCK)], act_v)
        pltpu.sync_copy(x_hbm.at[pl.ds(tok0, SC_TOK_BLOCK)], y_v)
        pltpu.async_copy(tab_hbm.at[idx_v.at[0]], rows_a, sem_a)

        def pair(j, c):
            ca = 2 * j
            pltpu.async_copy(tab_hbm.at[idx_v.at[ca + 1]], rows_b, sem_b)
            pltpu.make_async_copy(tab_hbm.at[idx_v.at[0]], rows_a, sem_a).wait()
            accum(rows_a, ca // _CHUNKS_PER_TOK, (ca % _CHUNKS_PER_TOK) * SC_CHUNK)

            @pl.when(j < nch // 2 - 1)
            def _():
                pltpu.async_copy(tab_hbm.at[idx_v.at[ca + 2]], rows_a, sem_a)
            pltpu.make_async_copy(tab_hbm.at[idx_v.at[0]], rows_b, sem_b).wait()
            cb = ca + 1
            accum(rows_b, cb // _CHUNKS_PER_TOK, (cb % _CHUNKS_PER_TOK) * SC_CHUNK)
            return c
        lax.fori_loop(0, nch // 2, pair, 0)
        pltpu.sync_copy(y_v, y_hbm.at[pl.ds(tok0, SC_TOK_BLOCK)])
        return carry

    lax.fori_loop(0, nblk, block, 0)


def sc_expert_output(table, eidx, act, x):
    n = x.shape[0]
    tok_per_w = n // SC_WORKERS
    mesh = plsc.VectorSubcoreMesh(core_axis_name="c", subcore_axis_name="s")
    nch = SC_TOK_BLOCK * _CHUNKS_PER_TOK
    k = pl.kernel(
        functools.partial(_sc_out_body, tok_per_w=tok_per_w), mesh=mesh,
        out_type=jax.ShapeDtypeStruct((n, D_MODEL), F32),
        scratch_types=[pltpu.VMEM((nch, SC_CHUNK), I32),
                       pltpu.VMEM((SC_TOK_BLOCK, P_PICKS), F32),
                       pltpu.VMEM((SC_TOK_BLOCK, D_MODEL), F32),
                       pltpu.VMEM((SC_CHUNK, D_MODEL), F32),
                       pltpu.VMEM((SC_CHUNK, D_MODEL), F32),
                       pltpu.SemaphoreType.DMA, pltpu.SemaphoreType.DMA],
        compiler_params=pltpu.CompilerParams(needs_layout_passes=False),
        name="sc_expert_output",
    )
    return k(table, eidx.reshape(n * _CHUNKS_PER_TOK, SC_CHUNK), act, x)


def peer_ffn(x, g, wq_bf16, keys, u_tab, v_tab, tm):
    xn, eidx, gate = peer_select(x, g, wq_bf16, keys, tm)
    hid = sc_expert_hidden(u_tab, eidx, xn)
    act = peer_act(hid, gate, tm)
    return sc_expert_output(v_tab, eidx, act, x)


def _prep_w_in(w_in):
    main = jnp.concatenate([w_in[:, :3072], w_in[:, 3088:3600]], axis=1)
    lr = jnp.pad(w_in[:, 3072:3088], ((0, 0), (0, LANES - GLA_RANK)))
    return jnp.concatenate([main, lr], axis=1).astype(BF16)


def _trunk(x, st_h, st_g, st_r, st_s, w, seq_len, tm, tmp, tb, chunk):
    bsz = x.shape[0]
    n = bsz * seq_len
    x2 = x.reshape(n, D_MODEL)

    z = norm_proj(x2, w["norm1_g"][0], w["w_in"], tm).reshape(bsz, seq_len, Z_WIDTH)
    s0_h = jnp.swapaxes(st_h, -1, -2)
    s0_g = jnp.swapaxes(st_g.reshape(bsz, 2, 2 * B_DK, LANES), -1, -2)
    o_a, sh_t = hgrn_recurrence(z, w["lb0"], w["hgrn_norm_g"], s0_h, tb, chunk)
    o_b, sg_t = gla_recurrence(z, w["gla_w2"], w["gla_b"], w["gla_norm_g"], s0_g, tb, chunk)
    new_h = jnp.swapaxes(sh_t, -1, -2)
    new_g = jnp.swapaxes(sg_t, -1, -2).reshape(bsz, B_HEADS, B_DK, LANES)
    x2 = out_proj2(x2, o_a.reshape(n, A_WIDTH), o_b.reshape(n, B_WIDTH), w["w_out_a"], w["w_out_b"], tm)
    x2 = peer_ffn(x2, w["norm2_g"][0], w["peer_wq"][0], w["peer_keys"][0], w["peer_u"][0], w["peer_v"][0], tmp)

    r, k, v, lw, a, kk, g, hl = rwkv_proj(
        x2, st_s, seq_len, w["norm1_g"][1], w["mu"], w["wr"], w["wk"], w["wv"], w["w_w1"], w["w_w2"], w["w0"],
        w["a_w1"], w["a_w2"], w["a0"], w["g_w1"], w["g_w2"], w["k_k"], w["k_a"], tm)
    new_s = hl.reshape(bsz, seq_len // tm, D_MODEL)[:, -1]
    pr = st_r.reshape(bsz, C_HEADS // 2, 2, C_HEAD, C_HEAD)
    zero = jnp.zeros_like(pr[:, :, 0])
    s0_r = jnp.concatenate([jnp.concatenate([pr[:, :, 0], zero], axis=-1),
                            jnp.concatenate([zero, pr[:, :, 1]], axis=-1)], axis=-2)
    sh3 = lambda t: t.reshape(bsz, seq_len, D_MODEL)
    o_c, sr_bd = rwkv_recurrence(sh3(r), sh3(k), sh3(v), sh3(lw), sh3(a), sh3(kk), sh3(g),
                                 w["r_k"], w["ln_g"], w["ln_b"], s0_r, tb, chunk)
    new_r = jnp.stack([sr_bd[:, :, :C_HEAD, :C_HEAD], sr_bd[:, :, C_HEAD:, C_HEAD:]], axis=2)
    new_r = new_r.reshape(bsz, C_HEADS, C_HEAD, C_HEAD)
    x2 = out_proj1(x2, o_c.reshape(n, D_MODEL), w["w_out_c"], tm)
    x2 = peer_ffn(x2, w["norm2_g"][1], w["peer_wq"][1], w["peer_keys"][1], w["peer_u"][1], w["peer_v"][1], tmp)

    y = final_norm(x2, w["final_g"], tm).reshape(bsz, seq_len, D_MODEL)
    return y, new_h[None], new_g[None], new_r[None], new_s[None]


def kernel(x_prompt, x_sample, state_hgrn, state_gla, state_rwkv, state_shift, w_in_ab, hgrn_lower_bounds, hgrn_norm_g, gla_gate_w2, gla_gate_b, gla_norm_g, w_out_ab, rwkv_mu, rwkv_w_rkv, rwkv_w_w1, rwkv_w_w2, rwkv_w0, rwkv_a_w1, rwkv_a_w2, rwkv_a0, rwkv_g_w1, rwkv_g_w2, rwkv_k_k, rwkv_k_a, rwkv_r_k, rwkv_ln_g, rwkv_ln_b, w_out_c, norm1_g, norm2_g, final_g, peer_w_q, peer_sub_keys, peer_u, peer_v):
    lbs = jnp.cumsum(jax.nn.softmax(hgrn_lower_bounds.astype(F32), axis=0), axis=0)
    w = dict(
        norm1_g=norm1_g, norm2_g=norm2_g, final_g=final_g,
        w_in=_prep_w_in(w_in_ab[0]), lb0=lbs[0], hgrn_norm_g=hgrn_norm_g[0],
        gla_w2=jnp.pad(gla_gate_w2[0], ((0, LANES - GLA_RANK), (0, 0))), gla_b=gla_gate_b[0],
        gla_norm_g=gla_norm_g[0],
        w_out_a=w_out_ab[0, :A_WIDTH].astype(BF16), w_out_b=w_out_ab[0, A_WIDTH:].astype(BF16),
        mu=rwkv_mu[0], wr=rwkv_w_rkv[0, 0].astype(BF16), wk=rwkv_w_rkv[0, 1].astype(BF16),
        wv=rwkv_w_rkv[0, 2].astype(BF16), w_w1=rwkv_w_w1[0], w_w2=rwkv_w_w2[0], w0=rwkv_w0[0],
        a_w1=rwkv_a_w1[0], a_w2=rwkv_a_w2[0], a0=rwkv_a0[0],
        g_w1=rwkv_g_w1[0].astype(BF16), g_w2=rwkv_g_w2[0].astype(BF16),
        k_k=rwkv_k_k[0], k_a=rwkv_k_a[0], r_k=rwkv_r_k[0], ln_g=rwkv_ln_g[0], ln_b=rwkv_ln_b[0],
        w_out_c=w_out_c[0].astype(BF16),
        peer_wq=peer_w_q.astype(BF16),
        peer_keys=peer_sub_keys.reshape(peer_sub_keys.shape[0], 2 * P_HEADS, P_KEYS, P_KEYS),
        peer_u=peer_u, peer_v=peer_v,
    )
    bp, tp, _ = x_prompt.shape
    bs, ts, _ = x_sample.shape
    zeros = lambda s: jnp.zeros((bp,) + s.shape[2:], F32)
    y_p, p_h, p_g, p_r, p_s = _trunk(x_prompt, zeros(state_hgrn), zeros(state_gla), zeros(state_rwkv),
                                     zeros(state_shift), w, tp, tm=256, tmp=256, tb=512, chunk=64)
    y_s, s_h, s_g, s_r, s_s = _trunk(x_sample, state_hgrn[0], state_gla[0], state_rwkv[0], state_shift[0],
                                     w, ts, tm=32, tmp=128, tb=32, chunk=32)
    return (y_p, y_s, p_h, p_g, p_r, p_s, s_h, s_g, s_r, s_s)
```

```python
import functools

import jax
import jax.numpy as jnp
from jax import lax
from jax.experimental import pallas as pl
from jax.experimental.pallas import tpu as pltpu
from jax.experimental.pallas import tpu_sc as plsc

F32 = jnp.float32
BF16 = jnp.bfloat16
I32 = jnp.int32
HI = lax.Precision.HIGHEST

D_MODEL = 1024
NORM_EPS = 1e-6
LANES = 128
SUBLANES = 8
VMEM_LIMIT = 56 * 1024 * 1024

A_WIDTH = 512
A_HEADS = 4
B_WIDTH = 512
B_HEADS = 4
B_DK = 64
GLA_RANK = 16
GLA_NORMALIZER = 16.0
Z_WIDTH = 3712
C_HEAD = 64
C_HEADS = 16
C_GN_EPS = 64e-5
P_HEADS = 8
P_KEYS = 128
P_TOPK = 16
P_PICKS = P_HEADS * P_TOPK
SC_CORES = 2
SC_SUBCORES = 16
SC_WORKERS = SC_CORES * SC_SUBCORES
SC_LANES = 16
SC_CHUNK = 32
SC_TOK_BLOCK = 8
PROMPT_GROUPS = 2


def _cparams(sem):
    return pltpu.CompilerParams(dimension_semantics=sem, vmem_limit_bytes=VMEM_LIMIT)


def _rms(x, g):
    ms = jnp.mean(x * x, axis=-1, keepdims=True)
    return x * lax.rsqrt(ms + NORM_EPS) * g


def _dot(a, b, precision=None):
    return jnp.dot(a, b, preferred_element_type=F32, precision=precision)


def _dot_nt(a, b, precision=None):
    return lax.dot_general(a, b, (((1,), (1,)), ((), ())), preferred_element_type=F32, precision=precision)


def _dot_tn(a, b, precision=None):
    return lax.dot_general(a, b, (((0,), (0,)), ((), ())), preferred_element_type=F32, precision=precision)


def _tri(n, strict):
    r = lax.broadcasted_iota(I32, (n, n), 0)
    c = lax.broadcasted_iota(I32, (n, n), 1)
    return (c < r) if strict else (c <= r)


def _cumsum_rows(g):
    return _dot(_tri(g.shape[0], False).astype(F32), g, precision=HI)


def _lane_mask(width, lo, hi):
    l = lax.broadcasted_iota(I32, (1, width), 1)
    return (l >= lo) & (l < hi)


def _sigmoid(x):
    return 1.0 / (1.0 + jnp.exp(-x))


def _silu(x):
    return x * _sigmoid(x)


def _norm_proj_kernel(x_ref, g_ref, w_ref, o_ref):
    hn = _rms(x_ref[...], g_ref[...])
    o_ref[...] = _dot(hn.astype(BF16), w_ref[...])


def norm_proj(x, g, w_bf16, tm):
    n, d = x.shape
    f = w_bf16.shape[1]
    return pl.pallas_call(
        _norm_proj_kernel,
        grid=(n // tm,),
        in_specs=[pl.BlockSpec((tm, d), lambda i: (i, 0)),
                  pl.BlockSpec((1, d), lambda i: (0, 0)),
                  pl.BlockSpec((d, f), lambda i: (0, 0))],
        out_specs=pl.BlockSpec((tm, f), lambda i: (i, 0)),
        out_shape=jax.ShapeDtypeStruct((n, f), F32),
        compiler_params=_cparams(("parallel",)),
        name="norm_proj",
    )(x, g.reshape(1, d), w_bf16)


def _out_proj2_kernel(x_ref, a_ref, b_ref, wa_ref, wb_ref, o_ref):
    y = _dot(a_ref[...].astype(BF16), wa_ref[...]) + _dot(b_ref[...].astype(BF16), wb_ref[...])
    o_ref[...] = x_ref[...] + y


def out_proj2(x, a, b, wa, wb, tm):
    n, d = x.shape
    ka, kb = a.shape[1], b.shape[1]
    return pl.pallas_call(
        _out_proj2_kernel,
        grid=(n // tm,),
        in_specs=[pl.BlockSpec((tm, d), lambda i: (i, 0)),
                  pl.BlockSpec((tm, ka), lambda i: (i, 0)),
                  pl.BlockSpec((tm, kb), lambda i: (i, 0)),
                  pl.BlockSpec((ka, d), lambda i: (0, 0)),
                  pl.BlockSpec((kb, d), lambda i: (0, 0))],
        out_specs=pl.BlockSpec((tm, d), lambda i: (i, 0)),
        out_shape=jax.ShapeDtypeStruct((n, d), F32),
        compiler_params=_cparams(("parallel",)),
        name="out_proj2",
    )(x, a, b, wa, wb)


def _out_proj1_kernel(x_ref, a_ref, wa_ref, o_ref):
    o_ref[...] = x_ref[...] + _dot(a_ref[...].astype(BF16), wa_ref[...])


def out_proj1(x, a, wa, tm):
    n, d = x.shape
    ka = a.shape[1]
    return pl.pallas_call(
        _out_proj1_kernel,
        grid=(n // tm,),
        in_specs=[pl.BlockSpec((tm, d), lambda i: (i, 0)),
                  pl.BlockSpec((tm, ka), lambda i: (i, 0)),
                  pl.BlockSpec((ka, d), lambda i: (0, 0))],
        out_specs=pl.BlockSpec((tm, d), lambda i: (i, 0)),
        out_shape=jax.ShapeDtypeStruct((n, d), F32),
        compiler_params=_cparams(("parallel",)),
        name="out_proj1",
    )(x, a, wa)


def _final_norm_kernel(x_ref, g_ref, o_ref):
    o_ref[...] = _rms(x_ref[...], g_ref[...])


def final_norm(x, g, tm):
    n, d = x.shape
    return pl.pallas_call(
        _final_norm_kernel,
        grid=(n // tm,),
        in_specs=[pl.BlockSpec((tm, d), lambda i: (i, 0)), pl.BlockSpec((1, d), lambda i: (0, 0))],
        out_specs=pl.BlockSpec((tm, d), lambda i: (i, 0)),
        out_shape=jax.ShapeDtypeStruct((n, d), F32),
        compiler_params=_cparams(("parallel",)),
        name="final_norm",
    )(x, g.reshape(1, d))


def _intra_chunk(q, k, b, heads):
    c = q.shape[0]
    nb = c // SUBLANES
    row = lax.broadcasted_iota(I32, (SUBLANES, 1), 0)
    qb = [q[SUBLANES * i:SUBLANES * (i + 1)] for i in range(nb)]
    bb = [b[SUBLANES * i:SUBLANES * (i + 1)] for i in range(nb)]
    outs = [[None] * nb for _ in heads]
    for s in range(c):
        rb0 = s // SUBLANES
        ks = k[s:s + 1, :]
        bs = b[s:s + 1, :]
        for rb in range(rb0, nb):
            p = qb[rb] * (ks * jnp.exp(bb[rb] - bs))
            for hi, (mask, v) in enumerate(heads):
                pm = p if mask is None else jnp.where(mask, p, 0.0)
                col = jnp.sum(pm, axis=-1, keepdims=True)
                if rb == rb0:
                    col = jnp.where(row + SUBLANES * rb >= s, col, 0.0)
                term = col * v[s:s + 1, :]
                outs[hi][rb] = term if outs[hi][rb] is None else outs[hi][rb] + term
    return [jnp.concatenate(o, axis=0) for o in outs]


def _gated_chunk(q, k, b, heads, st):
    intra = _intra_chunk(q, k, b, heads)
    qe = q * jnp.exp(b)
    b_last = b[-1:, :]
    kh = k * jnp.exp(b_last - b)
    outs = []
    for (mask, v), oi in zip(heads, intra):
        qm = qe if mask is None else jnp.where(mask, qe, 0.0)
        outs.append(oi + _dot_nt(qm, st, precision=HI))
    upd = _dot_tn(heads[0][1], kh, precision=HI)
    if len(heads) == 2:
        upd = jnp.where(heads[0][0], upd, _dot_tn(heads[1][1], kh, precision=HI))
    st = st * jnp.exp(b_last) + upd
    return outs, st


def _head_rms(o, g):
    ms = jnp.mean(o * o, axis=-1, keepdims=True)
    return o * lax.rsqrt(ms + NORM_EPS) * g


def _hgrn_kernel(zq_ref, zf_ref, zi_ref, zg_ref, lb_ref, ng_ref, s0_ref, o_ref, s_ref, st_scr, *, chunk):
    t = pl.program_id(2)

    @pl.when(t == 0)
    def _():
        st_scr[...] = s0_ref[0, 0]

    lb = lb_ref[0]
    nchunks = zq_ref.shape[1] // chunk

    def body(ci, carry):
        sl = pl.ds(pl.multiple_of(ci * chunk, chunk), chunk)
        f = lb + (1.0 - lb) * _sigmoid(zf_ref[0, sl, :])
        q = _silu(zq_ref[0, sl, :])
        b = _cumsum_rows(jnp.log(f))
        (o,), st = _gated_chunk(q, 1.0 - f, b, [(None, zi_ref[0, sl, :])], st_scr[...])
        st_scr[...] = st
        o_ref[0, sl, :] = _head_rms(o, ng_ref[...]) * _silu(zg_ref[0, sl, :])
        return carry

    lax.fori_loop(0, nchunks, body, 0)

    @pl.when(t == pl.num_programs(2) - 1)
    def _():
        s_ref[0, 0] = st_scr[...]


def hgrn_recurrence(z, lb, norm_g, s0_t, tb, chunk):
    bsz, t, _ = z.shape
    zspec = lambda off: pl.BlockSpec((1, tb, LANES), lambda b, h, i: (b, i, h + off))
    return pl.pallas_call(
        functools.partial(_hgrn_kernel, chunk=chunk),
        grid=(bsz, A_HEADS, t // tb),
        in_specs=[zspec(0), zspec(4), zspec(8), zspec(12),
                  pl.BlockSpec((1, 1, LANES), lambda b, h, i: (h, 0, 0)),
                  pl.BlockSpec((1, LANES), lambda b, h, i: (0, 0)),
                  pl.BlockSpec((1, 1, LANES, LANES), lambda b, h, i: (b, h, 0, 0))],
        out_specs=[pl.BlockSpec((1, tb, LANES), lambda b, h, i: (b, i, h)),
                   pl.BlockSpec((1, 1, LANES, LANES), lambda b, h, i: (b, h, 0, 0))],
        out_shape=[jax.ShapeDtypeStruct((bsz, t, A_WIDTH), F32),
                   jax.ShapeDtypeStruct((bsz, A_HEADS, LANES, LANES), F32)],
        scratch_shapes=[pltpu.VMEM((LANES, LANES), F32)],
        compiler_params=_cparams(("parallel", "parallel", "arbitrary")),
        name="hgrn_recurrence",
    )(z, z, z, z, lb.reshape(A_HEADS, 1, LANES), norm_g.reshape(1, LANES), s0_t)


def _gla_kernel(zq_ref, zk_ref, zv_ref, zg_ref, zlr_ref, w2_ref, gb_ref, ng_ref, s0_ref, o_ref, s_ref, st_scr,
                *, chunk):
    t = pl.program_id(2)

    @pl.when(t == 0)
    def _():
        st_scr[...] = s0_ref[0, 0]

    nchunks = zq_ref.shape[1] // chunk
    m0 = _lane_mask(LANES, 0, B_DK)
    m1 = _lane_mask(LANES, B_DK, LANES)

    def body(ci, carry):
        sl = pl.ds(pl.multiple_of(ci * chunk, chunk), chunk)
        pre = _dot(zlr_ref[0, sl, :], w2_ref[...], precision=HI) + gb_ref[...]
        log_g = (jnp.minimum(pre, 0.0) - jnp.log(1.0 + jnp.exp(-jnp.abs(pre)))) * (1.0 / GLA_NORMALIZER)
        q = zq_ref[0, sl, :] * (B_DK ** -0.5)
        b = _cumsum_rows(log_g)
        v = zv_ref[0, sl, :]
        (o0, o1), st = _gated_chunk(q, zk_ref[0, sl, :], b,
                                    [(m0, v[:, :LANES]), (m1, v[:, LANES:])], st_scr[...])
        st_scr[...] = st
        gate = _silu(zg_ref[0, sl, :])
        o_ref[0, sl, 0:LANES] = _head_rms(o0, ng_ref[...]) * gate[:, :LANES]
        o_ref[0, sl, LANES:2 * LANES] = _head_rms(o1, ng_ref[...]) * gate[:, LANES:]
        return carry

    lax.fori_loop(0, nchunks, body, 0)

    @pl.when(t == pl.num_programs(2) - 1)
    def _():
        s_ref[0, 0] = st_scr[...]


def gla_recurrence(z, w2pad, gate_b, norm_g, s0_t, tb, chunk):
    bsz, t, _ = z.shape
    npairs = B_HEADS // 2
    return pl.pallas_call(
        functools.partial(_gla_kernel, chunk=chunk),
        grid=(bsz, npairs, t // tb),
        in_specs=[pl.BlockSpec((1, tb, LANES), lambda b, p, i: (b, i, 16 + p)),
                  pl.BlockSpec((1, tb, LANES), lambda b, p, i: (b, i, 18 + p)),
                  pl.BlockSpec((1, tb, 2 * LANES), lambda b, p, i: (b, i, 10 + p)),
                  pl.BlockSpec((1, tb, 2 * LANES), lambda b, p, i: (b, i, 12 + p)),
                  pl.BlockSpec((1, tb, LANES), lambda b, p, i: (b, i, 28)),
                  pl.BlockSpec((LANES, LANES), lambda b, p, i: (0, p)),
                  pl.BlockSpec((1, LANES), lambda b, p, i: (0, p)),
                  pl.BlockSpec((1, LANES), lambda b, p, i: (0, 0)),
                  pl.BlockSpec((1, 1, LANES, LANES), lambda b, p, i: (b, p, 0, 0))],
        out_specs=[pl.BlockSpec((1, tb, 2 * LANES), lambda b, p, i: (b, i, p)),
                   pl.BlockSpec((1, 1, LANES, LANES), lambda b, p, i: (b, p, 0, 0))],
        out_shape=[jax.ShapeDtypeStruct((bsz, t, B_WIDTH), F32),
                   jax.ShapeDtypeStruct((bsz, npairs, LANES, LANES), F32)],
        scratch_shapes=[pltpu.VMEM((LANES, LANES), F32)],
        compiler_params=_cparams(("parallel", "parallel", "arbitrary")),
        name="gla_recurrence",
    )(z, z, z, z, z, w2pad, gate_b.reshape(1, 2 * LANES), norm_g.reshape(1, LANES), s0_t)


def _rwkv_proj_kernel(x_ref, xp_ref, xl_ref, g1_ref, mu_ref, wr_ref, wk_ref, wv_ref, ww1_ref, ww2_ref, w0_ref,
                      aw1_ref, aw2_ref, a0_ref, gw1_ref, gw2_ref, kk_ref, ka_ref,
                      r_out, k_out, v_out, lw_out, a_out, kk_out, g_out, hl_out, *, tiles_per_seq):
    i = pl.program_id(0)
    g1 = g1_ref[...]
    hn = _rms(x_ref[...], g1)
    tm = hn.shape[0]
    prev = _rms(xp_ref[...], g1)[SUBLANES - 1:SUBLANES, :]
    prev = jnp.where(i % tiles_per_seq == 0, xl_ref[0], prev)
    row = lax.broadcasted_iota(I32, (tm, 1), 0)
    xprev = jnp.where(row == 0, prev, pltpu.roll(hn, 1, axis=0))
    dx = xprev - hn

    def mix(j):
        return hn + dx * mu_ref[j:j + 1, :]

    r = _dot(mix(0).astype(BF16), wr_ref[...])
    k = _dot(mix(1).astype(BF16), wk_ref[...])
    v = _dot(mix(2).astype(BF16), wv_ref[...])
    wl = _dot(jnp.tanh(_dot(mix(3), ww1_ref[...], precision=HI)), ww2_ref[...], precision=HI)
    z = w0_ref[...] + wl
    wpre = -(jnp.maximum(-z, 0.0) + jnp.log(1.0 + jnp.exp(-jnp.abs(z)))) - 0.5
    al = _dot(_dot(mix(4), aw1_ref[...], precision=HI), aw2_ref[...], precision=HI)
    a = _sigmoid(a0_ref[...] + al)
    gg = _dot(_sigmoid(_dot(mix(5).astype(BF16), gw1_ref[...])).astype(BF16), gw2_ref[...])
    r_out[...] = r
    k_out[...] = k * (1.0 + (a - 1.0) * ka_ref[...])
    v_out[...] = v
    lw_out[...] = -jnp.exp(wpre)
    a_out[...] = a
    kk_out[...] = k * kk_ref[...]
    g_out[...] = gg
    hl_out[0] = hn[tm - 1:tm, :]


def rwkv_proj(x, x_last, seq_len, g1, mu, wr, wk, wv, ww1, ww2, w0, aw1, aw2, a0, gw1, gw2, k_k, k_a, tm):
    n, d = x.shape
    tiles_per_seq = seq_len // tm
    row = lambda a: a.reshape(1, d)
    full = lambda a: pl.BlockSpec(a.shape, lambda i: (0,) * a.ndim)
    tile = pl.BlockSpec((tm, d), lambda i: (i, 0))
    blocks8 = tm // SUBLANES
    args = (x, x, x_last.reshape(-1, 1, d), row(g1), mu, wr, wk, wv, ww1, ww2, row(w0), aw1, aw2, row(a0),
            gw1, gw2, row(k_k), row(k_a))
    in_specs = [tile,
                pl.BlockSpec((SUBLANES, d), lambda i: (jnp.maximum(i * blocks8 - 1, 0), 0)),
                pl.BlockSpec((1, 1, d), lambda i: (i // tiles_per_seq, 0, 0))]
    in_specs += [full(a) for a in args[3:]]
    outs = pl.pallas_call(
        functools.partial(_rwkv_proj_kernel, tiles_per_seq=tiles_per_seq),
        grid=(n // tm,),
        in_specs=in_specs,
        out_specs=[tile] * 7 + [pl.BlockSpec((1, 1, d), lambda i: (i, 0, 0))],
        out_shape=[jax.ShapeDtypeStruct((n, d), F32)] * 7 + [jax.ShapeDtypeStruct((n // tm, 1, d), F32)],
        compiler_params=_cparams(("parallel",)),
        name="rwkv_proj",
    )(*args)
    return outs


_NN = ((1,), (0,))
_NT = ((1,), (1,))
_TN = ((0,), (0,))
RWKV_AB_PASSES = 1
RWKV_INV_PASSES = 1
RWKV_APPLY_PASSES = 1
RWKV_STATE_PASSES = 3
RWKV_SEQS_PER_STEP = 4


def _split_bf16(a):
    hi = a.astype(BF16)
    return hi, (a - hi.astype(F32)).astype(BF16)


def _mm(a, b, dims, passes):
    if passes == 6:
        return lax.dot_general(a, b, (dims, ((), ())), preferred_element_type=F32, precision=HI)
    dg = lambda x, y: lax.dot_general(x, y, (dims, ((), ())), preferred_element_type=F32)
    ah, al = _split_bf16(a)
    bh, bl = _split_bf16(b)
    if passes == 1:
        return dg(ah, bh)
    return dg(ah, bh) + (dg(al, bh) + dg(ah, bl))


def _cumsum_rows3(g):
    tri = _tri(g.shape[0], False).astype(BF16)
    h1 = g.astype(BF16)
    r1 = g - h1.astype(F32)
    h2 = r1.astype(BF16)
    h3 = (r1 - h2.astype(F32)).astype(BF16)
    return _dot(tri, h1) + (_dot(tri, h2) + _dot(tri, h3))


def _pair_sum(x, m0):
    s0 = jnp.sum(jnp.where(m0, x, 0.0), axis=-1, keepdims=True)
    s1 = jnp.sum(jnp.where(m0, 0.0, x), axis=-1, keepdims=True)
    return jnp.where(m0, s0, s1)


def _rwkv_kernel(r_ref, k_ref, v_ref, lw_ref, a_ref, kk_ref, g_ref, rk_ref, lng_ref, lnb_ref, s0_ref,
                 o_ref, s_ref, mt_scr, *, chunk):
    t = pl.program_id(2)

    @pl.when(t == 0)
    def _():
        mt_scr[...] = s0_ref[:, 0]

    nrows = r_ref.shape[0]
    nchunks = r_ref.shape[1] // chunk
    c2 = 2 * chunk
    m0 = _lane_mask(LANES, 0, C_HEAD)
    rowi = lax.broadcasted_iota(I32, (LANES, LANES), 0)
    coli = lax.broadcasted_iota(I32, (LANES, LANES), 1)
    blockdiag = (rowi < C_HEAD) == (coli < C_HEAD)
    ti = lax.broadcasted_iota(I32, (c2, c2), 0)
    si = lax.broadcasted_iota(I32, (c2, c2), 1)
    same_head = (ti < chunk) == (si < chunk)
    tm_ = jnp.where(ti < chunk, ti, ti - chunk)
    sm_ = jnp.where(si < chunk, si, si - chunk)
    strict = same_head & (sm_ < tm_)
    incl = same_head & (sm_ <= tm_)

    def stack_heads(x):
        return jnp.concatenate([jnp.where(m0, x, 0.0), jnp.where(m0, 0.0, x)], axis=0)

    def twice(x):
        return jnp.concatenate([x, x], axis=0)

    def unstack(x2):
        return jnp.where(m0, x2[:chunk], x2[chunk:])

    eye = (ti == si).astype(F32)
    seqs = range(nrows)

    def body(ci, carry):
        sl = pl.ds(pl.multiple_of(ci * chunk, chunk), chunk)
        r = [r_ref[i, sl, :] for i in seqs]
        k = [k_ref[i, sl, :] for i in seqs]
        v = [v_ref[i, sl, :] for i in seqs]
        lw = [lw_ref[i, sl, :] for i in seqs]
        kkr = [kk_ref[i, sl, :] for i in seqs]
        kk = [x * lax.rsqrt(_pair_sum(x * x, m0) + 1e-12) for x in kkr]
        al = [a_ref[i, sl, :] * kk[i] for i in seqs]
        gam = [_cumsum_rows3(x) for x in lw]
        e_neg = [jnp.exp(-x) for x in gam]
        xr = [jnp.concatenate([stack_heads(kk[i] * jnp.exp(gam[i] - lw[i])),
                               stack_heads(r[i] * jnp.exp(gam[i]))], axis=0) for i in seqs]
        alk = [jnp.concatenate([twice(al[i] * e_neg[i]), twice(k[i] * e_neg[i])], axis=0) for i in seqs]
        mt = [mt_scr[i] for i in seqs]
        ab = [_mm(xr[i], alk[i], _NT, RWKV_AB_PASSES) for i in seqs]
        xm = [_mm(xr[i], mt[i], _NT, RWKV_STATE_PASSES) for i in seqs]
        a_al = [jnp.where(strict, x[:c2, :c2], 0.0) for x in ab]
        a_k = [jnp.where(strict, x[:c2, c2:], 0.0) for x in ab]
        b_alk = [jnp.concatenate([jnp.where(incl, x[c2:, c2:], 0.0), jnp.where(incl, -x[c2:, :c2], 0.0)], axis=1)
                 for x in ab]
        v2 = [twice(x) for x in v]
        rhs = [xm[i][:c2] + _mm(a_k[i], v2[i], _NN, RWKV_APPLY_PASSES) for i in seqs]
        p = [-x for x in a_al]
        tinv = [eye + x for x in p]
        span = 2
        while span < chunk:
            p = [_mm(x, x, _NN, RWKV_INV_PASSES) for x in p]
            tinv = [tinv[i] + _mm(tinv[i], p[i], _NN, RWKV_INV_PASSES) for i in seqs]
            span *= 2
        u = [unstack(_mm(tinv[i], rhs[i], _NN, RWKV_APPLY_PASSES)) for i in seqs]
        o = [unstack(xm[i][c2:] + _mm(b_alk[i], jnp.concatenate([v2[i], twice(u[i])], axis=0), _NN,
                                      RWKV_APPLY_PASSES)) for i in seqs]
        g_last = [x[-1:, :] for x in gam]
        e_end = [jnp.exp(g_last[i] - gam[i]) for i in seqs]
        upd = [_mm(jnp.concatenate([v[i], u[i]], axis=0),
                   jnp.concatenate([k[i] * e_end[i], -(al[i] * e_end[i])], axis=0), _TN, RWKV_STATE_PASSES)
               for i in seqs]
        for i in seqs:
            mt_scr[i] = mt[i] * jnp.exp(g_last[i]) + jnp.where(blockdiag, upd[i], 0.0)
        for i in seqs:
            mean = _pair_sum(o[i], m0) * (1.0 / C_HEAD)
            cen = o[i] - mean
            var = _pair_sum(cen * cen, m0) * (1.0 / C_HEAD)
            on = cen * lax.rsqrt(var + C_GN_EPS) * lng_ref[...] + lnb_ref[...]
            bonus = _pair_sum(r[i] * k[i] * rk_ref[...], m0) * v[i]
            o_ref[i, sl, :] = (on + bonus) * g_ref[i, sl, :]
        return carry

    lax.fori_loop(0, nchunks, body, 0)

    @pl.when(t == pl.num_programs(2) - 1)
    def _():
        s_ref[:, 0] = mt_scr[...]


def rwkv_recurrence(r, k, v, lw, a, kk, g, r_k, ln_g, ln_b, s0_bd, tb, chunk, nb):
    bsz, t, d = r.shape
    npairs = C_HEADS // 2
    seq = pl.BlockSpec((nb, tb, LANES), lambda b, p, i: (b, i, p))
    vec = pl.BlockSpec((1, LANES), lambda b, p, i: (0, p))
    st = pl.BlockSpec((nb, 1, LANES, LANES), lambda b, p, i: (b, p, 0, 0))
    return pl.pallas_call(
        functools.partial(_rwkv_kernel, chunk=chunk),
        grid=(bsz // nb, npairs, t // tb),
        in_specs=[seq] * 7 + [vec, vec, vec, st],
        out_specs=[seq, st],
        out_shape=[jax.ShapeDtypeStruct((bsz, t, d), F32),
                   jax.ShapeDtypeStruct((bsz, npairs, LANES, LANES), F32)],
        scratch_shapes=[pltpu.VMEM((nb, LANES, LANES), F32)],
        compiler_params=_cparams(("parallel", "parallel", "arbitrary")),
        name="rwkv_recurrence",
    )(r, k, v, lw, a, kk, g, r_k.reshape(1, d), ln_g.reshape(1, d), ln_b.reshape(1, d), s0_bd)


NEG_INF = float("-inf")


def _top16_rows(s):
    n = s.shape[0]
    key = lax.broadcasted_iota(I32, s.shape, 0)
    vals, idxs = [], []
    for _ in range(P_TOPK):
        m = jnp.max(s, axis=0, keepdims=True)
        am = jnp.min(jnp.where(s == m, key, n), axis=0, keepdims=True)
        vals.append(m)
        idxs.append(am)
        s = jnp.where(key == am, NEG_INF, s)
    return vals, idxs


def _top16_pairs(v0, i0, v1, i1):
    a0 = jnp.concatenate(v0[0:8], axis=0)
    a1 = jnp.concatenate(v0[8:16], axis=0)
    b0 = jnp.concatenate(v1[0:8], axis=0)
    b1 = jnp.concatenate(v1[8:16], axis=0)
    ia0 = jnp.concatenate(i0[0:8], axis=0) * P_KEYS
    ia1 = jnp.concatenate(i0[8:16], axis=0) * P_KEYS
    ib0 = jnp.concatenate(i1[0:8], axis=0)
    ib1 = jnp.concatenate(i1[8:16], axis=0)
    row = lax.broadcasted_iota(I32, (SUBLANES, 1), 0)
    slabs = []

    def add(val, eid, keep):
        slabs.append((val if keep is None else jnp.where(keep, val, NEG_INF), eid))

    add(v0[0] + b0, ia0[0:1] + ib0, None)
    add(v0[0] + b1, ia0[0:1] + ib1, None)
    add(v0[1] + b0, ia0[1:2] + ib0, None)
    add(v0[2] + b0, ia0[2:3] + ib0, row < 5)
    add(v0[3] + b0, ia0[3:4] + ib0, row < 4)
    add(a0 + v1[0], ia0 + ib0[0:1], row >= 4)
    add(a1 + v1[0], ia1 + ib0[0:1], None)
    add(a0 + v1[1], ia0 + ib0[1:2], row >= 4)
    add(a0 + v1[2], ia0 + ib0[2:3], row == 4)

    big = P_KEYS * P_KEYS
    out_v, out_e = [], []
    for _ in range(P_TOPK):
        m = slabs[0][0]
        for val, _e in slabs[1:]:
            m = jnp.maximum(m, val)
        m = jnp.max(m, axis=0, keepdims=True)
        e = None
        for val, eid in slabs:
            c = jnp.where(val == m, eid, big)
            e = c if e is None else jnp.minimum(e, c)
        e = jnp.min(e, axis=0, keepdims=True)
        out_v.append(m)
        out_e.append(e)
        slabs = [(jnp.where(eid == e, NEG_INF, val), eid) for val, eid in slabs]
    return out_v, out_e


def _peer_select_kernel(x_ref, g_ref, wq_ref, keys_ref, xn_out, eid_out, gate_out):
    hn = _rms(x_ref[...], g_ref[...])
    xn_out[...] = hn
    q = _dot(hn.astype(BF16), wq_ref[...])
    tm = q.shape[0]
    for lt in range(tm // LANES):
        rows = slice(lt * LANES, (lt + 1) * LANES)
        e_rows, g_rows = [], []
        for h in range(P_HEADS):
            tops = []
            for p in range(2):
                hp = 2 * h + p
                s = _dot_nt(keys_ref[hp], q[rows, hp * LANES:(hp + 1) * LANES], precision=HI)
                tops.append(_top16_rows(s))
            cs, ce = _top16_pairs(tops[0][0], tops[0][1], tops[1][0], tops[1][1])
            ex = [jnp.exp(c - cs[0]) for c in cs]
            tot = ex[0]
            for e in ex[1:]:
                tot = tot + e
            inv = 1.0 / tot
            e_rows += ce
            g_rows += [e * inv for e in ex]
        eid_out[rows, :] = jnp.concatenate(e_rows, axis=0).T
        gate_out[rows, :] = jnp.concatenate(g_rows, axis=0).T


def peer_select(x, g, wq_bf16, keys, tm):
    n, d = x.shape
    return pl.pallas_call(
        _peer_select_kernel,
        grid=(n // tm,),
        in_specs=[pl.BlockSpec((tm, d), lambda i: (i, 0)),
                  pl.BlockSpec((1, d), lambda i: (0, 0)),
                  pl.BlockSpec(wq_bf16.shape, lambda i: (0, 0)),
                  pl.BlockSpec(keys.shape, lambda i: (0, 0, 0))],
        out_specs=[pl.BlockSpec((tm, d), lambda i: (i, 0)),
                   pl.BlockSpec((tm, P_PICKS), lambda i: (i, 0)),
                   pl.BlockSpec((tm, P_PICKS), lambda i: (i, 0))],
        out_shape=[jax.ShapeDtypeStruct((n, d), F32),
                   jax.ShapeDtypeStruct((n, P_PICKS), I32),
                   jax.ShapeDtypeStruct((n, P_PICKS), F32)],
        compiler_params=_cparams(("parallel",)),
        name="peer_select",
    )(x, g.reshape(1, d), wq_bf16, keys)


def _peer_act_kernel(h_ref, g_ref, o_ref):
    h = h_ref[...]
    o_ref[...] = 0.5 * h * (1.0 + lax.erf(h * (2.0 ** -0.5))) * g_ref[...]


def peer_act(hid, gate, tm):
    n, p = hid.shape
    spec = pl.BlockSpec((tm, p), lambda i: (i, 0))
    return pl.pallas_call(
        _peer_act_kernel, grid=(n // tm,), in_specs=[spec, spec], out_specs=spec,
        out_shape=jax.ShapeDtypeStruct((n, p), F32),
        compiler_params=_cparams(("parallel",)), name="peer_act",
    )(hid, gate)


_CHUNKS_PER_TOK = P_PICKS // SC_CHUNK
_VREGS_PER_ROW = D_MODEL // SC_LANES


def _sc_worker():
    return lax.axis_index("s") * SC_CORES + lax.axis_index("c")


def _sc_hid_body(tab_hbm, idx_hbm, x_hbm, hid_hbm, idx_v, x_v, hid_v, part_v, rows_a, rows_b, sem_a, sem_b,
                 *, tok_per_w):
    wid = _sc_worker()
    nblk = tok_per_w // SC_TOK_BLOCK
    nch = SC_TOK_BLOCK * _CHUNKS_PER_TOK
    lane = lax.iota(I32, SC_LANES)
    half_v = _VREGS_PER_ROW // 2

    def dots(rows, x_row, res_ref, col0):
        for xp in range(2):
            base = xp * half_v * SC_LANES
            xh = [x_v[x_row, pl.ds(base + j * SC_LANES, SC_LANES)] for j in range(half_v)]

            def partial_dot(r):
                acc = rows[r, pl.ds(base, SC_LANES)] * xh[0]
                for j in range(1, half_v):
                    acc = acc + rows[r, pl.ds(base + j * SC_LANES, SC_LANES)] * xh[j]
                return acc

            if xp == 0:
                def first(r, c):
                    part_v[r, :] = partial_dot(r)
                    return c
                lax.fori_loop(0, SC_CHUNK, first, 0)
            else:
                for grp in range(SC_CHUNK // SC_LANES):
                    def second(rr, res):
                        r = grp * SC_LANES + rr
                        return jnp.where(lane == rr, jnp.sum(partial_dot(r) + part_v[r, :]), res)
                    res = lax.fori_loop(0, SC_LANES, second, jnp.zeros((SC_LANES,), F32))
                    res_ref[x_row, pl.ds(col0 + grp * SC_LANES, SC_LANES)] = res

    def block(bi, carry):
        tok0 = wid * tok_per_w + bi * SC_TOK_BLOCK
        pltpu.sync_copy(idx_hbm.at[pl.ds(tok0 * _CHUNKS_PER_TOK, nch)], idx_v)
        pltpu.sync_copy(x_hbm.at[pl.ds(tok0, SC_TOK_BLOCK)], x_v)
        pltpu.async_copy(tab_hbm.at[idx_v.at[0]], rows_a, sem_a)

        def pair(j, c):
            ca = 2 * j
            pltpu.async_copy(tab_hbm.at[idx_v.at[ca + 1]], rows_b, sem_b)
            pltpu.make_async_copy(tab_hbm.at[idx_v.at[0]], rows_a, sem_a).wait()
            dots(rows_a, ca // _CHUNKS_PER_TOK, hid_v, (ca % _CHUNKS_PER_TOK) * SC_CHUNK)

            @pl.when(j < nch // 2 - 1)
            def _():
                pltpu.async_copy(tab_hbm.at[idx_v.at[ca + 2]], rows_a, sem_a)
            pltpu.make_async_copy(tab_hbm.at[idx_v.at[0]], rows_b, sem_b).wait()
            cb = ca + 1
            dots(rows_b, cb // _CHUNKS_PER_TOK, hid_v, (cb % _CHUNKS_PER_TOK) * SC_CHUNK)
            return c
        lax.fori_loop(0, nch // 2, pair, 0)
        pltpu.sync_copy(hid_v, hid_hbm.at[pl.ds(tok0, SC_TOK_BLOCK)])
        return carry

    lax.fori_loop(0, nblk, block, 0)


def sc_expert_hidden(table, eidx, xn):
    n = xn.shape[0]
    tok_per_w = n // SC_WORKERS
    mesh = plsc.VectorSubcoreMesh(core_axis_name="c", subcore_axis_name="s")
    nch = SC_TOK_BLOCK * _CHUNKS_PER_TOK
    k = pl.kernel(
        functools.partial(_sc_hid_body, tok_per_w=tok_per_w), mesh=mesh,
        out_type=jax.ShapeDtypeStruct((n, P_PICKS), F32),
        scratch_types=[pltpu.VMEM((nch, SC_CHUNK), I32),
                       pltpu.VMEM((SC_TOK_BLOCK, D_MODEL), F32),
                       pltpu.VMEM((SC_TOK_BLOCK, P_PICKS), F32),
                       pltpu.VMEM((SC_CHUNK, SC_LANES), F32),
                       pltpu.VMEM((SC_CHUNK, D_MODEL), F32),
                       pltpu.VMEM((SC_CHUNK, D_MODEL), F32),
                       pltpu.SemaphoreType.DMA, pltpu.SemaphoreType.DMA],
        compiler_params=pltpu.CompilerParams(needs_layout_passes=False),
        name="sc_expert_hidden",
    )
    return k(table, eidx.reshape(n * _CHUNKS_PER_TOK, SC_CHUNK), xn)


def _sc_out_body(tab_hbm, idx_hbm, act_hbm, x_hbm, y_hbm, idx_v, act_v, y_v, rows_a, rows_b, sem_a, sem_b,
                 *, tok_per_w):
    wid = _sc_worker()
    nblk = tok_per_w // SC_TOK_BLOCK
    nch = SC_TOK_BLOCK * _CHUNKS_PER_TOK
    half_v = _VREGS_PER_ROW // 2

    def accum(rows, tok, col0):
        for hv in range(2):
            base = hv * half_v * SC_LANES
            acc0 = tuple(y_v[tok, pl.ds(base + j * SC_LANES, SC_LANES)] for j in range(half_v))

            def one(r, acc):
                aidx = jnp.full((SC_LANES,), col0, I32) + r
                w = plsc.load_gather(act_v, [jnp.full((SC_LANES,), tok, I32), aidx])
                return tuple(acc[j] + w * rows[r, pl.ds(base + j * SC_LANES, SC_LANES)] for j in range(half_v))
            acc = lax.fori_loop(0, SC_CHUNK, one, acc0)
            for j in range(half_v):
                y_v[tok, pl.ds(base + j * SC_LANES, SC_LANES)] = acc[j]

    def block(bi, carry):
        tok0 = wid * tok_per_w + bi * SC_TOK_BLOCK
        pltpu.sync_copy(idx_hbm.at[pl.ds(tok0 * _CHUNKS_PER_TOK, nch)], idx_v)
        pltpu.sync_copy(act_hbm.at[pl.ds(tok0, SC_TOK_BLOCK)], act_v)
        pltpu.sync_copy(x_hbm.at[pl.ds(tok0, SC_TOK_BLOCK)], y_v)
        pltpu.async_copy(tab_hbm.at[idx_v.at[0]], rows_a, sem_a)

        def pair(j, c):
            ca = 2 * j
            pltpu.async_copy(tab_hbm.at[idx_v.at[ca + 1]], rows_b, sem_b)
            pltpu.make_async_copy(tab_hbm.at[idx_v.at[0]], rows_a, sem_a).wait()
            accum(rows_a, ca // _CHUNKS_PER_TOK, (ca % _CHUNKS_PER_TOK) * SC_CHUNK)

            @pl.when(j < nch // 2 - 1)
            def _():
                pltpu.async_copy(tab_hbm.at[idx_v.at[ca + 2]], rows_a, sem_a)
            pltpu.make_async_copy(tab_hbm.at[idx_v.at[0]], rows_b, sem_b).wait()
            cb = ca + 1
            accum(rows_b, cb // _CHUNKS_PER_TOK, (cb % _CHUNKS_PER_TOK) * SC_CHUNK)
            return c
        lax.fori_loop(0, nch // 2, pair, 0)
        pltpu.sync_copy(y_v, y_hbm.at[pl.ds(tok0, SC_TOK_BLOCK)])
        return carry

    lax.fori_loop(0, nblk, block, 0)


def sc_expert_output(table, eidx, act, x):
    n = x.shape[0]
    tok_per_w = n // SC_WORKERS
    mesh = plsc.VectorSubcoreMesh(core_axis_name="c", subcore_axis_name="s")
    nch = SC_TOK_BLOCK * _CHUNKS_PER_TOK
    k = pl.kernel(
        functools.partial(_sc_out_body, tok_per_w=tok_per_w), mesh=mesh,
        out_type=jax.ShapeDtypeStruct((n, D_MODEL), F32),
        scratch_types=[pltpu.VMEM((nch, SC_CHUNK), I32),
                       pltpu.VMEM((SC_TOK_BLOCK, P_PICKS), F32),
                       pltpu.VMEM((SC_TOK_BLOCK, D_MODEL), F32),
                       pltpu.VMEM((SC_CHUNK, D_MODEL), F32),
                       pltpu.VMEM((SC_CHUNK, D_MODEL), F32),
                       pltpu.SemaphoreType.DMA, pltpu.SemaphoreType.DMA],
        compiler_params=pltpu.CompilerParams(needs_layout_passes=False),
        name="sc_expert_output",
    )
    return k(table, eidx.reshape(n * _CHUNKS_PER_TOK, SC_CHUNK), act, x)


def peer_ffn(x, g, wq_bf16, keys, u_tab, v_tab, tm):
    xn, eidx, gate = peer_select(x, g, wq_bf16, keys, tm)
    hid = sc_expert_hidden(u_tab, eidx, xn)
    act = peer_act(hid, gate, tm)
    return sc_expert_output(v_tab, eidx, act, x)


def _prep_w_in(w_in):
    main = jnp.concatenate([w_in[:, :3072], w_in[:, 3088:3600]], axis=1)
    lr = jnp.pad(w_in[:, 3072:3088], ((0, 0), (0, LANES - GLA_RANK)))
    return jnp.concatenate([main, lr], axis=1).astype(BF16)


def _trunk(x, st_h, st_g, st_r, st_s, w, seq_len, tm, tmp, tb, chunk):
    bsz = x.shape[0]
    n = bsz * seq_len
    x2 = x.reshape(n, D_MODEL)

    z = norm_proj(x2, w["norm1_g"][0], w["w_in"], tm).reshape(bsz, seq_len, Z_WIDTH)
    s0_h = jnp.swapaxes(st_h, -1, -2)
    s0_g = jnp.swapaxes(st_g.reshape(bsz, 2, 2 * B_DK, LANES), -1, -2)
    o_a, sh_t = hgrn_recurrence(z, w["lb0"], w["hgrn_norm_g"], s0_h, tb, chunk)
    o_b, sg_t = gla_recurrence(z, w["gla_w2"], w["gla_b"], w["gla_norm_g"], s0_g, tb, chunk)
    new_h = jnp.swapaxes(sh_t, -1, -2)
    new_g = jnp.swapaxes(sg_t, -1, -2).reshape(bsz, B_HEADS, B_DK, LANES)
    x2 = out_proj2(x2, o_a.reshape(n, A_WIDTH), o_b.reshape(n, B_WIDTH), w["w_out_a"], w["w_out_b"], tm)
    x2 = peer_ffn(x2, w["norm2_g"][0], w["peer_wq"][0], w["peer_keys"][0], w["peer_u"][0], w["peer_v"][0], tmp)

    r, k, v, lw, a, kk, g, hl = rwkv_proj(
        x2, st_s, seq_len, w["norm1_g"][1], w["mu"], w["wr"], w["wk"], w["wv"], w["w_w1"], w["w_w2"], w["w0"],
        w["a_w1"], w["a_w2"], w["a0"], w["g_w1"], w["g_w2"], w["k_k"], w["k_a"], tm)
    new_s = hl.reshape(bsz, seq_len // tm, D_MODEL)[:, -1]
    pr = st_r.reshape(bsz, C_HEADS // 2, 2, C_HEAD, C_HEAD)
    zero = jnp.zeros_like(pr[:, :, 0])
    s0_r = jnp.concatenate([jnp.concatenate([pr[:, :, 0], zero], axis=-1),
                            jnp.concatenate([zero, pr[:, :, 1]], axis=-1)], axis=-2)
    sh3 = lambda t: t.reshape(bsz, seq_len, D_MODEL)
    o_c, sr_bd = rwkv_recurrence(sh3(r), sh3(k), sh3(v), sh3(lw), sh3(a), sh3(kk), sh3(g),
                                 w["r_k"], w["ln_g"], w["ln_b"], s0_r, tb, chunk, RWKV_SEQS_PER_STEP)
    new_r = jnp.stack([sr_bd[:, :, :C_HEAD, :C_HEAD], sr_bd[:, :, C_HEAD:, C_HEAD:]], axis=2)
    new_r = new_r.reshape(bsz, C_HEADS, C_HEAD, C_HEAD)
    x2 = out_proj1(x2, o_c.reshape(n, D_MODEL), w["w_out_c"], tm)
    x2 = peer_ffn(x2, w["norm2_g"][1], w["peer_wq"][1], w["peer_keys"][1], w["peer_u"][1], w["peer_v"][1], tmp)

    y = final_norm(x2, w["final_g"], tm).reshape(bsz, seq_len, D_MODEL)
    return y, new_h[None], new_g[None], new_r[None], new_s[None]


def kernel(x_prompt, x_sample, state_hgrn, state_gla, state_rwkv, state_shift, w_in_ab, hgrn_lower_bounds, hgrn_norm_g, gla_gate_w2, gla_gate_b, gla_norm_g, w_out_ab, rwkv_mu, rwkv_w_rkv, rwkv_w_w1, rwkv_w_w2, rwkv_w0, rwkv_a_w1, rwkv_a_w2, rwkv_a0, rwkv_g_w1, rwkv_g_w2, rwkv_k_k, rwkv_k_a, rwkv_r_k, rwkv_ln_g, rwkv_ln_b, w_out_c, norm1_g, norm2_g, final_g, peer_w_q, peer_sub_keys, peer_u, peer_v):
    lbs = jnp.cumsum(jax.nn.softmax(hgrn_lower_bounds.astype(F32), axis=0), axis=0)
    w = dict(
        norm1_g=norm1_g, norm2_g=norm2_g, final_g=final_g,
        w_in=_prep_w_in(w_in_ab[0]), lb0=lbs[0], hgrn_norm_g=hgrn_norm_g[0],
        gla_w2=jnp.pad(gla_gate_w2[0], ((0, LANES - GLA_RANK), (0, 0))), gla_b=gla_gate_b[0],
        gla_norm_g=gla_norm_g[0],
        w_out_a=w_out_ab[0, :A_WIDTH].astype(BF16), w_out_b=w_out_ab[0, A_WIDTH:].astype(BF16),
        mu=rwkv_mu[0], wr=rwkv_w_rkv[0, 0].astype(BF16), wk=rwkv_w_rkv[0, 1].astype(BF16),
        wv=rwkv_w_rkv[0, 2].astype(BF16), w_w1=rwkv_w_w1[0], w_w2=rwkv_w_w2[0], w0=rwkv_w0[0],
        a_w1=rwkv_a_w1[0], a_w2=rwkv_a_w2[0], a0=rwkv_a0[0],
        g_w1=rwkv_g_w1[0].astype(BF16), g_w2=rwkv_g_w2[0].astype(BF16),
        k_k=rwkv_k_k[0], k_a=rwkv_k_a[0], r_k=rwkv_r_k[0], ln_g=rwkv_ln_g[0], ln_b=rwkv_ln_b[0],
        w_out_c=w_out_c[0].astype(BF16),
        peer_wq=peer_w_q.astype(BF16),
        peer_keys=peer_sub_keys.reshape(peer_sub_keys.shape[0], 2 * P_HEADS, P_KEYS, P_KEYS),
        peer_u=peer_u, peer_v=peer_v,
    )
    bp, tp, _ = x_prompt.shape
    bs, ts, _ = x_sample.shape
    gsz = bp // PROMPT_GROUPS
    zeros = lambda s: jnp.zeros((gsz,) + s.shape[2:], F32)
    groups = [_trunk(x_prompt[i * gsz:(i + 1) * gsz], zeros(state_hgrn), zeros(state_gla), zeros(state_rwkv),
                     zeros(state_shift), w, tp, tm=256, tmp=256, tb=512, chunk=64)
              for i in range(PROMPT_GROUPS)]
    y_p = jnp.concatenate([g[0] for g in groups], axis=0)
    p_h, p_g, p_r, p_s = (jnp.concatenate([g[j] for g in groups], axis=1) for j in range(1, 5))
    y_s, s_h, s_g, s_r, s_s = _trunk(x_sample, state_hgrn[0], state_gla[0], state_rwkv[0], state_shift[0],
                                     w, ts, tm=32, tmp=128, tb=32, chunk=32)
    return (y_p, y_s, p_h, p_g, p_r, p_s, s_h, s_g, s_r, s_s)
```

```python
import functools

import jax
import jax.numpy as jnp
from jax import lax
from jax.experimental import pallas as pl
from jax.experimental.pallas import tpu as pltpu
from jax.experimental.pallas import tpu_sc as plsc

F32 = jnp.float32
BF16 = jnp.bfloat16
I32 = jnp.int32
HI = lax.Precision.HIGHEST

D_MODEL = 1024
NORM_EPS = 1e-6
LANES = 128
SUBLANES = 8
VMEM_LIMIT = 56 * 1024 * 1024

A_WIDTH = 512
A_HEADS = 4
B_WIDTH = 512
B_HEADS = 4
B_DK = 64
GLA_RANK = 16
GLA_NORMALIZER = 16.0
Z_WIDTH = 3712
C_HEAD = 64
C_HEADS = 16
C_GN_EPS = 64e-5
P_HEADS = 8
P_KEYS = 128
P_TOPK = 16
P_PICKS = P_HEADS * P_TOPK
SC_CORES = 2
SC_SUBCORES = 16
SC_WORKERS = SC_CORES * SC_SUBCORES
SC_LANES = 16
SC_CHUNK = 32
SC_TOK_BLOCK = 16
PROMPT_GROUPS = 4


def _cparams(sem):
    return pltpu.CompilerParams(dimension_semantics=sem, vmem_limit_bytes=VMEM_LIMIT)


def _rms(x, g):
    ms = jnp.mean(x * x, axis=-1, keepdims=True)
    return x * lax.rsqrt(ms + NORM_EPS) * g


def _dot(a, b, precision=None):
    return jnp.dot(a, b, preferred_element_type=F32, precision=precision)


def _dot_nt(a, b, precision=None):
    return lax.dot_general(a, b, (((1,), (1,)), ((), ())), preferred_element_type=F32, precision=precision)


def _dot_tn(a, b, precision=None):
    return lax.dot_general(a, b, (((0,), (0,)), ((), ())), preferred_element_type=F32, precision=precision)


def _tri(n, strict):
    r = lax.broadcasted_iota(I32, (n, n), 0)
    c = lax.broadcasted_iota(I32, (n, n), 1)
    return (c < r) if strict else (c <= r)


def _cumsum_rows(g):
    return _dot(_tri(g.shape[0], False).astype(F32), g, precision=HI)


def _lane_mask(width, lo, hi):
    l = lax.broadcasted_iota(I32, (1, width), 1)
    return (l >= lo) & (l < hi)


def _sigmoid(x):
    return 1.0 / (1.0 + jnp.exp(-x))


def _silu(x):
    return x * _sigmoid(x)


def _norm_proj_kernel(x_ref, g_ref, w_ref, o_ref):
    hn = _rms(x_ref[...], g_ref[...])
    o_ref[...] = _dot(hn.astype(BF16), w_ref[...])


def norm_proj(x, g, w_bf16, tm):
    n, d = x.shape
    f = w_bf16.shape[1]
    return pl.pallas_call(
        _norm_proj_kernel,
        grid=(n // tm,),
        in_specs=[pl.BlockSpec((tm, d), lambda i: (i, 0)),
                  pl.BlockSpec((1, d), lambda i: (0, 0)),
                  pl.BlockSpec((d, f), lambda i: (0, 0))],
        out_specs=pl.BlockSpec((tm, f), lambda i: (i, 0)),
        out_shape=jax.ShapeDtypeStruct((n, f), F32),
        compiler_params=_cparams(("parallel",)),
        name="norm_proj",
    )(x, g.reshape(1, d), w_bf16)


def _out_proj2_kernel(x_ref, a_ref, b_ref, wa_ref, wb_ref, o_ref):
    y = _dot(a_ref[...].astype(BF16), wa_ref[...]) + _dot(b_ref[...].astype(BF16), wb_ref[...])
    o_ref[...] = x_ref[...] + y


def out_proj2(x, a, b, wa, wb, tm):
    n, d = x.shape
    ka, kb = a.shape[1], b.shape[1]
    return pl.pallas_call(
        _out_proj2_kernel,
        grid=(n // tm,),
        in_specs=[pl.BlockSpec((tm, d), lambda i: (i, 0)),
                  pl.BlockSpec((tm, ka), lambda i: (i, 0)),
                  pl.BlockSpec((tm, kb), lambda i: (i, 0)),
                  pl.BlockSpec((ka, d), lambda i: (0, 0)),
                  pl.BlockSpec((kb, d), lambda i: (0, 0))],
        out_specs=pl.BlockSpec((tm, d), lambda i: (i, 0)),
        out_shape=jax.ShapeDtypeStruct((n, d), F32),
        compiler_params=_cparams(("parallel",)),
        name="out_proj2",
    )(x, a, b, wa, wb)


def _out_proj1_kernel(x_ref, a_ref, wa_ref, o_ref):
    o_ref[...] = x_ref[...] + _dot(a_ref[...].astype(BF16), wa_ref[...])


def out_proj1(x, a, wa, tm):
    n, d = x.shape
    ka = a.shape[1]
    return pl.pallas_call(
        _out_proj1_kernel,
        grid=(n // tm,),
        in_specs=[pl.BlockSpec((tm, d), lambda i: (i, 0)),
                  pl.BlockSpec((tm, ka), lambda i: (i, 0)),
                  pl.BlockSpec((ka, d), lambda i: (0, 0))],
        out_specs=pl.BlockSpec((tm, d), lambda i: (i, 0)),
        out_shape=jax.ShapeDtypeStruct((n, d), F32),
        compiler_params=_cparams(("parallel",)),
        name="out_proj1",
    )(x, a, wa)


def _final_norm_kernel(x_ref, g_ref, o_ref):
    o_ref[...] = _rms(x_ref[...], g_ref[...])


def final_norm(x, g, tm):
    n, d = x.shape
    return pl.pallas_call(
        _final_norm_kernel,
        grid=(n // tm,),
        in_specs=[pl.BlockSpec((tm, d), lambda i: (i, 0)), pl.BlockSpec((1, d), lambda i: (0, 0))],
        out_specs=pl.BlockSpec((tm, d), lambda i: (i, 0)),
        out_shape=jax.ShapeDtypeStruct((n, d), F32),
        compiler_params=_cparams(("parallel",)),
        name="final_norm",
    )(x, g.reshape(1, d))


def _intra_chunk(q, k, b, heads):
    c = q.shape[0]
    nb = c // SUBLANES
    row = lax.broadcasted_iota(I32, (SUBLANES, 1), 0)
    qb = [q[SUBLANES * i:SUBLANES * (i + 1)] for i in range(nb)]
    bb = [b[SUBLANES * i:SUBLANES * (i + 1)] for i in range(nb)]
    outs = [[None] * nb for _ in heads]
    for s in range(c):
        rb0 = s // SUBLANES
        ks = k[s:s + 1, :]
        bs = b[s:s + 1, :]
        for rb in range(rb0, nb):
            p = qb[rb] * (ks * jnp.exp(bb[rb] - bs))
            for hi, (mask, v) in enumerate(heads):
                pm = p if mask is None else jnp.where(mask, p, 0.0)
                col = jnp.sum(pm, axis=-1, keepdims=True)
                if rb == rb0:
                    col = jnp.where(row + SUBLANES * rb >= s, col, 0.0)
                term = col * v[s:s + 1, :]
                outs[hi][rb] = term if outs[hi][rb] is None else outs[hi][rb] + term
    return [jnp.concatenate(o, axis=0) for o in outs]


def _gated_chunk(q, k, b, heads, st):
    intra = _intra_chunk(q, k, b, heads)
    qe = q * jnp.exp(b)
    b_last = b[-1:, :]
    kh = k * jnp.exp(b_last - b)
    outs = []
    for (mask, v), oi in zip(heads, intra):
        qm = qe if mask is None else jnp.where(mask, qe, 0.0)
        outs.append(oi + _dot_nt(qm, st, precision=HI))
    upd = _dot_tn(heads[0][1], kh, precision=HI)
    if len(heads) == 2:
        upd = jnp.where(heads[0][0], upd, _dot_tn(heads[1][1], kh, precision=HI))
    st = st * jnp.exp(b_last) + upd
    return outs, st


def _head_rms(o, g):
    ms = jnp.mean(o * o, axis=-1, keepdims=True)
    return o * lax.rsqrt(ms + NORM_EPS) * g


def _hgrn_kernel(zq_ref, zf_ref, zi_ref, zg_ref, lb_ref, ng_ref, s0_ref, o_ref, s_ref, st_scr, *, chunk):
    t = pl.program_id(2)

    @pl.when(t == 0)
    def _():
        st_scr[...] = s0_ref[0, 0]

    lb = lb_ref[0]
    nchunks = zq_ref.shape[1] // chunk

    def body(ci, carry):
        sl = pl.ds(pl.multiple_of(ci * chunk, chunk), chunk)
        f = lb + (1.0 - lb) * _sigmoid(zf_ref[0, sl, :])
        q = _silu(zq_ref[0, sl, :])
        b = _cumsum_rows(jnp.log(f))
        (o,), st = _gated_chunk(q, 1.0 - f, b, [(None, zi_ref[0, sl, :])], st_scr[...])
        st_scr[...] = st
        o_ref[0, sl, :] = _head_rms(o, ng_ref[...]) * _silu(zg_ref[0, sl, :])
        return carry

    lax.fori_loop(0, nchunks, body, 0)

    @pl.when(t == pl.num_programs(2) - 1)
    def _():
        s_ref[0, 0] = st_scr[...]


def hgrn_recurrence(z, lb, norm_g, s0_t, tb, chunk):
    bsz, t, _ = z.shape
    zspec = lambda off: pl.BlockSpec((1, tb, LANES), lambda b, h, i: (b, i, h + off))
    return pl.pallas_call(
        functools.partial(_hgrn_kernel, chunk=chunk),
        grid=(bsz, A_HEADS, t // tb),
        in_specs=[zspec(0), zspec(4), zspec(8), zspec(12),
                  pl.BlockSpec((1, 1, LANES), lambda b, h, i: (h, 0, 0)),
                  pl.BlockSpec((1, LANES), lambda b, h, i: (0, 0)),
                  pl.BlockSpec((1, 1, LANES, LANES), lambda b, h, i: (b, h, 0, 0))],
        out_specs=[pl.BlockSpec((1, tb, LANES), lambda b, h, i: (b, i, h)),
                   pl.BlockSpec((1, 1, LANES, LANES), lambda b, h, i: (b, h, 0, 0))],
        out_shape=[jax.ShapeDtypeStruct((bsz, t, A_WIDTH), F32),
                   jax.ShapeDtypeStruct((bsz, A_HEADS, LANES, LANES), F32)],
        scratch_shapes=[pltpu.VMEM((LANES, LANES), F32)],
        compiler_params=_cparams(("parallel", "parallel", "arbitrary")),
        name="hgrn_recurrence",
    )(z, z, z, z, lb.reshape(A_HEADS, 1, LANES), norm_g.reshape(1, LANES), s0_t)


def _gla_kernel(zq_ref, zk_ref, zv_ref, zg_ref, zlr_ref, w2_ref, gb_ref, ng_ref, s0_ref, o_ref, s_ref, st_scr,
                *, chunk):
    t = pl.program_id(2)

    @pl.when(t == 0)
    def _():
        st_scr[...] = s0_ref[0, 0]

    nchunks = zq_ref.shape[1] // chunk
    m0 = _lane_mask(LANES, 0, B_DK)
    m1 = _lane_mask(LANES, B_DK, LANES)

    def body(ci, carry):
        sl = pl.ds(pl.multiple_of(ci * chunk, chunk), chunk)
        pre = _dot(zlr_ref[0, sl, :], w2_ref[...], precision=HI) + gb_ref[...]
        log_g = (jnp.minimum(pre, 0.0) - jnp.log(1.0 + jnp.exp(-jnp.abs(pre)))) * (1.0 / GLA_NORMALIZER)
        q = zq_ref[0, sl, :] * (B_DK ** -0.5)
        b = _cumsum_rows(log_g)
        v = zv_ref[0, sl, :]
        (o0, o1), st = _gated_chunk(q, zk_ref[0, sl, :], b,
                                    [(m0, v[:, :LANES]), (m1, v[:, LANES:])], st_scr[...])
        st_scr[...] = st
        gate = _silu(zg_ref[0, sl, :])
        o_ref[0, sl, 0:LANES] = _head_rms(o0, ng_ref[...]) * gate[:, :LANES]
        o_ref[0, sl, LANES:2 * LANES] = _head_rms(o1, ng_ref[...]) * gate[:, LANES:]
        return carry

    lax.fori_loop(0, nchunks, body, 0)

    @pl.when(t == pl.num_programs(2) - 1)
    def _():
        s_ref[0, 0] = st_scr[...]


def gla_recurrence(z, w2pad, gate_b, norm_g, s0_t, tb, chunk):
    bsz, t, _ = z.shape
    npairs = B_HEADS // 2
    return pl.pallas_call(
        functools.partial(_gla_kernel, chunk=chunk),
        grid=(bsz, npairs, t // tb),
        in_specs=[pl.BlockSpec((1, tb, LANES), lambda b, p, i: (b, i, 16 + p)),
                  pl.BlockSpec((1, tb, LANES), lambda b, p, i: (b, i, 18 + p)),
                  pl.BlockSpec((1, tb, 2 * LANES), lambda b, p, i: (b, i, 10 + p)),
                  pl.BlockSpec((1, tb, 2 * LANES), lambda b, p, i: (b, i, 12 + p)),
                  pl.BlockSpec((1, tb, LANES), lambda b, p, i: (b, i, 28)),
                  pl.BlockSpec((LANES, LANES), lambda b, p, i: (0, p)),
                  pl.BlockSpec((1, LANES), lambda b, p, i: (0, p)),
                  pl.BlockSpec((1, LANES), lambda b, p, i: (0, 0)),
                  pl.BlockSpec((1, 1, LANES, LANES), lambda b, p, i: (b, p, 0, 0))],
        out_specs=[pl.BlockSpec((1, tb, 2 * LANES), lambda b, p, i: (b, i, p)),
                   pl.BlockSpec((1, 1, LANES, LANES), lambda b, p, i: (b, p, 0, 0))],
        out_shape=[jax.ShapeDtypeStruct((bsz, t, B_WIDTH), F32),
                   jax.ShapeDtypeStruct((bsz, npairs, LANES, LANES), F32)],
        scratch_shapes=[pltpu.VMEM((LANES, LANES), F32)],
        compiler_params=_cparams(("parallel", "parallel", "arbitrary")),
        name="gla_recurrence",
    )(z, z, z, z, z, w2pad, gate_b.reshape(1, 2 * LANES), norm_g.reshape(1, LANES), s0_t)


def _rwkv_proj_kernel(x_ref, xp_ref, xl_ref, g1_ref, mu_ref, wr_ref, wk_ref, wv_ref, ww1_ref, ww2_ref, w0_ref,
                      aw1_ref, aw2_ref, a0_ref, gw1_ref, gw2_ref, kk_ref, ka_ref,
                      r_out, k_out, v_out, lw_out, a_out, kk_out, g_out, hl_out, *, tiles_per_seq):
    i = pl.program_id(0)
    g1 = g1_ref[...]
    hn = _rms(x_ref[...], g1)
    tm = hn.shape[0]
    prev = _rms(xp_ref[...], g1)[SUBLANES - 1:SUBLANES, :]
    prev = jnp.where(i % tiles_per_seq == 0, xl_ref[0], prev)
    row = lax.broadcasted_iota(I32, (tm, 1), 0)
    xprev = jnp.where(row == 0, prev, pltpu.roll(hn, 1, axis=0))
    dx = xprev - hn

    def mix(j):
        return hn + dx * mu_ref[j:j + 1, :]

    r = _dot(mix(0).astype(BF16), wr_ref[...])
    k = _dot(mix(1).astype(BF16), wk_ref[...])
    v = _dot(mix(2).astype(BF16), wv_ref[...])
    wl = _dot(jnp.tanh(_dot(mix(3), ww1_ref[...], precision=HI)), ww2_ref[...], precision=HI)
    z = w0_ref[...] + wl
    wpre = -(jnp.maximum(-z, 0.0) + jnp.log(1.0 + jnp.exp(-jnp.abs(z)))) - 0.5
    al = _dot(_dot(mix(4), aw1_ref[...], precision=HI), aw2_ref[...], precision=HI)
    a = _sigmoid(a0_ref[...] + al)
    gg = _dot(_sigmoid(_dot(mix(5).astype(BF16), gw1_ref[...])).astype(BF16), gw2_ref[...])
    r_out[...] = r
    k_out[...] = k * (1.0 + (a - 1.0) * ka_ref[...])
    v_out[...] = v
    lw_out[...] = -jnp.exp(wpre)
    a_out[...] = a
    kk_out[...] = k * kk_ref[...]
    g_out[...] = gg
    hl_out[0] = hn[tm - 1:tm, :]


def rwkv_proj(x, x_last, seq_len, g1, mu, wr, wk, wv, ww1, ww2, w0, aw1, aw2, a0, gw1, gw2, k_k, k_a, tm):
    n, d = x.shape
    tiles_per_seq = seq_len // tm
    row = lambda a: a.reshape(1, d)
    full = lambda a: pl.BlockSpec(a.shape, lambda i: (0,) * a.ndim)
    tile = pl.BlockSpec((tm, d), lambda i: (i, 0))
    blocks8 = tm // SUBLANES
    args = (x, x, x_last.reshape(-1, 1, d), row(g1), mu, wr, wk, wv, ww1, ww2, row(w0), aw1, aw2, row(a0),
            gw1, gw2, row(k_k), row(k_a))
    in_specs = [tile,
                pl.BlockSpec((SUBLANES, d), lambda i: (jnp.maximum(i * blocks8 - 1, 0), 0)),
                pl.BlockSpec((1, 1, d), lambda i: (i // tiles_per_seq, 0, 0))]
    in_specs += [full(a) for a in args[3:]]
    outs = pl.pallas_call(
        functools.partial(_rwkv_proj_kernel, tiles_per_seq=tiles_per_seq),
        grid=(n // tm,),
        in_specs=in_specs,
        out_specs=[tile] * 7 + [pl.BlockSpec((1, 1, d), lambda i: (i, 0, 0))],
        out_shape=[jax.ShapeDtypeStruct((n, d), F32)] * 7 + [jax.ShapeDtypeStruct((n // tm, 1, d), F32)],
        compiler_params=_cparams(("parallel",)),
        name="rwkv_proj",
    )(*args)
    return outs


_NN = ((1,), (0,))
_NT = ((1,), (1,))
_TN = ((0,), (0,))
RWKV_AB_PASSES = 1
RWKV_INV_PASSES = 1
RWKV_APPLY_PASSES = 1
RWKV_STATE_PASSES = 3
RWKV_SEQS_PER_STEP = 4


def _split_bf16(a):
    hi = a.astype(BF16)
    return hi, (a - hi.astype(F32)).astype(BF16)


def _mm(a, b, dims, passes):
    if passes == 6:
        return lax.dot_general(a, b, (dims, ((), ())), preferred_element_type=F32, precision=HI)
    dg = lambda x, y: lax.dot_general(x, y, (dims, ((), ())), preferred_element_type=F32)
    ah, al = _split_bf16(a)
    bh, bl = _split_bf16(b)
    if passes == 1:
        return dg(ah, bh)
    return dg(ah, bh) + (dg(al, bh) + dg(ah, bl))


def _cumsum_rows3(g):
    tri = _tri(g.shape[0], False).astype(BF16)
    h1 = g.astype(BF16)
    r1 = g - h1.astype(F32)
    h2 = r1.astype(BF16)
    h3 = (r1 - h2.astype(F32)).astype(BF16)
    return _dot(tri, h1) + (_dot(tri, h2) + _dot(tri, h3))


def _pair_sum(x, m0):
    s0 = jnp.sum(jnp.where(m0, x, 0.0), axis=-1, keepdims=True)
    s1 = jnp.sum(jnp.where(m0, 0.0, x), axis=-1, keepdims=True)
    return jnp.where(m0, s0, s1)


def _rwkv_kernel(r_ref, k_ref, v_ref, lw_ref, a_ref, kk_ref, g_ref, rk_ref, lng_ref, lnb_ref, s0_ref,
                 o_ref, s_ref, mt_scr, *, chunk):
    t = pl.program_id(2)

    @pl.when(t == 0)
    def _():
        mt_scr[...] = s0_ref[:, 0]

    nrows = r_ref.shape[0]
    nchunks = r_ref.shape[1] // chunk
    c2 = 2 * chunk
    m0 = _lane_mask(LANES, 0, C_HEAD)
    rowi = lax.broadcasted_iota(I32, (LANES, LANES), 0)
    coli = lax.broadcasted_iota(I32, (LANES, LANES), 1)
    blockdiag = (rowi < C_HEAD) == (coli < C_HEAD)
    ti = lax.broadcasted_iota(I32, (c2, c2), 0)
    si = lax.broadcasted_iota(I32, (c2, c2), 1)
    same_head = (ti < chunk) == (si < chunk)
    tm_ = jnp.where(ti < chunk, ti, ti - chunk)
    sm_ = jnp.where(si < chunk, si, si - chunk)
    strict = same_head & (sm_ < tm_)
    incl = same_head & (sm_ <= tm_)

    def stack_heads(x):
        return jnp.concatenate([jnp.where(m0, x, 0.0), jnp.where(m0, 0.0, x)], axis=0)

    def twice(x):
        return jnp.concatenate([x, x], axis=0)

    def unstack(x2):
        return jnp.where(m0, x2[:chunk], x2[chunk:])

    eye = (ti == si).astype(F32)
    seqs = range(nrows)

    def body(ci, carry):
        sl = pl.ds(pl.multiple_of(ci * chunk, chunk), chunk)
        r = [r_ref[i, sl, :] for i in seqs]
        k = [k_ref[i, sl, :] for i in seqs]
        v = [v_ref[i, sl, :] for i in seqs]
        lw = [lw_ref[i, sl, :] for i in seqs]
        kkr = [kk_ref[i, sl, :] for i in seqs]
        kk = [x * lax.rsqrt(_pair_sum(x * x, m0) + 1e-12) for x in kkr]
        al = [a_ref[i, sl, :] * kk[i] for i in seqs]
        gam = [_cumsum_rows3(x) for x in lw]
        e_neg = [jnp.exp(-x) for x in gam]
        xr = [jnp.concatenate([stack_heads(kk[i] * jnp.exp(gam[i] - lw[i])),
                               stack_heads(r[i] * jnp.exp(gam[i]))], axis=0) for i in seqs]
        alk = [jnp.concatenate([twice(al[i] * e_neg[i]), twice(k[i] * e_neg[i])], axis=0) for i in seqs]
        mt = [mt_scr[i] for i in seqs]
        ab = [_mm(xr[i], alk[i], _NT, RWKV_AB_PASSES) for i in seqs]
        xm = [_mm(xr[i], mt[i], _NT, RWKV_STATE_PASSES) for i in seqs]
        a_al = [jnp.where(strict, x[:c2, :c2], 0.0) for x in ab]
        a_k = [jnp.where(strict, x[:c2, c2:], 0.0) for x in ab]
        b_alk = [jnp.concatenate([jnp.where(incl, x[c2:, c2:], 0.0), jnp.where(incl, -x[c2:, :c2], 0.0)], axis=1)
                 for x in ab]
        v2 = [twice(x) for x in v]
        rhs = [xm[i][:c2] + _mm(a_k[i], v2[i], _NN, RWKV_APPLY_PASSES) for i in seqs]
        p = [-x for x in a_al]
        tinv = [eye + x for x in p]
        span = 2
        while span < chunk:
            p = [_mm(x, x, _NN, RWKV_INV_PASSES) for x in p]
            tinv = [tinv[i] + _mm(tinv[i], p[i], _NN, RWKV_INV_PASSES) for i in seqs]
            span *= 2
        u = [unstack(_mm(tinv[i], rhs[i], _NN, RWKV_APPLY_PASSES)) for i in seqs]
        o = [unstack(xm[i][c2:] + _mm(b_alk[i], jnp.concatenate([v2[i], twice(u[i])], axis=0), _NN,
                                      RWKV_APPLY_PASSES)) for i in seqs]
        g_last = [x[-1:, :] for x in gam]
        e_end = [jnp.exp(g_last[i] - gam[i]) for i in seqs]
        upd = [_mm(jnp.concatenate([v[i], u[i]], axis=0),
                   jnp.concatenate([k[i] * e_end[i], -(al[i] * e_end[i])], axis=0), _TN, RWKV_STATE_PASSES)
               for i in seqs]
        for i in seqs:
            mt_scr[i] = mt[i] * jnp.exp(g_last[i]) + jnp.where(blockdiag, upd[i], 0.0)
        for i in seqs:
            mean = _pair_sum(o[i], m0) * (1.0 / C_HEAD)
            cen = o[i] - mean
            var = _pair_sum(cen * cen, m0) * (1.0 / C_HEAD)
            on = cen * lax.rsqrt(var + C_GN_EPS) * lng_ref[...] + lnb_ref[...]
            bonus = _pair_sum(r[i] * k[i] * rk_ref[...], m0) * v[i]
            o_ref[i, sl, :] = (on + bonus) * g_ref[i, sl, :]
        return carry

    lax.fori_loop(0, nchunks, body, 0)

    @pl.when(t == pl.num_programs(2) - 1)
    def _():
        s_ref[:, 0] = mt_scr[...]


def rwkv_recurrence(r, k, v, lw, a, kk, g, r_k, ln_g, ln_b, s0_bd, tb, chunk, nb):
    bsz, t, d = r.shape
    npairs = C_HEADS // 2
    seq = pl.BlockSpec((nb, tb, LANES), lambda b, p, i: (b, i, p))
    vec = pl.BlockSpec((1, LANES), lambda b, p, i: (0, p))
    st = pl.BlockSpec((nb, 1, LANES, LANES), lambda b, p, i: (b, p, 0, 0))
    return pl.pallas_call(
        functools.partial(_rwkv_kernel, chunk=chunk),
        grid=(bsz // nb, npairs, t // tb),
        in_specs=[seq] * 7 + [vec, vec, vec, st],
        out_specs=[seq, st],
        out_shape=[jax.ShapeDtypeStruct((bsz, t, d), F32),
                   jax.ShapeDtypeStruct((bsz, npairs, LANES, LANES), F32)],
        scratch_shapes=[pltpu.VMEM((nb, LANES, LANES), F32)],
        compiler_params=_cparams(("parallel", "parallel", "arbitrary")),
        name="rwkv_recurrence",
    )(r, k, v, lw, a, kk, g, r_k.reshape(1, d), ln_g.reshape(1, d), ln_b.reshape(1, d), s0_bd)


NEG_INF = float("-inf")


def _top16_rows(s):
    n = s.shape[0]
    key = lax.broadcasted_iota(I32, s.shape, 0)
    vals, idxs = [], []
    for _ in range(P_TOPK):
        m = jnp.max(s, axis=0, keepdims=True)
        am = jnp.min(jnp.where(s == m, key, n), axis=0, keepdims=True)
        vals.append(m)
        idxs.append(am)
        s = jnp.where(key == am, NEG_INF, s)
    return vals, idxs


def _top16_pairs(v0, i0, v1, i1):
    a0 = jnp.concatenate(v0[0:8], axis=0)
    a1 = jnp.concatenate(v0[8:16], axis=0)
    b0 = jnp.concatenate(v1[0:8], axis=0)
    b1 = jnp.concatenate(v1[8:16], axis=0)
    ia0 = jnp.concatenate(i0[0:8], axis=0) * P_KEYS
    ia1 = jnp.concatenate(i0[8:16], axis=0) * P_KEYS
    ib0 = jnp.concatenate(i1[0:8], axis=0)
    ib1 = jnp.concatenate(i1[8:16], axis=0)
    row = lax.broadcasted_iota(I32, (SUBLANES, 1), 0)
    slabs = []

    def add(val, eid, keep):
        slabs.append((val if keep is None else jnp.where(keep, val, NEG_INF), eid))

    add(v0[0] + b0, ia0[0:1] + ib0, None)
    add(v0[0] + b1, ia0[0:1] + ib1, None)
    add(v0[1] + b0, ia0[1:2] + ib0, None)
    add(v0[2] + b0, ia0[2:3] + ib0, row < 5)
    add(v0[3] + b0, ia0[3:4] + ib0, row < 4)
    add(a0 + v1[0], ia0 + ib0[0:1], row >= 4)
    add(a1 + v1[0], ia1 + ib0[0:1], None)
    add(a0 + v1[1], ia0 + ib0[1:2], row >= 4)
    add(a0 + v1[2], ia0 + ib0[2:3], row == 4)

    big = P_KEYS * P_KEYS
    out_v, out_e = [], []
    for _ in range(P_TOPK):
        m = slabs[0][0]
        for val, _e in slabs[1:]:
            m = jnp.maximum(m, val)
        m = jnp.max(m, axis=0, keepdims=True)
        e = None
        for val, eid in slabs:
            c = jnp.where(val == m, eid, big)
            e = c if e is None else jnp.minimum(e, c)
        e = jnp.min(e, axis=0, keepdims=True)
        out_v.append(m)
        out_e.append(e)
        slabs = [(jnp.where(eid == e, NEG_INF, val), eid) for val, eid in slabs]
    return out_v, out_e


def _peer_select_kernel(x_ref, g_ref, wq_ref, keys_ref, xn_out, eid_out, gate_out):
    hn = _rms(x_ref[...], g_ref[...])
    xn_out[...] = hn
    q = _dot(hn.astype(BF16), wq_ref[...])
    tm = q.shape[0]
    for lt in range(tm // LANES):
        rows = slice(lt * LANES, (lt + 1) * LANES)
        e_rows, g_rows = [], []
        for h in range(P_HEADS):
            tops = []
            for p in range(2):
                hp = 2 * h + p
                s = _dot_nt(keys_ref[hp], q[rows, hp * LANES:(hp + 1) * LANES], precision=HI)
                tops.append(_top16_rows(s))
            cs, ce = _top16_pairs(tops[0][0], tops[0][1], tops[1][0], tops[1][1])
            ex = [jnp.exp(c - cs[0]) for c in cs]
            tot = ex[0]
            for e in ex[1:]:
                tot = tot + e
            inv = 1.0 / tot
            e_rows += ce
            g_rows += [e * inv for e in ex]
        eid_out[rows, :] = jnp.concatenate(e_rows, axis=0).T
        gate_out[rows, :] = jnp.concatenate(g_rows, axis=0).T


def peer_select(x, g, wq_bf16, keys, tm):
    n, d = x.shape
    return pl.pallas_call(
        _peer_select_kernel,
        grid=(n // tm,),
        in_specs=[pl.BlockSpec((tm, d), lambda i: (i, 0)),
                  pl.BlockSpec((1, d), lambda i: (0, 0)),
                  pl.BlockSpec(wq_bf16.shape, lambda i: (0, 0)),
                  pl.BlockSpec(keys.shape, lambda i: (0, 0, 0))],
        out_specs=[pl.BlockSpec((tm, d), lambda i: (i, 0)),
                   pl.BlockSpec((tm, P_PICKS), lambda i: (i, 0)),
                   pl.BlockSpec((tm, P_PICKS), lambda i: (i, 0))],
        out_shape=[jax.ShapeDtypeStruct((n, d), F32),
                   jax.ShapeDtypeStruct((n, P_PICKS), I32),
                   jax.ShapeDtypeStruct((n, P_PICKS), F32)],
        compiler_params=_cparams(("parallel",)),
        name="peer_select",
    )(x, g.reshape(1, d), wq_bf16, keys)


def _peer_act_kernel(h_ref, g_ref, o_ref):
    h = h_ref[...]
    o_ref[...] = 0.5 * h * (1.0 + lax.erf(h * (2.0 ** -0.5))) * g_ref[...]


def peer_act(hid, gate, tm):
    n, p = hid.shape
    spec = pl.BlockSpec((tm, p), lambda i: (i, 0))
    return pl.pallas_call(
        _peer_act_kernel, grid=(n // tm,), in_specs=[spec, spec], out_specs=spec,
        out_shape=jax.ShapeDtypeStruct((n, p), F32),
        compiler_params=_cparams(("parallel",)), name="peer_act",
    )(hid, gate)


_CHUNKS_PER_TOK = P_PICKS // SC_CHUNK
_VREGS_PER_ROW = D_MODEL // SC_LANES


def _sc_worker():
    return lax.axis_index("s") * SC_CORES + lax.axis_index("c")


def _sc_hid_body(tab_hbm, idx_hbm, x_hbm, hid_hbm, idx_v, x_v, hid_v, part_v, rows_a, rows_b, sem_a, sem_b,
                 *, tok_per_w):
    wid = _sc_worker()
    nblk = tok_per_w // SC_TOK_BLOCK
    nch = SC_TOK_BLOCK * _CHUNKS_PER_TOK
    lane = lax.iota(I32, SC_LANES)
    half_v = _VREGS_PER_ROW // 2

    def dots(rows, x_row, res_ref, col0):
        for xp in range(2):
            base = xp * half_v * SC_LANES
            xh = [x_v[x_row, pl.ds(base + j * SC_LANES, SC_LANES)] for j in range(half_v)]

            def partial_dot(r):
                accs = [rows[r, pl.ds(base + j * SC_LANES, SC_LANES)] * xh[j] for j in range(4)]
                for j in range(4, half_v):
                    accs[j % 4] = accs[j % 4] + rows[r, pl.ds(base + j * SC_LANES, SC_LANES)] * xh[j]
                return (accs[0] + accs[1]) + (accs[2] + accs[3])

            if xp == 0:
                def first(r, c):
                    part_v[r, :] = partial_dot(r)
                    return c
                lax.fori_loop(0, SC_CHUNK, first, 0)
            else:
                for grp in range(SC_CHUNK // SC_LANES):
                    def second(rr, res):
                        r = grp * SC_LANES + rr
                        return jnp.where(lane == rr, jnp.sum(partial_dot(r) + part_v[r, :]), res)
                    res = lax.fori_loop(0, SC_LANES, second, jnp.zeros((SC_LANES,), F32))
                    res_ref[x_row, pl.ds(col0 + grp * SC_LANES, SC_LANES)] = res

    def block(bi, carry):
        tok0 = wid * tok_per_w + bi * SC_TOK_BLOCK
        pltpu.sync_copy(idx_hbm.at[pl.ds(tok0 * _CHUNKS_PER_TOK, nch)], idx_v)
        pltpu.sync_copy(x_hbm.at[pl.ds(tok0, SC_TOK_BLOCK)], x_v)
        pltpu.async_copy(tab_hbm.at[idx_v.at[0]], rows_a, sem_a)

        def pair(j, c):
            ca = 2 * j
            pltpu.async_copy(tab_hbm.at[idx_v.at[ca + 1]], rows_b, sem_b)
            pltpu.make_async_copy(tab_hbm.at[idx_v.at[0]], rows_a, sem_a).wait()
            dots(rows_a, ca // _CHUNKS_PER_TOK, hid_v, (ca % _CHUNKS_PER_TOK) * SC_CHUNK)

            @pl.when(j < nch // 2 - 1)
            def _():
                pltpu.async_copy(tab_hbm.at[idx_v.at[ca + 2]], rows_a, sem_a)
            pltpu.make_async_copy(tab_hbm.at[idx_v.at[0]], rows_b, sem_b).wait()
            cb = ca + 1
            dots(rows_b, cb // _CHUNKS_PER_TOK, hid_v, (cb % _CHUNKS_PER_TOK) * SC_CHUNK)
            return c
        lax.fori_loop(0, nch // 2, pair, 0)
        pltpu.sync_copy(hid_v, hid_hbm.at[pl.ds(tok0, SC_TOK_BLOCK)])
        return carry

    lax.fori_loop(0, nblk, block, 0)


def sc_expert_hidden(table, eidx, xn):
    n = xn.shape[0]
    tok_per_w = n // SC_WORKERS
    mesh = plsc.VectorSubcoreMesh(core_axis_name="c", subcore_axis_name="s")
    nch = SC_TOK_BLOCK * _CHUNKS_PER_TOK
    k = pl.kernel(
        functools.partial(_sc_hid_body, tok_per_w=tok_per_w), mesh=mesh,
        out_type=jax.ShapeDtypeStruct((n, P_PICKS), F32),
        scratch_types=[pltpu.VMEM((nch, SC_CHUNK), I32),
                       pltpu.VMEM((SC_TOK_BLOCK, D_MODEL), F32),
                       pltpu.VMEM((SC_TOK_BLOCK, P_PICKS), F32),
                       pltpu.VMEM((SC_CHUNK, SC_LANES), F32),
                       pltpu.VMEM((SC_CHUNK, D_MODEL), F32),
                       pltpu.VMEM((SC_CHUNK, D_MODEL), F32),
                       pltpu.SemaphoreType.DMA, pltpu.SemaphoreType.DMA],
        compiler_params=pltpu.CompilerParams(needs_layout_passes=False),
        name="sc_expert_hidden",
    )
    return k(table, eidx.reshape(n * _CHUNKS_PER_TOK, SC_CHUNK), xn)


def _sc_out_body(tab_hbm, idx_hbm, act_hbm, x_hbm, y_hbm, idx_v, act_v, y_v, rows_a, rows_b, sem_a, sem_b,
                 *, tok_per_w):
    wid = _sc_worker()
    nblk = tok_per_w // SC_TOK_BLOCK
    nch = SC_TOK_BLOCK * _CHUNKS_PER_TOK
    half_v = _VREGS_PER_ROW // 2

    def accum(rows, tok, col0):
        for hv in range(2):
            base = hv * half_v * SC_LANES
            acc0 = tuple(y_v[tok, pl.ds(base + j * SC_LANES, SC_LANES)] for j in range(half_v))

            def one(r, acc):
                aidx = jnp.full((SC_LANES,), col0, I32) + r
                w = plsc.load_gather(act_v, [jnp.full((SC_LANES,), tok, I32), aidx])
                return tuple(acc[j] + w * rows[r, pl.ds(base + j * SC_LANES, SC_LANES)] for j in range(half_v))
            acc = lax.fori_loop(0, SC_CHUNK, one, acc0)
            for j in range(half_v):
                y_v[tok, pl.ds(base + j * SC_LANES, SC_LANES)] = acc[j]

    def block(bi, carry):
        tok0 = wid * tok_per_w + bi * SC_TOK_BLOCK
        pltpu.sync_copy(idx_hbm.at[pl.ds(tok0 * _CHUNKS_PER_TOK, nch)], idx_v)
        pltpu.sync_copy(act_hbm.at[pl.ds(tok0, SC_TOK_BLOCK)], act_v)
        pltpu.sync_copy(x_hbm.at[pl.ds(tok0, SC_TOK_BLOCK)], y_v)
        pltpu.async_copy(tab_hbm.at[idx_v.at[0]], rows_a, sem_a)

        def pair(j, c):
            ca = 2 * j
            pltpu.async_copy(tab_hbm.at[idx_v.at[ca + 1]], rows_b, sem_b)
            pltpu.make_async_copy(tab_hbm.at[idx_v.at[0]], rows_a, sem_a).wait()
            accum(rows_a, ca // _CHUNKS_PER_TOK, (ca % _CHUNKS_PER_TOK) * SC_CHUNK)

            @pl.when(j < nch // 2 - 1)
            def _():
                pltpu.async_copy(tab_hbm.at[idx_v.at[ca + 2]], rows_a, sem_a)
            pltpu.make_async_copy(tab_hbm.at[idx_v.at[0]], rows_b, sem_b).wait()
            cb = ca + 1
            accum(rows_b, cb // _CHUNKS_PER_TOK, (cb % _CHUNKS_PER_TOK) * SC_CHUNK)
            return c
        lax.fori_loop(0, nch // 2, pair, 0)
        pltpu.sync_copy(y_v, y_hbm.at[pl.ds(tok0, SC_TOK_BLOCK)])
        return carry

    lax.fori_loop(0, nblk, block, 0)


def sc_expert_output(table, eidx, act, x):
    n = x.shape[0]
    tok_per_w = n // SC_WORKERS
    mesh = plsc.VectorSubcoreMesh(core_axis_name="c", subcore_axis_name="s")
    nch = SC_TOK_BLOCK * _CHUNKS_PER_TOK
    k = pl.kernel(
        functools.partial(_sc_out_body, tok_per_w=tok_per_w), mesh=mesh,
        out_type=jax.ShapeDtypeStruct((n, D_MODEL), F32),
        scratch_types=[pltpu.VMEM((nch, SC_CHUNK), I32),
                       pltpu.VMEM((SC_TOK_BLOCK, P_PICKS), F32),
                       pltpu.VMEM((SC_TOK_BLOCK, D_MODEL), F32),
                       pltpu.VMEM((SC_CHUNK, D_MODEL), F32),
                       pltpu.VMEM((SC_CHUNK, D_MODEL), F32),
                       pltpu.SemaphoreType.DMA, pltpu.SemaphoreType.DMA],
        compiler_params=pltpu.CompilerParams(needs_layout_passes=False),
        name="sc_expert_output",
    )
    return k(table, eidx.reshape(n * _CHUNKS_PER_TOK, SC_CHUNK), act, x)


def peer_ffn(x, g, wq_bf16, keys, u_tab, v_tab, tm):
    xn, eidx, gate = peer_select(x, g, wq_bf16, keys, tm)
    hid = sc_expert_hidden(u_tab, eidx, xn)
    act = peer_act(hid, gate, tm)
    return sc_expert_output(v_tab, eidx, act, x)


def _prep_w_in(w_in):
    main = jnp.concatenate([w_in[:, :3072], w_in[:, 3088:3600]], axis=1)
    lr = jnp.pad(w_in[:, 3072:3088], ((0, 0), (0, LANES - GLA_RANK)))
    return jnp.concatenate([main, lr], axis=1).astype(BF16)


def _trunk(x, st_h, st_g, st_r, st_s, w, seq_len, tm, tmp, tb, chunk):
    bsz = x.shape[0]
    n = bsz * seq_len
    x2 = x.reshape(n, D_MODEL)

    z = norm_proj(x2, w["norm1_g"][0], w["w_in"], tm).reshape(bsz, seq_len, Z_WIDTH)
    s0_h = jnp.swapaxes(st_h, -1, -2)
    s0_g = jnp.swapaxes(st_g.reshape(bsz, 2, 2 * B_DK, LANES), -1, -2)
    o_a, sh_t = hgrn_recurrence(z, w["lb0"], w["hgrn_norm_g"], s0_h, tb, chunk)
    o_b, sg_t = gla_recurrence(z, w["gla_w2"], w["gla_b"], w["gla_norm_g"], s0_g, tb, chunk)
    new_h = jnp.swapaxes(sh_t, -1, -2)
    new_g = jnp.swapaxes(sg_t, -1, -2).reshape(bsz, B_HEADS, B_DK, LANES)
    x2 = out_proj2(x2, o_a.reshape(n, A_WIDTH), o_b.reshape(n, B_WIDTH), w["w_out_a"], w["w_out_b"], tm)
    x2 = peer_ffn(x2, w["norm2_g"][0], w["peer_wq"][0], w["peer_keys"][0], w["peer_u"][0], w["peer_v"][0], tmp)

    r, k, v, lw, a, kk, g, hl = rwkv_proj(
        x2, st_s, seq_len, w["norm1_g"][1], w["mu"], w["wr"], w["wk"], w["wv"], w["w_w1"], w["w_w2"], w["w0"],
        w["a_w1"], w["a_w2"], w["a0"], w["g_w1"], w["g_w2"], w["k_k"], w["k_a"], tm)
    new_s = hl.reshape(bsz, seq_len // tm, D_MODEL)[:, -1]
    pr = st_r.reshape(bsz, C_HEADS // 2, 2, C_HEAD, C_HEAD)
    zero = jnp.zeros_like(pr[:, :, 0])
    s0_r = jnp.concatenate([jnp.concatenate([pr[:, :, 0], zero], axis=-1),
                            jnp.concatenate([zero, pr[:, :, 1]], axis=-1)], axis=-2)
    sh3 = lambda t: t.reshape(bsz, seq_len, D_MODEL)
    o_c, sr_bd = rwkv_recurrence(sh3(r), sh3(k), sh3(v), sh3(lw), sh3(a), sh3(kk), sh3(g),
                                 w["r_k"], w["ln_g"], w["ln_b"], s0_r, tb, chunk, RWKV_SEQS_PER_STEP)
    new_r = jnp.stack([sr_bd[:, :, :C_HEAD, :C_HEAD], sr_bd[:, :, C_HEAD:, C_HEAD:]], axis=2)
    new_r = new_r.reshape(bsz, C_HEADS, C_HEAD, C_HEAD)
    x2 = out_proj1(x2, o_c.reshape(n, D_MODEL), w["w_out_c"], tm)
    x2 = peer_ffn(x2, w["norm2_g"][1], w["peer_wq"][1], w["peer_keys"][1], w["peer_u"][1], w["peer_v"][1], tmp)

    y = final_norm(x2, w["final_g"], tm).reshape(bsz, seq_len, D_MODEL)
    return y, new_h[None], new_g[None], new_r[None], new_s[None]


def kernel(x_prompt, x_sample, state_hgrn, state_gla, state_rwkv, state_shift, w_in_ab, hgrn_lower_bounds, hgrn_norm_g, gla_gate_w2, gla_gate_b, gla_norm_g, w_out_ab, rwkv_mu, rwkv_w_rkv, rwkv_w_w1, rwkv_w_w2, rwkv_w0, rwkv_a_w1, rwkv_a_w2, rwkv_a0, rwkv_g_w1, rwkv_g_w2, rwkv_k_k, rwkv_k_a, rwkv_r_k, rwkv_ln_g, rwkv_ln_b, w_out_c, norm1_g, norm2_g, final_g, peer_w_q, peer_sub_keys, peer_u, peer_v):
    lbs = jnp.cumsum(jax.nn.softmax(hgrn_lower_bounds.astype(F32), axis=0), axis=0)
    w = dict(
        norm1_g=norm1_g, norm2_g=norm2_g, final_g=final_g,
        w_in=_prep_w_in(w_in_ab[0]), lb0=lbs[0], hgrn_norm_g=hgrn_norm_g[0],
        gla_w2=jnp.pad(gla_gate_w2[0], ((0, LANES - GLA_RANK), (0, 0))), gla_b=gla_gate_b[0],
        gla_norm_g=gla_norm_g[0],
        w_out_a=w_out_ab[0, :A_WIDTH].astype(BF16), w_out_b=w_out_ab[0, A_WIDTH:].astype(BF16),
        mu=rwkv_mu[0], wr=rwkv_w_rkv[0, 0].astype(BF16), wk=rwkv_w_rkv[0, 1].astype(BF16),
        wv=rwkv_w_rkv[0, 2].astype(BF16), w_w1=rwkv_w_w1[0], w_w2=rwkv_w_w2[0], w0=rwkv_w0[0],
        a_w1=rwkv_a_w1[0], a_w2=rwkv_a_w2[0], a0=rwkv_a0[0],
        g_w1=rwkv_g_w1[0].astype(BF16), g_w2=rwkv_g_w2[0].astype(BF16),
        k_k=rwkv_k_k[0], k_a=rwkv_k_a[0], r_k=rwkv_r_k[0], ln_g=rwkv_ln_g[0], ln_b=rwkv_ln_b[0],
        w_out_c=w_out_c[0].astype(BF16),
        peer_wq=peer_w_q.astype(BF16),
        peer_keys=peer_sub_keys.reshape(peer_sub_keys.shape[0], 2 * P_HEADS, P_KEYS, P_KEYS),
        peer_u=peer_u, peer_v=peer_v,
    )
    bp, tp, _ = x_prompt.shape
    bs, ts, _ = x_sample.shape
    gsz = bp // PROMPT_GROUPS
    zeros = lambda s: jnp.zeros((gsz,) + s.shape[2:], F32)
    groups = [_trunk(x_prompt[i * gsz:(i + 1) * gsz], zeros(state_hgrn), zeros(state_gla), zeros(state_rwkv),
                     zeros(state_shift), w, tp, tm=256, tmp=256, tb=512, chunk=64)
              for i in range(PROMPT_GROUPS)]
    y_p = jnp.concatenate([g[0] for g in groups], axis=0)
    p_h, p_g, p_r, p_s = (jnp.concatenate([g[j] for g in groups], axis=1) for j in range(1, 5))
    y_s, s_h, s_g, s_r, s_s = _trunk(x_sample, state_hgrn[0], state_gla[0], state_rwkv[0], state_shift[0],
                                     w, ts, tm=32, tmp=128, tb=32, chunk=32)
    return (y_p, y_s, p_h, p_g, p_r, p_s, s_h, s_g, s_r, s_s)
```

```python
import functools

import jax
import jax.numpy as jnp
from jax import lax
from jax.experimental import pallas as pl
from jax.experimental.pallas import tpu as pltpu
from jax.experimental.pallas import tpu_sc as plsc

F32 = jnp.float32
BF16 = jnp.bfloat16
I32 = jnp.int32
HI = lax.Precision.HIGHEST

D_MODEL = 1024
NORM_EPS = 1e-6
LANES = 128
SUBLANES = 8
VMEM_LIMIT = 56 * 1024 * 1024

A_WIDTH = 512
A_HEADS = 4
B_WIDTH = 512
B_HEADS = 4
B_DK = 64
GLA_RANK = 16
GLA_NORMALIZER = 16.0
Z_WIDTH = 3712
C_HEAD = 64
C_HEADS = 16
C_GN_EPS = 64e-5
P_HEADS = 8
P_KEYS = 128
P_TOPK = 16
P_PICKS = P_HEADS * P_TOPK
SC_CORES = 2
SC_SUBCORES = 16
SC_WORKERS = SC_CORES * SC_SUBCORES
SC_LANES = 16
SC_CHUNK = 64
SC_TOK_BLOCK = 16
SC_WORDS = D_MODEL // 2
PROMPT_GROUPS = 4


def _cparams(sem):
    return pltpu.CompilerParams(dimension_semantics=sem, vmem_limit_bytes=VMEM_LIMIT)


def _rms(x, g):
    ms = jnp.mean(x * x, axis=-1, keepdims=True)
    return x * lax.rsqrt(ms + NORM_EPS) * g


def _dot(a, b, precision=None):
    return jnp.dot(a, b, preferred_element_type=F32, precision=precision)


def _dot_nt(a, b, precision=None):
    return lax.dot_general(a, b, (((1,), (1,)), ((), ())), preferred_element_type=F32, precision=precision)


def _dot_tn(a, b, precision=None):
    return lax.dot_general(a, b, (((0,), (0,)), ((), ())), preferred_element_type=F32, precision=precision)


def _tri(n, strict):
    r = lax.broadcasted_iota(I32, (n, n), 0)
    c = lax.broadcasted_iota(I32, (n, n), 1)
    return (c < r) if strict else (c <= r)


def _cumsum_rows(g):
    return _dot(_tri(g.shape[0], False).astype(F32), g, precision=HI)


def _lane_mask(width, lo, hi):
    l = lax.broadcasted_iota(I32, (1, width), 1)
    return (l >= lo) & (l < hi)


def _sigmoid(x):
    return 1.0 / (1.0 + jnp.exp(-x))


def _silu(x):
    return x * _sigmoid(x)


def _norm_proj_kernel(x_ref, g_ref, w_ref, o_ref):
    hn = _rms(x_ref[...], g_ref[...])
    o_ref[...] = _dot(hn.astype(BF16), w_ref[...])


def norm_proj(x, g, w_bf16, tm):
    n, d = x.shape
    f = w_bf16.shape[1]
    return pl.pallas_call(
        _norm_proj_kernel,
        grid=(n // tm,),
        in_specs=[pl.BlockSpec((tm, d), lambda i: (i, 0)),
                  pl.BlockSpec((1, d), lambda i: (0, 0)),
                  pl.BlockSpec((d, f), lambda i: (0, 0))],
        out_specs=pl.BlockSpec((tm, f), lambda i: (i, 0)),
        out_shape=jax.ShapeDtypeStruct((n, f), F32),
        compiler_params=_cparams(("parallel",)),
        name="norm_proj",
    )(x, g.reshape(1, d), w_bf16)


def _out_proj2_kernel(x_ref, a_ref, b_ref, wa_ref, wb_ref, o_ref):
    y = _dot(a_ref[...].astype(BF16), wa_ref[...]) + _dot(b_ref[...].astype(BF16), wb_ref[...])
    o_ref[...] = x_ref[...] + y


def out_proj2(x, a, b, wa, wb, tm):
    n, d = x.shape
    ka, kb = a.shape[1], b.shape[1]
    return pl.pallas_call(
        _out_proj2_kernel,
        grid=(n // tm,),
        in_specs=[pl.BlockSpec((tm, d), lambda i: (i, 0)),
                  pl.BlockSpec((tm, ka), lambda i: (i, 0)),
                  pl.BlockSpec((tm, kb), lambda i: (i, 0)),
                  pl.BlockSpec((ka, d), lambda i: (0, 0)),
                  pl.BlockSpec((kb, d), lambda i: (0, 0))],
        out_specs=pl.BlockSpec((tm, d), lambda i: (i, 0)),
        out_shape=jax.ShapeDtypeStruct((n, d), F32),
        compiler_params=_cparams(("parallel",)),
        name="out_proj2",
    )(x, a, b, wa, wb)


def _out_proj1_kernel(x_ref, a_ref, wa_ref, o_ref):
    o_ref[...] = x_ref[...] + _dot(a_ref[...].astype(BF16), wa_ref[...])


def out_proj1(x, a, wa, tm):
    n, d = x.shape
    ka = a.shape[1]
    return pl.pallas_call(
        _out_proj1_kernel,
        grid=(n // tm,),
        in_specs=[pl.BlockSpec((tm, d), lambda i: (i, 0)),
                  pl.BlockSpec((tm, ka), lambda i: (i, 0)),
                  pl.BlockSpec((ka, d), lambda i: (0, 0))],
        out_specs=pl.BlockSpec((tm, d), lambda i: (i, 0)),
        out_shape=jax.ShapeDtypeStruct((n, d), F32),
        compiler_params=_cparams(("parallel",)),
        name="out_proj1",
    )(x, a, wa)


def _final_norm_kernel(x_ref, g_ref, o_ref):
    o_ref[...] = _rms(x_ref[...], g_ref[...])


def final_norm(x, g, tm):
    n, d = x.shape
    return pl.pallas_call(
        _final_norm_kernel,
        grid=(n // tm,),
        in_specs=[pl.BlockSpec((tm, d), lambda i: (i, 0)), pl.BlockSpec((1, d), lambda i: (0, 0))],
        out_specs=pl.BlockSpec((tm, d), lambda i: (i, 0)),
        out_shape=jax.ShapeDtypeStruct((n, d), F32),
        compiler_params=_cparams(("parallel",)),
        name="final_norm",
    )(x, g.reshape(1, d))


def _intra_chunk(q, k, b, heads):
    c = q.shape[0]
    nb = c // SUBLANES
    row = lax.broadcasted_iota(I32, (SUBLANES, 1), 0)
    qb = [q[SUBLANES * i:SUBLANES * (i + 1)] for i in range(nb)]
    bb = [b[SUBLANES * i:SUBLANES * (i + 1)] for i in range(nb)]
    outs = [[None] * nb for _ in heads]
    for s in range(c):
        rb0 = s // SUBLANES
        ks = k[s:s + 1, :]
        bs = b[s:s + 1, :]
        for rb in range(rb0, nb):
            p = qb[rb] * (ks * jnp.exp(bb[rb] - bs))
            for hi, (mask, v) in enumerate(heads):
                pm = p if mask is None else jnp.where(mask, p, 0.0)
                col = jnp.sum(pm, axis=-1, keepdims=True)
                if rb == rb0:
                    col = jnp.where(row + SUBLANES * rb >= s, col, 0.0)
                term = col * v[s:s + 1, :]
                outs[hi][rb] = term if outs[hi][rb] is None else outs[hi][rb] + term
    return [jnp.concatenate(o, axis=0) for o in outs]


def _gated_chunk(q, k, b, heads, st):
    intra = _intra_chunk(q, k, b, heads)
    qe = q * jnp.exp(b)
    b_last = b[-1:, :]
    kh = k * jnp.exp(b_last - b)
    outs = []
    for (mask, v), oi in zip(heads, intra):
        qm = qe if mask is None else jnp.where(mask, qe, 0.0)
        outs.append(oi + _dot_nt(qm, st, precision=HI))
    upd = _dot_tn(heads[0][1], kh, precision=HI)
    if len(heads) == 2:
        upd = jnp.where(heads[0][0], upd, _dot_tn(heads[1][1], kh, precision=HI))
    st = st * jnp.exp(b_last) + upd
    return outs, st


def _head_rms(o, g):
    ms = jnp.mean(o * o, axis=-1, keepdims=True)
    return o * lax.rsqrt(ms + NORM_EPS) * g


def _hgrn_kernel(zq_ref, zf_ref, zi_ref, zg_ref, lb_ref, ng_ref, s0_ref, o_ref, s_ref, st_scr, *, chunk):
    t = pl.program_id(2)

    @pl.when(t == 0)
    def _():
        st_scr[...] = s0_ref[0, 0]

    lb = lb_ref[0]
    nchunks = zq_ref.shape[1] // chunk

    def body(ci, carry):
        sl = pl.ds(pl.multiple_of(ci * chunk, chunk), chunk)
        f = lb + (1.0 - lb) * _sigmoid(zf_ref[0, sl, :])
        q = _silu(zq_ref[0, sl, :])
        b = _cumsum_rows(jnp.log(f))
        (o,), st = _gated_chunk(q, 1.0 - f, b, [(None, zi_ref[0, sl, :])], st_scr[...])
        st_scr[...] = st
        o_ref[0, sl, :] = _head_rms(o, ng_ref[...]) * _silu(zg_ref[0, sl, :])
        return carry

    lax.fori_loop(0, nchunks, body, 0)

    @pl.when(t == pl.num_programs(2) - 1)
    def _():
        s_ref[0, 0] = st_scr[...]


def hgrn_recurrence(z, lb, norm_g, s0_t, tb, chunk):
    bsz, t, _ = z.shape
    zspec = lambda off: pl.BlockSpec((1, tb, LANES), lambda b, h, i: (b, i, h + off))
    return pl.pallas_call(
        functools.partial(_hgrn_kernel, chunk=chunk),
        grid=(bsz, A_HEADS, t // tb),
        in_specs=[zspec(0), zspec(4), zspec(8), zspec(12),
                  pl.BlockSpec((1, 1, LANES), lambda b, h, i: (h, 0, 0)),
                  pl.BlockSpec((1, LANES), lambda b, h, i: (0, 0)),
                  pl.BlockSpec((1, 1, LANES, LANES), lambda b, h, i: (b, h, 0, 0))],
        out_specs=[pl.BlockSpec((1, tb, LANES), lambda b, h, i: (b, i, h)),
                   pl.BlockSpec((1, 1, LANES, LANES), lambda b, h, i: (b, h, 0, 0))],
        out_shape=[jax.ShapeDtypeStruct((bsz, t, A_WIDTH), F32),
                   jax.ShapeDtypeStruct((bsz, A_HEADS, LANES, LANES), F32)],
        scratch_shapes=[pltpu.VMEM((LANES, LANES), F32)],
        compiler_params=_cparams(("parallel", "parallel", "arbitrary")),
        name="hgrn_recurrence",
    )(z, z, z, z, lb.reshape(A_HEADS, 1, LANES), norm_g.reshape(1, LANES), s0_t)


def _gla_kernel(zq_ref, zk_ref, zv_ref, zg_ref, zlr_ref, w2_ref, gb_ref, ng_ref, s0_ref, o_ref, s_ref, st_scr,
                *, chunk):
    t = pl.program_id(2)

    @pl.when(t == 0)
    def _():
        st_scr[...] = s0_ref[0, 0]

    nchunks = zq_ref.shape[1] // chunk
    m0 = _lane_mask(LANES, 0, B_DK)
    m1 = _lane_mask(LANES, B_DK, LANES)

    def body(ci, carry):
        sl = pl.ds(pl.multiple_of(ci * chunk, chunk), chunk)
        pre = _dot(zlr_ref[0, sl, :], w2_ref[...], precision=HI) + gb_ref[...]
        log_g = (jnp.minimum(pre, 0.0) - jnp.log(1.0 + jnp.exp(-jnp.abs(pre)))) * (1.0 / GLA_NORMALIZER)
        q = zq_ref[0, sl, :] * (B_DK ** -0.5)
        b = _cumsum_rows(log_g)
        v = zv_ref[0, sl, :]
        (o0, o1), st = _gated_chunk(q, zk_ref[0, sl, :], b,
                                    [(m0, v[:, :LANES]), (m1, v[:, LANES:])], st_scr[...])
        st_scr[...] = st
        gate = _silu(zg_ref[0, sl, :])
        o_ref[0, sl, 0:LANES] = _head_rms(o0, ng_ref[...]) * gate[:, :LANES]
        o_ref[0, sl, LANES:2 * LANES] = _head_rms(o1, ng_ref[...]) * gate[:, LANES:]
        return carry

    lax.fori_loop(0, nchunks, body, 0)

    @pl.when(t == pl.num_programs(2) - 1)
    def _():
        s_ref[0, 0] = st_scr[...]


def gla_recurrence(z, w2pad, gate_b, norm_g, s0_t, tb, chunk):
    bsz, t, _ = z.shape
    npairs = B_HEADS // 2
    return pl.pallas_call(
        functools.partial(_gla_kernel, chunk=chunk),
        grid=(bsz, npairs, t // tb),
        in_specs=[pl.BlockSpec((1, tb, LANES), lambda b, p, i: (b, i, 16 + p)),
                  pl.BlockSpec((1, tb, LANES), lambda b, p, i: (b, i, 18 + p)),
                  pl.BlockSpec((1, tb, 2 * LANES), lambda b, p, i: (b, i, 10 + p)),
                  pl.BlockSpec((1, tb, 2 * LANES), lambda b, p, i: (b, i, 12 + p)),
                  pl.BlockSpec((1, tb, LANES), lambda b, p, i: (b, i, 28)),
                  pl.BlockSpec((LANES, LANES), lambda b, p, i: (0, p)),
                  pl.BlockSpec((1, LANES), lambda b, p, i: (0, p)),
                  pl.BlockSpec((1, LANES), lambda b, p, i: (0, 0)),
                  pl.BlockSpec((1, 1, LANES, LANES), lambda b, p, i: (b, p, 0, 0))],
        out_specs=[pl.BlockSpec((1, tb, 2 * LANES), lambda b, p, i: (b, i, p)),
                   pl.BlockSpec((1, 1, LANES, LANES), lambda b, p, i: (b, p, 0, 0))],
        out_shape=[jax.ShapeDtypeStruct((bsz, t, B_WIDTH), F32),
                   jax.ShapeDtypeStruct((bsz, npairs, LANES, LANES), F32)],
        scratch_shapes=[pltpu.VMEM((LANES, LANES), F32)],
        compiler_params=_cparams(("parallel", "parallel", "arbitrary")),
        name="gla_recurrence",
    )(z, z, z, z, z, w2pad, gate_b.reshape(1, 2 * LANES), norm_g.reshape(1, LANES), s0_t)


def _rwkv_proj_kernel(x_ref, xp_ref, xl_ref, g1_ref, mu_ref, wr_ref, wk_ref, wv_ref, ww1_ref, ww2_ref, w0_ref,
                      aw1_ref, aw2_ref, a0_ref, gw1_ref, gw2_ref, kk_ref, ka_ref,
                      r_out, k_out, v_out, lw_out, a_out, kk_out, g_out, hl_out, *, tiles_per_seq):
    i = pl.program_id(0)
    g1 = g1_ref[...]
    hn = _rms(x_ref[...], g1)
    tm = hn.shape[0]
    prev = _rms(xp_ref[...], g1)[SUBLANES - 1:SUBLANES, :]
    prev = jnp.where(i % tiles_per_seq == 0, xl_ref[0], prev)
    row = lax.broadcasted_iota(I32, (tm, 1), 0)
    xprev = jnp.where(row == 0, prev, pltpu.roll(hn, 1, axis=0))
    dx = xprev - hn

    def mix(j):
        return hn + dx * mu_ref[j:j + 1, :]

    r = _dot(mix(0).astype(BF16), wr_ref[...])
    k = _dot(mix(1).astype(BF16), wk_ref[...])
    v = _dot(mix(2).astype(BF16), wv_ref[...])
    wl = _dot(jnp.tanh(_dot(mix(3), ww1_ref[...], precision=HI)), ww2_ref[...], precision=HI)
    z = w0_ref[...] + wl
    wpre = -(jnp.maximum(-z, 0.0) + jnp.log(1.0 + jnp.exp(-jnp.abs(z)))) - 0.5
    al = _dot(_dot(mix(4), aw1_ref[...], precision=HI), aw2_ref[...], precision=HI)
    a = _sigmoid(a0_ref[...] + al)
    gg = _dot(_sigmoid(_dot(mix(5).astype(BF16), gw1_ref[...])).astype(BF16), gw2_ref[...])
    r_out[...] = r
    k_out[...] = k * (1.0 + (a - 1.0) * ka_ref[...])
    v_out[...] = v
    lw_out[...] = -jnp.exp(wpre)
    a_out[...] = a
    kk_out[...] = k * kk_ref[...]
    g_out[...] = gg
    hl_out[0] = hn[tm - 1:tm, :]


def rwkv_proj(x, x_last, seq_len, g1, mu, wr, wk, wv, ww1, ww2, w0, aw1, aw2, a0, gw1, gw2, k_k, k_a, tm):
    n, d = x.shape
    tiles_per_seq = seq_len // tm
    row = lambda a: a.reshape(1, d)
    full = lambda a: pl.BlockSpec(a.shape, lambda i: (0,) * a.ndim)
    tile = pl.BlockSpec((tm, d), lambda i: (i, 0))
    blocks8 = tm // SUBLANES
    args = (x, x, x_last.reshape(-1, 1, d), row(g1), mu, wr, wk, wv, ww1, ww2, row(w0), aw1, aw2, row(a0),
            gw1, gw2, row(k_k), row(k_a))
    in_specs = [tile,
                pl.BlockSpec((SUBLANES, d), lambda i: (jnp.maximum(i * blocks8 - 1, 0), 0)),
                pl.BlockSpec((1, 1, d), lambda i: (i // tiles_per_seq, 0, 0))]
    in_specs += [full(a) for a in args[3:]]
    outs = pl.pallas_call(
        functools.partial(_rwkv_proj_kernel, tiles_per_seq=tiles_per_seq),
        grid=(n // tm,),
        in_specs=in_specs,
        out_specs=[tile] * 7 + [pl.BlockSpec((1, 1, d), lambda i: (i, 0, 0))],
        out_shape=[jax.ShapeDtypeStruct((n, d), F32)] * 7 + [jax.ShapeDtypeStruct((n // tm, 1, d), F32)],
        compiler_params=_cparams(("parallel",)),
        name="rwkv_proj",
    )(*args)
    return outs


_NN = ((1,), (0,))
_NT = ((1,), (1,))
_TN = ((0,), (0,))
RWKV_AB_PASSES = 1
RWKV_INV_PASSES = 1
RWKV_APPLY_PASSES = 1
RWKV_STATE_PASSES = 3
RWKV_SEQS_PER_STEP = 4


def _split_bf16(a):
    hi = a.astype(BF16)
    return hi, (a - hi.astype(F32)).astype(BF16)


def _mm(a, b, dims, passes):
    if passes == 6:
        return lax.dot_general(a, b, (dims, ((), ())), preferred_element_type=F32, precision=HI)
    dg = lambda x, y: lax.dot_general(x, y, (dims, ((), ())), preferred_element_type=F32)
    ah, al = _split_bf16(a)
    bh, bl = _split_bf16(b)
    if passes == 1:
        return dg(ah, bh)
    return dg(ah, bh) + (dg(al, bh) + dg(ah, bl))


def _cumsum_rows3(g):
    tri = _tri(g.shape[0], False).astype(BF16)
    h1 = g.astype(BF16)
    r1 = g - h1.astype(F32)
    h2 = r1.astype(BF16)
    h3 = (r1 - h2.astype(F32)).astype(BF16)
    return _dot(tri, h1) + (_dot(tri, h2) + _dot(tri, h3))


def _pair_sum(x, m0):
    s0 = jnp.sum(jnp.where(m0, x, 0.0), axis=-1, keepdims=True)
    s1 = jnp.sum(jnp.where(m0, 0.0, x), axis=-1, keepdims=True)
    return jnp.where(m0, s0, s1)


def _rwkv_kernel(r_ref, k_ref, v_ref, lw_ref, a_ref, kk_ref, g_ref, rk_ref, lng_ref, lnb_ref, s0_ref,
                 o_ref, s_ref, mt_scr, *, chunk):
    t = pl.program_id(2)

    @pl.when(t == 0)
    def _():
        mt_scr[...] = s0_ref[:, 0]

    nrows = r_ref.shape[0]
    nchunks = r_ref.shape[1] // chunk
    c2 = 2 * chunk
    m0 = _lane_mask(LANES, 0, C_HEAD)
    rowi = lax.broadcasted_iota(I32, (LANES, LANES), 0)
    coli = lax.broadcasted_iota(I32, (LANES, LANES), 1)
    blockdiag = (rowi < C_HEAD) == (coli < C_HEAD)
    ti = lax.broadcasted_iota(I32, (c2, c2), 0)
    si = lax.broadcasted_iota(I32, (c2, c2), 1)
    same_head = (ti < chunk) == (si < chunk)
    tm_ = jnp.where(ti < chunk, ti, ti - chunk)
    sm_ = jnp.where(si < chunk, si, si - chunk)
    strict = same_head & (sm_ < tm_)
    incl = same_head & (sm_ <= tm_)

    def stack_heads(x):
        return jnp.concatenate([jnp.where(m0, x, 0.0), jnp.where(m0, 0.0, x)], axis=0)

    def twice(x):
        return jnp.concatenate([x, x], axis=0)

    def unstack(x2):
        return jnp.where(m0, x2[:chunk], x2[chunk:])

    eye = (ti == si).astype(F32)
    seqs = range(nrows)

    def body(ci, carry):
        sl = pl.ds(pl.multiple_of(ci * chunk, chunk), chunk)
        r = [r_ref[i, sl, :] for i in seqs]
        k = [k_ref[i, sl, :] for i in seqs]
        v = [v_ref[i, sl, :] for i in seqs]
        lw = [lw_ref[i, sl, :] for i in seqs]
        kkr = [kk_ref[i, sl, :] for i in seqs]
        kk = [x * lax.rsqrt(_pair_sum(x * x, m0) + 1e-12) for x in kkr]
        al = [a_ref[i, sl, :] * kk[i] for i in seqs]
        gam = [_cumsum_rows3(x) for x in lw]
        e_neg = [jnp.exp(-x) for x in gam]
        xr = [jnp.concatenate([stack_heads(kk[i] * jnp.exp(gam[i] - lw[i])),
                               stack_heads(r[i] * jnp.exp(gam[i]))], axis=0) for i in seqs]
        alk = [jnp.concatenate([twice(al[i] * e_neg[i]), twice(k[i] * e_neg[i])], axis=0) for i in seqs]
        mt = [mt_scr[i] for i in seqs]
        ab = [_mm(xr[i], alk[i], _NT, RWKV_AB_PASSES) for i in seqs]
        xm = [_mm(xr[i], mt[i], _NT, RWKV_STATE_PASSES) for i in seqs]
        a_al = [jnp.where(strict, x[:c2, :c2], 0.0) for x in ab]
        a_k = [jnp.where(strict, x[:c2, c2:], 0.0) for x in ab]
        b_alk = [jnp.concatenate([jnp.where(incl, x[c2:, c2:], 0.0), jnp.where(incl, -x[c2:, :c2], 0.0)], axis=1)
                 for x in ab]
        v2 = [twice(x) for x in v]
        rhs = [xm[i][:c2] + _mm(a_k[i], v2[i], _NN, RWKV_APPLY_PASSES) for i in seqs]
        p = [-x for x in a_al]
        tinv = [eye + x for x in p]
        span = 2
        while span < chunk:
            p = [_mm(x, x, _NN, RWKV_INV_PASSES) for x in p]
            tinv = [tinv[i] + _mm(tinv[i], p[i], _NN, RWKV_INV_PASSES) for i in seqs]
            span *= 2
        u = [unstack(_mm(tinv[i], rhs[i], _NN, RWKV_APPLY_PASSES)) for i in seqs]
        o = [unstack(xm[i][c2:] + _mm(b_alk[i], jnp.concatenate([v2[i], twice(u[i])], axis=0), _NN,
                                      RWKV_APPLY_PASSES)) for i in seqs]
        g_last = [x[-1:, :] for x in gam]
        e_end = [jnp.exp(g_last[i] - gam[i]) for i in seqs]
        upd = [_mm(jnp.concatenate([v[i], u[i]], axis=0),
                   jnp.concatenate([k[i] * e_end[i], -(al[i] * e_end[i])], axis=0), _TN, RWKV_STATE_PASSES)
               for i in seqs]
        for i in seqs:
            mt_scr[i] = mt[i] * jnp.exp(g_last[i]) + jnp.where(blockdiag, upd[i], 0.0)
        for i in seqs:
            mean = _pair_sum(o[i], m0) * (1.0 / C_HEAD)
            cen = o[i] - mean
            var = _pair_sum(cen * cen, m0) * (1.0 / C_HEAD)
            on = cen * lax.rsqrt(var + C_GN_EPS) * lng_ref[...] + lnb_ref[...]
            bonus = _pair_sum(r[i] * k[i] * rk_ref[...], m0) * v[i]
            o_ref[i, sl, :] = (on + bonus) * g_ref[i, sl, :]
        return carry

    lax.fori_loop(0, nchunks, body, 0)

    @pl.when(t == pl.num_programs(2) - 1)
    def _():
        s_ref[:, 0] = mt_scr[...]


def rwkv_recurrence(r, k, v, lw, a, kk, g, r_k, ln_g, ln_b, s0_bd, tb, chunk, nb):
    bsz, t, d = r.shape
    npairs = C_HEADS // 2
    seq = pl.BlockSpec((nb, tb, LANES), lambda b, p, i: (b, i, p))
    vec = pl.BlockSpec((1, LANES), lambda b, p, i: (0, p))
    st = pl.BlockSpec((nb, 1, LANES, LANES), lambda b, p, i: (b, p, 0, 0))
    return pl.pallas_call(
        functools.partial(_rwkv_kernel, chunk=chunk),
        grid=(bsz // nb, npairs, t // tb),
        in_specs=[seq] * 7 + [vec, vec, vec, st],
        out_specs=[seq, st],
        out_shape=[jax.ShapeDtypeStruct((bsz, t, d), F32),
                   jax.ShapeDtypeStruct((bsz, npairs, LANES, LANES), F32)],
        scratch_shapes=[pltpu.VMEM((nb, LANES, LANES), F32)],
        compiler_params=_cparams(("parallel", "parallel", "arbitrary")),
        name="rwkv_recurrence",
    )(r, k, v, lw, a, kk, g, r_k.reshape(1, d), ln_g.reshape(1, d), ln_b.reshape(1, d), s0_bd)


NEG_INF = float("-inf")


def _top16_rows(s):
    n = s.shape[0]
    key = lax.broadcasted_iota(I32, s.shape, 0)
    vals, idxs = [], []
    for _ in range(P_TOPK):
        m = jnp.max(s, axis=0, keepdims=True)
        am = jnp.min(jnp.where(s == m, key, n), axis=0, keepdims=True)
        vals.append(m)
        idxs.append(am)
        s = jnp.where(key == am, NEG_INF, s)
    return vals, idxs


def _top16_pairs(v0, i0, v1, i1):
    a0 = jnp.concatenate(v0[0:8], axis=0)
    a1 = jnp.concatenate(v0[8:16], axis=0)
    b0 = jnp.concatenate(v1[0:8], axis=0)
    b1 = jnp.concatenate(v1[8:16], axis=0)
    ia0 = jnp.concatenate(i0[0:8], axis=0) * P_KEYS
    ia1 = jnp.concatenate(i0[8:16], axis=0) * P_KEYS
    ib0 = jnp.concatenate(i1[0:8], axis=0)
    ib1 = jnp.concatenate(i1[8:16], axis=0)
    row = lax.broadcasted_iota(I32, (SUBLANES, 1), 0)
    slabs = []

    def add(val, eid, keep):
        slabs.append((val if keep is None else jnp.where(keep, val, NEG_INF), eid))

    add(v0[0] + b0, ia0[0:1] + ib0, None)
    add(v0[0] + b1, ia0[0:1] + ib1, None)
    add(v0[1] + b0, ia0[1:2] + ib0, None)
    add(v0[2] + b0, ia0[2:3] + ib0, row < 5)
    add(v0[3] + b0, ia0[3:4] + ib0, row < 4)
    add(a0 + v1[0], ia0 + ib0[0:1], row >= 4)
    add(a1 + v1[0], ia1 + ib0[0:1], None)
    add(a0 + v1[1], ia0 + ib0[1:2], row >= 4)
    add(a0 + v1[2], ia0 + ib0[2:3], row == 4)

    big = P_KEYS * P_KEYS
    out_v, out_e = [], []
    for _ in range(P_TOPK):
        m = slabs[0][0]
        for val, _e in slabs[1:]:
            m = jnp.maximum(m, val)
        m = jnp.max(m, axis=0, keepdims=True)
        e = None
        for val, eid in slabs:
            c = jnp.where(val == m, eid, big)
            e = c if e is None else jnp.minimum(e, c)
        e = jnp.min(e, axis=0, keepdims=True)
        out_v.append(m)
        out_e.append(e)
        slabs = [(jnp.where(eid == e, NEG_INF, val), eid) for val, eid in slabs]
    return out_v, out_e


def _peer_select_kernel(x_ref, g_ref, wq_ref, keys_ref, xn_out, eid_out, gate_out):
    hn = _rms(x_ref[...], g_ref[...])
    xn_out[...] = hn
    q = _dot(hn.astype(BF16), wq_ref[...])
    tm = q.shape[0]
    for lt in range(tm // LANES):
        rows = slice(lt * LANES, (lt + 1) * LANES)
        e_rows, g_rows = [], []
        for h in range(P_HEADS):
            tops = []
            for p in range(2):
                hp = 2 * h + p
                s = _dot_nt(keys_ref[hp], q[rows, hp * LANES:(hp + 1) * LANES], precision=HI)
                tops.append(_top16_rows(s))
            cs, ce = _top16_pairs(tops[0][0], tops[0][1], tops[1][0], tops[1][1])
            ex = [jnp.exp(c - cs[0]) for c in cs]
            tot = ex[0]
            for e in ex[1:]:
                tot = tot + e
            inv = 1.0 / tot
            e_rows += ce
            g_rows += [e * inv for e in ex]
        eid_out[rows, :] = jnp.concatenate(e_rows, axis=0).T
        gate_out[rows, :] = jnp.concatenate(g_rows, axis=0).T


def peer_select(x, g, wq_bf16, keys, tm):
    n, d = x.shape
    return pl.pallas_call(
        _peer_select_kernel,
        grid=(n // tm,),
        in_specs=[pl.BlockSpec((tm, d), lambda i: (i, 0)),
                  pl.BlockSpec((1, d), lambda i: (0, 0)),
                  pl.BlockSpec(wq_bf16.shape, lambda i: (0, 0)),
                  pl.BlockSpec(keys.shape, lambda i: (0, 0, 0))],
        out_specs=[pl.BlockSpec((tm, d), lambda i: (i, 0)),
                   pl.BlockSpec((tm, P_PICKS), lambda i: (i, 0)),
                   pl.BlockSpec((tm, P_PICKS), lambda i: (i, 0))],
        out_shape=[jax.ShapeDtypeStruct((n, d), F32),
                   jax.ShapeDtypeStruct((n, P_PICKS), I32),
                   jax.ShapeDtypeStruct((n, P_PICKS), F32)],
        compiler_params=_cparams(("parallel",)),
        name="peer_select",
    )(x, g.reshape(1, d), wq_bf16, keys)


def _peer_act_kernel(h_ref, g_ref, o_ref):
    h = h_ref[...]
    o_ref[...] = 0.5 * h * (1.0 + lax.erf(h * (2.0 ** -0.5))) * g_ref[...]


def peer_act(hid, gate, tm):
    n, p = hid.shape
    spec = pl.BlockSpec((tm, p), lambda i: (i, 0))
    return pl.pallas_call(
        _peer_act_kernel, grid=(n // tm,), in_specs=[spec, spec], out_specs=spec,
        out_shape=jax.ShapeDtypeStruct((n, p), F32),
        compiler_params=_cparams(("parallel",)), name="peer_act",
    )(hid, gate)


_CHUNKS_PER_TOK = P_PICKS // SC_CHUNK
_WORD_VREGS = SC_WORDS // SC_LANES
_HIGH_HALF = -65536


def pack_rows_bf16(t):
    b = lax.bitcast_convert_type(t.astype(BF16), jnp.uint16).astype(jnp.uint32)
    w = b[..., :SC_WORDS] | (b[..., SC_WORDS:] << 16)
    return lax.bitcast_convert_type(w, I32)


def _low_f32(w):
    return lax.bitcast_convert_type(w << 16, F32)


def _high_f32(w):
    return lax.bitcast_convert_type(w & _HIGH_HALF, F32)


def _sc_worker():
    return lax.axis_index("s") * SC_CORES + lax.axis_index("c")


def _sc_hid_body(tab_hbm, idx_hbm, x_hbm, hid_hbm, idx_v, x_v, hid_v, part_v, rows_a, rows_b, sem_a, sem_b,
                 *, tok_per_w):
    wid = _sc_worker()
    nblk = tok_per_w // SC_TOK_BLOCK
    nch = SC_TOK_BLOCK * _CHUNKS_PER_TOK
    lane = lax.iota(I32, SC_LANES)
    half_w = _WORD_VREGS // 2

    def dots(rows, x_row, res_ref, col0):
        for xp in range(2):
            base = xp * half_w * SC_LANES
            xlo = [x_v[x_row, pl.ds(base + j * SC_LANES, SC_LANES)] for j in range(half_w)]
            xhi = [x_v[x_row, pl.ds(SC_WORDS + base + j * SC_LANES, SC_LANES)] for j in range(half_w)]

            def partial_dot(r):
                accs = []
                for j in range(half_w):
                    w = rows[r, pl.ds(base + j * SC_LANES, SC_LANES)]
                    term = _low_f32(w) * xlo[j] + _high_f32(w) * xhi[j]
                    if j < 4:
                        accs.append(term)
                    else:
                        accs[j % 4] = accs[j % 4] + term
                return (accs[0] + accs[1]) + (accs[2] + accs[3])

            if xp == 0:
                def first(r, c):
                    part_v[r, :] = partial_dot(r)
                    return c
                lax.fori_loop(0, SC_CHUNK, first, 0)
            else:
                for grp in range(SC_CHUNK // SC_LANES):
                    def second(rr, res):
                        r = grp * SC_LANES + rr
                        return jnp.where(lane == rr, jnp.sum(partial_dot(r) + part_v[r, :]), res)
                    res = lax.fori_loop(0, SC_LANES, second, jnp.zeros((SC_LANES,), F32))
                    res_ref[x_row, pl.ds(col0 + grp * SC_LANES, SC_LANES)] = res

    def block(bi, carry):
        tok0 = wid * tok_per_w + bi * SC_TOK_BLOCK
        pltpu.sync_copy(idx_hbm.at[pl.ds(tok0 * _CHUNKS_PER_TOK, nch)], idx_v)
        pltpu.sync_copy(x_hbm.at[pl.ds(tok0, SC_TOK_BLOCK)], x_v)
        pltpu.async_copy(tab_hbm.at[idx_v.at[0]], rows_a, sem_a)

        def pair(j, c):
            ca = 2 * j
            pltpu.async_copy(tab_hbm.at[idx_v.at[ca + 1]], rows_b, sem_b)
            pltpu.make_async_copy(tab_hbm.at[idx_v.at[0]], rows_a, sem_a).wait()
            dots(rows_a, ca // _CHUNKS_PER_TOK, hid_v, (ca % _CHUNKS_PER_TOK) * SC_CHUNK)

            @pl.when(j < nch // 2 - 1)
            def _():
                pltpu.async_copy(tab_hbm.at[idx_v.at[ca + 2]], rows_a, sem_a)
            pltpu.make_async_copy(tab_hbm.at[idx_v.at[0]], rows_b, sem_b).wait()
            cb = ca + 1
            dots(rows_b, cb // _CHUNKS_PER_TOK, hid_v, (cb % _CHUNKS_PER_TOK) * SC_CHUNK)
            return c
        lax.fori_loop(0, nch // 2, pair, 0)
        pltpu.sync_copy(hid_v, hid_hbm.at[pl.ds(tok0, SC_TOK_BLOCK)])
        return carry

    lax.fori_loop(0, nblk, block, 0)


def sc_expert_hidden(table_packed, eidx, xn):
    n = xn.shape[0]
    tok_per_w = n // SC_WORKERS
    mesh = plsc.VectorSubcoreMesh(core_axis_name="c", subcore_axis_name="s")
    nch = SC_TOK_BLOCK * _CHUNKS_PER_TOK
    k = pl.kernel(
        functools.partial(_sc_hid_body, tok_per_w=tok_per_w), mesh=mesh,
        out_type=jax.ShapeDtypeStruct((n, P_PICKS), F32),
        scratch_types=[pltpu.VMEM((nch, SC_CHUNK), I32),
                       pltpu.VMEM((SC_TOK_BLOCK, D_MODEL), F32),
                       pltpu.VMEM((SC_TOK_BLOCK, P_PICKS), F32),
                       pltpu.VMEM((SC_CHUNK, SC_LANES), F32),
                       pltpu.VMEM((SC_CHUNK, SC_WORDS), I32),
                       pltpu.VMEM((SC_CHUNK, SC_WORDS), I32),
                       pltpu.SemaphoreType.DMA, pltpu.SemaphoreType.DMA],
        compiler_params=pltpu.CompilerParams(needs_layout_passes=False),
        name="sc_expert_hidden",
    )
    return k(table_packed, eidx.reshape(n * _CHUNKS_PER_TOK, SC_CHUNK), xn)


def _sc_out_body(tab_hbm, idx_hbm, act_hbm, x_hbm, y_hbm, idx_v, act_v, y_v, rows_a, rows_b, sem_a, sem_b,
                 *, tok_per_w):
    wid = _sc_worker()
    nblk = tok_per_w // SC_TOK_BLOCK
    nch = SC_TOK_BLOCK * _CHUNKS_PER_TOK
    half_w = _WORD_VREGS // 2

    def accum(rows, tok, col0):
        for hv in range(2):
            base = hv * half_w * SC_LANES
            acc0 = (tuple(y_v[tok, pl.ds(base + j * SC_LANES, SC_LANES)] for j in range(half_w))
                    + tuple(y_v[tok, pl.ds(SC_WORDS + base + j * SC_LANES, SC_LANES)] for j in range(half_w)))

            def one(r, acc):
                aidx = jnp.full((SC_LANES,), col0, I32) + r
                wgt = plsc.load_gather(act_v, [jnp.full((SC_LANES,), tok, I32), aidx])
                lo, hi = [], []
                for j in range(half_w):
                    w = rows[r, pl.ds(base + j * SC_LANES, SC_LANES)]
                    lo.append(acc[j] + wgt * _low_f32(w))
                    hi.append(acc[half_w + j] + wgt * _high_f32(w))
                return tuple(lo + hi)
            acc = lax.fori_loop(0, SC_CHUNK, one, acc0)
            for j in range(half_w):
                y_v[tok, pl.ds(base + j * SC_LANES, SC_LANES)] = acc[j]
                y_v[tok, pl.ds(SC_WORDS + base + j * SC_LANES, SC_LANES)] = acc[half_w + j]

    def block(bi, carry):
        tok0 = wid * tok_per_w + bi * SC_TOK_BLOCK
        pltpu.sync_copy(idx_hbm.at[pl.ds(tok0 * _CHUNKS_PER_TOK, nch)], idx_v)
        pltpu.sync_copy(act_hbm.at[pl.ds(tok0, SC_TOK_BLOCK)], act_v)
        pltpu.sync_copy(x_hbm.at[pl.ds(tok0, SC_TOK_BLOCK)], y_v)
        pltpu.async_copy(tab_hbm.at[idx_v.at[0]], rows_a, sem_a)

        def pair(j, c):
            ca = 2 * j
            pltpu.async_copy(tab_hbm.at[idx_v.at[ca + 1]], rows_b, sem_b)
            pltpu.make_async_copy(tab_hbm.at[idx_v.at[0]], rows_a, sem_a).wait()
            accum(rows_a, ca // _CHUNKS_PER_TOK, (ca % _CHUNKS_PER_TOK) * SC_CHUNK)

            @pl.when(j < nch // 2 - 1)
            def _():
                pltpu.async_copy(tab_hbm.at[idx_v.at[ca + 2]], rows_a, sem_a)
            pltpu.make_async_copy(tab_hbm.at[idx_v.at[0]], rows_b, sem_b).wait()
            cb = ca + 1
            accum(rows_b, cb // _CHUNKS_PER_TOK, (cb % _CHUNKS_PER_TOK) * SC_CHUNK)
            return c
        lax.fori_loop(0, nch // 2, pair, 0)
        pltpu.sync_copy(y_v, y_hbm.at[pl.ds(tok0, SC_TOK_BLOCK)])
        return carry

    lax.fori_loop(0, nblk, block, 0)


def sc_expert_output(table_packed, eidx, act, x):
    n = x.shape[0]
    tok_per_w = n // SC_WORKERS
    mesh = plsc.VectorSubcoreMesh(core_axis_name="c", subcore_axis_name="s")
    nch = SC_TOK_BLOCK * _CHUNKS_PER_TOK
    k = pl.kernel(
        functools.partial(_sc_out_body, tok_per_w=tok_per_w), mesh=mesh,
        out_type=jax.ShapeDtypeStruct((n, D_MODEL), F32),
        scratch_types=[pltpu.VMEM((nch, SC_CHUNK), I32),
                       pltpu.VMEM((SC_TOK_BLOCK, P_PICKS), F32),
                       pltpu.VMEM((SC_TOK_BLOCK, D_MODEL), F32),
                       pltpu.VMEM((SC_CHUNK, SC_WORDS), I32),
                       pltpu.VMEM((SC_CHUNK, SC_WORDS), I32),
                       pltpu.SemaphoreType.DMA, pltpu.SemaphoreType.DMA],
        compiler_params=pltpu.CompilerParams(needs_layout_passes=False),
        name="sc_expert_output",
    )
    return k(table_packed, eidx.reshape(n * _CHUNKS_PER_TOK, SC_CHUNK), act, x)


def peer_ffn(x, g, wq_bf16, keys, u_packed, v_packed, tm):
    xn, eidx, gate = peer_select(x, g, wq_bf16, keys, tm)
    hid = sc_expert_hidden(u_packed, eidx, xn)
    act = peer_act(hid, gate, tm)
    return sc_expert_output(v_packed, eidx, act, x)


def _prep_w_in(w_in):
    main = jnp.concatenate([w_in[:, :3072], w_in[:, 3088:3600]], axis=1)
    lr = jnp.pad(w_in[:, 3072:3088], ((0, 0), (0, LANES - GLA_RANK)))
    return jnp.concatenate([main, lr], axis=1).astype(BF16)


def _trunk(x, st_h, st_g, st_r, st_s, w, seq_len, tm, tmp, tb, chunk):
    bsz = x.shape[0]
    n = bsz * seq_len
    x2 = x.reshape(n, D_MODEL)

    z = norm_proj(x2, w["norm1_g"][0], w["w_in"], tm).reshape(bsz, seq_len, Z_WIDTH)
    s0_h = jnp.swapaxes(st_h, -1, -2)
    s0_g = jnp.swapaxes(st_g.reshape(bsz, 2, 2 * B_DK, LANES), -1, -2)
    o_a, sh_t = hgrn_recurrence(z, w["lb0"], w["hgrn_norm_g"], s0_h, tb, chunk)
    o_b, sg_t = gla_recurrence(z, w["gla_w2"], w["gla_b"], w["gla_norm_g"], s0_g, tb, chunk)
    new_h = jnp.swapaxes(sh_t, -1, -2)
    new_g = jnp.swapaxes(sg_t, -1, -2).reshape(bsz, B_HEADS, B_DK, LANES)
    x2 = out_proj2(x2, o_a.reshape(n, A_WIDTH), o_b.reshape(n, B_WIDTH), w["w_out_a"], w["w_out_b"], tm)
    x2 = peer_ffn(x2, w["norm2_g"][0], w["peer_wq"][0], w["peer_keys"][0], w["peer_u"][0], w["peer_v"][0], tmp)

    r, k, v, lw, a, kk, g, hl = rwkv_proj(
        x2, st_s, seq_len, w["norm1_g"][1], w["mu"], w["wr"], w["wk"], w["wv"], w["w_w1"], w["w_w2"], w["w0"],
        w["a_w1"], w["a_w2"], w["a0"], w["g_w1"], w["g_w2"], w["k_k"], w["k_a"], tm)
    new_s = hl.reshape(bsz, seq_len // tm, D_MODEL)[:, -1]
    pr = st_r.reshape(bsz, C_HEADS // 2, 2, C_HEAD, C_HEAD)
    zero = jnp.zeros_like(pr[:, :, 0])
    s0_r = jnp.concatenate([jnp.concatenate([pr[:, :, 0], zero], axis=-1),
                            jnp.concatenate([zero, pr[:, :, 1]], axis=-1)], axis=-2)
    sh3 = lambda t: t.reshape(bsz, seq_len, D_MODEL)
    o_c, sr_bd = rwkv_recurrence(sh3(r), sh3(k), sh3(v), sh3(lw), sh3(a), sh3(kk), sh3(g),
                                 w["r_k"], w["ln_g"], w["ln_b"], s0_r, tb, chunk, RWKV_SEQS_PER_STEP)
    new_r = jnp.stack([sr_bd[:, :, :C_HEAD, :C_HEAD], sr_bd[:, :, C_HEAD:, C_HEAD:]], axis=2)
    new_r = new_r.reshape(bsz, C_HEADS, C_HEAD, C_HEAD)
    x2 = out_proj1(x2, o_c.reshape(n, D_MODEL), w["w_out_c"], tm)
    x2 = peer_ffn(x2, w["norm2_g"][1], w["peer_wq"][1], w["peer_keys"][1], w["peer_u"][1], w["peer_v"][1], tmp)

    y = final_norm(x2, w["final_g"], tm).reshape(bsz, seq_len, D_MODEL)
    return y, new_h[None], new_g[None], new_r[None], new_s[None]


def kernel(x_prompt, x_sample, state_hgrn, state_gla, state_rwkv, state_shift, w_in_ab, hgrn_lower_bounds, hgrn_norm_g, gla_gate_w2, gla_gate_b, gla_norm_g, w_out_ab, rwkv_mu, rwkv_w_rkv, rwkv_w_w1, rwkv_w_w2, rwkv_w0, rwkv_a_w1, rwkv_a_w2, rwkv_a0, rwkv_g_w1, rwkv_g_w2, rwkv_k_k, rwkv_k_a, rwkv_r_k, rwkv_ln_g, rwkv_ln_b, w_out_c, norm1_g, norm2_g, final_g, peer_w_q, peer_sub_keys, peer_u, peer_v):
    lbs = jnp.cumsum(jax.nn.softmax(hgrn_lower_bounds.astype(F32), axis=0), axis=0)
    w = dict(
        norm1_g=norm1_g, norm2_g=norm2_g, final_g=final_g,
        w_in=_prep_w_in(w_in_ab[0]), lb0=lbs[0], hgrn_norm_g=hgrn_norm_g[0],
        gla_w2=jnp.pad(gla_gate_w2[0], ((0, LANES - GLA_RANK), (0, 0))), gla_b=gla_gate_b[0],
        gla_norm_g=gla_norm_g[0],
        w_out_a=w_out_ab[0, :A_WIDTH].astype(BF16), w_out_b=w_out_ab[0, A_WIDTH:].astype(BF16),
        mu=rwkv_mu[0], wr=rwkv_w_rkv[0, 0].astype(BF16), wk=rwkv_w_rkv[0, 1].astype(BF16),
        wv=rwkv_w_rkv[0, 2].astype(BF16), w_w1=rwkv_w_w1[0], w_w2=rwkv_w_w2[0], w0=rwkv_w0[0],
        a_w1=rwkv_a_w1[0], a_w2=rwkv_a_w2[0], a0=rwkv_a0[0],
        g_w1=rwkv_g_w1[0].astype(BF16), g_w2=rwkv_g_w2[0].astype(BF16),
        k_k=rwkv_k_k[0], k_a=rwkv_k_a[0], r_k=rwkv_r_k[0], ln_g=rwkv_ln_g[0], ln_b=rwkv_ln_b[0],
        w_out_c=w_out_c[0].astype(BF16),
        peer_wq=peer_w_q.astype(BF16),
        peer_keys=peer_sub_keys.reshape(peer_sub_keys.shape[0], 2 * P_HEADS, P_KEYS, P_KEYS),
        peer_u=pack_rows_bf16(peer_u), peer_v=pack_rows_bf16(peer_v),
    )
    bp, tp, _ = x_prompt.shape
    bs, ts, _ = x_sample.shape
    gsz = bp // PROMPT_GROUPS
    zeros = lambda s: jnp.zeros((gsz,) + s.shape[2:], F32)
    groups = [_trunk(x_prompt[i * gsz:(i + 1) * gsz], zeros(state_hgrn), zeros(state_gla), zeros(state_rwkv),
                     zeros(state_shift), w, tp, tm=256, tmp=256, tb=512, chunk=64)
              for i in range(PROMPT_GROUPS)]
    y_p = jnp.concatenate([g[0] for g in groups], axis=0)
    p_h, p_g, p_r, p_s = (jnp.concatenate([g[j] for g in groups], axis=1) for j in range(1, 5))
    y_s, s_h, s_g, s_r, s_s = _trunk(x_sample, state_hgrn[0], state_gla[0], state_rwkv[0], state_shift[0],
                                     w, ts, tm=32, tmp=128, tb=32, chunk=32)
    return (y_p, y_s, p_h, p_g, p_r, p_s, s_h, s_g, s_r, s_s)
```

```python
import functools

import jax
import jax.numpy as jnp
from jax import lax
from jax.experimental import pallas as pl
from jax.experimental.pallas import tpu as pltpu
from jax.experimental.pallas import tpu_sc as plsc

F32 = jnp.float32
BF16 = jnp.bfloat16
I32 = jnp.int32
HI = lax.Precision.HIGHEST

D_MODEL = 1024
NORM_EPS = 1e-6
LANES = 128
SUBLANES = 8
VMEM_LIMIT = 56 * 1024 * 1024

A_WIDTH = 512
A_HEADS = 4
B_WIDTH = 512
B_HEADS = 4
B_DK = 64
GLA_RANK = 16
GLA_NORMALIZER = 16.0
Z_WIDTH = 3712
C_HEAD = 64
C_HEADS = 16
C_GN_EPS = 64e-5
P_HEADS = 8
P_KEYS = 128
P_TOPK = 16
P_PICKS = P_HEADS * P_TOPK
SC_CORES = 2
SC_SUBCORES = 16
SC_WORKERS = SC_CORES * SC_SUBCORES
SC_LANES = 16
SC_CHUNK = 64
SC_TOK_BLOCK = 16
SC_WORDS = D_MODEL // 2
PROMPT_GROUPS = (2, 2, 4, 4, 4)


def _cparams(sem):
    return pltpu.CompilerParams(dimension_semantics=sem, vmem_limit_bytes=VMEM_LIMIT)


def _rms(x, g):
    ms = jnp.mean(x * x, axis=-1, keepdims=True)
    return x * lax.rsqrt(ms + NORM_EPS) * g


def _dot(a, b, precision=None):
    return jnp.dot(a, b, preferred_element_type=F32, precision=precision)


def _dot_nt(a, b, precision=None):
    return lax.dot_general(a, b, (((1,), (1,)), ((), ())), preferred_element_type=F32, precision=precision)


def _dot_tn(a, b, precision=None):
    return lax.dot_general(a, b, (((0,), (0,)), ((), ())), preferred_element_type=F32, precision=precision)


def _tri(n, strict):
    r = lax.broadcasted_iota(I32, (n, n), 0)
    c = lax.broadcasted_iota(I32, (n, n), 1)
    return (c < r) if strict else (c <= r)


def _cumsum_rows(g):
    return _dot(_tri(g.shape[0], False).astype(F32), g, precision=HI)


def _lane_mask(width, lo, hi):
    l = lax.broadcasted_iota(I32, (1, width), 1)
    return (l >= lo) & (l < hi)


def _sigmoid(x):
    return 1.0 / (1.0 + jnp.exp(-x))


def _silu(x):
    return x * _sigmoid(x)


def _norm_proj_kernel(x_ref, g_ref, w_ref, o_ref):
    hn = _rms(x_ref[...], g_ref[...])
    o_ref[...] = _dot(hn.astype(BF16), w_ref[...])


def norm_proj(x, g, w_bf16, tm):
    n, d = x.shape
    f = w_bf16.shape[1]
    return pl.pallas_call(
        _norm_proj_kernel,
        grid=(n // tm,),
        in_specs=[pl.BlockSpec((tm, d), lambda i: (i, 0)),
                  pl.BlockSpec((1, d), lambda i: (0, 0)),
                  pl.BlockSpec((d, f), lambda i: (0, 0))],
        out_specs=pl.BlockSpec((tm, f), lambda i: (i, 0)),
        out_shape=jax.ShapeDtypeStruct((n, f), F32),
        compiler_params=_cparams(("parallel",)),
        name="norm_proj",
    )(x, g.reshape(1, d), w_bf16)


def _out_proj2_kernel(x_ref, a_ref, b_ref, wa_ref, wb_ref, o_ref):
    y = _dot(a_ref[...].astype(BF16), wa_ref[...]) + _dot(b_ref[...].astype(BF16), wb_ref[...])
    o_ref[...] = x_ref[...] + y


def out_proj2(x, a, b, wa, wb, tm):
    n, d = x.shape
    ka, kb = a.shape[1], b.shape[1]
    return pl.pallas_call(
        _out_proj2_kernel,
        grid=(n // tm,),
        in_specs=[pl.BlockSpec((tm, d), lambda i: (i, 0)),
                  pl.BlockSpec((tm, ka), lambda i: (i, 0)),
                  pl.BlockSpec((tm, kb), lambda i: (i, 0)),
                  pl.BlockSpec((ka, d), lambda i: (0, 0)),
                  pl.BlockSpec((kb, d), lambda i: (0, 0))],
        out_specs=pl.BlockSpec((tm, d), lambda i: (i, 0)),
        out_shape=jax.ShapeDtypeStruct((n, d), F32),
        compiler_params=_cparams(("parallel",)),
        name="out_proj2",
    )(x, a, b, wa, wb)


def _out_proj1_kernel(x_ref, a_ref, wa_ref, o_ref):
    o_ref[...] = x_ref[...] + _dot(a_ref[...].astype(BF16), wa_ref[...])


def out_proj1(x, a, wa, tm):
    n, d = x.shape
    ka = a.shape[1]
    return pl.pallas_call(
        _out_proj1_kernel,
        grid=(n // tm,),
        in_specs=[pl.BlockSpec((tm, d), lambda i: (i, 0)),
                  pl.BlockSpec((tm, ka), lambda i: (i, 0)),
                  pl.BlockSpec((ka, d), lambda i: (0, 0))],
        out_specs=pl.BlockSpec((tm, d), lambda i: (i, 0)),
        out_shape=jax.ShapeDtypeStruct((n, d), F32),
        compiler_params=_cparams(("parallel",)),
        name="out_proj1",
    )(x, a, wa)


def _final_norm_kernel(x_ref, g_ref, o_ref):
    o_ref[...] = _rms(x_ref[...], g_ref[...])


def final_norm(x, g, tm):
    n, d = x.shape
    return pl.pallas_call(
        _final_norm_kernel,
        grid=(n // tm,),
        in_specs=[pl.BlockSpec((tm, d), lambda i: (i, 0)), pl.BlockSpec((1, d), lambda i: (0, 0))],
        out_specs=pl.BlockSpec((tm, d), lambda i: (i, 0)),
        out_shape=jax.ShapeDtypeStruct((n, d), F32),
        compiler_params=_cparams(("parallel",)),
        name="final_norm",
    )(x, g.reshape(1, d))


def _intra_chunk(q, k, b, heads):
    c = q.shape[0]
    nb = c // SUBLANES
    row = lax.broadcasted_iota(I32, (SUBLANES, 1), 0)
    qb = [q[SUBLANES * i:SUBLANES * (i + 1)] for i in range(nb)]
    bb = [b[SUBLANES * i:SUBLANES * (i + 1)] for i in range(nb)]
    outs = [[None] * nb for _ in heads]
    for s in range(c):
        rb0 = s // SUBLANES
        ks = k[s:s + 1, :]
        bs = b[s:s + 1, :]
        for rb in range(rb0, nb):
            p = qb[rb] * (ks * jnp.exp(bb[rb] - bs))
            for hi, (mask, v) in enumerate(heads):
                pm = p if mask is None else jnp.where(mask, p, 0.0)
                col = jnp.sum(pm, axis=-1, keepdims=True)
                if rb == rb0:
                    col = jnp.where(row + SUBLANES * rb >= s, col, 0.0)
                term = col * v[s:s + 1, :]
                outs[hi][rb] = term if outs[hi][rb] is None else outs[hi][rb] + term
    return [jnp.concatenate(o, axis=0) for o in outs]


def _gated_chunk(q, k, b, heads, st):
    intra = _intra_chunk(q, k, b, heads)
    qe = q * jnp.exp(b)
    b_last = b[-1:, :]
    kh = k * jnp.exp(b_last - b)
    outs = []
    for (mask, v), oi in zip(heads, intra):
        qm = qe if mask is None else jnp.where(mask, qe, 0.0)
        outs.append(oi + _dot_nt(qm, st, precision=HI))
    upd = _dot_tn(heads[0][1], kh, precision=HI)
    if len(heads) == 2:
        upd = jnp.where(heads[0][0], upd, _dot_tn(heads[1][1], kh, precision=HI))
    st = st * jnp.exp(b_last) + upd
    return outs, st


def _head_rms(o, g):
    ms = jnp.mean(o * o, axis=-1, keepdims=True)
    return o * lax.rsqrt(ms + NORM_EPS) * g


def _hgrn_kernel(zq_ref, zf_ref, zi_ref, zg_ref, lb_ref, ng_ref, s0_ref, o_ref, s_ref, st_scr, *, chunk):
    t = pl.program_id(2)

    @pl.when(t == 0)
    def _():
        st_scr[...] = s0_ref[0, 0]

    lb = lb_ref[0]
    nchunks = zq_ref.shape[1] // chunk

    def body(ci, carry):
        sl = pl.ds(pl.multiple_of(ci * chunk, chunk), chunk)
        f = lb + (1.0 - lb) * _sigmoid(zf_ref[0, sl, :])
        q = _silu(zq_ref[0, sl, :])
        b = _cumsum_rows(jnp.log(f))
        (o,), st = _gated_chunk(q, 1.0 - f, b, [(None, zi_ref[0, sl, :])], st_scr[...])
        st_scr[...] = st
        o_ref[0, sl, :] = _head_rms(o, ng_ref[...]) * _silu(zg_ref[0, sl, :])
        return carry

    lax.fori_loop(0, nchunks, body, 0)

    @pl.when(t == pl.num_programs(2) - 1)
    def _():
        s_ref[0, 0] = st_scr[...]


def hgrn_recurrence(z, lb, norm_g, s0_t, tb, chunk):
    bsz, t, _ = z.shape
    zspec = lambda off: pl.BlockSpec((1, tb, LANES), lambda b, h, i: (b, i, h + off))
    return pl.pallas_call(
        functools.partial(_hgrn_kernel, chunk=chunk),
        grid=(bsz, A_HEADS, t // tb),
        in_specs=[zspec(0), zspec(4), zspec(8), zspec(12),
                  pl.BlockSpec((1, 1, LANES), lambda b, h, i: (h, 0, 0)),
                  pl.BlockSpec((1, LANES), lambda b, h, i: (0, 0)),
                  pl.BlockSpec((1, 1, LANES, LANES), lambda b, h, i: (b, h, 0, 0))],
        out_specs=[pl.BlockSpec((1, tb, LANES), lambda b, h, i: (b, i, h)),
                   pl.BlockSpec((1, 1, LANES, LANES), lambda b, h, i: (b, h, 0, 0))],
        out_shape=[jax.ShapeDtypeStruct((bsz, t, A_WIDTH), F32),
                   jax.ShapeDtypeStruct((bsz, A_HEADS, LANES, LANES), F32)],
        scratch_shapes=[pltpu.VMEM((LANES, LANES), F32)],
        compiler_params=_cparams(("parallel", "parallel", "arbitrary")),
        name="hgrn_recurrence",
    )(z, z, z, z, lb.reshape(A_HEADS, 1, LANES), norm_g.reshape(1, LANES), s0_t)


def _gla_kernel(zq_ref, zk_ref, zv_ref, zg_ref, zlr_ref, w2_ref, gb_ref, ng_ref, s0_ref, o_ref, s_ref, st_scr,
                *, chunk):
    t = pl.program_id(2)

    @pl.when(t == 0)
    def _():
        st_scr[...] = s0_ref[0, 0]

    nchunks = zq_ref.shape[1] // chunk
    m0 = _lane_mask(LANES, 0, B_DK)
    m1 = _lane_mask(LANES, B_DK, LANES)

    def body(ci, carry):
        sl = pl.ds(pl.multiple_of(ci * chunk, chunk), chunk)
        pre = _dot(zlr_ref[0, sl, :], w2_ref[...], precision=HI) + gb_ref[...]
        log_g = (jnp.minimum(pre, 0.0) - jnp.log(1.0 + jnp.exp(-jnp.abs(pre)))) * (1.0 / GLA_NORMALIZER)
        q = zq_ref[0, sl, :] * (B_DK ** -0.5)
        b = _cumsum_rows(log_g)
        v = zv_ref[0, sl, :]
        (o0, o1), st = _gated_chunk(q, zk_ref[0, sl, :], b,
                                    [(m0, v[:, :LANES]), (m1, v[:, LANES:])], st_scr[...])
        st_scr[...] = st
        gate = _silu(zg_ref[0, sl, :])
        o_ref[0, sl, 0:LANES] = _head_rms(o0, ng_ref[...]) * gate[:, :LANES]
        o_ref[0, sl, LANES:2 * LANES] = _head_rms(o1, ng_ref[...]) * gate[:, LANES:]
        return carry

    lax.fori_loop(0, nchunks, body, 0)

    @pl.when(t == pl.num_programs(2) - 1)
    def _():
        s_ref[0, 0] = st_scr[...]


def gla_recurrence(z, w2pad, gate_b, norm_g, s0_t, tb, chunk):
    bsz, t, _ = z.shape
    npairs = B_HEADS // 2
    return pl.pallas_call(
        functools.partial(_gla_kernel, chunk=chunk),
        grid=(bsz, npairs, t // tb),
        in_specs=[pl.BlockSpec((1, tb, LANES), lambda b, p, i: (b, i, 16 + p)),
                  pl.BlockSpec((1, tb, LANES), lambda b, p, i: (b, i, 18 + p)),
                  pl.BlockSpec((1, tb, 2 * LANES), lambda b, p, i: (b, i, 10 + p)),
                  pl.BlockSpec((1, tb, 2 * LANES), lambda b, p, i: (b, i, 12 + p)),
                  pl.BlockSpec((1, tb, LANES), lambda b, p, i: (b, i, 28)),
                  pl.BlockSpec((LANES, LANES), lambda b, p, i: (0, p)),
                  pl.BlockSpec((1, LANES), lambda b, p, i: (0, p)),
                  pl.BlockSpec((1, LANES), lambda b, p, i: (0, 0)),
                  pl.BlockSpec((1, 1, LANES, LANES), lambda b, p, i: (b, p, 0, 0))],
        out_specs=[pl.BlockSpec((1, tb, 2 * LANES), lambda b, p, i: (b, i, p)),
                   pl.BlockSpec((1, 1, LANES, LANES), lambda b, p, i: (b, p, 0, 0))],
        out_shape=[jax.ShapeDtypeStruct((bsz, t, B_WIDTH), F32),
                   jax.ShapeDtypeStruct((bsz, npairs, LANES, LANES), F32)],
        scratch_shapes=[pltpu.VMEM((LANES, LANES), F32)],
        compiler_params=_cparams(("parallel", "parallel", "arbitrary")),
        name="gla_recurrence",
    )(z, z, z, z, z, w2pad, gate_b.reshape(1, 2 * LANES), norm_g.reshape(1, LANES), s0_t)


def _rwkv_proj_kernel(x_ref, xp_ref, xl_ref, g1_ref, mu_ref, wr_ref, wk_ref, wv_ref, ww1_ref, ww2_ref, w0_ref,
                      aw1_ref, aw2_ref, a0_ref, gw1_ref, gw2_ref, kk_ref, ka_ref,
                      r_out, k_out, v_out, lw_out, a_out, kk_out, g_out, hl_out, *, tiles_per_seq):
    i = pl.program_id(0)
    g1 = g1_ref[...]
    hn = _rms(x_ref[...], g1)
    tm = hn.shape[0]
    prev = _rms(xp_ref[...], g1)[SUBLANES - 1:SUBLANES, :]
    prev = jnp.where(i % tiles_per_seq == 0, xl_ref[0], prev)
    row = lax.broadcasted_iota(I32, (tm, 1), 0)
    xprev = jnp.where(row == 0, prev, pltpu.roll(hn, 1, axis=0))
    dx = xprev - hn

    def mix(j):
        return hn + dx * mu_ref[j:j + 1, :]

    r = _dot(mix(0).astype(BF16), wr_ref[...])
    k = _dot(mix(1).astype(BF16), wk_ref[...])
    v = _dot(mix(2).astype(BF16), wv_ref[...])
    wl = _dot(jnp.tanh(_dot(mix(3), ww1_ref[...], precision=HI)), ww2_ref[...], precision=HI)
    z = w0_ref[...] + wl
    wpre = -(jnp.maximum(-z, 0.0) + jnp.log(1.0 + jnp.exp(-jnp.abs(z)))) - 0.5
    al = _dot(_dot(mix(4), aw1_ref[...], precision=HI), aw2_ref[...], precision=HI)
    a = _sigmoid(a0_ref[...] + al)
    gg = _dot(_sigmoid(_dot(mix(5).astype(BF16), gw1_ref[...])).astype(BF16), gw2_ref[...])
    r_out[...] = r
    k_out[...] = k * (1.0 + (a - 1.0) * ka_ref[...])
    v_out[...] = v
    lw_out[...] = -jnp.exp(wpre)
    a_out[...] = a
    kk_out[...] = k * kk_ref[...]
    g_out[...] = gg
    hl_out[0] = hn[tm - 1:tm, :]


def rwkv_proj(x, x_last, seq_len, g1, mu, wr, wk, wv, ww1, ww2, w0, aw1, aw2, a0, gw1, gw2, k_k, k_a, tm):
    n, d = x.shape
    tiles_per_seq = seq_len // tm
    row = lambda a: a.reshape(1, d)
    full = lambda a: pl.BlockSpec(a.shape, lambda i: (0,) * a.ndim)
    tile = pl.BlockSpec((tm, d), lambda i: (i, 0))
    blocks8 = tm // SUBLANES
    args = (x, x, x_last.reshape(-1, 1, d), row(g1), mu, wr, wk, wv, ww1, ww2, row(w0), aw1, aw2, row(a0),
            gw1, gw2, row(k_k), row(k_a))
    in_specs = [tile,
                pl.BlockSpec((SUBLANES, d), lambda i: (jnp.maximum(i * blocks8 - 1, 0), 0)),
                pl.BlockSpec((1, 1, d), lambda i: (i // tiles_per_seq, 0, 0))]
    in_specs += [full(a) for a in args[3:]]
    outs = pl.pallas_call(
        functools.partial(_rwkv_proj_kernel, tiles_per_seq=tiles_per_seq),
        grid=(n // tm,),
        in_specs=in_specs,
        out_specs=[tile] * 7 + [pl.BlockSpec((1, 1, d), lambda i: (i, 0, 0))],
        out_shape=[jax.ShapeDtypeStruct((n, d), F32)] * 7 + [jax.ShapeDtypeStruct((n // tm, 1, d), F32)],
        compiler_params=_cparams(("parallel",)),
        name="rwkv_proj",
    )(*args)
    return outs


_NN = ((1,), (0,))
_NT = ((1,), (1,))
_TN = ((0,), (0,))
RWKV_AB_PASSES = 1
RWKV_INV_PASSES = 1
RWKV_APPLY_PASSES = 1
RWKV_STATE_PASSES = 3
RWKV_SEQS_PER_STEP = 4


def _split_bf16(a):
    hi = a.astype(BF16)
    return hi, (a - hi.astype(F32)).astype(BF16)


def _mm(a, b, dims, passes):
    if passes == 6:
        return lax.dot_general(a, b, (dims, ((), ())), preferred_element_type=F32, precision=HI)
    dg = lambda x, y: lax.dot_general(x, y, (dims, ((), ())), preferred_element_type=F32)
    ah, al = _split_bf16(a)
    bh, bl = _split_bf16(b)
    if passes == 1:
        return dg(ah, bh)
    return dg(ah, bh) + (dg(al, bh) + dg(ah, bl))


def _cumsum_rows3(g):
    tri = _tri(g.shape[0], False).astype(BF16)
    h1 = g.astype(BF16)
    r1 = g - h1.astype(F32)
    h2 = r1.astype(BF16)
    h3 = (r1 - h2.astype(F32)).astype(BF16)
    return _dot(tri, h1) + (_dot(tri, h2) + _dot(tri, h3))


def _pair_sum(x, m0):
    s0 = jnp.sum(jnp.where(m0, x, 0.0), axis=-1, keepdims=True)
    s1 = jnp.sum(jnp.where(m0, 0.0, x), axis=-1, keepdims=True)
    return jnp.where(m0, s0, s1)


def _rwkv_kernel(r_ref, k_ref, v_ref, lw_ref, a_ref, kk_ref, g_ref, rk_ref, lng_ref, lnb_ref, s0_ref,
                 o_ref, s_ref, mt_scr, *, chunk):
    t = pl.program_id(2)

    @pl.when(t == 0)
    def _():
        mt_scr[...] = s0_ref[:, 0]

    nrows = r_ref.shape[0]
    nchunks = r_ref.shape[1] // chunk
    c2 = 2 * chunk
    m0 = _lane_mask(LANES, 0, C_HEAD)
    rowi = lax.broadcasted_iota(I32, (LANES, LANES), 0)
    coli = lax.broadcasted_iota(I32, (LANES, LANES), 1)
    blockdiag = (rowi < C_HEAD) == (coli < C_HEAD)
    ti = lax.broadcasted_iota(I32, (c2, c2), 0)
    si = lax.broadcasted_iota(I32, (c2, c2), 1)
    same_head = (ti < chunk) == (si < chunk)
    tm_ = jnp.where(ti < chunk, ti, ti - chunk)
    sm_ = jnp.where(si < chunk, si, si - chunk)
    strict = same_head & (sm_ < tm_)
    incl = same_head & (sm_ <= tm_)

    def stack_heads(x):
        return jnp.concatenate([jnp.where(m0, x, 0.0), jnp.where(m0, 0.0, x)], axis=0)

    def twice(x):
        return jnp.concatenate([x, x], axis=0)

    def unstack(x2):
        return jnp.where(m0, x2[:chunk], x2[chunk:])

    eye = (ti == si).astype(F32)
    seqs = range(nrows)

    def body(ci, carry):
        sl = pl.ds(pl.multiple_of(ci * chunk, chunk), chunk)
        r = [r_ref[i, sl, :] for i in seqs]
        k = [k_ref[i, sl, :] for i in seqs]
        v = [v_ref[i, sl, :] for i in seqs]
        lw = [lw_ref[i, sl, :] for i in seqs]
        kkr = [kk_ref[i, sl, :] for i in seqs]
        kk = [x * lax.rsqrt(_pair_sum(x * x, m0) + 1e-12) for x in kkr]
        al = [a_ref[i, sl, :] * kk[i] for i in seqs]
        gam = [_cumsum_rows3(x) for x in lw]
        e_neg = [jnp.exp(-x) for x in gam]
        xr = [jnp.concatenate([stack_heads(kk[i] * jnp.exp(gam[i] - lw[i])),
                               stack_heads(r[i] * jnp.exp(gam[i]))], axis=0) for i in seqs]
        alk = [jnp.concatenate([twice(al[i] * e_neg[i]), twice(k[i] * e_neg[i])], axis=0) for i in seqs]
        mt = [mt_scr[i] for i in seqs]
        ab = [_mm(xr[i], alk[i], _NT, RWKV_AB_PASSES) for i in seqs]
        xm = [_mm(xr[i], mt[i], _NT, RWKV_STATE_PASSES) for i in seqs]
        a_al = [jnp.where(strict, x[:c2, :c2], 0.0) for x in ab]
        a_k = [jnp.where(strict, x[:c2, c2:], 0.0) for x in ab]
        b_alk = [jnp.concatenate([jnp.where(incl, x[c2:, c2:], 0.0), jnp.where(incl, -x[c2:, :c2], 0.0)], axis=1)
                 for x in ab]
        v2 = [twice(x) for x in v]
        rhs = [xm[i][:c2] + _mm(a_k[i], v2[i], _NN, RWKV_APPLY_PASSES) for i in seqs]
        p = [-x for x in a_al]
        tinv = [eye + x for x in p]
        span = 2
        while span < chunk:
            p = [_mm(x, x, _NN, RWKV_INV_PASSES) for x in p]
            tinv = [tinv[i] + _mm(tinv[i], p[i], _NN, RWKV_INV_PASSES) for i in seqs]
            span *= 2
        u = [unstack(_mm(tinv[i], rhs[i], _NN, RWKV_APPLY_PASSES)) for i in seqs]
        o = [unstack(xm[i][c2:] + _mm(b_alk[i], jnp.concatenate([v2[i], twice(u[i])], axis=0), _NN,
                                      RWKV_APPLY_PASSES)) for i in seqs]
        g_last = [x[-1:, :] for x in gam]
        e_end = [jnp.exp(g_last[i] - gam[i]) for i in seqs]
        upd = [_mm(jnp.concatenate([v[i], u[i]], axis=0),
                   jnp.concatenate([k[i] * e_end[i], -(al[i] * e_end[i])], axis=0), _TN, RWKV_STATE_PASSES)
               for i in seqs]
        for i in seqs:
            mt_scr[i] = mt[i] * jnp.exp(g_last[i]) + jnp.where(blockdiag, upd[i], 0.0)
        for i in seqs:
            mean = _pair_sum(o[i], m0) * (1.0 / C_HEAD)
            cen = o[i] - mean
            var = _pair_sum(cen * cen, m0) * (1.0 / C_HEAD)
            on = cen * lax.rsqrt(var + C_GN_EPS) * lng_ref[...] + lnb_ref[...]
            bonus = _pair_sum(r[i] * k[i] * rk_ref[...], m0) * v[i]
            o_ref[i, sl, :] = (on + bonus) * g_ref[i, sl, :]
        return carry

    lax.fori_loop(0, nchunks, body, 0)

    @pl.when(t == pl.num_programs(2) - 1)
    def _():
        s_ref[:, 0] = mt_scr[...]


def rwkv_recurrence(r, k, v, lw, a, kk, g, r_k, ln_g, ln_b, s0_bd, tb, chunk, nb):
    bsz, t, d = r.shape
    npairs = C_HEADS // 2
    seq = pl.BlockSpec((nb, tb, LANES), lambda b, p, i: (b, i, p))
    vec = pl.BlockSpec((1, LANES), lambda b, p, i: (0, p))
    st = pl.BlockSpec((nb, 1, LANES, LANES), lambda b, p, i: (b, p, 0, 0))
    return pl.pallas_call(
        functools.partial(_rwkv_kernel, chunk=chunk),
        grid=(bsz // nb, npairs, t // tb),
        in_specs=[seq] * 7 + [vec, vec, vec, st],
        out_specs=[seq, st],
        out_shape=[jax.ShapeDtypeStruct((bsz, t, d), F32),
                   jax.ShapeDtypeStruct((bsz, npairs, LANES, LANES), F32)],
        scratch_shapes=[pltpu.VMEM((nb, LANES, LANES), F32)],
        compiler_params=_cparams(("parallel", "parallel", "arbitrary")),
        name="rwkv_recurrence",
    )(r, k, v, lw, a, kk, g, r_k.reshape(1, d), ln_g.reshape(1, d), ln_b.reshape(1, d), s0_bd)


NEG_INF = float("-inf")


def _top16_rows(s):
    n = s.shape[0]
    key = lax.broadcasted_iota(I32, s.shape, 0)
    vals, idxs = [], []
    for _ in range(P_TOPK):
        m = jnp.max(s, axis=0, keepdims=True)
        am = jnp.min(jnp.where(s == m, key, n), axis=0, keepdims=True)
        vals.append(m)
        idxs.append(am)
        s = jnp.where(key == am, NEG_INF, s)
    return vals, idxs


def _top16_pairs(v0, i0, v1, i1):
    a0 = jnp.concatenate(v0[0:8], axis=0)
    a1 = jnp.concatenate(v0[8:16], axis=0)
    b0 = jnp.concatenate(v1[0:8], axis=0)
    b1 = jnp.concatenate(v1[8:16], axis=0)
    ia0 = jnp.concatenate(i0[0:8], axis=0) * P_KEYS
    ia1 = jnp.concatenate(i0[8:16], axis=0) * P_KEYS
    ib0 = jnp.concatenate(i1[0:8], axis=0)
    ib1 = jnp.concatenate(i1[8:16], axis=0)
    row = lax.broadcasted_iota(I32, (SUBLANES, 1), 0)
    slabs = []

    def add(val, eid, keep):
        slabs.append((val if keep is None else jnp.where(keep, val, NEG_INF), eid))

    add(v0[0] + b0, ia0[0:1] + ib0, None)
    add(v0[0] + b1, ia0[0:1] + ib1, None)
    add(v0[1] + b0, ia0[1:2] + ib0, None)
    add(v0[2] + b0, ia0[2:3] + ib0, row < 5)
    add(v0[3] + b0, ia0[3:4] + ib0, row < 4)
    add(a0 + v1[0], ia0 + ib0[0:1], row >= 4)
    add(a1 + v1[0], ia1 + ib0[0:1], None)
    add(a0 + v1[1], ia0 + ib0[1:2], row >= 4)
    add(a0 + v1[2], ia0 + ib0[2:3], row == 4)

    big = P_KEYS * P_KEYS
    out_v, out_e = [], []
    for _ in range(P_TOPK):
        m = slabs[0][0]
        for val, _e in slabs[1:]:
            m = jnp.maximum(m, val)
        m = jnp.max(m, axis=0, keepdims=True)
        e = None
        for val, eid in slabs:
            c = jnp.where(val == m, eid, big)
            e = c if e is None else jnp.minimum(e, c)
        e = jnp.min(e, axis=0, keepdims=True)
        out_v.append(m)
        out_e.append(e)
        slabs = [(jnp.where(eid == e, NEG_INF, val), eid) for val, eid in slabs]
    return out_v, out_e


def _peer_select_kernel(x_ref, g_ref, wq_ref, keys_ref, xn_out, eid_out, gate_out):
    hn = _rms(x_ref[...], g_ref[...])
    xn_out[...] = hn
    q = _dot(hn.astype(BF16), wq_ref[...])
    tm = q.shape[0]
    for lt in range(tm // LANES):
        rows = slice(lt * LANES, (lt + 1) * LANES)
        e_rows, g_rows = [], []
        for h in range(P_HEADS):
            tops = []
            for p in range(2):
                hp = 2 * h + p
                s = _dot_nt(keys_ref[hp], q[rows, hp * LANES:(hp + 1) * LANES], precision=HI)
                tops.append(_top16_rows(s))
            cs, ce = _top16_pairs(tops[0][0], tops[0][1], tops[1][0], tops[1][1])
            ex = [jnp.exp(c - cs[0]) for c in cs]
            tot = ex[0]
            for e in ex[1:]:
                tot = tot + e
            inv = 1.0 / tot
            e_rows += ce
            g_rows += [e * inv for e in ex]
        eid_out[rows, :] = jnp.concatenate(e_rows, axis=0).T
        gate_out[rows, :] = jnp.concatenate(g_rows, axis=0).T


def peer_select(x, g, wq_bf16, keys, tm):
    n, d = x.shape
    return pl.pallas_call(
        _peer_select_kernel,
        grid=(n // tm,),
        in_specs=[pl.BlockSpec((tm, d), lambda i: (i, 0)),
                  pl.BlockSpec((1, d), lambda i: (0, 0)),
                  pl.BlockSpec(wq_bf16.shape, lambda i: (0, 0)),
                  pl.BlockSpec(keys.shape, lambda i: (0, 0, 0))],
        out_specs=[pl.BlockSpec((tm, d), lambda i: (i, 0)),
                   pl.BlockSpec((tm, P_PICKS), lambda i: (i, 0)),
                   pl.BlockSpec((tm, P_PICKS), lambda i: (i, 0))],
        out_shape=[jax.ShapeDtypeStruct((n, d), F32),
                   jax.ShapeDtypeStruct((n, P_PICKS), I32),
                   jax.ShapeDtypeStruct((n, P_PICKS), F32)],
        compiler_params=_cparams(("parallel",)),
        name="peer_select",
    )(x, g.reshape(1, d), wq_bf16, keys)


def _peer_act_kernel(h_ref, g_ref, o_ref):
    h = h_ref[...]
    o_ref[...] = 0.5 * h * (1.0 + lax.erf(h * (2.0 ** -0.5))) * g_ref[...]


def peer_act(hid, gate, tm):
    n, p = hid.shape
    spec = pl.BlockSpec((tm, p), lambda i: (i, 0))
    return pl.pallas_call(
        _peer_act_kernel, grid=(n // tm,), in_specs=[spec, spec], out_specs=spec,
        out_shape=jax.ShapeDtypeStruct((n, p), F32),
        compiler_params=_cparams(("parallel",)), name="peer_act",
    )(hid, gate)


_CHUNKS_PER_TOK = P_PICKS // SC_CHUNK
_WORD_VREGS = SC_WORDS // SC_LANES
U32 = jnp.uint32


def pack_rows_bf16(t):
    lo = lax.bitcast_convert_type(t[..., :SC_WORDS].astype(BF16), jnp.uint16).astype(U32)
    a = lax.bitcast_convert_type(t[..., SC_WORDS:], U32)
    sign = a & U32(0x80000000)
    mag = a & U32(0x7FFFFFFF)
    steps = (jnp.maximum(mag + U32(0x8000), lo) - lo) >> 16
    near = (steps << 16) + lo
    near = jnp.where(near >= U32(0x7F800000), near - U32(0x10000), near)
    return lax.bitcast_convert_type(sign | near, I32)


def _low_f32(w):
    return lax.bitcast_convert_type(w << 16, F32)


def _high_f32(w):
    return lax.bitcast_convert_type(w, F32)


def _sc_worker():
    return lax.axis_index("s") * SC_CORES + lax.axis_index("c")


def _sc_hid_body(tab_hbm, idx_hbm, x_hbm, hid_hbm, idx_v, x_v, hid_v, part_v, rows_a, rows_b, sem_a, sem_b,
                 *, tok_per_w):
    wid = _sc_worker()
    nblk = tok_per_w // SC_TOK_BLOCK
    nch = SC_TOK_BLOCK * _CHUNKS_PER_TOK
    lane = lax.iota(I32, SC_LANES)
    half_w = _WORD_VREGS // 2

    def dots(rows, x_row, res_ref, col0):
        for xp in range(2):
            base = xp * half_w * SC_LANES
            xlo = [x_v[x_row, pl.ds(base + j * SC_LANES, SC_LANES)] for j in range(half_w)]
            xhi = [x_v[x_row, pl.ds(SC_WORDS + base + j * SC_LANES, SC_LANES)] for j in range(half_w)]

            def partial_dot(r):
                accs = []
                for j in range(half_w):
                    w = rows[r, pl.ds(base + j * SC_LANES, SC_LANES)]
                    term = _low_f32(w) * xlo[j] + _high_f32(w) * xhi[j]
                    if j < 4:
                        accs.append(term)
                    else:
                        accs[j % 4] = accs[j % 4] + term
                return (accs[0] + accs[1]) + (accs[2] + accs[3])

            if xp == 0:
                def first(r, c):
                    part_v[r, :] = partial_dot(r)
                    return c
                lax.fori_loop(0, SC_CHUNK, first, 0)
            else:
                for grp in range(SC_CHUNK // SC_LANES):
                    def second(rr, res):
                        r = grp * SC_LANES + rr
                        return jnp.where(lane == rr, jnp.sum(partial_dot(r) + part_v[r, :]), res)
                    res = lax.fori_loop(0, SC_LANES, second, jnp.zeros((SC_LANES,), F32))
                    res_ref[x_row, pl.ds(col0 + grp * SC_LANES, SC_LANES)] = res

    def block(bi, carry):
        tok0 = wid * tok_per_w + bi * SC_TOK_BLOCK
        pltpu.sync_copy(idx_hbm.at[pl.ds(tok0 * _CHUNKS_PER_TOK, nch)], idx_v)
        pltpu.sync_copy(x_hbm.at[pl.ds(tok0, SC_TOK_BLOCK)], x_v)
        pltpu.async_copy(tab_hbm.at[idx_v.at[0]], rows_a, sem_a)

        def pair(j, c):
            ca = 2 * j
            pltpu.async_copy(tab_hbm.at[idx_v.at[ca + 1]], rows_b, sem_b)
            pltpu.make_async_copy(tab_hbm.at[idx_v.at[0]], rows_a, sem_a).wait()
            dots(rows_a, ca // _CHUNKS_PER_TOK, hid_v, (ca % _CHUNKS_PER_TOK) * SC_CHUNK)

            @pl.when(j < nch // 2 - 1)
            def _():
                pltpu.async_copy(tab_hbm.at[idx_v.at[ca + 2]], rows_a, sem_a)
            pltpu.make_async_copy(tab_hbm.at[idx_v.at[0]], rows_b, sem_b).wait()
            cb = ca + 1
            dots(rows_b, cb // _CHUNKS_PER_TOK, hid_v, (cb % _CHUNKS_PER_TOK) * SC_CHUNK)
            return c
        lax.fori_loop(0, nch // 2, pair, 0)
        pltpu.sync_copy(hid_v, hid_hbm.at[pl.ds(tok0, SC_TOK_BLOCK)])
        return carry

    lax.fori_loop(0, nblk, block, 0)


def sc_expert_hidden(table_packed, eidx, xn):
    n = xn.shape[0]
    tok_per_w = n // SC_WORKERS
    mesh = plsc.VectorSubcoreMesh(core_axis_name="c", subcore_axis_name="s")
    nch = SC_TOK_BLOCK * _CHUNKS_PER_TOK
    k = pl.kernel(
        functools.partial(_sc_hid_body, tok_per_w=tok_per_w), mesh=mesh,
        out_type=jax.ShapeDtypeStruct((n, P_PICKS), F32),
        scratch_types=[pltpu.VMEM((nch, SC_CHUNK), I32),
                       pltpu.VMEM((SC_TOK_BLOCK, D_MODEL), F32),
                       pltpu.VMEM((SC_TOK_BLOCK, P_PICKS), F32),
                       pltpu.VMEM((SC_CHUNK, SC_LANES), F32),
                       pltpu.VMEM((SC_CHUNK, SC_WORDS), I32),
                       pltpu.VMEM((SC_CHUNK, SC_WORDS), I32),
                       pltpu.SemaphoreType.DMA, pltpu.SemaphoreType.DMA],
        compiler_params=pltpu.CompilerParams(needs_layout_passes=False),
        name="sc_expert_hidden",
    )
    return k(table_packed, eidx.reshape(n * _CHUNKS_PER_TOK, SC_CHUNK), xn)


def _sc_out_body(tab_hbm, idx_hbm, act_hbm, x_hbm, y_hbm, idx_v, act_v, y_v, rows_a, rows_b, sem_a, sem_b,
                 *, tok_per_w):
    wid = _sc_worker()
    nblk = tok_per_w // SC_TOK_BLOCK
    nch = SC_TOK_BLOCK * _CHUNKS_PER_TOK
    half_w = _WORD_VREGS // 2

    def accum(rows, tok, col0):
        for hv in range(2):
            base = hv * half_w * SC_LANES
            acc0 = (tuple(y_v[tok, pl.ds(base + j * SC_LANES, SC_LANES)] for j in range(half_w))
                    + tuple(y_v[tok, pl.ds(SC_WORDS + base + j * SC_LANES, SC_LANES)] for j in range(half_w)))

            def one(r, acc):
                aidx = jnp.full((SC_LANES,), col0, I32) + r
                wgt = plsc.load_gather(act_v, [jnp.full((SC_LANES,), tok, I32), aidx])
                lo, hi = [], []
                for j in range(half_w):
                    w = rows[r, pl.ds(base + j * SC_LANES, SC_LANES)]
                    lo.append(acc[j] + wgt * _low_f32(w))
                    hi.append(acc[half_w + j] + wgt * _high_f32(w))
                return tuple(lo + hi)
            acc = lax.fori_loop(0, SC_CHUNK, one, acc0)
            for j in range(half_w):
                y_v[tok, pl.ds(base + j * SC_LANES, SC_LANES)] = acc[j]
                y_v[tok, pl.ds(SC_WORDS + base + j * SC_LANES, SC_LANES)] = acc[half_w + j]

    def block(bi, carry):
        tok0 = wid * tok_per_w + bi * SC_TOK_BLOCK
        pltpu.sync_copy(idx_hbm.at[pl.ds(tok0 * _CHUNKS_PER_TOK, nch)], idx_v)
        pltpu.sync_copy(act_hbm.at[pl.ds(tok0, SC_TOK_BLOCK)], act_v)
        pltpu.sync_copy(x_hbm.at[pl.ds(tok0, SC_TOK_BLOCK)], y_v)
        pltpu.async_copy(tab_hbm.at[idx_v.at[0]], rows_a, sem_a)

        def pair(j, c):
            ca = 2 * j
            pltpu.async_copy(tab_hbm.at[idx_v.at[ca + 1]], rows_b, sem_b)
            pltpu.make_async_copy(tab_hbm.at[idx_v.at[0]], rows_a, sem_a).wait()
            accum(rows_a, ca // _CHUNKS_PER_TOK, (ca % _CHUNKS_PER_TOK) * SC_CHUNK)

            @pl.when(j < nch // 2 - 1)
            def _():
                pltpu.async_copy(tab_hbm.at[idx_v.at[ca + 2]], rows_a, sem_a)
            pltpu.make_async_copy(tab_hbm.at[idx_v.at[0]], rows_b, sem_b).wait()
            cb = ca + 1
            accum(rows_b, cb // _CHUNKS_PER_TOK, (cb % _CHUNKS_PER_TOK) * SC_CHUNK)
            return c
        lax.fori_loop(0, nch // 2, pair, 0)
        pltpu.sync_copy(y_v, y_hbm.at[pl.ds(tok0, SC_TOK_BLOCK)])
        return carry

    lax.fori_loop(0, nblk, block, 0)


def sc_expert_output(table_packed, eidx, act, x):
    n = x.shape[0]
    tok_per_w = n // SC_WORKERS
    mesh = plsc.VectorSubcoreMesh(core_axis_name="c", subcore_axis_name="s")
    nch = SC_TOK_BLOCK * _CHUNKS_PER_TOK
    k = pl.kernel(
        functools.partial(_sc_out_body, tok_per_w=tok_per_w), mesh=mesh,
        out_type=jax.ShapeDtypeStruct((n, D_MODEL), F32),
        scratch_types=[pltpu.VMEM((nch, SC_CHUNK), I32),
                       pltpu.VMEM((SC_TOK_BLOCK, P_PICKS), F32),
                       pltpu.VMEM((SC_TOK_BLOCK, D_MODEL), F32),
                       pltpu.VMEM((SC_CHUNK, SC_WORDS), I32),
                       pltpu.VMEM((SC_CHUNK, SC_WORDS), I32),
                       pltpu.SemaphoreType.DMA, pltpu.SemaphoreType.DMA],
        compiler_params=pltpu.CompilerParams(needs_layout_passes=False),
        name="sc_expert_output",
    )
    return k(table_packed, eidx.reshape(n * _CHUNKS_PER_TOK, SC_CHUNK), act, x)


def peer_ffn(x, g, wq_bf16, keys, u_packed, v_packed, tm):
    xn, eidx, gate = peer_select(x, g, wq_bf16, keys, tm)
    hid = sc_expert_hidden(u_packed, eidx, xn)
    act = peer_act(hid, gate, tm)
    return sc_expert_output(v_packed, eidx, act, x)


def _prep_w_in(w_in):
    main = jnp.concatenate([w_in[:, :3072], w_in[:, 3088:3600]], axis=1)
    lr = jnp.pad(w_in[:, 3072:3088], ((0, 0), (0, LANES - GLA_RANK)))
    return jnp.concatenate([main, lr], axis=1).astype(BF16)


def _trunk(x, st_h, st_g, st_r, st_s, w, seq_len, tm, tmp, tb, chunk):
    bsz = x.shape[0]
    n = bsz * seq_len
    x2 = x.reshape(n, D_MODEL)

    z = norm_proj(x2, w["norm1_g"][0], w["w_in"], tm).reshape(bsz, seq_len, Z_WIDTH)
    s0_h = jnp.swapaxes(st_h, -1, -2)
    s0_g = jnp.swapaxes(st_g.reshape(bsz, 2, 2 * B_DK, LANES), -1, -2)
    o_a, sh_t = hgrn_recurrence(z, w["lb0"], w["hgrn_norm_g"], s0_h, tb, chunk)
    o_b, sg_t = gla_recurrence(z, w["gla_w2"], w["gla_b"], w["gla_norm_g"], s0_g, tb, chunk)
    new_h = jnp.swapaxes(sh_t, -1, -2)
    new_g = jnp.swapaxes(sg_t, -1, -2).reshape(bsz, B_HEADS, B_DK, LANES)
    x2 = out_proj2(x2, o_a.reshape(n, A_WIDTH), o_b.reshape(n, B_WIDTH), w["w_out_a"], w["w_out_b"], tm)
    x2 = peer_ffn(x2, w["norm2_g"][0], w["peer_wq"][0], w["peer_keys"][0], w["peer_u"][0], w["peer_v"][0], tmp)

    r, k, v, lw, a, kk, g, hl = rwkv_proj(
        x2, st_s, seq_len, w["norm1_g"][1], w["mu"], w["wr"], w["wk"], w["wv"], w["w_w1"], w["w_w2"], w["w0"],
        w["a_w1"], w["a_w2"], w["a0"], w["g_w1"], w["g_w2"], w["k_k"], w["k_a"], tm)
    new_s = hl.reshape(bsz, seq_len // tm, D_MODEL)[:, -1]
    pr = st_r.reshape(bsz, C_HEADS // 2, 2, C_HEAD, C_HEAD)
    zero = jnp.zeros_like(pr[:, :, 0])
    s0_r = jnp.concatenate([jnp.concatenate([pr[:, :, 0], zero], axis=-1),
                            jnp.concatenate([zero, pr[:, :, 1]], axis=-1)], axis=-2)
    sh3 = lambda t: t.reshape(bsz, seq_len, D_MODEL)
    o_c, sr_bd = rwkv_recurrence(sh3(r), sh3(k), sh3(v), sh3(lw), sh3(a), sh3(kk), sh3(g),
                                 w["r_k"], w["ln_g"], w["ln_b"], s0_r, tb, chunk, min(RWKV_SEQS_PER_STEP, bsz))
    new_r = jnp.stack([sr_bd[:, :, :C_HEAD, :C_HEAD], sr_bd[:, :, C_HEAD:, C_HEAD:]], axis=2)
    new_r = new_r.reshape(bsz, C_HEADS, C_HEAD, C_HEAD)
    x2 = out_proj1(x2, o_c.reshape(n, D_MODEL), w["w_out_c"], tm)
    x2 = peer_ffn(x2, w["norm2_g"][1], w["peer_wq"][1], w["peer_keys"][1], w["peer_u"][1], w["peer_v"][1], tmp)

    y = final_norm(x2, w["final_g"], tm).reshape(bsz, seq_len, D_MODEL)
    return y, new_h[None], new_g[None], new_r[None], new_s[None]


def kernel(x_prompt, x_sample, state_hgrn, state_gla, state_rwkv, state_shift, w_in_ab, hgrn_lower_bounds, hgrn_norm_g, gla_gate_w2, gla_gate_b, gla_norm_g, w_out_ab, rwkv_mu, rwkv_w_rkv, rwkv_w_w1, rwkv_w_w2, rwkv_w0, rwkv_a_w1, rwkv_a_w2, rwkv_a0, rwkv_g_w1, rwkv_g_w2, rwkv_k_k, rwkv_k_a, rwkv_r_k, rwkv_ln_g, rwkv_ln_b, w_out_c, norm1_g, norm2_g, final_g, peer_w_q, peer_sub_keys, peer_u, peer_v):
    lbs = jnp.cumsum(jax.nn.softmax(hgrn_lower_bounds.astype(F32), axis=0), axis=0)
    w = dict(
        norm1_g=norm1_g, norm2_g=norm2_g, final_g=final_g,
        w_in=_prep_w_in(w_in_ab[0]), lb0=lbs[0], hgrn_norm_g=hgrn_norm_g[0],
        gla_w2=jnp.pad(gla_gate_w2[0], ((0, LANES - GLA_RANK), (0, 0))), gla_b=gla_gate_b[0],
        gla_norm_g=gla_norm_g[0],
        w_out_a=w_out_ab[0, :A_WIDTH].astype(BF16), w_out_b=w_out_ab[0, A_WIDTH:].astype(BF16),
        mu=rwkv_mu[0], wr=rwkv_w_rkv[0, 0].astype(BF16), wk=rwkv_w_rkv[0, 1].astype(BF16),
        wv=rwkv_w_rkv[0, 2].astype(BF16), w_w1=rwkv_w_w1[0], w_w2=rwkv_w_w2[0], w0=rwkv_w0[0],
        a_w1=rwkv_a_w1[0], a_w2=rwkv_a_w2[0], a0=rwkv_a0[0],
        g_w1=rwkv_g_w1[0].astype(BF16), g_w2=rwkv_g_w2[0].astype(BF16),
        k_k=rwkv_k_k[0], k_a=rwkv_k_a[0], r_k=rwkv_r_k[0], ln_g=rwkv_ln_g[0], ln_b=rwkv_ln_b[0],
        w_out_c=w_out_c[0].astype(BF16),
        peer_wq=peer_w_q.astype(BF16),
        peer_keys=peer_sub_keys.reshape(peer_sub_keys.shape[0], 2 * P_HEADS, P_KEYS, P_KEYS),
        peer_u=pack_rows_bf16(peer_u), peer_v=pack_rows_bf16(peer_v),
    )
    bp, tp, _ = x_prompt.shape
    bs, ts, _ = x_sample.shape
    assert sum(PROMPT_GROUPS) == bp
    groups, start = [], 0
    for gsz in PROMPT_GROUPS:
        zeros = lambda s: jnp.zeros((gsz,) + s.shape[2:], F32)
        groups.append(_trunk(x_prompt[start:start + gsz], zeros(state_hgrn), zeros(state_gla), zeros(state_rwkv),
                             zeros(state_shift), w, tp, tm=256, tmp=256, tb=512, chunk=64))
        start += gsz
    y_p = jnp.concatenate([g[0] for g in groups], axis=0)
    p_h, p_g, p_r, p_s = (jnp.concatenate([g[j] for g in groups], axis=1) for j in range(1, 5))
    y_s, s_h, s_g, s_r, s_s = _trunk(x_sample, state_hgrn[0], state_gla[0], state_rwkv[0], state_shift[0],
                                     w, ts, tm=32, tmp=128, tb=32, chunk=32)
    return (y_p, y_s, p_h, p_g, p_r, p_s, s_h, s_g, s_r, s_s)
```

```python
import functools

import jax
import jax.numpy as jnp
from jax import lax
from jax.experimental import pallas as pl
from jax.experimental.pallas import tpu as pltpu
from jax.experimental.pallas import tpu_sc as plsc

F32 = jnp.float32
BF16 = jnp.bfloat16
I32 = jnp.int32
HI = lax.Precision.HIGHEST

D_MODEL = 1024
NORM_EPS = 1e-6
LANES = 128
SUBLANES = 8
VMEM_LIMIT = 56 * 1024 * 1024

A_WIDTH = 512
A_HEADS = 4
B_WIDTH = 512
B_HEADS = 4
B_DK = 64
GLA_RANK = 16
GLA_NORMALIZER = 16.0
Z_WIDTH = 3712
C_HEAD = 64
C_HEADS = 16
C_GN_EPS = 64e-5
P_HEADS = 8
P_KEYS = 128
P_TOPK = 16
P_PICKS = P_HEADS * P_TOPK
SC_CORES = 2
SC_SUBCORES = 16
SC_WORKERS = SC_CORES * SC_SUBCORES
SC_LANES = 16
SC_CHUNK = 64
SC_TOK_BLOCK = 16
SC_WORDS = D_MODEL // 2
PROMPT_GROUPS = (2, 2, 4, 4, 4)


def _cparams(sem):
    return pltpu.CompilerParams(dimension_semantics=sem, vmem_limit_bytes=VMEM_LIMIT)


def _rms(x, g):
    ms = jnp.mean(x * x, axis=-1, keepdims=True)
    return x * lax.rsqrt(ms + NORM_EPS) * g


def _dot(a, b, precision=None):
    return jnp.dot(a, b, preferred_element_type=F32, precision=precision)


def _dot_nt(a, b, precision=None):
    return lax.dot_general(a, b, (((1,), (1,)), ((), ())), preferred_element_type=F32, precision=precision)


def _dot_tn(a, b, precision=None):
    return lax.dot_general(a, b, (((0,), (0,)), ((), ())), preferred_element_type=F32, precision=precision)


def _tri(n, strict):
    r = lax.broadcasted_iota(I32, (n, n), 0)
    c = lax.broadcasted_iota(I32, (n, n), 1)
    return (c < r) if strict else (c <= r)


def _cumsum_rows(g):
    return _dot(_tri(g.shape[0], False).astype(F32), g, precision=HI)


def _lane_mask(width, lo, hi):
    l = lax.broadcasted_iota(I32, (1, width), 1)
    return (l >= lo) & (l < hi)


def _sigmoid(x):
    return 1.0 / (1.0 + jnp.exp(-x))


def _silu(x):
    return x * _sigmoid(x)


def _norm_proj_kernel(x_ref, g_ref, w_ref, o_ref):
    hn = _rms(x_ref[...], g_ref[...])
    o_ref[...] = _dot(hn.astype(BF16), w_ref[...])


def norm_proj(x, g, w_bf16, tm):
    n, d = x.shape
    f = w_bf16.shape[1]
    return pl.pallas_call(
        _norm_proj_kernel,
        grid=(n // tm,),
        in_specs=[pl.BlockSpec((tm, d), lambda i: (i, 0)),
                  pl.BlockSpec((1, d), lambda i: (0, 0)),
                  pl.BlockSpec((d, f), lambda i: (0, 0))],
        out_specs=pl.BlockSpec((tm, f), lambda i: (i, 0)),
        out_shape=jax.ShapeDtypeStruct((n, f), F32),
        compiler_params=_cparams(("parallel",)),
        name="norm_proj",
    )(x, g.reshape(1, d), w_bf16)


def _out_proj2_kernel(x_ref, a_ref, b_ref, wa_ref, wb_ref, o_ref):
    y = _dot(a_ref[...].astype(BF16), wa_ref[...]) + _dot(b_ref[...].astype(BF16), wb_ref[...])
    o_ref[...] = x_ref[...] + y


def out_proj2(x, a, b, wa, wb, tm):
    n, d = x.shape
    ka, kb = a.shape[1], b.shape[1]
    return pl.pallas_call(
        _out_proj2_kernel,
        grid=(n // tm,),
        in_specs=[pl.BlockSpec((tm, d), lambda i: (i, 0)),
                  pl.BlockSpec((tm, ka), lambda i: (i, 0)),
                  pl.BlockSpec((tm, kb), lambda i: (i, 0)),
                  pl.BlockSpec((ka, d), lambda i: (0, 0)),
                  pl.BlockSpec((kb, d), lambda i: (0, 0))],
        out_specs=pl.BlockSpec((tm, d), lambda i: (i, 0)),
        out_shape=jax.ShapeDtypeStruct((n, d), F32),
        compiler_params=_cparams(("parallel",)),
        name="out_proj2",
    )(x, a, b, wa, wb)


def _out_proj1_kernel(x_ref, a_ref, wa_ref, o_ref):
    o_ref[...] = x_ref[...] + _dot(a_ref[...].astype(BF16), wa_ref[...])


def out_proj1(x, a, wa, tm):
    n, d = x.shape
    ka = a.shape[1]
    return pl.pallas_call(
        _out_proj1_kernel,
        grid=(n // tm,),
        in_specs=[pl.BlockSpec((tm, d), lambda i: (i, 0)),
                  pl.BlockSpec((tm, ka), lambda i: (i, 0)),
                  pl.BlockSpec((ka, d), lambda i: (0, 0))],
        out_specs=pl.BlockSpec((tm, d), lambda i: (i, 0)),
        out_shape=jax.ShapeDtypeStruct((n, d), F32),
        compiler_params=_cparams(("parallel",)),
        name="out_proj1",
    )(x, a, wa)


def _final_norm_kernel(x_ref, g_ref, o_ref):
    o_ref[...] = _rms(x_ref[...], g_ref[...])


def final_norm(x, g, tm):
    n, d = x.shape
    return pl.pallas_call(
        _final_norm_kernel,
        grid=(n // tm,),
        in_specs=[pl.BlockSpec((tm, d), lambda i: (i, 0)), pl.BlockSpec((1, d), lambda i: (0, 0))],
        out_specs=pl.BlockSpec((tm, d), lambda i: (i, 0)),
        out_shape=jax.ShapeDtypeStruct((n, d), F32),
        compiler_params=_cparams(("parallel",)),
        name="final_norm",
    )(x, g.reshape(1, d))


def _intra_chunk(q, k, b, heads):
    c = q.shape[0]
    nb = c // SUBLANES
    row = lax.broadcasted_iota(I32, (SUBLANES, 1), 0)
    qb = [q[SUBLANES * i:SUBLANES * (i + 1)] for i in range(nb)]
    bb = [b[SUBLANES * i:SUBLANES * (i + 1)] for i in range(nb)]
    outs = [[None] * nb for _ in heads]
    for s in range(c):
        rb0 = s // SUBLANES
        ks = k[s:s + 1, :]
        bs = b[s:s + 1, :]
        for rb in range(rb0, nb):
            p = qb[rb] * (ks * jnp.exp(bb[rb] - bs))
            for hi, (mask, v) in enumerate(heads):
                pm = p if mask is None else jnp.where(mask, p, 0.0)
                col = jnp.sum(pm, axis=-1, keepdims=True)
                if rb == rb0:
                    col = jnp.where(row + SUBLANES * rb >= s, col, 0.0)
                term = col * v[s:s + 1, :]
                outs[hi][rb] = term if outs[hi][rb] is None else outs[hi][rb] + term
    return [jnp.concatenate(o, axis=0) for o in outs]


def _gated_chunk(q, k, b, heads, st):
    intra = _intra_chunk(q, k, b, heads)
    qe = q * jnp.exp(b)
    b_last = b[-1:, :]
    kh = k * jnp.exp(b_last - b)
    outs = []
    for (mask, v), oi in zip(heads, intra):
        qm = qe if mask is None else jnp.where(mask, qe, 0.0)
        outs.append(oi + _dot_nt(qm, st, precision=HI))
    upd = _dot_tn(heads[0][1], kh, precision=HI)
    if len(heads) == 2:
        upd = jnp.where(heads[0][0], upd, _dot_tn(heads[1][1], kh, precision=HI))
    st = st * jnp.exp(b_last) + upd
    return outs, st


def _head_rms(o, g):
    ms = jnp.mean(o * o, axis=-1, keepdims=True)
    return o * lax.rsqrt(ms + NORM_EPS) * g


def _hgrn_kernel(zq_ref, zf_ref, zi_ref, zg_ref, lb_ref, ng_ref, s0_ref, o_ref, s_ref, st_scr, *, chunk):
    t = pl.program_id(2)

    @pl.when(t == 0)
    def _():
        st_scr[...] = s0_ref[0, 0]

    lb = lb_ref[0]
    nchunks = zq_ref.shape[1] // chunk

    def body(ci, carry):
        sl = pl.ds(pl.multiple_of(ci * chunk, chunk), chunk)
        f = lb + (1.0 - lb) * _sigmoid(zf_ref[0, sl, :])
        q = _silu(zq_ref[0, sl, :])
        b = _cumsum_rows(jnp.log(f))
        (o,), st = _gated_chunk(q, 1.0 - f, b, [(None, zi_ref[0, sl, :])], st_scr[...])
        st_scr[...] = st
        o_ref[0, sl, :] = _head_rms(o, ng_ref[...]) * _silu(zg_ref[0, sl, :])
        return carry

    lax.fori_loop(0, nchunks, body, 0)

    @pl.when(t == pl.num_programs(2) - 1)
    def _():
        s_ref[0, 0] = st_scr[...]


def hgrn_recurrence(z, lb, norm_g, s0_t, tb, chunk):
    bsz, t, _ = z.shape
    zspec = lambda off: pl.BlockSpec((1, tb, LANES), lambda b, h, i: (b, i, h + off))
    return pl.pallas_call(
        functools.partial(_hgrn_kernel, chunk=chunk),
        grid=(bsz, A_HEADS, t // tb),
        in_specs=[zspec(0), zspec(4), zspec(8), zspec(12),
                  pl.BlockSpec((1, 1, LANES), lambda b, h, i: (h, 0, 0)),
                  pl.BlockSpec((1, LANES), lambda b, h, i: (0, 0)),
                  pl.BlockSpec((1, 1, LANES, LANES), lambda b, h, i: (b, h, 0, 0))],
        out_specs=[pl.BlockSpec((1, tb, LANES), lambda b, h, i: (b, i, h)),
                   pl.BlockSpec((1, 1, LANES, LANES), lambda b, h, i: (b, h, 0, 0))],
        out_shape=[jax.ShapeDtypeStruct((bsz, t, A_WIDTH), F32),
                   jax.ShapeDtypeStruct((bsz, A_HEADS, LANES, LANES), F32)],
        scratch_shapes=[pltpu.VMEM((LANES, LANES), F32)],
        compiler_params=_cparams(("parallel", "parallel", "arbitrary")),
        name="hgrn_recurrence",
    )(z, z, z, z, lb.reshape(A_HEADS, 1, LANES), norm_g.reshape(1, LANES), s0_t)


def _gla_kernel(zq_ref, zk_ref, zv_ref, zg_ref, zlr_ref, w2_ref, gb_ref, ng_ref, s0_ref, o_ref, s_ref, st_scr,
                *, chunk):
    t = pl.program_id(2)

    @pl.when(t == 0)
    def _():
        st_scr[...] = s0_ref[0, 0]

    nchunks = zq_ref.shape[1] // chunk
    m0 = _lane_mask(LANES, 0, B_DK)
    m1 = _lane_mask(LANES, B_DK, LANES)

    def body(ci, carry):
        sl = pl.ds(pl.multiple_of(ci * chunk, chunk), chunk)
        pre = _dot(zlr_ref[0, sl, :], w2_ref[...], precision=HI) + gb_ref[...]
        log_g = (jnp.minimum(pre, 0.0) - jnp.log(1.0 + jnp.exp(-jnp.abs(pre)))) * (1.0 / GLA_NORMALIZER)
        q = zq_ref[0, sl, :] * (B_DK ** -0.5)
        b = _cumsum_rows(log_g)
        v = zv_ref[0, sl, :]
        (o0, o1), st = _gated_chunk(q, zk_ref[0, sl, :], b,
                                    [(m0, v[:, :LANES]), (m1, v[:, LANES:])], st_scr[...])
        st_scr[...] = st
        gate = _silu(zg_ref[0, sl, :])
        o_ref[0, sl, 0:LANES] = _head_rms(o0, ng_ref[...]) * gate[:, :LANES]
        o_ref[0, sl, LANES:2 * LANES] = _head_rms(o1, ng_ref[...]) * gate[:, LANES:]
        return carry

    lax.fori_loop(0, nchunks, body, 0)

    @pl.when(t == pl.num_programs(2) - 1)
    def _():
        s_ref[0, 0] = st_scr[...]


def gla_recurrence(z, w2pad, gate_b, norm_g, s0_t, tb, chunk):
    bsz, t, _ = z.shape
    npairs = B_HEADS // 2
    return pl.pallas_call(
        functools.partial(_gla_kernel, chunk=chunk),
        grid=(bsz, npairs, t // tb),
        in_specs=[pl.BlockSpec((1, tb, LANES), lambda b, p, i: (b, i, 16 + p)),
                  pl.BlockSpec((1, tb, LANES), lambda b, p, i: (b, i, 18 + p)),
                  pl.BlockSpec((1, tb, 2 * LANES), lambda b, p, i: (b, i, 10 + p)),
                  pl.BlockSpec((1, tb, 2 * LANES), lambda b, p, i: (b, i, 12 + p)),
                  pl.BlockSpec((1, tb, LANES), lambda b, p, i: (b, i, 28)),
                  pl.BlockSpec((LANES, LANES), lambda b, p, i: (0, p)),
                  pl.BlockSpec((1, LANES), lambda b, p, i: (0, p)),
                  pl.BlockSpec((1, LANES), lambda b, p, i: (0, 0)),
                  pl.BlockSpec((1, 1, LANES, LANES), lambda b, p, i: (b, p, 0, 0))],
        out_specs=[pl.BlockSpec((1, tb, 2 * LANES), lambda b, p, i: (b, i, p)),
                   pl.BlockSpec((1, 1, LANES, LANES), lambda b, p, i: (b, p, 0, 0))],
        out_shape=[jax.ShapeDtypeStruct((bsz, t, B_WIDTH), F32),
                   jax.ShapeDtypeStruct((bsz, npairs, LANES, LANES), F32)],
        scratch_shapes=[pltpu.VMEM((LANES, LANES), F32)],
        compiler_params=_cparams(("parallel", "parallel", "arbitrary")),
        name="gla_recurrence",
    )(z, z, z, z, z, w2pad, gate_b.reshape(1, 2 * LANES), norm_g.reshape(1, LANES), s0_t)


def _rwkv_proj_kernel(x_ref, xp_ref, xl_ref, g1_ref, mu_ref, wr_ref, wk_ref, wv_ref, ww1_ref, ww2_ref, w0_ref,
                      aw1_ref, aw2_ref, a0_ref, gw1_ref, gw2_ref, kk_ref, ka_ref,
                      r_out, k_out, v_out, lw_out, a_out, kk_out, g_out, hl_out, *, tiles_per_seq):
    i = pl.program_id(0)
    g1 = g1_ref[...]
    hn = _rms(x_ref[...], g1)
    tm = hn.shape[0]
    prev = _rms(xp_ref[...], g1)[SUBLANES - 1:SUBLANES, :]
    prev = jnp.where(i % tiles_per_seq == 0, xl_ref[0], prev)
    row = lax.broadcasted_iota(I32, (tm, 1), 0)
    xprev = jnp.where(row == 0, prev, pltpu.roll(hn, 1, axis=0))
    dx = xprev - hn

    def mix(j):
        return hn + dx * mu_ref[j:j + 1, :]

    r = _dot(mix(0).astype(BF16), wr_ref[...])
    k = _dot(mix(1).astype(BF16), wk_ref[...])
    v = _dot(mix(2).astype(BF16), wv_ref[...])
    wl = _dot(jnp.tanh(_dot(mix(3), ww1_ref[...], precision=HI)), ww2_ref[...], precision=HI)
    z = w0_ref[...] + wl
    wpre = -(jnp.maximum(-z, 0.0) + jnp.log(1.0 + jnp.exp(-jnp.abs(z)))) - 0.5
    al = _dot(_dot(mix(4), aw1_ref[...], precision=HI), aw2_ref[...], precision=HI)
    a = _sigmoid(a0_ref[...] + al)
    gg = _dot(_sigmoid(_dot(mix(5).astype(BF16), gw1_ref[...])).astype(BF16), gw2_ref[...])
    r_out[...] = r
    k_out[...] = k * (1.0 + (a - 1.0) * ka_ref[...])
    v_out[...] = v
    lw_out[...] = -jnp.exp(wpre)
    a_out[...] = a
    kk_out[...] = k * kk_ref[...]
    g_out[...] = gg
    hl_out[0] = hn[tm - 1:tm, :]


def rwkv_proj(x, x_last, seq_len, g1, mu, wr, wk, wv, ww1, ww2, w0, aw1, aw2, a0, gw1, gw2, k_k, k_a, tm):
    n, d = x.shape
    tiles_per_seq = seq_len // tm
    row = lambda a: a.reshape(1, d)
    full = lambda a: pl.BlockSpec(a.shape, lambda i: (0,) * a.ndim)
    tile = pl.BlockSpec((tm, d), lambda i: (i, 0))
    blocks8 = tm // SUBLANES
    args = (x, x, x_last.reshape(-1, 1, d), row(g1), mu, wr, wk, wv, ww1, ww2, row(w0), aw1, aw2, row(a0),
            gw1, gw2, row(k_k), row(k_a))
    in_specs = [tile,
                pl.BlockSpec((SUBLANES, d), lambda i: (jnp.maximum(i * blocks8 - 1, 0), 0)),
                pl.BlockSpec((1, 1, d), lambda i: (i // tiles_per_seq, 0, 0))]
    in_specs += [full(a) for a in args[3:]]
    outs = pl.pallas_call(
        functools.partial(_rwkv_proj_kernel, tiles_per_seq=tiles_per_seq),
        grid=(n // tm,),
        in_specs=in_specs,
        out_specs=[tile] * 7 + [pl.BlockSpec((1, 1, d), lambda i: (i, 0, 0))],
        out_shape=[jax.ShapeDtypeStruct((n, d), F32)] * 7 + [jax.ShapeDtypeStruct((n // tm, 1, d), F32)],
        compiler_params=_cparams(("parallel",)),
        name="rwkv_proj",
    )(*args)
    return outs


_NN = ((1,), (0,))
_NT = ((1,), (1,))
_TN = ((0,), (0,))
RWKV_AB_PASSES = 1
RWKV_INV_PASSES = 1
RWKV_APPLY_PASSES = 1
RWKV_STATE_PASSES = 3
RWKV_SEQS_PER_STEP = 4


def _split_bf16(a):
    hi = a.astype(BF16)
    return hi, (a - hi.astype(F32)).astype(BF16)


def _mm(a, b, dims, passes):
    if passes == 6:
        return lax.dot_general(a, b, (dims, ((), ())), preferred_element_type=F32, precision=HI)
    dg = lambda x, y: lax.dot_general(x, y, (dims, ((), ())), preferred_element_type=F32)
    ah, al = _split_bf16(a)
    bh, bl = _split_bf16(b)
    if passes == 1:
        return dg(ah, bh)
    return dg(ah, bh) + (dg(al, bh) + dg(ah, bl))


def _cumsum_rows3(g):
    tri = _tri(g.shape[0], False).astype(BF16)
    h1 = g.astype(BF16)
    r1 = g - h1.astype(F32)
    h2 = r1.astype(BF16)
    h3 = (r1 - h2.astype(F32)).astype(BF16)
    return _dot(tri, h1) + (_dot(tri, h2) + _dot(tri, h3))


def _pair_sum(x, m0):
    s0 = jnp.sum(jnp.where(m0, x, 0.0), axis=-1, keepdims=True)
    s1 = jnp.sum(jnp.where(m0, 0.0, x), axis=-1, keepdims=True)
    return jnp.where(m0, s0, s1)


def _rwkv_kernel(r_ref, k_ref, v_ref, lw_ref, a_ref, kk_ref, g_ref, rk_ref, lng_ref, lnb_ref, s0_ref,
                 o_ref, s_ref, mt_scr, *, chunk):
    t = pl.program_id(2)

    @pl.when(t == 0)
    def _():
        mt_scr[...] = s0_ref[:, 0]

    nrows = r_ref.shape[0]
    nchunks = r_ref.shape[1] // chunk
    c2 = 2 * chunk
    m0 = _lane_mask(LANES, 0, C_HEAD)
    rowi = lax.broadcasted_iota(I32, (LANES, LANES), 0)
    coli = lax.broadcasted_iota(I32, (LANES, LANES), 1)
    blockdiag = (rowi < C_HEAD) == (coli < C_HEAD)
    ti = lax.broadcasted_iota(I32, (c2, c2), 0)
    si = lax.broadcasted_iota(I32, (c2, c2), 1)
    same_head = (ti < chunk) == (si < chunk)
    tm_ = jnp.where(ti < chunk, ti, ti - chunk)
    sm_ = jnp.where(si < chunk, si, si - chunk)
    strict = same_head & (sm_ < tm_)
    incl = same_head & (sm_ <= tm_)

    def stack_heads(x):
        return jnp.concatenate([jnp.where(m0, x, 0.0), jnp.where(m0, 0.0, x)], axis=0)

    def twice(x):
        return jnp.concatenate([x, x], axis=0)

    def unstack(x2):
        return jnp.where(m0, x2[:chunk], x2[chunk:])

    eye = (ti == si).astype(F32)
    seqs = range(nrows)

    def body(ci, carry):
        sl = pl.ds(pl.multiple_of(ci * chunk, chunk), chunk)
        r = [r_ref[i, sl, :] for i in seqs]
        k = [k_ref[i, sl, :] for i in seqs]
        v = [v_ref[i, sl, :] for i in seqs]
        lw = [lw_ref[i, sl, :] for i in seqs]
        kkr = [kk_ref[i, sl, :] for i in seqs]
        kk = [x * lax.rsqrt(_pair_sum(x * x, m0) + 1e-12) for x in kkr]
        al = [a_ref[i, sl, :] * kk[i] for i in seqs]
        gam = [_cumsum_rows3(x) for x in lw]
        e_neg = [jnp.exp(-x) for x in gam]
        xr = [jnp.concatenate([stack_heads(kk[i] * jnp.exp(gam[i] - lw[i])),
                               stack_heads(r[i] * jnp.exp(gam[i]))], axis=0) for i in seqs]
        alk = [jnp.concatenate([twice(al[i] * e_neg[i]), twice(k[i] * e_neg[i])], axis=0) for i in seqs]
        mt = [mt_scr[i] for i in seqs]
        ab = [_mm(xr[i], alk[i], _NT, RWKV_AB_PASSES) for i in seqs]
        xm = [_mm(xr[i], mt[i], _NT, RWKV_STATE_PASSES) for i in seqs]
        a_al = [jnp.where(strict, x[:c2, :c2], 0.0) for x in ab]
        a_k = [jnp.where(strict, x[:c2, c2:], 0.0) for x in ab]
        b_alk = [jnp.concatenate([jnp.where(incl, x[c2:, c2:], 0.0), jnp.where(incl, -x[c2:, :c2], 0.0)], axis=1)
                 for x in ab]
        v2 = [twice(x) for x in v]
        rhs = [xm[i][:c2] + _mm(a_k[i], v2[i], _NN, RWKV_APPLY_PASSES) for i in seqs]
        p = [-x for x in a_al]
        tinv = [eye + x for x in p]
        span = 2
        while span < chunk:
            p = [_mm(x, x, _NN, RWKV_INV_PASSES) for x in p]
            tinv = [tinv[i] + _mm(tinv[i], p[i], _NN, RWKV_INV_PASSES) for i in seqs]
            span *= 2
        u = [unstack(_mm(tinv[i], rhs[i], _NN, RWKV_APPLY_PASSES)) for i in seqs]
        o = [unstack(xm[i][c2:] + _mm(b_alk[i], jnp.concatenate([v2[i], twice(u[i])], axis=0), _NN,
                                      RWKV_APPLY_PASSES)) for i in seqs]
        g_last = [x[-1:, :] for x in gam]
        e_end = [jnp.exp(g_last[i] - gam[i]) for i in seqs]
        upd = [_mm(jnp.concatenate([v[i], u[i]], axis=0),
                   jnp.concatenate([k[i] * e_end[i], -(al[i] * e_end[i])], axis=0), _TN, RWKV_STATE_PASSES)
               for i in seqs]
        for i in seqs:
            mt_scr[i] = mt[i] * jnp.exp(g_last[i]) + jnp.where(blockdiag, upd[i], 0.0)
        for i in seqs:
            mean = _pair_sum(o[i], m0) * (1.0 / C_HEAD)
            cen = o[i] - mean
            var = _pair_sum(cen * cen, m0) * (1.0 / C_HEAD)
            on = cen * lax.rsqrt(var + C_GN_EPS) * lng_ref[...] + lnb_ref[...]
            bonus = _pair_sum(r[i] * k[i] * rk_ref[...], m0) * v[i]
            o_ref[i, sl, :] = (on + bonus) * g_ref[i, sl, :]
        return carry

    lax.fori_loop(0, nchunks, body, 0)

    @pl.when(t == pl.num_programs(2) - 1)
    def _():
        s_ref[:, 0] = mt_scr[...]


def rwkv_recurrence(r, k, v, lw, a, kk, g, r_k, ln_g, ln_b, s0_bd, tb, chunk, nb):
    bsz, t, d = r.shape
    npairs = C_HEADS // 2
    seq = pl.BlockSpec((nb, tb, LANES), lambda b, p, i: (b, i, p))
    vec = pl.BlockSpec((1, LANES), lambda b, p, i: (0, p))
    st = pl.BlockSpec((nb, 1, LANES, LANES), lambda b, p, i: (b, p, 0, 0))
    return pl.pallas_call(
        functools.partial(_rwkv_kernel, chunk=chunk),
        grid=(bsz // nb, npairs, t // tb),
        in_specs=[seq] * 7 + [vec, vec, vec, st],
        out_specs=[seq, st],
        out_shape=[jax.ShapeDtypeStruct((bsz, t, d), F32),
                   jax.ShapeDtypeStruct((bsz, npairs, LANES, LANES), F32)],
        scratch_shapes=[pltpu.VMEM((nb, LANES, LANES), F32)],
        compiler_params=_cparams(("parallel", "parallel", "arbitrary")),
        name="rwkv_recurrence",
    )(r, k, v, lw, a, kk, g, r_k.reshape(1, d), ln_g.reshape(1, d), ln_b.reshape(1, d), s0_bd)


NEG_INF = float("-inf")


def _top16_rows(s):
    n = s.shape[0]
    key = lax.broadcasted_iota(I32, s.shape, 0)
    vals, idxs = [], []
    for _ in range(P_TOPK):
        m = jnp.max(s, axis=0, keepdims=True)
        am = jnp.min(jnp.where(s == m, key, n), axis=0, keepdims=True)
        vals.append(m)
        idxs.append(am)
        s = jnp.where(key == am, NEG_INF, s)
    return vals, idxs


def _top16_pairs(v0, i0, v1, i1):
    a0 = jnp.concatenate(v0[0:8], axis=0)
    a1 = jnp.concatenate(v0[8:16], axis=0)
    b0 = jnp.concatenate(v1[0:8], axis=0)
    b1 = jnp.concatenate(v1[8:16], axis=0)
    ia0 = jnp.concatenate(i0[0:8], axis=0) * P_KEYS
    ia1 = jnp.concatenate(i0[8:16], axis=0) * P_KEYS
    ib0 = jnp.concatenate(i1[0:8], axis=0)
    ib1 = jnp.concatenate(i1[8:16], axis=0)
    row = lax.broadcasted_iota(I32, (SUBLANES, 1), 0)
    slabs = []

    def add(val, eid, keep):
        slabs.append((val if keep is None else jnp.where(keep, val, NEG_INF), eid))

    add(v0[0] + b0, ia0[0:1] + ib0, None)
    add(v0[0] + b1, ia0[0:1] + ib1, None)
    add(v0[1] + b0, ia0[1:2] + ib0, None)
    add(v0[2] + b0, ia0[2:3] + ib0, row < 5)
    add(v0[3] + b0, ia0[3:4] + ib0, row < 4)
    add(a0 + v1[0], ia0 + ib0[0:1], row >= 4)
    add(a1 + v1[0], ia1 + ib0[0:1], None)
    add(a0 + v1[1], ia0 + ib0[1:2], row >= 4)
    add(a0 + v1[2], ia0 + ib0[2:3], row == 4)

    big = P_KEYS * P_KEYS
    out_v, out_e = [], []
    for _ in range(P_TOPK):
        m = slabs[0][0]
        for val, _e in slabs[1:]:
            m = jnp.maximum(m, val)
        m = jnp.max(m, axis=0, keepdims=True)
        e = None
        for val, eid in slabs:
            c = jnp.where(val == m, eid, big)
            e = c if e is None else jnp.minimum(e, c)
        e = jnp.min(e, axis=0, keepdims=True)
        out_v.append(m)
        out_e.append(e)
        slabs = [(jnp.where(eid == e, NEG_INF, val), eid) for val, eid in slabs]
    return out_v, out_e


def _peer_select_kernel(x_ref, g_ref, wq_ref, keys_ref, xn_out, eid_out, gate_out):
    hn = _rms(x_ref[...], g_ref[...])
    xn_out[...] = hn
    q = _dot(hn.astype(BF16), wq_ref[...])
    tm = q.shape[0]
    for lt in range(tm // LANES):
        rows = slice(lt * LANES, (lt + 1) * LANES)
        e_rows, g_rows = [], []
        for h in range(P_HEADS):
            tops = []
            for p in range(2):
                hp = 2 * h + p
                s = _dot_nt(keys_ref[hp], q[rows, hp * LANES:(hp + 1) * LANES], precision=HI)
                tops.append(_top16_rows(s))
            cs, ce = _top16_pairs(tops[0][0], tops[0][1], tops[1][0], tops[1][1])
            ex = [jnp.exp(c - cs[0]) for c in cs]
            tot = ex[0]
            for e in ex[1:]:
                tot = tot + e
            inv = 1.0 / tot
            e_rows += ce
            g_rows += [e * inv for e in ex]
        eid_out[rows, :] = jnp.concatenate(e_rows, axis=0).T
        gate_out[rows, :] = jnp.concatenate(g_rows, axis=0).T


def peer_select(x, g, wq_bf16, keys, tm):
    n, d = x.shape
    return pl.pallas_call(
        _peer_select_kernel,
        grid=(n // tm,),
        in_specs=[pl.BlockSpec((tm, d), lambda i: (i, 0)),
                  pl.BlockSpec((1, d), lambda i: (0, 0)),
                  pl.BlockSpec(wq_bf16.shape, lambda i: (0, 0)),
                  pl.BlockSpec(keys.shape, lambda i: (0, 0, 0))],
        out_specs=[pl.BlockSpec((tm, d), lambda i: (i, 0)),
                   pl.BlockSpec((tm, P_PICKS), lambda i: (i, 0)),
                   pl.BlockSpec((tm, P_PICKS), lambda i: (i, 0))],
        out_shape=[jax.ShapeDtypeStruct((n, d), F32),
                   jax.ShapeDtypeStruct((n, P_PICKS), I32),
                   jax.ShapeDtypeStruct((n, P_PICKS), F32)],
        compiler_params=_cparams(("parallel",)),
        name="peer_select",
    )(x, g.reshape(1, d), wq_bf16, keys)


def _peer_act_kernel(h_ref, g_ref, o_ref):
    h = h_ref[...]
    o_ref[...] = 0.5 * h * (1.0 + lax.erf(h * (2.0 ** -0.5))) * g_ref[...]


def peer_act(hid, gate, tm):
    n, p = hid.shape
    spec = pl.BlockSpec((tm, p), lambda i: (i, 0))
    return pl.pallas_call(
        _peer_act_kernel, grid=(n // tm,), in_specs=[spec, spec], out_specs=spec,
        out_shape=jax.ShapeDtypeStruct((n, p), F32),
        compiler_params=_cparams(("parallel",)), name="peer_act",
    )(hid, gate)


_CHUNKS_PER_TOK = P_PICKS // SC_CHUNK
_WORD_VREGS = SC_WORDS // SC_LANES
U32 = jnp.uint32


def pack_rows_bf16(t):
    lo = lax.bitcast_convert_type(t[..., :SC_WORDS].astype(BF16), jnp.uint16).astype(U32)
    a = lax.bitcast_convert_type(t[..., SC_WORDS:], U32)
    sign = a & U32(0x80000000)
    mag = a & U32(0x7FFFFFFF)
    steps = (jnp.maximum(mag + U32(0x8000), lo) - lo) >> 16
    near = (steps << 16) + lo
    near = jnp.where(near >= U32(0x7F800000), near - U32(0x10000), near)
    return lax.bitcast_convert_type(sign | near, I32)


def _low_f32(w):
    return lax.bitcast_convert_type(w << 16, F32)


def _high_f32(w):
    return lax.bitcast_convert_type(w, F32)


def _sc_worker():
    return lax.axis_index("s") * SC_CORES + lax.axis_index("c")


def _sc_hid_body(tab_hbm, idx_hbm, x_hbm, hid_hbm, idx_v, x_v, hid_v, part_v, rows_a, rows_b, sem_a, sem_b,
                 *, tok_per_w):
    wid = _sc_worker()
    nblk = tok_per_w // SC_TOK_BLOCK
    nch = SC_TOK_BLOCK * _CHUNKS_PER_TOK
    lane = lax.iota(I32, SC_LANES)
    half_w = _WORD_VREGS // 2

    def dots(rows, x_row, res_ref, col0):
        for xp in range(2):
            base = xp * half_w * SC_LANES
            xlo = [x_v[x_row, pl.ds(base + j * SC_LANES, SC_LANES)] for j in range(half_w)]
            xhi = [x_v[x_row, pl.ds(SC_WORDS + base + j * SC_LANES, SC_LANES)] for j in range(half_w)]

            def partial_dot(r):
                accs = []
                for j in range(half_w):
                    w = rows[r, pl.ds(base + j * SC_LANES, SC_LANES)]
                    term = _low_f32(w) * xlo[j] + _high_f32(w) * xhi[j]
                    if j < 4:
                        accs.append(term)
                    else:
                        accs[j % 4] = accs[j % 4] + term
                return (accs[0] + accs[1]) + (accs[2] + accs[3])

            def one_row(r, c):
                acc = partial_dot(r)
                part_v[r, :] = acc if xp == 0 else acc + part_v[r, :]
                return c
            lax.fori_loop(0, SC_CHUNK, one_row, 0)

        for grp in range(SC_CHUNK // SC_LANES):
            rid = lane + grp * SC_LANES
            res = plsc.load_gather(part_v, [rid, lane])
            for l in range(1, SC_LANES):
                res = res + plsc.load_gather(part_v, [rid, (lane + l) & (SC_LANES - 1)])
            res_ref[x_row, pl.ds(col0 + grp * SC_LANES, SC_LANES)] = res

    def block(bi, carry):
        tok0 = wid * tok_per_w + bi * SC_TOK_BLOCK
        pltpu.sync_copy(idx_hbm.at[pl.ds(tok0 * _CHUNKS_PER_TOK, nch)], idx_v)
        pltpu.sync_copy(x_hbm.at[pl.ds(tok0, SC_TOK_BLOCK)], x_v)
        pltpu.async_copy(tab_hbm.at[idx_v.at[0]], rows_a, sem_a)

        def pair(j, c):
            ca = 2 * j
            pltpu.async_copy(tab_hbm.at[idx_v.at[ca + 1]], rows_b, sem_b)
            pltpu.make_async_copy(tab_hbm.at[idx_v.at[0]], rows_a, sem_a).wait()
            dots(rows_a, ca // _CHUNKS_PER_TOK, hid_v, (ca % _CHUNKS_PER_TOK) * SC_CHUNK)

            @pl.when(j < nch // 2 - 1)
            def _():
                pltpu.async_copy(tab_hbm.at[idx_v.at[ca + 2]], rows_a, sem_a)
            pltpu.make_async_copy(tab_hbm.at[idx_v.at[0]], rows_b, sem_b).wait()
            cb = ca + 1
            dots(rows_b, cb // _CHUNKS_PER_TOK, hid_v, (cb % _CHUNKS_PER_TOK) * SC_CHUNK)
            return c
        lax.fori_loop(0, nch // 2, pair, 0)
        pltpu.sync_copy(hid_v, hid_hbm.at[pl.ds(tok0, SC_TOK_BLOCK)])
        return carry

    lax.fori_loop(0, nblk, block, 0)


def sc_expert_hidden(table_packed, eidx, xn):
    n = xn.shape[0]
    tok_per_w = n // SC_WORKERS
    mesh = plsc.VectorSubcoreMesh(core_axis_name="c", subcore_axis_name="s")
    nch = SC_TOK_BLOCK * _CHUNKS_PER_TOK
    k = pl.kernel(
        functools.partial(_sc_hid_body, tok_per_w=tok_per_w), mesh=mesh,
        out_type=jax.ShapeDtypeStruct((n, P_PICKS), F32),
        scratch_types=[pltpu.VMEM((nch, SC_CHUNK), I32),
                       pltpu.VMEM((SC_TOK_BLOCK, D_MODEL), F32),
                       pltpu.VMEM((SC_TOK_BLOCK, P_PICKS), F32),
                       pltpu.VMEM((SC_CHUNK, SC_LANES), F32),
                       pltpu.VMEM((SC_CHUNK, SC_WORDS), I32),
                       pltpu.VMEM((SC_CHUNK, SC_WORDS), I32),
                       pltpu.SemaphoreType.DMA, pltpu.SemaphoreType.DMA],
        compiler_params=pltpu.CompilerParams(needs_layout_passes=False),
        name="sc_expert_hidden",
    )
    return k(table_packed, eidx.reshape(n * _CHUNKS_PER_TOK, SC_CHUNK), xn)


def _sc_out_body(tab_hbm, idx_hbm, act_hbm, x_hbm, y_hbm, idx_v, act_v, y_v, rows_a, rows_b, sem_a, sem_b,
                 *, tok_per_w):
    wid = _sc_worker()
    nblk = tok_per_w // SC_TOK_BLOCK
    nch = SC_TOK_BLOCK * _CHUNKS_PER_TOK
    half_w = _WORD_VREGS // 2

    def accum(rows, tok, col0):
        for hv in range(2):
            base = hv * half_w * SC_LANES
            acc0 = (tuple(y_v[tok, pl.ds(base + j * SC_LANES, SC_LANES)] for j in range(half_w))
                    + tuple(y_v[tok, pl.ds(SC_WORDS + base + j * SC_LANES, SC_LANES)] for j in range(half_w)))

            def one(r, acc):
                aidx = jnp.full((SC_LANES,), col0, I32) + r
                wgt = plsc.load_gather(act_v, [jnp.full((SC_LANES,), tok, I32), aidx])
                lo, hi = [], []
                for j in range(half_w):
                    w = rows[r, pl.ds(base + j * SC_LANES, SC_LANES)]
                    lo.append(acc[j] + wgt * _low_f32(w))
                    hi.append(acc[half_w + j] + wgt * _high_f32(w))
                return tuple(lo + hi)
            acc = lax.fori_loop(0, SC_CHUNK, one, acc0)
            for j in range(half_w):
                y_v[tok, pl.ds(base + j * SC_LANES, SC_LANES)] = acc[j]
                y_v[tok, pl.ds(SC_WORDS + base + j * SC_LANES, SC_LANES)] = acc[half_w + j]

    def block(bi, carry):
        tok0 = wid * tok_per_w + bi * SC_TOK_BLOCK
        pltpu.sync_copy(idx_hbm.at[pl.ds(tok0 * _CHUNKS_PER_TOK, nch)], idx_v)
        pltpu.sync_copy(act_hbm.at[pl.ds(tok0, SC_TOK_BLOCK)], act_v)
        pltpu.sync_copy(x_hbm.at[pl.ds(tok0, SC_TOK_BLOCK)], y_v)
        pltpu.async_copy(tab_hbm.at[idx_v.at[0]], rows_a, sem_a)

        def pair(j, c):
            ca = 2 * j
            pltpu.async_copy(tab_hbm.at[idx_v.at[ca + 1]], rows_b, sem_b)
            pltpu.make_async_copy(tab_hbm.at[idx_v.at[0]], rows_a, sem_a).wait()
            accum(rows_a, ca // _CHUNKS_PER_TOK, (ca % _CHUNKS_PER_TOK) * SC_CHUNK)

            @pl.when(j < nch // 2 - 1)
            def _():
                pltpu.async_copy(tab_hbm.at[idx_v.at[ca + 2]], rows_a, sem_a)
            pltpu.make_async_copy(tab_hbm.at[idx_v.at[0]], rows_b, sem_b).wait()
            cb = ca + 1
            accum(rows_b, cb // _CHUNKS_PER_TOK, (cb % _CHUNKS_PER_TOK) * SC_CHUNK)
            return c
        lax.fori_loop(0, nch // 2, pair, 0)
        pltpu.sync_copy(y_v, y_hbm.at[pl.ds(tok0, SC_TOK_BLOCK)])
        return carry

    lax.fori_loop(0, nblk, block, 0)


def sc_expert_output(table_packed, eidx, act, x):
    n = x.shape[0]
    tok_per_w = n // SC_WORKERS
    mesh = plsc.VectorSubcoreMesh(core_axis_name="c", subcore_axis_name="s")
    nch = SC_TOK_BLOCK * _CHUNKS_PER_TOK
    k = pl.kernel(
        functools.partial(_sc_out_body, tok_per_w=tok_per_w), mesh=mesh,
        out_type=jax.ShapeDtypeStruct((n, D_MODEL), F32),
        scratch_types=[pltpu.VMEM((nch, SC_CHUNK), I32),
                       pltpu.VMEM((SC_TOK_BLOCK, P_PICKS), F32),
                       pltpu.VMEM((SC_TOK_BLOCK, D_MODEL), F32),
                       pltpu.VMEM((SC_CHUNK, SC_WORDS), I32),
                       pltpu.VMEM((SC_CHUNK, SC_WORDS), I32),
                       pltpu.SemaphoreType.DMA, pltpu.SemaphoreType.DMA],
        compiler_params=pltpu.CompilerParams(needs_layout_passes=False),
        name="sc_expert_output",
    )
    return k(table_packed, eidx.reshape(n * _CHUNKS_PER_TOK, SC_CHUNK), act, x)


def peer_ffn(x, g, wq_bf16, keys, u_packed, v_packed, tm):
    xn, eidx, gate = peer_select(x, g, wq_bf16, keys, tm)
    hid = sc_expert_hidden(u_packed, eidx, xn)
    act = peer_act(hid, gate, tm)
    return sc_expert_output(v_packed, eidx, act, x)


def _prep_w_in(w_in):
    main = jnp.concatenate([w_in[:, :3072], w_in[:, 3088:3600]], axis=1)
    lr = jnp.pad(w_in[:, 3072:3088], ((0, 0), (0, LANES - GLA_RANK)))
    return jnp.concatenate([main, lr], axis=1).astype(BF16)


def _trunk(x, st_h, st_g, st_r, st_s, w, seq_len, tm, tmp, tb, chunk):
    bsz = x.shape[0]
    n = bsz * seq_len
    x2 = x.reshape(n, D_MODEL)

    z = norm_proj(x2, w["norm1_g"][0], w["w_in"], tm).reshape(bsz, seq_len, Z_WIDTH)
    s0_h = jnp.swapaxes(st_h, -1, -2)
    s0_g = jnp.swapaxes(st_g.reshape(bsz, 2, 2 * B_DK, LANES), -1, -2)
    o_a, sh_t = hgrn_recurrence(z, w["lb0"], w["hgrn_norm_g"], s0_h, tb, chunk)
    o_b, sg_t = gla_recurrence(z, w["gla_w2"], w["gla_b"], w["gla_norm_g"], s0_g, tb, chunk)
    new_h = jnp.swapaxes(sh_t, -1, -2)
    new_g = jnp.swapaxes(sg_t, -1, -2).reshape(bsz, B_HEADS, B_DK, LANES)
    x2 = out_proj2(x2, o_a.reshape(n, A_WIDTH), o_b.reshape(n, B_WIDTH), w["w_out_a"], w["w_out_b"], tm)
    x2 = peer_ffn(x2, w["norm2_g"][0], w["peer_wq"][0], w["peer_keys"][0], w["peer_u"][0], w["peer_v"][0], tmp)

    r, k, v, lw, a, kk, g, hl = rwkv_proj(
        x2, st_s, seq_len, w["norm1_g"][1], w["mu"], w["wr"], w["wk"], w["wv"], w["w_w1"], w["w_w2"], w["w0"],
        w["a_w1"], w["a_w2"], w["a0"], w["g_w1"], w["g_w2"], w["k_k"], w["k_a"], tm)
    new_s = hl.reshape(bsz, seq_len // tm, D_MODEL)[:, -1]
    pr = st_r.reshape(bsz, C_HEADS // 2, 2, C_HEAD, C_HEAD)
    zero = jnp.zeros_like(pr[:, :, 0])
    s0_r = jnp.concatenate([jnp.concatenate([pr[:, :, 0], zero], axis=-1),
                            jnp.concatenate([zero, pr[:, :, 1]], axis=-1)], axis=-2)
    sh3 = lambda t: t.reshape(bsz, seq_len, D_MODEL)
    o_c, sr_bd = rwkv_recurrence(sh3(r), sh3(k), sh3(v), sh3(lw), sh3(a), sh3(kk), sh3(g),
                                 w["r_k"], w["ln_g"], w["ln_b"], s0_r, tb, chunk, min(RWKV_SEQS_PER_STEP, bsz))
    new_r = jnp.stack([sr_bd[:, :, :C_HEAD, :C_HEAD], sr_bd[:, :, C_HEAD:, C_HEAD:]], axis=2)
    new_r = new_r.reshape(bsz, C_HEADS, C_HEAD, C_HEAD)
    x2 = out_proj1(x2, o_c.reshape(n, D_MODEL), w["w_out_c"], tm)
    x2 = peer_ffn(x2, w["norm2_g"][1], w["peer_wq"][1], w["peer_keys"][1], w["peer_u"][1], w["peer_v"][1], tmp)

    y = final_norm(x2, w["final_g"], tm).reshape(bsz, seq_len, D_MODEL)
    return y, new_h[None], new_g[None], new_r[None], new_s[None]


def kernel(x_prompt, x_sample, state_hgrn, state_gla, state_rwkv, state_shift, w_in_ab, hgrn_lower_bounds, hgrn_norm_g, gla_gate_w2, gla_gate_b, gla_norm_g, w_out_ab, rwkv_mu, rwkv_w_rkv, rwkv_w_w1, rwkv_w_w2, rwkv_w0, rwkv_a_w1, rwkv_a_w2, rwkv_a0, rwkv_g_w1, rwkv_g_w2, rwkv_k_k, rwkv_k_a, rwkv_r_k, rwkv_ln_g, rwkv_ln_b, w_out_c, norm1_g, norm2_g, final_g, peer_w_q, peer_sub_keys, peer_u, peer_v):
    lbs = jnp.cumsum(jax.nn.softmax(hgrn_lower_bounds.astype(F32), axis=0), axis=0)
    w = dict(
        norm1_g=norm1_g, norm2_g=norm2_g, final_g=final_g,
        w_in=_prep_w_in(w_in_ab[0]), lb0=lbs[0], hgrn_norm_g=hgrn_norm_g[0],
        gla_w2=jnp.pad(gla_gate_w2[0], ((0, LANES - GLA_RANK), (0, 0))), gla_b=gla_gate_b[0],
        gla_norm_g=gla_norm_g[0],
        w_out_a=w_out_ab[0, :A_WIDTH].astype(BF16), w_out_b=w_out_ab[0, A_WIDTH:].astype(BF16),
        mu=rwkv_mu[0], wr=rwkv_w_rkv[0, 0].astype(BF16), wk=rwkv_w_rkv[0, 1].astype(BF16),
        wv=rwkv_w_rkv[0, 2].astype(BF16), w_w1=rwkv_w_w1[0], w_w2=rwkv_w_w2[0], w0=rwkv_w0[0],
        a_w1=rwkv_a_w1[0], a_w2=rwkv_a_w2[0], a0=rwkv_a0[0],
        g_w1=rwkv_g_w1[0].astype(BF16), g_w2=rwkv_g_w2[0].astype(BF16),
        k_k=rwkv_k_k[0], k_a=rwkv_k_a[0], r_k=rwkv_r_k[0], ln_g=rwkv_ln_g[0], ln_b=rwkv_ln_b[0],
        w_out_c=w_out_c[0].astype(BF16),
        peer_wq=peer_w_q.astype(BF16),
        peer_keys=peer_sub_keys.reshape(peer_sub_keys.shape[0], 2 * P_HEADS, P_KEYS, P_KEYS),
        peer_u=pack_rows_bf16(peer_u), peer_v=pack_rows_bf16(peer_v),
    )
    bp, tp, _ = x_prompt.shape
    bs, ts, _ = x_sample.shape
    assert sum(PROMPT_GROUPS) == bp
    groups, start = [], 0
    for gsz in PROMPT_GROUPS:
        zeros = lambda s: jnp.zeros((gsz,) + s.shape[2:], F32)
        groups.append(_trunk(x_prompt[start:start + gsz], zeros(state_hgrn), zeros(state_gla), zeros(state_rwkv),
                             zeros(state_shift), w, tp, tm=256, tmp=256, tb=512, chunk=64))
        start += gsz
    y_p = jnp.concatenate([g[0] for g in groups], axis=0)
    p_h, p_g, p_r, p_s = (jnp.concatenate([g[j] for g in groups], axis=1) for j in range(1, 5))
    y_s, s_h, s_g, s_r, s_s = _trunk(x_sample, state_hgrn[0], state_gla[0], state_rwkv[0], state_shift[0],
                                     w, ts, tm=32, tmp=128, tb=32, chunk=32)
    return (y_p, y_s, p_h, p_g, p_r, p_s, s_h, s_g, s_r, s_s)
```

```python
import functools

import jax
import jax.numpy as jnp
from jax import lax
from jax.experimental import pallas as pl
from jax.experimental.pallas import tpu as pltpu
from jax.experimental.pallas import tpu_sc as plsc

F32 = jnp.float32
BF16 = jnp.bfloat16
I32 = jnp.int32
HI = lax.Precision.HIGHEST

D_MODEL = 1024
NORM_EPS = 1e-6
LANES = 128
SUBLANES = 8
VMEM_LIMIT = 56 * 1024 * 1024

A_WIDTH = 512
A_HEADS = 4
B_WIDTH = 512
B_HEADS = 4
B_DK = 64
GLA_RANK = 16
GLA_NORMALIZER = 16.0
Z_WIDTH = 3712
C_HEAD = 64
C_HEADS = 16
C_GN_EPS = 64e-5
P_HEADS = 8
P_KEYS = 128
P_TOPK = 16
P_PICKS = P_HEADS * P_TOPK
SC_CORES = 2
SC_SUBCORES = 16
SC_WORKERS = SC_CORES * SC_SUBCORES
SC_LANES = 16
SC_CHUNK = 64
SC_TOK_BLOCK = 16
SC_WORDS = D_MODEL // 2
SC_HID_ROWS = 4
SC_HID_WORDS = 4
PROMPT_GROUPS = (2, 2, 4, 4, 4)


def _cparams(sem):
    return pltpu.CompilerParams(dimension_semantics=sem, vmem_limit_bytes=VMEM_LIMIT)


def _rms(x, g):
    ms = jnp.mean(x * x, axis=-1, keepdims=True)
    return x * lax.rsqrt(ms + NORM_EPS) * g


def _dot(a, b, precision=None):
    return jnp.dot(a, b, preferred_element_type=F32, precision=precision)


def _dot_nt(a, b, precision=None):
    return lax.dot_general(a, b, (((1,), (1,)), ((), ())), preferred_element_type=F32, precision=precision)


def _dot_tn(a, b, precision=None):
    return lax.dot_general(a, b, (((0,), (0,)), ((), ())), preferred_element_type=F32, precision=precision)


def _tri(n, strict):
    r = lax.broadcasted_iota(I32, (n, n), 0)
    c = lax.broadcasted_iota(I32, (n, n), 1)
    return (c < r) if strict else (c <= r)


def _cumsum_rows(g):
    return _dot(_tri(g.shape[0], False).astype(F32), g, precision=HI)


def _lane_mask(width, lo, hi):
    l = lax.broadcasted_iota(I32, (1, width), 1)
    return (l >= lo) & (l < hi)


def _sigmoid(x):
    return 1.0 / (1.0 + jnp.exp(-x))


def _silu(x):
    return x * _sigmoid(x)


def _norm_proj_kernel(x_ref, g_ref, w_ref, o_ref):
    hn = _rms(x_ref[...], g_ref[...])
    o_ref[...] = _dot(hn.astype(BF16), w_ref[...])


def norm_proj(x, g, w_bf16, tm):
    n, d = x.shape
    f = w_bf16.shape[1]
    return pl.pallas_call(
        _norm_proj_kernel,
        grid=(n // tm,),
        in_specs=[pl.BlockSpec((tm, d), lambda i: (i, 0)),
                  pl.BlockSpec((1, d), lambda i: (0, 0)),
                  pl.BlockSpec((d, f), lambda i: (0, 0))],
        out_specs=pl.BlockSpec((tm, f), lambda i: (i, 0)),
        out_shape=jax.ShapeDtypeStruct((n, f), F32),
        compiler_params=_cparams(("parallel",)),
        name="norm_proj",
    )(x, g.reshape(1, d), w_bf16)


def _out_proj2_kernel(x_ref, a_ref, b_ref, wa_ref, wb_ref, o_ref):
    y = _dot(a_ref[...].astype(BF16), wa_ref[...]) + _dot(b_ref[...].astype(BF16), wb_ref[...])
    o_ref[...] = x_ref[...] + y


def out_proj2(x, a, b, wa, wb, tm):
    n, d = x.shape
    ka, kb = a.shape[1], b.shape[1]
    return pl.pallas_call(
        _out_proj2_kernel,
        grid=(n // tm,),
        in_specs=[pl.BlockSpec((tm, d), lambda i: (i, 0)),
                  pl.BlockSpec((tm, ka), lambda i: (i, 0)),
                  pl.BlockSpec((tm, kb), lambda i: (i, 0)),
                  pl.BlockSpec((ka, d), lambda i: (0, 0)),
                  pl.BlockSpec((kb, d), lambda i: (0, 0))],
        out_specs=pl.BlockSpec((tm, d), lambda i: (i, 0)),
        out_shape=jax.ShapeDtypeStruct((n, d), F32),
        compiler_params=_cparams(("parallel",)),
        name="out_proj2",
    )(x, a, b, wa, wb)


def _out_proj1_kernel(x_ref, a_ref, wa_ref, o_ref):
    o_ref[...] = x_ref[...] + _dot(a_ref[...].astype(BF16), wa_ref[...])


def out_proj1(x, a, wa, tm):
    n, d = x.shape
    ka = a.shape[1]
    return pl.pallas_call(
        _out_proj1_kernel,
        grid=(n // tm,),
        in_specs=[pl.BlockSpec((tm, d), lambda i: (i, 0)),
                  pl.BlockSpec((tm, ka), lambda i: (i, 0)),
                  pl.BlockSpec((ka, d), lambda i: (0, 0))],
        out_specs=pl.BlockSpec((tm, d), lambda i: (i, 0)),
        out_shape=jax.ShapeDtypeStruct((n, d), F32),
        compiler_params=_cparams(("parallel",)),
        name="out_proj1",
    )(x, a, wa)


def _final_norm_kernel(x_ref, g_ref, o_ref):
    o_ref[...] = _rms(x_ref[...], g_ref[...])


def final_norm(x, g, tm):
    n, d = x.shape
    return pl.pallas_call(
        _final_norm_kernel,
        grid=(n // tm,),
        in_specs=[pl.BlockSpec((tm, d), lambda i: (i, 0)), pl.BlockSpec((1, d), lambda i: (0, 0))],
        out_specs=pl.BlockSpec((tm, d), lambda i: (i, 0)),
        out_shape=jax.ShapeDtypeStruct((n, d), F32),
        compiler_params=_cparams(("parallel",)),
        name="final_norm",
    )(x, g.reshape(1, d))


def _intra_chunk(q, k, b, heads):
    c = q.shape[0]
    nb = c // SUBLANES
    row = lax.broadcasted_iota(I32, (SUBLANES, 1), 0)
    qb = [q[SUBLANES * i:SUBLANES * (i + 1)] for i in range(nb)]
    bb = [b[SUBLANES * i:SUBLANES * (i + 1)] for i in range(nb)]
    outs = [[None] * nb for _ in heads]
    for s in range(c):
        rb0 = s // SUBLANES
        ks = k[s:s + 1, :]
        bs = b[s:s + 1, :]
        for rb in range(rb0, nb):
            p = qb[rb] * (ks * jnp.exp(bb[rb] - bs))
            for hi, (mask, v) in enumerate(heads):
                pm = p if mask is None else jnp.where(mask, p, 0.0)
                col = jnp.sum(pm, axis=-1, keepdims=True)
                if rb == rb0:
                    col = jnp.where(row + SUBLANES * rb >= s, col, 0.0)
                term = col * v[s:s + 1, :]
                outs[hi][rb] = term if outs[hi][rb] is None else outs[hi][rb] + term
    return [jnp.concatenate(o, axis=0) for o in outs]


def _gated_chunk(q, k, b, heads, st):
    intra = _intra_chunk(q, k, b, heads)
    qe = q * jnp.exp(b)
    b_last = b[-1:, :]
    kh = k * jnp.exp(b_last - b)
    outs = []
    for (mask, v), oi in zip(heads, intra):
        qm = qe if mask is None else jnp.where(mask, qe, 0.0)
        outs.append(oi + _dot_nt(qm, st, precision=HI))
    upd = _dot_tn(heads[0][1], kh, precision=HI)
    if len(heads) == 2:
        upd = jnp.where(heads[0][0], upd, _dot_tn(heads[1][1], kh, precision=HI))
    st = st * jnp.exp(b_last) + upd
    return outs, st


def _head_rms(o, g):
    ms = jnp.mean(o * o, axis=-1, keepdims=True)
    return o * lax.rsqrt(ms + NORM_EPS) * g


def _hgrn_kernel(zq_ref, zf_ref, zi_ref, zg_ref, lb_ref, ng_ref, s0_ref, o_ref, s_ref, st_scr, *, chunk):
    t = pl.program_id(2)

    @pl.when(t == 0)
    def _():
        st_scr[...] = s0_ref[0, 0]

    lb = lb_ref[0]
    nchunks = zq_ref.shape[1] // chunk

    def body(ci, carry):
        sl = pl.ds(pl.multiple_of(ci * chunk, chunk), chunk)
        f = lb + (1.0 - lb) * _sigmoid(zf_ref[0, sl, :])
        q = _silu(zq_ref[0, sl, :])
        b = _cumsum_rows(jnp.log(f))
        (o,), st = _gated_chunk(q, 1.0 - f, b, [(None, zi_ref[0, sl, :])], st_scr[...])
        st_scr[...] = st
        o_ref[0, sl, :] = _head_rms(o, ng_ref[...]) * _silu(zg_ref[0, sl, :])
        return carry

    lax.fori_loop(0, nchunks, body, 0)

    @pl.when(t == pl.num_programs(2) - 1)
    def _():
        s_ref[0, 0] = st_scr[...]


def hgrn_recurrence(z, lb, norm_g, s0_t, tb, chunk):
    bsz, t, _ = z.shape
    zspec = lambda off: pl.BlockSpec((1, tb, LANES), lambda b, h, i: (b, i, h + off))
    return pl.pallas_call(
        functools.partial(_hgrn_kernel, chunk=chunk),
        grid=(bsz, A_HEADS, t // tb),
        in_specs=[zspec(0), zspec(4), zspec(8), zspec(12),
                  pl.BlockSpec((1, 1, LANES), lambda b, h, i: (h, 0, 0)),
                  pl.BlockSpec((1, LANES), lambda b, h, i: (0, 0)),
                  pl.BlockSpec((1, 1, LANES, LANES), lambda b, h, i: (b, h, 0, 0))],
        out_specs=[pl.BlockSpec((1, tb, LANES), lambda b, h, i: (b, i, h)),
                   pl.BlockSpec((1, 1, LANES, LANES), lambda b, h, i: (b, h, 0, 0))],
        out_shape=[jax.ShapeDtypeStruct((bsz, t, A_WIDTH), F32),
                   jax.ShapeDtypeStruct((bsz, A_HEADS, LANES, LANES), F32)],
        scratch_shapes=[pltpu.VMEM((LANES, LANES), F32)],
        compiler_params=_cparams(("parallel", "parallel", "arbitrary")),
        name="hgrn_recurrence",
    )(z, z, z, z, lb.reshape(A_HEADS, 1, LANES), norm_g.reshape(1, LANES), s0_t)


def _gla_kernel(zq_ref, zk_ref, zv_ref, zg_ref, zlr_ref, w2_ref, gb_ref, ng_ref, s0_ref, o_ref, s_ref, st_scr,
                *, chunk):
    t = pl.program_id(2)

    @pl.when(t == 0)
    def _():
        st_scr[...] = s0_ref[0, 0]

    nchunks = zq_ref.shape[1] // chunk
    m0 = _lane_mask(LANES, 0, B_DK)
    m1 = _lane_mask(LANES, B_DK, LANES)

    def body(ci, carry):
        sl = pl.ds(pl.multiple_of(ci * chunk, chunk), chunk)
        pre = _dot(zlr_ref[0, sl, :], w2_ref[...], precision=HI) + gb_ref[...]
        log_g = (jnp.minimum(pre, 0.0) - jnp.log(1.0 + jnp.exp(-jnp.abs(pre)))) * (1.0 / GLA_NORMALIZER)
        q = zq_ref[0, sl, :] * (B_DK ** -0.5)
        b = _cumsum_rows(log_g)
        v = zv_ref[0, sl, :]
        (o0, o1), st = _gated_chunk(q, zk_ref[0, sl, :], b,
                                    [(m0, v[:, :LANES]), (m1, v[:, LANES:])], st_scr[...])
        st_scr[...] = st
        gate = _silu(zg_ref[0, sl, :])
        o_ref[0, sl, 0:LANES] = _head_rms(o0, ng_ref[...]) * gate[:, :LANES]
        o_ref[0, sl, LANES:2 * LANES] = _head_rms(o1, ng_ref[...]) * gate[:, LANES:]
        return carry

    lax.fori_loop(0, nchunks, body, 0)

    @pl.when(t == pl.num_programs(2) - 1)
    def _():
        s_ref[0, 0] = st_scr[...]


def gla_recurrence(z, w2pad, gate_b, norm_g, s0_t, tb, chunk):
    bsz, t, _ = z.shape
    npairs = B_HEADS // 2
    return pl.pallas_call(
        functools.partial(_gla_kernel, chunk=chunk),
        grid=(bsz, npairs, t // tb),
        in_specs=[pl.BlockSpec((1, tb, LANES), lambda b, p, i: (b, i, 16 + p)),
                  pl.BlockSpec((1, tb, LANES), lambda b, p, i: (b, i, 18 + p)),
                  pl.BlockSpec((1, tb, 2 * LANES), lambda b, p, i: (b, i, 10 + p)),
                  pl.BlockSpec((1, tb, 2 * LANES), lambda b, p, i: (b, i, 12 + p)),
                  pl.BlockSpec((1, tb, LANES), lambda b, p, i: (b, i, 28)),
                  pl.BlockSpec((LANES, LANES), lambda b, p, i: (0, p)),
                  pl.BlockSpec((1, LANES), lambda b, p, i: (0, p)),
                  pl.BlockSpec((1, LANES), lambda b, p, i: (0, 0)),
                  pl.BlockSpec((1, 1, LANES, LANES), lambda b, p, i: (b, p, 0, 0))],
        out_specs=[pl.BlockSpec((1, tb, 2 * LANES), lambda b, p, i: (b, i, p)),
                   pl.BlockSpec((1, 1, LANES, LANES), lambda b, p, i: (b, p, 0, 0))],
        out_shape=[jax.ShapeDtypeStruct((bsz, t, B_WIDTH), F32),
                   jax.ShapeDtypeStruct((bsz, npairs, LANES, LANES), F32)],
        scratch_shapes=[pltpu.VMEM((LANES, LANES), F32)],
        compiler_params=_cparams(("parallel", "parallel", "arbitrary")),
        name="gla_recurrence",
    )(z, z, z, z, z, w2pad, gate_b.reshape(1, 2 * LANES), norm_g.reshape(1, LANES), s0_t)


def _rwkv_proj_kernel(x_ref, xp_ref, xl_ref, g1_ref, mu_ref, wr_ref, wk_ref, wv_ref, ww1_ref, ww2_ref, w0_ref,
                      aw1_ref, aw2_ref, a0_ref, gw1_ref, gw2_ref, kk_ref, ka_ref,
                      r_out, k_out, v_out, lw_out, a_out, kk_out, g_out, hl_out, *, tiles_per_seq):
    i = pl.program_id(0)
    g1 = g1_ref[...]
    hn = _rms(x_ref[...], g1)
    tm = hn.shape[0]
    prev = _rms(xp_ref[...], g1)[SUBLANES - 1:SUBLANES, :]
    prev = jnp.where(i % tiles_per_seq == 0, xl_ref[0], prev)
    row = lax.broadcasted_iota(I32, (tm, 1), 0)
    xprev = jnp.where(row == 0, prev, pltpu.roll(hn, 1, axis=0))
    dx = xprev - hn

    def mix(j):
        return hn + dx * mu_ref[j:j + 1, :]

    r = _dot(mix(0).astype(BF16), wr_ref[...])
    k = _dot(mix(1).astype(BF16), wk_ref[...])
    v = _dot(mix(2).astype(BF16), wv_ref[...])
    wl = _dot(jnp.tanh(_dot(mix(3), ww1_ref[...], precision=HI)), ww2_ref[...], precision=HI)
    z = w0_ref[...] + wl
    wpre = -(jnp.maximum(-z, 0.0) + jnp.log(1.0 + jnp.exp(-jnp.abs(z)))) - 0.5
    al = _dot(_dot(mix(4), aw1_ref[...], precision=HI), aw2_ref[...], precision=HI)
    a = _sigmoid(a0_ref[...] + al)
    gg = _dot(_sigmoid(_dot(mix(5).astype(BF16), gw1_ref[...])).astype(BF16), gw2_ref[...])
    r_out[...] = r
    k_out[...] = k * (1.0 + (a - 1.0) * ka_ref[...])
    v_out[...] = v
    lw_out[...] = -jnp.exp(wpre)
    a_out[...] = a
    kk_out[...] = k * kk_ref[...]
    g_out[...] = gg
    hl_out[0] = hn[tm - 1:tm, :]


def rwkv_proj(x, x_last, seq_len, g1, mu, wr, wk, wv, ww1, ww2, w0, aw1, aw2, a0, gw1, gw2, k_k, k_a, tm):
    n, d = x.shape
    tiles_per_seq = seq_len // tm
    row = lambda a: a.reshape(1, d)
    full = lambda a: pl.BlockSpec(a.shape, lambda i: (0,) * a.ndim)
    tile = pl.BlockSpec((tm, d), lambda i: (i, 0))
    blocks8 = tm // SUBLANES
    args = (x, x, x_last.reshape(-1, 1, d), row(g1), mu, wr, wk, wv, ww1, ww2, row(w0), aw1, aw2, row(a0),
            gw1, gw2, row(k_k), row(k_a))
    in_specs = [tile,
                pl.BlockSpec((SUBLANES, d), lambda i: (jnp.maximum(i * blocks8 - 1, 0), 0)),
                pl.BlockSpec((1, 1, d), lambda i: (i // tiles_per_seq, 0, 0))]
    in_specs += [full(a) for a in args[3:]]
    outs = pl.pallas_call(
        functools.partial(_rwkv_proj_kernel, tiles_per_seq=tiles_per_seq),
        grid=(n // tm,),
        in_specs=in_specs,
        out_specs=[tile] * 7 + [pl.BlockSpec((1, 1, d), lambda i: (i, 0, 0))],
        out_shape=[jax.ShapeDtypeStruct((n, d), F32)] * 7 + [jax.ShapeDtypeStruct((n // tm, 1, d), F32)],
        compiler_params=_cparams(("parallel",)),
        name="rwkv_proj",
    )(*args)
    return outs


_NN = ((1,), (0,))
_NT = ((1,), (1,))
_TN = ((0,), (0,))
RWKV_AB_PASSES = 1
RWKV_INV_PASSES = 1
RWKV_APPLY_PASSES = 1
RWKV_STATE_PASSES = 3
RWKV_SEQS_PER_STEP = 4


def _split_bf16(a):
    hi = a.astype(BF16)
    return hi, (a - hi.astype(F32)).astype(BF16)


def _mm(a, b, dims, passes):
    if passes == 6:
        return lax.dot_general(a, b, (dims, ((), ())), preferred_element_type=F32, precision=HI)
    dg = lambda x, y: lax.dot_general(x, y, (dims, ((), ())), preferred_element_type=F32)
    ah, al = _split_bf16(a)
    bh, bl = _split_bf16(b)
    if passes == 1:
        return dg(ah, bh)
    return dg(ah, bh) + (dg(al, bh) + dg(ah, bl))


def _cumsum_rows3(g):
    tri = _tri(g.shape[0], False).astype(BF16)
    h1 = g.astype(BF16)
    r1 = g - h1.astype(F32)
    h2 = r1.astype(BF16)
    h3 = (r1 - h2.astype(F32)).astype(BF16)
    return _dot(tri, h1) + (_dot(tri, h2) + _dot(tri, h3))


def _pair_sum(x, m0):
    s0 = jnp.sum(jnp.where(m0, x, 0.0), axis=-1, keepdims=True)
    s1 = jnp.sum(jnp.where(m0, 0.0, x), axis=-1, keepdims=True)
    return jnp.where(m0, s0, s1)


def _rwkv_kernel(r_ref, k_ref, v_ref, lw_ref, a_ref, kk_ref, g_ref, rk_ref, lng_ref, lnb_ref, s0_ref,
                 o_ref, s_ref, mt_scr, *, chunk):
    t = pl.program_id(2)

    @pl.when(t == 0)
    def _():
        mt_scr[...] = s0_ref[:, 0]

    nrows = r_ref.shape[0]
    nchunks = r_ref.shape[1] // chunk
    c2 = 2 * chunk
    m0 = _lane_mask(LANES, 0, C_HEAD)
    rowi = lax.broadcasted_iota(I32, (LANES, LANES), 0)
    coli = lax.broadcasted_iota(I32, (LANES, LANES), 1)
    blockdiag = (rowi < C_HEAD) == (coli < C_HEAD)
    ti = lax.broadcasted_iota(I32, (c2, c2), 0)
    si = lax.broadcasted_iota(I32, (c2, c2), 1)
    same_head = (ti < chunk) == (si < chunk)
    tm_ = jnp.where(ti < chunk, ti, ti - chunk)
    sm_ = jnp.where(si < chunk, si, si - chunk)
    strict = same_head & (sm_ < tm_)
    incl = same_head & (sm_ <= tm_)

    def stack_heads(x):
        return jnp.concatenate([jnp.where(m0, x, 0.0), jnp.where(m0, 0.0, x)], axis=0)

    def twice(x):
        return jnp.concatenate([x, x], axis=0)

    def unstack(x2):
        return jnp.where(m0, x2[:chunk], x2[chunk:])

    eye = (ti == si).astype(F32)
    seqs = range(nrows)

    def body(ci, carry):
        sl = pl.ds(pl.multiple_of(ci * chunk, chunk), chunk)
        r = [r_ref[i, sl, :] for i in seqs]
        k = [k_ref[i, sl, :] for i in seqs]
        v = [v_ref[i, sl, :] for i in seqs]
        lw = [lw_ref[i, sl, :] for i in seqs]
        kkr = [kk_ref[i, sl, :] for i in seqs]
        kk = [x * lax.rsqrt(_pair_sum(x * x, m0) + 1e-12) for x in kkr]
        al = [a_ref[i, sl, :] * kk[i] for i in seqs]
        gam = [_cumsum_rows3(x) for x in lw]
        e_neg = [jnp.exp(-x) for x in gam]
        xr = [jnp.concatenate([stack_heads(kk[i] * jnp.exp(gam[i] - lw[i])),
                               stack_heads(r[i] * jnp.exp(gam[i]))], axis=0) for i in seqs]
        alk = [jnp.concatenate([twice(al[i] * e_neg[i]), twice(k[i] * e_neg[i])], axis=0) for i in seqs]
        mt = [mt_scr[i] for i in seqs]
        ab = [_mm(xr[i], alk[i], _NT, RWKV_AB_PASSES) for i in seqs]
        xm = [_mm(xr[i], mt[i], _NT, RWKV_STATE_PASSES) for i in seqs]
        a_al = [jnp.where(strict, x[:c2, :c2], 0.0) for x in ab]
        a_k = [jnp.where(strict, x[:c2, c2:], 0.0) for x in ab]
        b_alk = [jnp.concatenate([jnp.where(incl, x[c2:, c2:], 0.0), jnp.where(incl, -x[c2:, :c2], 0.0)], axis=1)
                 for x in ab]
        v2 = [twice(x) for x in v]
        rhs = [xm[i][:c2] + _mm(a_k[i], v2[i], _NN, RWKV_APPLY_PASSES) for i in seqs]
        p = [-x for x in a_al]
        tinv = [eye + x for x in p]
        span = 2
        while span < chunk:
            p = [_mm(x, x, _NN, RWKV_INV_PASSES) for x in p]
            tinv = [tinv[i] + _mm(tinv[i], p[i], _NN, RWKV_INV_PASSES) for i in seqs]
            span *= 2
        u = [unstack(_mm(tinv[i], rhs[i], _NN, RWKV_APPLY_PASSES)) for i in seqs]
        o = [unstack(xm[i][c2:] + _mm(b_alk[i], jnp.concatenate([v2[i], twice(u[i])], axis=0), _NN,
                                      RWKV_APPLY_PASSES)) for i in seqs]
        g_last = [x[-1:, :] for x in gam]
        e_end = [jnp.exp(g_last[i] - gam[i]) for i in seqs]
        upd = [_mm(jnp.concatenate([v[i], u[i]], axis=0),
                   jnp.concatenate([k[i] * e_end[i], -(al[i] * e_end[i])], axis=0), _TN, RWKV_STATE_PASSES)
               for i in seqs]
        for i in seqs:
            mt_scr[i] = mt[i] * jnp.exp(g_last[i]) + jnp.where(blockdiag, upd[i], 0.0)
        for i in seqs:
            mean = _pair_sum(o[i], m0) * (1.0 / C_HEAD)
            cen = o[i] - mean
            var = _pair_sum(cen * cen, m0) * (1.0 / C_HEAD)
            on = cen * lax.rsqrt(var + C_GN_EPS) * lng_ref[...] + lnb_ref[...]
            bonus = _pair_sum(r[i] * k[i] * rk_ref[...], m0) * v[i]
            o_ref[i, sl, :] = (on + bonus) * g_ref[i, sl, :]
        return carry

    lax.fori_loop(0, nchunks, body, 0)

    @pl.when(t == pl.num_programs(2) - 1)
    def _():
        s_ref[:, 0] = mt_scr[...]


def rwkv_recurrence(r, k, v, lw, a, kk, g, r_k, ln_g, ln_b, s0_bd, tb, chunk, nb):
    bsz, t, d = r.shape
    npairs = C_HEADS // 2
    seq = pl.BlockSpec((nb, tb, LANES), lambda b, p, i: (b, i, p))
    vec = pl.BlockSpec((1, LANES), lambda b, p, i: (0, p))
    st = pl.BlockSpec((nb, 1, LANES, LANES), lambda b, p, i: (b, p, 0, 0))
    return pl.pallas_call(
        functools.partial(_rwkv_kernel, chunk=chunk),
        grid=(bsz // nb, npairs, t // tb),
        in_specs=[seq] * 7 + [vec, vec, vec, st],
        out_specs=[seq, st],
        out_shape=[jax.ShapeDtypeStruct((bsz, t, d), F32),
                   jax.ShapeDtypeStruct((bsz, npairs, LANES, LANES), F32)],
        scratch_shapes=[pltpu.VMEM((nb, LANES, LANES), F32)],
        compiler_params=_cparams(("parallel", "parallel", "arbitrary")),
        name="rwkv_recurrence",
    )(r, k, v, lw, a, kk, g, r_k.reshape(1, d), ln_g.reshape(1, d), ln_b.reshape(1, d), s0_bd)


NEG_INF = float("-inf")


def _top16_rows(s):
    n = s.shape[0]
    key = lax.broadcasted_iota(I32, s.shape, 0)
    vals, idxs = [], []
    for _ in range(P_TOPK):
        m = jnp.max(s, axis=0, keepdims=True)
        am = jnp.min(jnp.where(s == m, key, n), axis=0, keepdims=True)
        vals.append(m)
        idxs.append(am)
        s = jnp.where(key == am, NEG_INF, s)
    return vals, idxs


def _top16_pairs(v0, i0, v1, i1):
    a0 = jnp.concatenate(v0[0:8], axis=0)
    a1 = jnp.concatenate(v0[8:16], axis=0)
    b0 = jnp.concatenate(v1[0:8], axis=0)
    b1 = jnp.concatenate(v1[8:16], axis=0)
    ia0 = jnp.concatenate(i0[0:8], axis=0) * P_KEYS
    ia1 = jnp.concatenate(i0[8:16], axis=0) * P_KEYS
    ib0 = jnp.concatenate(i1[0:8], axis=0)
    ib1 = jnp.concatenate(i1[8:16], axis=0)
    row = lax.broadcasted_iota(I32, (SUBLANES, 1), 0)
    slabs = []

    def add(val, eid, keep):
        slabs.append((val if keep is None else jnp.where(keep, val, NEG_INF), eid))

    add(v0[0] + b0, ia0[0:1] + ib0, None)
    add(v0[0] + b1, ia0[0:1] + ib1, None)
    add(v0[1] + b0, ia0[1:2] + ib0, None)
    add(v0[2] + b0, ia0[2:3] + ib0, row < 5)
    add(v0[3] + b0, ia0[3:4] + ib0, row < 4)
    add(a0 + v1[0], ia0 + ib0[0:1], row >= 4)
    add(a1 + v1[0], ia1 + ib0[0:1], None)
    add(a0 + v1[1], ia0 + ib0[1:2], row >= 4)
    add(a0 + v1[2], ia0 + ib0[2:3], row == 4)

    big = P_KEYS * P_KEYS
    out_v, out_e = [], []
    for _ in range(P_TOPK):
        m = slabs[0][0]
        for val, _e in slabs[1:]:
            m = jnp.maximum(m, val)
        m = jnp.max(m, axis=0, keepdims=True)
        e = None
        for val, eid in slabs:
            c = jnp.where(val == m, eid, big)
            e = c if e is None else jnp.minimum(e, c)
        e = jnp.min(e, axis=0, keepdims=True)
        out_v.append(m)
        out_e.append(e)
        slabs = [(jnp.where(eid == e, NEG_INF, val), eid) for val, eid in slabs]
    return out_v, out_e


def _peer_select_kernel(x_ref, g_ref, wq_ref, keys_ref, xn_out, eid_out, gate_out):
    hn = _rms(x_ref[...], g_ref[...])
    xn_out[...] = hn
    q = _dot(hn.astype(BF16), wq_ref[...])
    tm = q.shape[0]
    for lt in range(tm // LANES):
        rows = slice(lt * LANES, (lt + 1) * LANES)
        e_rows, g_rows = [], []
        for h in range(P_HEADS):
            tops = []
            for p in range(2):
                hp = 2 * h + p
                s = _dot_nt(keys_ref[hp], q[rows, hp * LANES:(hp + 1) * LANES], precision=HI)
                tops.append(_top16_rows(s))
            cs, ce = _top16_pairs(tops[0][0], tops[0][1], tops[1][0], tops[1][1])
            ex = [jnp.exp(c - cs[0]) for c in cs]
            tot = ex[0]
            for e in ex[1:]:
                tot = tot + e
            inv = 1.0 / tot
            e_rows += ce
            g_rows += [e * inv for e in ex]
        eid_out[rows, :] = jnp.concatenate(e_rows, axis=0).T
        gate_out[rows, :] = jnp.concatenate(g_rows, axis=0).T


def peer_select(x, g, wq_bf16, keys, tm):
    n, d = x.shape
    return pl.pallas_call(
        _peer_select_kernel,
        grid=(n // tm,),
        in_specs=[pl.BlockSpec((tm, d), lambda i: (i, 0)),
                  pl.BlockSpec((1, d), lambda i: (0, 0)),
                  pl.BlockSpec(wq_bf16.shape, lambda i: (0, 0)),
                  pl.BlockSpec(keys.shape, lambda i: (0, 0, 0))],
        out_specs=[pl.BlockSpec((tm, d), lambda i: (i, 0)),
                   pl.BlockSpec((tm, P_PICKS), lambda i: (i, 0)),
                   pl.BlockSpec((tm, P_PICKS), lambda i: (i, 0))],
        out_shape=[jax.ShapeDtypeStruct((n, d), F32),
                   jax.ShapeDtypeStruct((n, P_PICKS), I32),
                   jax.ShapeDtypeStruct((n, P_PICKS), F32)],
        compiler_params=_cparams(("parallel",)),
        name="peer_select",
    )(x, g.reshape(1, d), wq_bf16, keys)


def _peer_act_kernel(h_ref, g_ref, o_ref):
    h = h_ref[...]
    o_ref[...] = 0.5 * h * (1.0 + lax.erf(h * (2.0 ** -0.5))) * g_ref[...]


def peer_act(hid, gate, tm):
    n, p = hid.shape
    spec = pl.BlockSpec((tm, p), lambda i: (i, 0))
    return pl.pallas_call(
        _peer_act_kernel, grid=(n // tm,), in_specs=[spec, spec], out_specs=spec,
        out_shape=jax.ShapeDtypeStruct((n, p), F32),
        compiler_params=_cparams(("parallel",)), name="peer_act",
    )(hid, gate)


_CHUNKS_PER_TOK = P_PICKS // SC_CHUNK
_WORD_VREGS = SC_WORDS // SC_LANES
U32 = jnp.uint32


def pack_rows_bf16(t):
    lo = lax.bitcast_convert_type(t[..., :SC_WORDS].astype(BF16), jnp.uint16).astype(U32)
    a = lax.bitcast_convert_type(t[..., SC_WORDS:], U32)
    sign = a & U32(0x80000000)
    mag = a & U32(0x7FFFFFFF)
    steps = (jnp.maximum(mag + U32(0x8000), lo) - lo) >> 16
    near = (steps << 16) + lo
    near = jnp.where(near >= U32(0x7F800000), near - U32(0x10000), near)
    return lax.bitcast_convert_type(sign | near, I32)


def _low_f32(w):
    return lax.bitcast_convert_type(w << 16, F32)


def _high_f32(w):
    return lax.bitcast_convert_type(w, F32)


def _sc_worker():
    return lax.axis_index("s") * SC_CORES + lax.axis_index("c")


def _sc_hid_body(tab_hbm, idx_hbm, x_hbm, hid_hbm, idx_v, x_v, hid_v, rows_a, rows_b, sem_a, sem_b, *, tok_per_w):
    wid = _sc_worker()
    nblk = tok_per_w // SC_TOK_BLOCK
    nch = SC_TOK_BLOCK * _CHUNKS_PER_TOK
    lane = lax.iota(I32, SC_LANES)
    quads = SC_LANES // SC_HID_ROWS

    def dots(rows, x_row, res_ref, col0):
        for grp in range(SC_CHUNK // SC_LANES):
            def some_rows(q, res):
                r0 = grp * SC_LANES + q * SC_HID_ROWS

                def some_words(jj, accs):
                    accs = list(accs)
                    for jw in range(SC_HID_WORDS):
                        off = (jj * SC_HID_WORDS + jw) * SC_LANES
                        xl = x_v[x_row, pl.ds(off, SC_LANES)]
                        xh = x_v[x_row, pl.ds(SC_WORDS + off, SC_LANES)]
                        for i in range(SC_HID_ROWS):
                            w = rows[r0 + i, pl.ds(off, SC_LANES)]
                            k = 2 * i + jw % 2
                            accs[k] = accs[k] + (_low_f32(w) * xl + _high_f32(w) * xh)
                    return tuple(accs)

                zero = jnp.zeros((SC_LANES,), F32)
                accs = lax.fori_loop(0, _WORD_VREGS // SC_HID_WORDS, some_words, (zero,) * (2 * SC_HID_ROWS))
                for i in range(SC_HID_ROWS):
                    res = jnp.where(lane == q * SC_HID_ROWS + i, jnp.sum(accs[2 * i] + accs[2 * i + 1]), res)
                return res
            res = lax.fori_loop(0, quads, some_rows, jnp.zeros((SC_LANES,), F32))
            res_ref[x_row, pl.ds(col0 + grp * SC_LANES, SC_LANES)] = res

    def block(bi, carry):
        tok0 = wid * tok_per_w + bi * SC_TOK_BLOCK
        pltpu.sync_copy(idx_hbm.at[pl.ds(tok0 * _CHUNKS_PER_TOK, nch)], idx_v)
        pltpu.sync_copy(x_hbm.at[pl.ds(tok0, SC_TOK_BLOCK)], x_v)
        pltpu.async_copy(tab_hbm.at[idx_v.at[0]], rows_a, sem_a)

        def pair(j, c):
            ca = 2 * j
            pltpu.async_copy(tab_hbm.at[idx_v.at[ca + 1]], rows_b, sem_b)
            pltpu.make_async_copy(tab_hbm.at[idx_v.at[0]], rows_a, sem_a).wait()
            dots(rows_a, ca // _CHUNKS_PER_TOK, hid_v, (ca % _CHUNKS_PER_TOK) * SC_CHUNK)

            @pl.when(j < nch // 2 - 1)
            def _():
                pltpu.async_copy(tab_hbm.at[idx_v.at[ca + 2]], rows_a, sem_a)
            pltpu.make_async_copy(tab_hbm.at[idx_v.at[0]], rows_b, sem_b).wait()
            cb = ca + 1
            dots(rows_b, cb // _CHUNKS_PER_TOK, hid_v, (cb % _CHUNKS_PER_TOK) * SC_CHUNK)
            return c
        lax.fori_loop(0, nch // 2, pair, 0)
        pltpu.sync_copy(hid_v, hid_hbm.at[pl.ds(tok0, SC_TOK_BLOCK)])
        return carry

    lax.fori_loop(0, nblk, block, 0)


def sc_expert_hidden(table_packed, eidx, xn):
    n = xn.shape[0]
    tok_per_w = n // SC_WORKERS
    mesh = plsc.VectorSubcoreMesh(core_axis_name="c", subcore_axis_name="s")
    nch = SC_TOK_BLOCK * _CHUNKS_PER_TOK
    k = pl.kernel(
        functools.partial(_sc_hid_body, tok_per_w=tok_per_w), mesh=mesh,
        out_type=jax.ShapeDtypeStruct((n, P_PICKS), F32),
        scratch_types=[pltpu.VMEM((nch, SC_CHUNK), I32),
                       pltpu.VMEM((SC_TOK_BLOCK, D_MODEL), F32),
                       pltpu.VMEM((SC_TOK_BLOCK, P_PICKS), F32),
                       pltpu.VMEM((SC_CHUNK, SC_WORDS), I32),
                       pltpu.VMEM((SC_CHUNK, SC_WORDS), I32),
                       pltpu.SemaphoreType.DMA, pltpu.SemaphoreType.DMA],
        compiler_params=pltpu.CompilerParams(needs_layout_passes=False),
        name="sc_expert_hidden",
    )
    return k(table_packed, eidx.reshape(n * _CHUNKS_PER_TOK, SC_CHUNK), xn)


def _sc_out_body(tab_hbm, idx_hbm, act_hbm, x_hbm, y_hbm, idx_v, act_v, y_v, rows_a, rows_b, sem_a, sem_b,
                 *, tok_per_w):
    wid = _sc_worker()
    nblk = tok_per_w // SC_TOK_BLOCK
    nch = SC_TOK_BLOCK * _CHUNKS_PER_TOK
    half_w = _WORD_VREGS // 2

    def accum(rows, tok, col0):
        for hv in range(2):
            base = hv * half_w * SC_LANES
            acc0 = (tuple(y_v[tok, pl.ds(base + j * SC_LANES, SC_LANES)] for j in range(half_w))
                    + tuple(y_v[tok, pl.ds(SC_WORDS + base + j * SC_LANES, SC_LANES)] for j in range(half_w)))

            def one(r, acc):
                aidx = jnp.full((SC_LANES,), col0, I32) + r
                wgt = plsc.load_gather(act_v, [jnp.full((SC_LANES,), tok, I32), aidx])
                lo, hi = [], []
                for j in range(half_w):
                    w = rows[r, pl.ds(base + j * SC_LANES, SC_LANES)]
                    lo.append(acc[j] + wgt * _low_f32(w))
                    hi.append(acc[half_w + j] + wgt * _high_f32(w))
                return tuple(lo + hi)
            acc = lax.fori_loop(0, SC_CHUNK, one, acc0)
            for j in range(half_w):
                y_v[tok, pl.ds(base + j * SC_LANES, SC_LANES)] = acc[j]
                y_v[tok, pl.ds(SC_WORDS + base + j * SC_LANES, SC_LANES)] = acc[half_w + j]

    def block(bi, carry):
        tok0 = wid * tok_per_w + bi * SC_TOK_BLOCK
        pltpu.sync_copy(idx_hbm.at[pl.ds(tok0 * _CHUNKS_PER_TOK, nch)], idx_v)
        pltpu.sync_copy(act_hbm.at[pl.ds(tok0, SC_TOK_BLOCK)], act_v)
        pltpu.sync_copy(x_hbm.at[pl.ds(tok0, SC_TOK_BLOCK)], y_v)
        pltpu.async_copy(tab_hbm.at[idx_v.at[0]], rows_a, sem_a)

        def pair(j, c):
            ca = 2 * j
            pltpu.async_copy(tab_hbm.at[idx_v.at[ca + 1]], rows_b, sem_b)
            pltpu.make_async_copy(tab_hbm.at[idx_v.at[0]], rows_a, sem_a).wait()
            accum(rows_a, ca // _CHUNKS_PER_TOK, (ca % _CHUNKS_PER_TOK) * SC_CHUNK)

            @pl.when(j < nch // 2 - 1)
            def _():
                pltpu.async_copy(tab_hbm.at[idx_v.at[ca + 2]], rows_a, sem_a)
            pltpu.make_async_copy(tab_hbm.at[idx_v.at[0]], rows_b, sem_b).wait()
            cb = ca + 1
            accum(rows_b, cb // _CHUNKS_PER_TOK, (cb % _CHUNKS_PER_TOK) * SC_CHUNK)
            return c
        lax.fori_loop(0, nch // 2, pair, 0)
        pltpu.sync_copy(y_v, y_hbm.at[pl.ds(tok0, SC_TOK_BLOCK)])
        return carry

    lax.fori_loop(0, nblk, block, 0)


def sc_expert_output(table_packed, eidx, act, x):
    n = x.shape[0]
    tok_per_w = n // SC_WORKERS
    mesh = plsc.VectorSubcoreMesh(core_axis_name="c", subcore_axis_name="s")
    nch = SC_TOK_BLOCK * _CHUNKS_PER_TOK
    k = pl.kernel(
        functools.partial(_sc_out_body, tok_per_w=tok_per_w), mesh=mesh,
        out_type=jax.ShapeDtypeStruct((n, D_MODEL), F32),
        scratch_types=[pltpu.VMEM((nch, SC_CHUNK), I32),
                       pltpu.VMEM((SC_TOK_BLOCK, P_PICKS), F32),
                       pltpu.VMEM((SC_TOK_BLOCK, D_MODEL), F32),
                       pltpu.VMEM((SC_CHUNK, SC_WORDS), I32),
                       pltpu.VMEM((SC_CHUNK, SC_WORDS), I32),
                       pltpu.SemaphoreType.DMA, pltpu.SemaphoreType.DMA],
        compiler_params=pltpu.CompilerParams(needs_layout_passes=False),
        name="sc_expert_output",
    )
    return k(table_packed, eidx.reshape(n * _CHUNKS_PER_TOK, SC_CHUNK), act, x)


def peer_ffn(x, g, wq_bf16, keys, u_packed, v_packed, tm):
    xn, eidx, gate = peer_select(x, g, wq_bf16, keys, tm)
    hid = sc_expert_hidden(u_packed, eidx, xn)
    act = peer_act(hid, gate, tm)
    return sc_expert_output(v_packed, eidx, act, x)


def _prep_w_in(w_in):
    main = jnp.concatenate([w_in[:, :3072], w_in[:, 3088:3600]], axis=1)
    lr = jnp.pad(w_in[:, 3072:3088], ((0, 0), (0, LANES - GLA_RANK)))
    return jnp.concatenate([main, lr], axis=1).astype(BF16)


def _trunk(x, st_h, st_g, st_r, st_s, w, seq_len, tm, tmp, tb, chunk):
    bsz = x.shape[0]
    n = bsz * seq_len
    x2 = x.reshape(n, D_MODEL)

    z = norm_proj(x2, w["norm1_g"][0], w["w_in"], tm).reshape(bsz, seq_len, Z_WIDTH)
    s0_h = jnp.swapaxes(st_h, -1, -2)
    s0_g = jnp.swapaxes(st_g.reshape(bsz, 2, 2 * B_DK, LANES), -1, -2)
    o_a, sh_t = hgrn_recurrence(z, w["lb0"], w["hgrn_norm_g"], s0_h, tb, chunk)
    o_b, sg_t = gla_recurrence(z, w["gla_w2"], w["gla_b"], w["gla_norm_g"], s0_g, tb, chunk)
    new_h = jnp.swapaxes(sh_t, -1, -2)
    new_g = jnp.swapaxes(sg_t, -1, -2).reshape(bsz, B_HEADS, B_DK, LANES)
    x2 = out_proj2(x2, o_a.reshape(n, A_WIDTH), o_b.reshape(n, B_WIDTH), w["w_out_a"], w["w_out_b"], tm)
    x2 = peer_ffn(x2, w["norm2_g"][0], w["peer_wq"][0], w["peer_keys"][0], w["peer_u"][0], w["peer_v"][0], tmp)

    r, k, v, lw, a, kk, g, hl = rwkv_proj(
        x2, st_s, seq_len, w["norm1_g"][1], w["mu"], w["wr"], w["wk"], w["wv"], w["w_w1"], w["w_w2"], w["w0"],
        w["a_w1"], w["a_w2"], w["a0"], w["g_w1"], w["g_w2"], w["k_k"], w["k_a"], tm)
    new_s = hl.reshape(bsz, seq_len // tm, D_MODEL)[:, -1]
    pr = st_r.reshape(bsz, C_HEADS // 2, 2, C_HEAD, C_HEAD)
    zero = jnp.zeros_like(pr[:, :, 0])
    s0_r = jnp.concatenate([jnp.concatenate([pr[:, :, 0], zero], axis=-1),
                            jnp.concatenate([zero, pr[:, :, 1]], axis=-1)], axis=-2)
    sh3 = lambda t: t.reshape(bsz, seq_len, D_MODEL)
    o_c, sr_bd = rwkv_recurrence(sh3(r), sh3(k), sh3(v), sh3(lw), sh3(a), sh3(kk), sh3(g),
                                 w["r_k"], w["ln_g"], w["ln_b"], s0_r, tb, chunk, min(RWKV_SEQS_PER_STEP, bsz))
    new_r = jnp.stack([sr_bd[:, :, :C_HEAD, :C_HEAD], sr_bd[:, :, C_HEAD:, C_HEAD:]], axis=2)
    new_r = new_r.reshape(bsz, C_HEADS, C_HEAD, C_HEAD)
    x2 = out_proj1(x2, o_c.reshape(n, D_MODEL), w["w_out_c"], tm)
    x2 = peer_ffn(x2, w["norm2_g"][1], w["peer_wq"][1], w["peer_keys"][1], w["peer_u"][1], w["peer_v"][1], tmp)

    y = final_norm(x2, w["final_g"], tm).reshape(bsz, seq_len, D_MODEL)
    return y, new_h[None], new_g[None], new_r[None], new_s[None]


def kernel(x_prompt, x_sample, state_hgrn, state_gla, state_rwkv, state_shift, w_in_ab, hgrn_lower_bounds, hgrn_norm_g, gla_gate_w2, gla_gate_b, gla_norm_g, w_out_ab, rwkv_mu, rwkv_w_rkv, rwkv_w_w1, rwkv_w_w2, rwkv_w0, rwkv_a_w1, rwkv_a_w2, rwkv_a0, rwkv_g_w1, rwkv_g_w2, rwkv_k_k, rwkv_k_a, rwkv_r_k, rwkv_ln_g, rwkv_ln_b, w_out_c, norm1_g, norm2_g, final_g, peer_w_q, peer_sub_keys, peer_u, peer_v):
    lbs = jnp.cumsum(jax.nn.softmax(hgrn_lower_bounds.astype(F32), axis=0), axis=0)
    w = dict(
        norm1_g=norm1_g, norm2_g=norm2_g, final_g=final_g,
        w_in=_prep_w_in(w_in_ab[0]), lb0=lbs[0], hgrn_norm_g=hgrn_norm_g[0],
        gla_w2=jnp.pad(gla_gate_w2[0], ((0, LANES - GLA_RANK), (0, 0))), gla_b=gla_gate_b[0],
        gla_norm_g=gla_norm_g[0],
        w_out_a=w_out_ab[0, :A_WIDTH].astype(BF16), w_out_b=w_out_ab[0, A_WIDTH:].astype(BF16),
        mu=rwkv_mu[0], wr=rwkv_w_rkv[0, 0].astype(BF16), wk=rwkv_w_rkv[0, 1].astype(BF16),
        wv=rwkv_w_rkv[0, 2].astype(BF16), w_w1=rwkv_w_w1[0], w_w2=rwkv_w_w2[0], w0=rwkv_w0[0],
        a_w1=rwkv_a_w1[0], a_w2=rwkv_a_w2[0], a0=rwkv_a0[0],
        g_w1=rwkv_g_w1[0].astype(BF16), g_w2=rwkv_g_w2[0].astype(BF16),
        k_k=rwkv_k_k[0], k_a=rwkv_k_a[0], r_k=rwkv_r_k[0], ln_g=rwkv_ln_g[0], ln_b=rwkv_ln_b[0],
        w_out_c=w_out_c[0].astype(BF16),
        peer_wq=peer_w_q.astype(BF16),
        peer_keys=peer_sub_keys.reshape(peer_sub_keys.shape[0], 2 * P_HEADS, P_KEYS, P_KEYS),
        peer_u=pack_rows_bf16(peer_u), peer_v=pack_rows_bf16(peer_v),
    )
    bp, tp, _ = x_prompt.shape
    bs, ts, _ = x_sample.shape
    assert sum(PROMPT_GROUPS) == bp
    groups, start = [], 0
    for gsz in PROMPT_GROUPS:
        zeros = lambda s: jnp.zeros((gsz,) + s.shape[2:], F32)
        groups.append(_trunk(x_prompt[start:start + gsz], zeros(state_hgrn), zeros(state_gla), zeros(state_rwkv),
                             zeros(state_shift), w, tp, tm=256, tmp=256, tb=512, chunk=64))
        start += gsz
    y_p = jnp.concatenate([g[0] for g in groups], axis=0)
    p_h, p_g, p_r, p_s = (jnp.concatenate([g[j] for g in groups], axis=1) for j in range(1, 5))
    y_s, s_h, s_g, s_r, s_s = _trunk(x_sample, state_hgrn[0], state_gla[0], state_rwkv[0], state_shift[0],
                                     w, ts, tm=32, tmp=128, tb=32, chunk=32)
    return (y_p, y_s, p_h, p_g, p_r, p_s, s_h, s_g, s_r, s_s)
```

```python
import functools

import jax
import jax.numpy as jnp
from jax import lax
from jax.experimental import pallas as pl
from jax.experimental.pallas import tpu as pltpu
from jax.experimental.pallas import tpu_sc as plsc

F32 = jnp.float32
BF16 = jnp.bfloat16
I32 = jnp.int32
HI = lax.Precision.HIGHEST

D_MODEL = 1024
NORM_EPS = 1e-6
LANES = 128
SUBLANES = 8
VMEM_LIMIT = 56 * 1024 * 1024

A_WIDTH = 512
A_HEADS = 4
B_WIDTH = 512
B_HEADS = 4
B_DK = 64
GLA_RANK = 16
GLA_NORMALIZER = 16.0
Z_WIDTH = 3712
C_HEAD = 64
C_HEADS = 16
C_GN_EPS = 64e-5
P_HEADS = 8
P_KEYS = 128
P_TOPK = 16
P_PICKS = P_HEADS * P_TOPK
SC_CORES = 2
SC_SUBCORES = 16
SC_WORKERS = SC_CORES * SC_SUBCORES
SC_LANES = 16
SC_CHUNK = 64
SC_TOK_BLOCK = 16
SC_WORDS = D_MODEL // 2
SC_HID_ROWS = 4
SC_HID_WORDS = 4
PROMPT_GROUPS = (1, 1, 2, 4, 4, 4)
GROUP_LAG = 2


def _cparams(sem):
    return pltpu.CompilerParams(dimension_semantics=sem, vmem_limit_bytes=VMEM_LIMIT)


def _rms(x, g):
    ms = jnp.mean(x * x, axis=-1, keepdims=True)
    return x * lax.rsqrt(ms + NORM_EPS) * g


def _dot(a, b, precision=None):
    return jnp.dot(a, b, preferred_element_type=F32, precision=precision)


def _dot_nt(a, b, precision=None):
    return lax.dot_general(a, b, (((1,), (1,)), ((), ())), preferred_element_type=F32, precision=precision)


def _dot_tn(a, b, precision=None):
    return lax.dot_general(a, b, (((0,), (0,)), ((), ())), preferred_element_type=F32, precision=precision)


def _tri(n, strict):
    r = lax.broadcasted_iota(I32, (n, n), 0)
    c = lax.broadcasted_iota(I32, (n, n), 1)
    return (c < r) if strict else (c <= r)


def _cumsum_rows(g):
    return _dot(_tri(g.shape[0], False).astype(F32), g, precision=HI)


def _lane_mask(width, lo, hi):
    l = lax.broadcasted_iota(I32, (1, width), 1)
    return (l >= lo) & (l < hi)


def _sigmoid(x):
    return 1.0 / (1.0 + jnp.exp(-x))


def _silu(x):
    return x * _sigmoid(x)


def _norm_proj_kernel(x_ref, g_ref, w_ref, o_ref):
    hn = _rms(x_ref[...], g_ref[...])
    o_ref[...] = _dot(hn.astype(BF16), w_ref[...])


def norm_proj(x, g, w_bf16, tm):
    n, d = x.shape
    f = w_bf16.shape[1]
    return pl.pallas_call(
        _norm_proj_kernel,
        grid=(n // tm,),
        in_specs=[pl.BlockSpec((tm, d), lambda i: (i, 0)),
                  pl.BlockSpec((1, d), lambda i: (0, 0)),
                  pl.BlockSpec((d, f), lambda i: (0, 0))],
        out_specs=pl.BlockSpec((tm, f), lambda i: (i, 0)),
        out_shape=jax.ShapeDtypeStruct((n, f), F32),
        compiler_params=_cparams(("parallel",)),
        name="norm_proj",
    )(x, g.reshape(1, d), w_bf16)


def _out_proj2_kernel(x_ref, a_ref, b_ref, wa_ref, wb_ref, o_ref):
    y = _dot(a_ref[...].astype(BF16), wa_ref[...]) + _dot(b_ref[...].astype(BF16), wb_ref[...])
    o_ref[...] = x_ref[...] + y


def out_proj2(x, a, b, wa, wb, tm):
    n, d = x.shape
    ka, kb = a.shape[1], b.shape[1]
    return pl.pallas_call(
        _out_proj2_kernel,
        grid=(n // tm,),
        in_specs=[pl.BlockSpec((tm, d), lambda i: (i, 0)),
                  pl.BlockSpec((tm, ka), lambda i: (i, 0)),
                  pl.BlockSpec((tm, kb), lambda i: (i, 0)),
                  pl.BlockSpec((ka, d), lambda i: (0, 0)),
                  pl.BlockSpec((kb, d), lambda i: (0, 0))],
        out_specs=pl.BlockSpec((tm, d), lambda i: (i, 0)),
        out_shape=jax.ShapeDtypeStruct((n, d), F32),
        compiler_params=_cparams(("parallel",)),
        name="out_proj2",
    )(x, a, b, wa, wb)


def _out_proj1_kernel(x_ref, a_ref, wa_ref, o_ref):
    o_ref[...] = x_ref[...] + _dot(a_ref[...].astype(BF16), wa_ref[...])


def out_proj1(x, a, wa, tm):
    n, d = x.shape
    ka = a.shape[1]
    return pl.pallas_call(
        _out_proj1_kernel,
        grid=(n // tm,),
        in_specs=[pl.BlockSpec((tm, d), lambda i: (i, 0)),
                  pl.BlockSpec((tm, ka), lambda i: (i, 0)),
                  pl.BlockSpec((ka, d), lambda i: (0, 0))],
        out_specs=pl.BlockSpec((tm, d), lambda i: (i, 0)),
        out_shape=jax.ShapeDtypeStruct((n, d), F32),
        compiler_params=_cparams(("parallel",)),
        name="out_proj1",
    )(x, a, wa)


def _final_norm_kernel(x_ref, g_ref, o_ref):
    o_ref[...] = _rms(x_ref[...], g_ref[...])


def final_norm(x, g, tm):
    n, d = x.shape
    return pl.pallas_call(
        _final_norm_kernel,
        grid=(n // tm,),
        in_specs=[pl.BlockSpec((tm, d), lambda i: (i, 0)), pl.BlockSpec((1, d), lambda i: (0, 0))],
        out_specs=pl.BlockSpec((tm, d), lambda i: (i, 0)),
        out_shape=jax.ShapeDtypeStruct((n, d), F32),
        compiler_params=_cparams(("parallel",)),
        name="final_norm",
    )(x, g.reshape(1, d))


def _intra_chunk(q, k, b, heads):
    c = q.shape[0]
    nb = c // SUBLANES
    row = lax.broadcasted_iota(I32, (SUBLANES, 1), 0)
    qb = [q[SUBLANES * i:SUBLANES * (i + 1)] for i in range(nb)]
    bb = [b[SUBLANES * i:SUBLANES * (i + 1)] for i in range(nb)]
    outs = [[None] * nb for _ in heads]
    for s in range(c):
        rb0 = s // SUBLANES
        ks = k[s:s + 1, :]
        bs = b[s:s + 1, :]
        for rb in range(rb0, nb):
            p = qb[rb] * (ks * jnp.exp(bb[rb] - bs))
            for hi, (mask, v) in enumerate(heads):
                pm = p if mask is None else jnp.where(mask, p, 0.0)
                col = jnp.sum(pm, axis=-1, keepdims=True)
                if rb == rb0:
                    col = jnp.where(row + SUBLANES * rb >= s, col, 0.0)
                term = col * v[s:s + 1, :]
                outs[hi][rb] = term if outs[hi][rb] is None else outs[hi][rb] + term
    return [jnp.concatenate(o, axis=0) for o in outs]


def _gated_chunk(q, k, b, heads, st):
    intra = _intra_chunk(q, k, b, heads)
    qe = q * jnp.exp(b)
    b_last = b[-1:, :]
    kh = k * jnp.exp(b_last - b)
    outs = []
    for (mask, v), oi in zip(heads, intra):
        qm = qe if mask is None else jnp.where(mask, qe, 0.0)
        outs.append(oi + _dot_nt(qm, st, precision=HI))
    upd = _dot_tn(heads[0][1], kh, precision=HI)
    if len(heads) == 2:
        upd = jnp.where(heads[0][0], upd, _dot_tn(heads[1][1], kh, precision=HI))
    st = st * jnp.exp(b_last) + upd
    return outs, st


def _head_rms(o, g):
    ms = jnp.mean(o * o, axis=-1, keepdims=True)
    return o * lax.rsqrt(ms + NORM_EPS) * g


def _hgrn_kernel(zq_ref, zf_ref, zi_ref, zg_ref, lb_ref, ng_ref, s0_ref, o_ref, s_ref, st_scr, *, chunk):
    t = pl.program_id(2)

    @pl.when(t == 0)
    def _():
        st_scr[...] = s0_ref[0, 0]

    lb = lb_ref[0]
    nchunks = zq_ref.shape[1] // chunk

    def body(ci, carry):
        sl = pl.ds(pl.multiple_of(ci * chunk, chunk), chunk)
        f = lb + (1.0 - lb) * _sigmoid(zf_ref[0, sl, :])
        q = _silu(zq_ref[0, sl, :])
        b = _cumsum_rows(jnp.log(f))
        (o,), st = _gated_chunk(q, 1.0 - f, b, [(None, zi_ref[0, sl, :])], st_scr[...])
        st_scr[...] = st
        o_ref[0, sl, :] = _head_rms(o, ng_ref[...]) * _silu(zg_ref[0, sl, :])
        return carry

    lax.fori_loop(0, nchunks, body, 0)

    @pl.when(t == pl.num_programs(2) - 1)
    def _():
        s_ref[0, 0] = st_scr[...]


def hgrn_recurrence(z, lb, norm_g, s0_t, tb, chunk):
    bsz, t, _ = z.shape
    zspec = lambda off: pl.BlockSpec((1, tb, LANES), lambda b, h, i: (b, i, h + off))
    return pl.pallas_call(
        functools.partial(_hgrn_kernel, chunk=chunk),
        grid=(bsz, A_HEADS, t // tb),
        in_specs=[zspec(0), zspec(4), zspec(8), zspec(12),
                  pl.BlockSpec((1, 1, LANES), lambda b, h, i: (h, 0, 0)),
                  pl.BlockSpec((1, LANES), lambda b, h, i: (0, 0)),
                  pl.BlockSpec((1, 1, LANES, LANES), lambda b, h, i: (b, h, 0, 0))],
        out_specs=[pl.BlockSpec((1, tb, LANES), lambda b, h, i: (b, i, h)),
                   pl.BlockSpec((1, 1, LANES, LANES), lambda b, h, i: (b, h, 0, 0))],
        out_shape=[jax.ShapeDtypeStruct((bsz, t, A_WIDTH), F32),
                   jax.ShapeDtypeStruct((bsz, A_HEADS, LANES, LANES), F32)],
        scratch_shapes=[pltpu.VMEM((LANES, LANES), F32)],
        compiler_params=_cparams(("parallel", "parallel", "arbitrary")),
        name="hgrn_recurrence",
    )(z, z, z, z, lb.reshape(A_HEADS, 1, LANES), norm_g.reshape(1, LANES), s0_t)


def _gla_kernel(zq_ref, zk_ref, zv_ref, zg_ref, zlr_ref, w2_ref, gb_ref, ng_ref, s0_ref, o_ref, s_ref, st_scr,
                *, chunk):
    t = pl.program_id(2)

    @pl.when(t == 0)
    def _():
        st_scr[...] = s0_ref[0, 0]

    nchunks = zq_ref.shape[1] // chunk
    m0 = _lane_mask(LANES, 0, B_DK)
    m1 = _lane_mask(LANES, B_DK, LANES)

    def body(ci, carry):
        sl = pl.ds(pl.multiple_of(ci * chunk, chunk), chunk)
        pre = _dot(zlr_ref[0, sl, :], w2_ref[...], precision=HI) + gb_ref[...]
        log_g = (jnp.minimum(pre, 0.0) - jnp.log(1.0 + jnp.exp(-jnp.abs(pre)))) * (1.0 / GLA_NORMALIZER)
        q = zq_ref[0, sl, :] * (B_DK ** -0.5)
        b = _cumsum_rows(log_g)
        v = zv_ref[0, sl, :]
        (o0, o1), st = _gated_chunk(q, zk_ref[0, sl, :], b,
                                    [(m0, v[:, :LANES]), (m1, v[:, LANES:])], st_scr[...])
        st_scr[...] = st
        gate = _silu(zg_ref[0, sl, :])
        o_ref[0, sl, 0:LANES] = _head_rms(o0, ng_ref[...]) * gate[:, :LANES]
        o_ref[0, sl, LANES:2 * LANES] = _head_rms(o1, ng_ref[...]) * gate[:, LANES:]
        return carry

    lax.fori_loop(0, nchunks, body, 0)

    @pl.when(t == pl.num_programs(2) - 1)
    def _():
        s_ref[0, 0] = st_scr[...]


def gla_recurrence(z, w2pad, gate_b, norm_g, s0_t, tb, chunk):
    bsz, t, _ = z.shape
    npairs = B_HEADS // 2
    return pl.pallas_call(
        functools.partial(_gla_kernel, chunk=chunk),
        grid=(bsz, npairs, t // tb),
        in_specs=[pl.BlockSpec((1, tb, LANES), lambda b, p, i: (b, i, 16 + p)),
                  pl.BlockSpec((1, tb, LANES), lambda b, p, i: (b, i, 18 + p)),
                  pl.BlockSpec((1, tb, 2 * LANES), lambda b, p, i: (b, i, 10 + p)),
                  pl.BlockSpec((1, tb, 2 * LANES), lambda b, p, i: (b, i, 12 + p)),
                  pl.BlockSpec((1, tb, LANES), lambda b, p, i: (b, i, 28)),
                  pl.BlockSpec((LANES, LANES), lambda b, p, i: (0, p)),
                  pl.BlockSpec((1, LANES), lambda b, p, i: (0, p)),
                  pl.BlockSpec((1, LANES), lambda b, p, i: (0, 0)),
                  pl.BlockSpec((1, 1, LANES, LANES), lambda b, p, i: (b, p, 0, 0))],
        out_specs=[pl.BlockSpec((1, tb, 2 * LANES), lambda b, p, i: (b, i, p)),
                   pl.BlockSpec((1, 1, LANES, LANES), lambda b, p, i: (b, p, 0, 0))],
        out_shape=[jax.ShapeDtypeStruct((bsz, t, B_WIDTH), F32),
                   jax.ShapeDtypeStruct((bsz, npairs, LANES, LANES), F32)],
        scratch_shapes=[pltpu.VMEM((LANES, LANES), F32)],
        compiler_params=_cparams(("parallel", "parallel", "arbitrary")),
        name="gla_recurrence",
    )(z, z, z, z, z, w2pad, gate_b.reshape(1, 2 * LANES), norm_g.reshape(1, LANES), s0_t)


def _rwkv_proj_kernel(x_ref, xp_ref, xl_ref, g1_ref, mu_ref, wr_ref, wk_ref, wv_ref, ww1_ref, ww2_ref, w0_ref,
                      aw1_ref, aw2_ref, a0_ref, gw1_ref, gw2_ref, kk_ref, ka_ref,
                      r_out, k_out, v_out, lw_out, a_out, kk_out, g_out, hl_out, *, tiles_per_seq):
    i = pl.program_id(0)
    g1 = g1_ref[...]
    hn = _rms(x_ref[...], g1)
    tm = hn.shape[0]
    prev = _rms(xp_ref[...], g1)[SUBLANES - 1:SUBLANES, :]
    prev = jnp.where(i % tiles_per_seq == 0, xl_ref[0], prev)
    row = lax.broadcasted_iota(I32, (tm, 1), 0)
    xprev = jnp.where(row == 0, prev, pltpu.roll(hn, 1, axis=0))
    dx = xprev - hn

    def mix(j):
        return hn + dx * mu_ref[j:j + 1, :]

    r = _dot(mix(0).astype(BF16), wr_ref[...])
    k = _dot(mix(1).astype(BF16), wk_ref[...])
    v = _dot(mix(2).astype(BF16), wv_ref[...])
    wl = _dot(jnp.tanh(_dot(mix(3), ww1_ref[...], precision=HI)), ww2_ref[...], precision=HI)
    z = w0_ref[...] + wl
    wpre = -(jnp.maximum(-z, 0.0) + jnp.log(1.0 + jnp.exp(-jnp.abs(z)))) - 0.5
    al = _dot(_dot(mix(4), aw1_ref[...], precision=HI), aw2_ref[...], precision=HI)
    a = _sigmoid(a0_ref[...] + al)
    gg = _dot(_sigmoid(_dot(mix(5).astype(BF16), gw1_ref[...])).astype(BF16), gw2_ref[...])
    r_out[...] = r
    k_out[...] = k * (1.0 + (a - 1.0) * ka_ref[...])
    v_out[...] = v
    lw_out[...] = -jnp.exp(wpre)
    a_out[...] = a
    kk_out[...] = k * kk_ref[...]
    g_out[...] = gg
    hl_out[0] = hn[tm - 1:tm, :]


def rwkv_proj(x, x_last, seq_len, g1, mu, wr, wk, wv, ww1, ww2, w0, aw1, aw2, a0, gw1, gw2, k_k, k_a, tm):
    n, d = x.shape
    tiles_per_seq = seq_len // tm
    row = lambda a: a.reshape(1, d)
    full = lambda a: pl.BlockSpec(a.shape, lambda i: (0,) * a.ndim)
    tile = pl.BlockSpec((tm, d), lambda i: (i, 0))
    blocks8 = tm // SUBLANES
    args = (x, x, x_last.reshape(-1, 1, d), row(g1), mu, wr, wk, wv, ww1, ww2, row(w0), aw1, aw2, row(a0),
            gw1, gw2, row(k_k), row(k_a))
    in_specs = [tile,
                pl.BlockSpec((SUBLANES, d), lambda i: (jnp.maximum(i * blocks8 - 1, 0), 0)),
                pl.BlockSpec((1, 1, d), lambda i: (i // tiles_per_seq, 0, 0))]
    in_specs += [full(a) for a in args[3:]]
    outs = pl.pallas_call(
        functools.partial(_rwkv_proj_kernel, tiles_per_seq=tiles_per_seq),
        grid=(n // tm,),
        in_specs=in_specs,
        out_specs=[tile] * 7 + [pl.BlockSpec((1, 1, d), lambda i: (i, 0, 0))],
        out_shape=[jax.ShapeDtypeStruct((n, d), F32)] * 7 + [jax.ShapeDtypeStruct((n // tm, 1, d), F32)],
        compiler_params=_cparams(("parallel",)),
        name="rwkv_proj",
    )(*args)
    return outs


_NN = ((1,), (0,))
_NT = ((1,), (1,))
_TN = ((0,), (0,))
RWKV_AB_PASSES = 1
RWKV_INV_PASSES = 1
RWKV_APPLY_PASSES = 1
RWKV_STATE_PASSES = 3
RWKV_SEQS_PER_STEP = 4


def _split_bf16(a):
    hi = a.astype(BF16)
    return hi, (a - hi.astype(F32)).astype(BF16)


def _mm(a, b, dims, passes):
    if passes == 6:
        return lax.dot_general(a, b, (dims, ((), ())), preferred_element_type=F32, precision=HI)
    dg = lambda x, y: lax.dot_general(x, y, (dims, ((), ())), preferred_element_type=F32)
    ah, al = _split_bf16(a)
    bh, bl = _split_bf16(b)
    if passes == 1:
        return dg(ah, bh)
    return dg(ah, bh) + (dg(al, bh) + dg(ah, bl))


def _cumsum_rows3(g):
    tri = _tri(g.shape[0], False).astype(BF16)
    h1 = g.astype(BF16)
    r1 = g - h1.astype(F32)
    h2 = r1.astype(BF16)
    h3 = (r1 - h2.astype(F32)).astype(BF16)
    return _dot(tri, h1) + (_dot(tri, h2) + _dot(tri, h3))


def _pair_sum(x, m0):
    s0 = jnp.sum(jnp.where(m0, x, 0.0), axis=-1, keepdims=True)
    s1 = jnp.sum(jnp.where(m0, 0.0, x), axis=-1, keepdims=True)
    return jnp.where(m0, s0, s1)


def _rwkv_kernel(r_ref, k_ref, v_ref, lw_ref, a_ref, kk_ref, g_ref, rk_ref, lng_ref, lnb_ref, s0_ref,
                 o_ref, s_ref, mt_scr, *, chunk):
    t = pl.program_id(2)

    @pl.when(t == 0)
    def _():
        mt_scr[...] = s0_ref[:, 0]

    nrows = r_ref.shape[0]
    nchunks = r_ref.shape[1] // chunk
    c2 = 2 * chunk
    m0 = _lane_mask(LANES, 0, C_HEAD)
    rowi = lax.broadcasted_iota(I32, (LANES, LANES), 0)
    coli = lax.broadcasted_iota(I32, (LANES, LANES), 1)
    blockdiag = (rowi < C_HEAD) == (coli < C_HEAD)
    ti = lax.broadcasted_iota(I32, (c2, c2), 0)
    si = lax.broadcasted_iota(I32, (c2, c2), 1)
    same_head = (ti < chunk) == (si < chunk)
    tm_ = jnp.where(ti < chunk, ti, ti - chunk)
    sm_ = jnp.where(si < chunk, si, si - chunk)
    strict = same_head & (sm_ < tm_)
    incl = same_head & (sm_ <= tm_)

    def stack_heads(x):
        return jnp.concatenate([jnp.where(m0, x, 0.0), jnp.where(m0, 0.0, x)], axis=0)

    def twice(x):
        return jnp.concatenate([x, x], axis=0)

    def unstack(x2):
        return jnp.where(m0, x2[:chunk], x2[chunk:])

    eye = (ti == si).astype(F32)
    seqs = range(nrows)

    def body(ci, carry):
        sl = pl.ds(pl.multiple_of(ci * chunk, chunk), chunk)
        r = [r_ref[i, sl, :] for i in seqs]
        k = [k_ref[i, sl, :] for i in seqs]
        v = [v_ref[i, sl, :] for i in seqs]
        lw = [lw_ref[i, sl, :] for i in seqs]
        kkr = [kk_ref[i, sl, :] for i in seqs]
        kk = [x * lax.rsqrt(_pair_sum(x * x, m0) + 1e-12) for x in kkr]
        al = [a_ref[i, sl, :] * kk[i] for i in seqs]
        gam = [_cumsum_rows3(x) for x in lw]
        e_neg = [jnp.exp(-x) for x in gam]
        xr = [jnp.concatenate([stack_heads(kk[i] * jnp.exp(gam[i] - lw[i])),
                               stack_heads(r[i] * jnp.exp(gam[i]))], axis=0) for i in seqs]
        alk = [jnp.concatenate([twice(al[i] * e_neg[i]), twice(k[i] * e_neg[i])], axis=0) for i in seqs]
        mt = [mt_scr[i] for i in seqs]
        ab = [_mm(xr[i], alk[i], _NT, RWKV_AB_PASSES) for i in seqs]
        xm = [_mm(xr[i], mt[i], _NT, RWKV_STATE_PASSES) for i in seqs]
        a_al = [jnp.where(strict, x[:c2, :c2], 0.0) for x in ab]
        a_k = [jnp.where(strict, x[:c2, c2:], 0.0) for x in ab]
        b_alk = [jnp.concatenate([jnp.where(incl, x[c2:, c2:], 0.0), jnp.where(incl, -x[c2:, :c2], 0.0)], axis=1)
                 for x in ab]
        v2 = [twice(x) for x in v]
        rhs = [xm[i][:c2] + _mm(a_k[i], v2[i], _NN, RWKV_APPLY_PASSES) for i in seqs]
        p = [-x for x in a_al]
        tinv = [eye + x for x in p]
        span = 2
        while span < chunk:
            p = [_mm(x, x, _NN, RWKV_INV_PASSES) for x in p]
            tinv = [tinv[i] + _mm(tinv[i], p[i], _NN, RWKV_INV_PASSES) for i in seqs]
            span *= 2
        u = [unstack(_mm(tinv[i], rhs[i], _NN, RWKV_APPLY_PASSES)) for i in seqs]
        o = [unstack(xm[i][c2:] + _mm(b_alk[i], jnp.concatenate([v2[i], twice(u[i])], axis=0), _NN,
                                      RWKV_APPLY_PASSES)) for i in seqs]
        g_last = [x[-1:, :] for x in gam]
        e_end = [jnp.exp(g_last[i] - gam[i]) for i in seqs]
        upd = [_mm(jnp.concatenate([v[i], u[i]], axis=0),
                   jnp.concatenate([k[i] * e_end[i], -(al[i] * e_end[i])], axis=0), _TN, RWKV_STATE_PASSES)
               for i in seqs]
        for i in seqs:
            mt_scr[i] = mt[i] * jnp.exp(g_last[i]) + jnp.where(blockdiag, upd[i], 0.0)
        for i in seqs:
            mean = _pair_sum(o[i], m0) * (1.0 / C_HEAD)
            cen = o[i] - mean
            var = _pair_sum(cen * cen, m0) * (1.0 / C_HEAD)
            on = cen * lax.rsqrt(var + C_GN_EPS) * lng_ref[...] + lnb_ref[...]
            bonus = _pair_sum(r[i] * k[i] * rk_ref[...], m0) * v[i]
            o_ref[i, sl, :] = (on + bonus) * g_ref[i, sl, :]
        return carry

    lax.fori_loop(0, nchunks, body, 0)

    @pl.when(t == pl.num_programs(2) - 1)
    def _():
        s_ref[:, 0] = mt_scr[...]


def rwkv_recurrence(r, k, v, lw, a, kk, g, r_k, ln_g, ln_b, s0_bd, tb, chunk, nb):
    bsz, t, d = r.shape
    npairs = C_HEADS // 2
    seq = pl.BlockSpec((nb, tb, LANES), lambda b, p, i: (b, i, p))
    vec = pl.BlockSpec((1, LANES), lambda b, p, i: (0, p))
    st = pl.BlockSpec((nb, 1, LANES, LANES), lambda b, p, i: (b, p, 0, 0))
    return pl.pallas_call(
        functools.partial(_rwkv_kernel, chunk=chunk),
        grid=(bsz // nb, npairs, t // tb),
        in_specs=[seq] * 7 + [vec, vec, vec, st],
        out_specs=[seq, st],
        out_shape=[jax.ShapeDtypeStruct((bsz, t, d), F32),
                   jax.ShapeDtypeStruct((bsz, npairs, LANES, LANES), F32)],
        scratch_shapes=[pltpu.VMEM((nb, LANES, LANES), F32)],
        compiler_params=_cparams(("parallel", "parallel", "arbitrary")),
        name="rwkv_recurrence",
    )(r, k, v, lw, a, kk, g, r_k.reshape(1, d), ln_g.reshape(1, d), ln_b.reshape(1, d), s0_bd)


NEG_INF = float("-inf")


def _top16_rows(s):
    n = s.shape[0]
    key = lax.broadcasted_iota(I32, s.shape, 0)
    vals, idxs = [], []
    for _ in range(P_TOPK):
        m = jnp.max(s, axis=0, keepdims=True)
        am = jnp.min(jnp.where(s == m, key, n), axis=0, keepdims=True)
        vals.append(m)
        idxs.append(am)
        s = jnp.where(key == am, NEG_INF, s)
    return vals, idxs


def _top16_pairs(v0, i0, v1, i1):
    a0 = jnp.concatenate(v0[0:8], axis=0)
    a1 = jnp.concatenate(v0[8:16], axis=0)
    b0 = jnp.concatenate(v1[0:8], axis=0)
    b1 = jnp.concatenate(v1[8:16], axis=0)
    ia0 = jnp.concatenate(i0[0:8], axis=0) * P_KEYS
    ia1 = jnp.concatenate(i0[8:16], axis=0) * P_KEYS
    ib0 = jnp.concatenate(i1[0:8], axis=0)
    ib1 = jnp.concatenate(i1[8:16], axis=0)
    row = lax.broadcasted_iota(I32, (SUBLANES, 1), 0)
    slabs = []

    def add(val, eid, keep):
        slabs.append((val if keep is None else jnp.where(keep, val, NEG_INF), eid))

    add(v0[0] + b0, ia0[0:1] + ib0, None)
    add(v0[0] + b1, ia0[0:1] + ib1, None)
    add(v0[1] + b0, ia0[1:2] + ib0, None)
    add(v0[2] + b0, ia0[2:3] + ib0, row < 5)
    add(v0[3] + b0, ia0[3:4] + ib0, row < 4)
    add(a0 + v1[0], ia0 + ib0[0:1], row >= 4)
    add(a1 + v1[0], ia1 + ib0[0:1], None)
    add(a0 + v1[1], ia0 + ib0[1:2], row >= 4)
    add(a0 + v1[2], ia0 + ib0[2:3], row == 4)

    big = P_KEYS * P_KEYS
    out_v, out_e = [], []
    for _ in range(P_TOPK):
        m = slabs[0][0]
        for val, _e in slabs[1:]:
            m = jnp.maximum(m, val)
        m = jnp.max(m, axis=0, keepdims=True)
        e = None
        for val, eid in slabs:
            c = jnp.where(val == m, eid, big)
            e = c if e is None else jnp.minimum(e, c)
        e = jnp.min(e, axis=0, keepdims=True)
        out_v.append(m)
        out_e.append(e)
        slabs = [(jnp.where(eid == e, NEG_INF, val), eid) for val, eid in slabs]
    return out_v, out_e


def _peer_select_kernel(x_ref, g_ref, wq_ref, keys_ref, xn_out, eid_out, gate_out):
    hn = _rms(x_ref[...], g_ref[...])
    xn_out[...] = hn
    q = _dot(hn.astype(BF16), wq_ref[...])
    tm = q.shape[0]
    for lt in range(tm // LANES):
        rows = slice(lt * LANES, (lt + 1) * LANES)
        e_rows, g_rows = [], []
        for h in range(P_HEADS):
            tops = []
            for p in range(2):
                hp = 2 * h + p
                s = _dot_nt(keys_ref[hp], q[rows, hp * LANES:(hp + 1) * LANES], precision=HI)
                tops.append(_top16_rows(s))
            cs, ce = _top16_pairs(tops[0][0], tops[0][1], tops[1][0], tops[1][1])
            ex = [jnp.exp(c - cs[0]) for c in cs]
            tot = ex[0]
            for e in ex[1:]:
                tot = tot + e
            inv = 1.0 / tot
            e_rows += ce
            g_rows += [e * inv for e in ex]
        eid_out[rows, :] = jnp.concatenate(e_rows, axis=0).T
        gate_out[rows, :] = jnp.concatenate(g_rows, axis=0).T


def peer_select(x, g, wq_bf16, keys, tm):
    n, d = x.shape
    return pl.pallas_call(
        _peer_select_kernel,
        grid=(n // tm,),
        in_specs=[pl.BlockSpec((tm, d), lambda i: (i, 0)),
                  pl.BlockSpec((1, d), lambda i: (0, 0)),
                  pl.BlockSpec(wq_bf16.shape, lambda i: (0, 0)),
                  pl.BlockSpec(keys.shape, lambda i: (0, 0, 0))],
        out_specs=[pl.BlockSpec((tm, d), lambda i: (i, 0)),
                   pl.BlockSpec((tm, P_PICKS), lambda i: (i, 0)),
                   pl.BlockSpec((tm, P_PICKS), lambda i: (i, 0))],
        out_shape=[jax.ShapeDtypeStruct((n, d), F32),
                   jax.ShapeDtypeStruct((n, P_PICKS), I32),
                   jax.ShapeDtypeStruct((n, P_PICKS), F32)],
        compiler_params=_cparams(("parallel",)),
        name="peer_select",
    )(x, g.reshape(1, d), wq_bf16, keys)


def _peer_act_kernel(h_ref, g_ref, o_ref):
    h = h_ref[...]
    o_ref[...] = 0.5 * h * (1.0 + lax.erf(h * (2.0 ** -0.5))) * g_ref[...]


def peer_act(hid, gate, tm):
    n, p = hid.shape
    spec = pl.BlockSpec((tm, p), lambda i: (i, 0))
    return pl.pallas_call(
        _peer_act_kernel, grid=(n // tm,), in_specs=[spec, spec], out_specs=spec,
        out_shape=jax.ShapeDtypeStruct((n, p), F32),
        compiler_params=_cparams(("parallel",)), name="peer_act",
    )(hid, gate)


_CHUNKS_PER_TOK = P_PICKS // SC_CHUNK
_WORD_VREGS = SC_WORDS // SC_LANES
U32 = jnp.uint32


def pack_rows_bf16(t):
    lo = lax.bitcast_convert_type(t[..., :SC_WORDS].astype(BF16), jnp.uint16).astype(U32)
    a = lax.bitcast_convert_type(t[..., SC_WORDS:], U32)
    sign = a & U32(0x80000000)
    mag = a & U32(0x7FFFFFFF)
    steps = (jnp.maximum(mag + U32(0x8000), lo) - lo) >> 16
    near = (steps << 16) + lo
    near = jnp.where(near >= U32(0x7F800000), near - U32(0x10000), near)
    return lax.bitcast_convert_type(sign | near, I32)


def _low_f32(w):
    return lax.bitcast_convert_type(w << 16, F32)


def _high_f32(w):
    return lax.bitcast_convert_type(w, F32)


def _sc_worker():
    return lax.axis_index("s") * SC_CORES + lax.axis_index("c")


def _sc_hid_body(tab_hbm, idx_hbm, x_hbm, hid_hbm, idx_v, x_v, hid_v, rows_a, rows_b, sem_a, sem_b, *, tok_per_w):
    wid = _sc_worker()
    nblk = tok_per_w // SC_TOK_BLOCK
    nch = SC_TOK_BLOCK * _CHUNKS_PER_TOK
    lane = lax.iota(I32, SC_LANES)
    quads = SC_LANES // SC_HID_ROWS

    def dots(rows, x_row, res_ref, col0):
        for grp in range(SC_CHUNK // SC_LANES):
            def some_rows(q, res):
                r0 = grp * SC_LANES + q * SC_HID_ROWS

                def some_words(jj, accs):
                    accs = list(accs)
                    for jw in range(SC_HID_WORDS):
                        off = (jj * SC_HID_WORDS + jw) * SC_LANES
                        xl = x_v[x_row, pl.ds(off, SC_LANES)]
                        xh = x_v[x_row, pl.ds(SC_WORDS + off, SC_LANES)]
                        for i in range(SC_HID_ROWS):
                            w = rows[r0 + i, pl.ds(off, SC_LANES)]
                            k = 2 * i + jw % 2
                            accs[k] = accs[k] + (_low_f32(w) * xl + _high_f32(w) * xh)
                    return tuple(accs)

                zero = jnp.zeros((SC_LANES,), F32)
                accs = lax.fori_loop(0, _WORD_VREGS // SC_HID_WORDS, some_words, (zero,) * (2 * SC_HID_ROWS))
                for i in range(SC_HID_ROWS):
                    res = jnp.where(lane == q * SC_HID_ROWS + i, jnp.sum(accs[2 * i] + accs[2 * i + 1]), res)
                return res
            res = lax.fori_loop(0, quads, some_rows, jnp.zeros((SC_LANES,), F32))
            res_ref[x_row, pl.ds(col0 + grp * SC_LANES, SC_LANES)] = res

    def block(bi, carry):
        tok0 = wid * tok_per_w + bi * SC_TOK_BLOCK
        pltpu.sync_copy(idx_hbm.at[pl.ds(tok0 * _CHUNKS_PER_TOK, nch)], idx_v)
        pltpu.sync_copy(x_hbm.at[pl.ds(tok0, SC_TOK_BLOCK)], x_v)
        pltpu.async_copy(tab_hbm.at[idx_v.at[0]], rows_a, sem_a)

        def pair(j, c):
            ca = 2 * j
            pltpu.async_copy(tab_hbm.at[idx_v.at[ca + 1]], rows_b, sem_b)
            pltpu.make_async_copy(tab_hbm.at[idx_v.at[0]], rows_a, sem_a).wait()
            dots(rows_a, ca // _CHUNKS_PER_TOK, hid_v, (ca % _CHUNKS_PER_TOK) * SC_CHUNK)

            @pl.when(j < nch // 2 - 1)
            def _():
                pltpu.async_copy(tab_hbm.at[idx_v.at[ca + 2]], rows_a, sem_a)
            pltpu.make_async_copy(tab_hbm.at[idx_v.at[0]], rows_b, sem_b).wait()
            cb = ca + 1
            dots(rows_b, cb // _CHUNKS_PER_TOK, hid_v, (cb % _CHUNKS_PER_TOK) * SC_CHUNK)
            return c
        lax.fori_loop(0, nch // 2, pair, 0)
        pltpu.sync_copy(hid_v, hid_hbm.at[pl.ds(tok0, SC_TOK_BLOCK)])
        return carry

    lax.fori_loop(0, nblk, block, 0)


def sc_expert_hidden(table_packed, eidx, xn):
    n = xn.shape[0]
    tok_per_w = n // SC_WORKERS
    mesh = plsc.VectorSubcoreMesh(core_axis_name="c", subcore_axis_name="s")
    nch = SC_TOK_BLOCK * _CHUNKS_PER_TOK
    k = pl.kernel(
        functools.partial(_sc_hid_body, tok_per_w=tok_per_w), mesh=mesh,
        out_type=jax.ShapeDtypeStruct((n, P_PICKS), F32),
        scratch_types=[pltpu.VMEM((nch, SC_CHUNK), I32),
                       pltpu.VMEM((SC_TOK_BLOCK, D_MODEL), F32),
                       pltpu.VMEM((SC_TOK_BLOCK, P_PICKS), F32),
                       pltpu.VMEM((SC_CHUNK, SC_WORDS), I32),
                       pltpu.VMEM((SC_CHUNK, SC_WORDS), I32),
                       pltpu.SemaphoreType.DMA, pltpu.SemaphoreType.DMA],
        compiler_params=pltpu.CompilerParams(needs_layout_passes=False),
        name="sc_expert_hidden",
    )
    return k(table_packed, eidx.reshape(n * _CHUNKS_PER_TOK, SC_CHUNK), xn)


def _sc_out_body(tab_hbm, idx_hbm, act_hbm, x_hbm, y_hbm, idx_v, act_v, y_v, rows_a, rows_b, sem_a, sem_b,
                 *, tok_per_w):
    wid = _sc_worker()
    nblk = tok_per_w // SC_TOK_BLOCK
    nch = SC_TOK_BLOCK * _CHUNKS_PER_TOK
    half_w = _WORD_VREGS // 2

    def accum(rows, tok, col0):
        for hv in range(2):
            base = hv * half_w * SC_LANES
            acc0 = (tuple(y_v[tok, pl.ds(base + j * SC_LANES, SC_LANES)] for j in range(half_w))
                    + tuple(y_v[tok, pl.ds(SC_WORDS + base + j * SC_LANES, SC_LANES)] for j in range(half_w)))

            def one(r, acc):
                aidx = jnp.full((SC_LANES,), col0, I32) + r
                wgt = plsc.load_gather(act_v, [jnp.full((SC_LANES,), tok, I32), aidx])
                lo, hi = [], []
                for j in range(half_w):
                    w = rows[r, pl.ds(base + j * SC_LANES, SC_LANES)]
                    lo.append(acc[j] + wgt * _low_f32(w))
                    hi.append(acc[half_w + j] + wgt * _high_f32(w))
                return tuple(lo + hi)
            acc = lax.fori_loop(0, SC_CHUNK, one, acc0)
            for j in range(half_w):
                y_v[tok, pl.ds(base + j * SC_LANES, SC_LANES)] = acc[j]
                y_v[tok, pl.ds(SC_WORDS + base + j * SC_LANES, SC_LANES)] = acc[half_w + j]

    def block(bi, carry):
        tok0 = wid * tok_per_w + bi * SC_TOK_BLOCK
        pltpu.sync_copy(idx_hbm.at[pl.ds(tok0 * _CHUNKS_PER_TOK, nch)], idx_v)
        pltpu.sync_copy(act_hbm.at[pl.ds(tok0, SC_TOK_BLOCK)], act_v)
        pltpu.sync_copy(x_hbm.at[pl.ds(tok0, SC_TOK_BLOCK)], y_v)
        pltpu.async_copy(tab_hbm.at[idx_v.at[0]], rows_a, sem_a)

        def pair(j, c):
            ca = 2 * j
            pltpu.async_copy(tab_hbm.at[idx_v.at[ca + 1]], rows_b, sem_b)
            pltpu.make_async_copy(tab_hbm.at[idx_v.at[0]], rows_a, sem_a).wait()
            accum(rows_a, ca // _CHUNKS_PER_TOK, (ca % _CHUNKS_PER_TOK) * SC_CHUNK)

            @pl.when(j < nch // 2 - 1)
            def _():
                pltpu.async_copy(tab_hbm.at[idx_v.at[ca + 2]], rows_a, sem_a)
            pltpu.make_async_copy(tab_hbm.at[idx_v.at[0]], rows_b, sem_b).wait()
            cb = ca + 1
            accum(rows_b, cb // _CHUNKS_PER_TOK, (cb % _CHUNKS_PER_TOK) * SC_CHUNK)
            return c
        lax.fori_loop(0, nch // 2, pair, 0)
        pltpu.sync_copy(y_v, y_hbm.at[pl.ds(tok0, SC_TOK_BLOCK)])
        return carry

    lax.fori_loop(0, nblk, block, 0)


def sc_expert_output(table_packed, eidx, act, x):
    n = x.shape[0]
    tok_per_w = n // SC_WORKERS
    mesh = plsc.VectorSubcoreMesh(core_axis_name="c", subcore_axis_name="s")
    nch = SC_TOK_BLOCK * _CHUNKS_PER_TOK
    k = pl.kernel(
        functools.partial(_sc_out_body, tok_per_w=tok_per_w), mesh=mesh,
        out_type=jax.ShapeDtypeStruct((n, D_MODEL), F32),
        scratch_types=[pltpu.VMEM((nch, SC_CHUNK), I32),
                       pltpu.VMEM((SC_TOK_BLOCK, P_PICKS), F32),
                       pltpu.VMEM((SC_TOK_BLOCK, D_MODEL), F32),
                       pltpu.VMEM((SC_CHUNK, SC_WORDS), I32),
                       pltpu.VMEM((SC_CHUNK, SC_WORDS), I32),
                       pltpu.SemaphoreType.DMA, pltpu.SemaphoreType.DMA],
        compiler_params=pltpu.CompilerParams(needs_layout_passes=False),
        name="sc_expert_output",
    )
    return k(table_packed, eidx.reshape(n * _CHUNKS_PER_TOK, SC_CHUNK), act, x)


def peer_ffn(x, g, wq_bf16, keys, u_packed, v_packed, tm):
    xn, eidx, gate = peer_select(x, g, wq_bf16, keys, tm)
    hid = sc_expert_hidden(u_packed, eidx, xn)
    act = peer_act(hid, gate, tm)
    return sc_expert_output(v_packed, eidx, act, x), act


def _prep_w_in(w_in):
    main = jnp.concatenate([w_in[:, :3072], w_in[:, 3088:3600]], axis=1)
    lr = jnp.pad(w_in[:, 3072:3088], ((0, 0), (0, LANES - GLA_RANK)))
    return jnp.concatenate([main, lr], axis=1).astype(BF16)


def _trunk(x, st_h, st_g, st_r, st_s, w, seq_len, tm, tmp, tb, chunk, after=None):
    bsz = x.shape[0]
    n = bsz * seq_len
    if after is not None:
        x, _ = lax.optimization_barrier((x, after))
    x2 = x.reshape(n, D_MODEL)

    z = norm_proj(x2, w["norm1_g"][0], w["w_in"], tm).reshape(bsz, seq_len, Z_WIDTH)
    s0_h = jnp.swapaxes(st_h, -1, -2)
    s0_g = jnp.swapaxes(st_g.reshape(bsz, 2, 2 * B_DK, LANES), -1, -2)
    o_a, sh_t = hgrn_recurrence(z, w["lb0"], w["hgrn_norm_g"], s0_h, tb, chunk)
    o_b, sg_t = gla_recurrence(z, w["gla_w2"], w["gla_b"], w["gla_norm_g"], s0_g, tb, chunk)
    new_h = jnp.swapaxes(sh_t, -1, -2)
    new_g = jnp.swapaxes(sg_t, -1, -2).reshape(bsz, B_HEADS, B_DK, LANES)
    x2 = out_proj2(x2, o_a.reshape(n, A_WIDTH), o_b.reshape(n, B_WIDTH), w["w_out_a"], w["w_out_b"], tm)
    x2, act0 = peer_ffn(x2, w["norm2_g"][0], w["peer_wq"][0], w["peer_keys"][0], w["peer_u"][0], w["peer_v"][0],
                        tmp)

    r, k, v, lw, a, kk, g, hl = rwkv_proj(
        x2, st_s, seq_len, w["norm1_g"][1], w["mu"], w["wr"], w["wk"], w["wv"], w["w_w1"], w["w_w2"], w["w0"],
        w["a_w1"], w["a_w2"], w["a0"], w["g_w1"], w["g_w2"], w["k_k"], w["k_a"], tm)
    new_s = hl.reshape(bsz, seq_len // tm, D_MODEL)[:, -1]
    pr = st_r.reshape(bsz, C_HEADS // 2, 2, C_HEAD, C_HEAD)
    zero = jnp.zeros_like(pr[:, :, 0])
    s0_r = jnp.concatenate([jnp.concatenate([pr[:, :, 0], zero], axis=-1),
                            jnp.concatenate([zero, pr[:, :, 1]], axis=-1)], axis=-2)
    sh3 = lambda t: t.reshape(bsz, seq_len, D_MODEL)
    o_c, sr_bd = rwkv_recurrence(sh3(r), sh3(k), sh3(v), sh3(lw), sh3(a), sh3(kk), sh3(g),
                                 w["r_k"], w["ln_g"], w["ln_b"], s0_r, tb, chunk, min(RWKV_SEQS_PER_STEP, bsz))
    new_r = jnp.stack([sr_bd[:, :, :C_HEAD, :C_HEAD], sr_bd[:, :, C_HEAD:, C_HEAD:]], axis=2)
    new_r = new_r.reshape(bsz, C_HEADS, C_HEAD, C_HEAD)
    x2 = out_proj1(x2, o_c.reshape(n, D_MODEL), w["w_out_c"], tm)
    x2, _ = peer_ffn(x2, w["norm2_g"][1], w["peer_wq"][1], w["peer_keys"][1], w["peer_u"][1], w["peer_v"][1], tmp)

    y = final_norm(x2, w["final_g"], tm).reshape(bsz, seq_len, D_MODEL)
    return y, new_h[None], new_g[None], new_r[None], new_s[None], act0


def kernel(x_prompt, x_sample, state_hgrn, state_gla, state_rwkv, state_shift, w_in_ab, hgrn_lower_bounds, hgrn_norm_g, gla_gate_w2, gla_gate_b, gla_norm_g, w_out_ab, rwkv_mu, rwkv_w_rkv, rwkv_w_w1, rwkv_w_w2, rwkv_w0, rwkv_a_w1, rwkv_a_w2, rwkv_a0, rwkv_g_w1, rwkv_g_w2, rwkv_k_k, rwkv_k_a, rwkv_r_k, rwkv_ln_g, rwkv_ln_b, w_out_c, norm1_g, norm2_g, final_g, peer_w_q, peer_sub_keys, peer_u, peer_v):
    lbs = jnp.cumsum(jax.nn.softmax(hgrn_lower_bounds.astype(F32), axis=0), axis=0)
    w = dict(
        norm1_g=norm1_g, norm2_g=norm2_g, final_g=final_g,
        w_in=_prep_w_in(w_in_ab[0]), lb0=lbs[0], hgrn_norm_g=hgrn_norm_g[0],
        gla_w2=jnp.pad(gla_gate_w2[0], ((0, LANES - GLA_RANK), (0, 0))), gla_b=gla_gate_b[0],
        gla_norm_g=gla_norm_g[0],
        w_out_a=w_out_ab[0, :A_WIDTH].astype(BF16), w_out_b=w_out_ab[0, A_WIDTH:].astype(BF16),
        mu=rwkv_mu[0], wr=rwkv_w_rkv[0, 0].astype(BF16), wk=rwkv_w_rkv[0, 1].astype(BF16),
        wv=rwkv_w_rkv[0, 2].astype(BF16), w_w1=rwkv_w_w1[0], w_w2=rwkv_w_w2[0], w0=rwkv_w0[0],
        a_w1=rwkv_a_w1[0], a_w2=rwkv_a_w2[0], a0=rwkv_a0[0],
        g_w1=rwkv_g_w1[0].astype(BF16), g_w2=rwkv_g_w2[0].astype(BF16),
        k_k=rwkv_k_k[0], k_a=rwkv_k_a[0], r_k=rwkv_r_k[0], ln_g=rwkv_ln_g[0], ln_b=rwkv_ln_b[0],
        w_out_c=w_out_c[0].astype(BF16),
        peer_wq=peer_w_q.astype(BF16),
        peer_keys=peer_sub_keys.reshape(peer_sub_keys.shape[0], 2 * P_HEADS, P_KEYS, P_KEYS),
        peer_u=pack_rows_bf16(peer_u), peer_v=pack_rows_bf16(peer_v),
    )
    bp, tp, _ = x_prompt.shape
    bs, ts, _ = x_sample.shape
    assert sum(PROMPT_GROUPS) == bp
    groups, start = [], 0
    for gi, gsz in enumerate(PROMPT_GROUPS):
        zeros = lambda s: jnp.zeros((gsz,) + s.shape[2:], F32)
        after = groups[gi - GROUP_LAG][5] if gi >= GROUP_LAG else None
        groups.append(_trunk(x_prompt[start:start + gsz], zeros(state_hgrn), zeros(state_gla), zeros(state_rwkv),
                             zeros(state_shift), w, tp, tm=256, tmp=256, tb=512, chunk=64, after=after))
        start += gsz
    y_p = jnp.concatenate([g[0] for g in groups], axis=0)
    p_h, p_g, p_r, p_s = (jnp.concatenate([g[j] for g in groups], axis=1) for j in range(1, 5))
    y_s, s_h, s_g, s_r, s_s, _ = _trunk(x_sample, state_hgrn[0], state_gla[0], state_rwkv[0], state_shift[0],
                                        w, ts, tm=32, tmp=128, tb=32, chunk=32)
    return (y_p, y_s, p_h, p_g, p_r, p_s, s_h, s_g, s_r, s_s)
```

```python
import functools

import jax
import jax.numpy as jnp
from jax import lax
from jax.experimental import pallas as pl
from jax.experimental.pallas import tpu as pltpu
from jax.experimental.pallas import tpu_sc as plsc

F32 = jnp.float32
BF16 = jnp.bfloat16
I32 = jnp.int32
HI = lax.Precision.HIGHEST

D_MODEL = 1024
NORM_EPS = 1e-6
LANES = 128
SUBLANES = 8
VMEM_LIMIT = 56 * 1024 * 1024

A_WIDTH = 512
A_HEADS = 4
B_WIDTH = 512
B_HEADS = 4
B_DK = 64
GLA_RANK = 16
GLA_NORMALIZER = 16.0
Z_WIDTH = 3712
C_HEAD = 64
C_HEADS = 16
C_GN_EPS = 64e-5
P_HEADS = 8
P_KEYS = 128
P_TOPK = 16
P_PICKS = P_HEADS * P_TOPK
SC_CORES = 2
SC_SUBCORES = 16
SC_WORKERS = SC_CORES * SC_SUBCORES
SC_LANES = 16
SC_CHUNK = 64
SC_TOK_BLOCK = 16
SC_WORDS = D_MODEL // 2
SC_HID_ROWS = 4
SC_HID_WORDS = 4
PROMPT_GROUPS = (2, 2, 4, 4, 4)
GATED_SEQS_PER_STEP = 2


def _cparams(sem):
    return pltpu.CompilerParams(dimension_semantics=sem, vmem_limit_bytes=VMEM_LIMIT)


def _rms(x, g):
    ms = jnp.mean(x * x, axis=-1, keepdims=True)
    return x * lax.rsqrt(ms + NORM_EPS) * g


def _dot(a, b, precision=None):
    return jnp.dot(a, b, preferred_element_type=F32, precision=precision)


def _dot_nt(a, b, precision=None):
    return lax.dot_general(a, b, (((1,), (1,)), ((), ())), preferred_element_type=F32, precision=precision)


def _dot_tn(a, b, precision=None):
    return lax.dot_general(a, b, (((0,), (0,)), ((), ())), preferred_element_type=F32, precision=precision)


def _tri(n, strict):
    r = lax.broadcasted_iota(I32, (n, n), 0)
    c = lax.broadcasted_iota(I32, (n, n), 1)
    return (c < r) if strict else (c <= r)


def _cumsum_rows(g):
    return _dot(_tri(g.shape[0], False).astype(F32), g, precision=HI)


def _lane_mask(width, lo, hi):
    l = lax.broadcasted_iota(I32, (1, width), 1)
    return (l >= lo) & (l < hi)


def _sigmoid(x):
    return 1.0 / (1.0 + jnp.exp(-x))


def _silu(x):
    return x * _sigmoid(x)


def _norm_proj_kernel(x_ref, g_ref, w_ref, o_ref):
    hn = _rms(x_ref[...], g_ref[...])
    o_ref[...] = _dot(hn.astype(BF16), w_ref[...])


def norm_proj(x, g, w_bf16, tm):
    n, d = x.shape
    f = w_bf16.shape[1]
    return pl.pallas_call(
        _norm_proj_kernel,
        grid=(n // tm,),
        in_specs=[pl.BlockSpec((tm, d), lambda i: (i, 0)),
                  pl.BlockSpec((1, d), lambda i: (0, 0)),
                  pl.BlockSpec((d, f), lambda i: (0, 0))],
        out_specs=pl.BlockSpec((tm, f), lambda i: (i, 0)),
        out_shape=jax.ShapeDtypeStruct((n, f), F32),
        compiler_params=_cparams(("parallel",)),
        name="norm_proj",
    )(x, g.reshape(1, d), w_bf16)


def _out_proj2_kernel(x_ref, a_ref, b_ref, wa_ref, wb_ref, o_ref):
    y = _dot(a_ref[...].astype(BF16), wa_ref[...]) + _dot(b_ref[...].astype(BF16), wb_ref[...])
    o_ref[...] = x_ref[...] + y


def out_proj2(x, a, b, wa, wb, tm):
    n, d = x.shape
    ka, kb = a.shape[1], b.shape[1]
    return pl.pallas_call(
        _out_proj2_kernel,
        grid=(n // tm,),
        in_specs=[pl.BlockSpec((tm, d), lambda i: (i, 0)),
                  pl.BlockSpec((tm, ka), lambda i: (i, 0)),
                  pl.BlockSpec((tm, kb), lambda i: (i, 0)),
                  pl.BlockSpec((ka, d), lambda i: (0, 0)),
                  pl.BlockSpec((kb, d), lambda i: (0, 0))],
        out_specs=pl.BlockSpec((tm, d), lambda i: (i, 0)),
        out_shape=jax.ShapeDtypeStruct((n, d), F32),
        compiler_params=_cparams(("parallel",)),
        name="out_proj2",
    )(x, a, b, wa, wb)


def _out_proj1_kernel(x_ref, a_ref, wa_ref, o_ref):
    o_ref[...] = x_ref[...] + _dot(a_ref[...].astype(BF16), wa_ref[...])


def out_proj1(x, a, wa, tm):
    n, d = x.shape
    ka = a.shape[1]
    return pl.pallas_call(
        _out_proj1_kernel,
        grid=(n // tm,),
        in_specs=[pl.BlockSpec((tm, d), lambda i: (i, 0)),
                  pl.BlockSpec((tm, ka), lambda i: (i, 0)),
                  pl.BlockSpec((ka, d), lambda i: (0, 0))],
        out_specs=pl.BlockSpec((tm, d), lambda i: (i, 0)),
        out_shape=jax.ShapeDtypeStruct((n, d), F32),
        compiler_params=_cparams(("parallel",)),
        name="out_proj1",
    )(x, a, wa)


def _final_norm_kernel(x_ref, g_ref, o_ref):
    o_ref[...] = _rms(x_ref[...], g_ref[...])


def final_norm(x, g, tm):
    n, d = x.shape
    return pl.pallas_call(
        _final_norm_kernel,
        grid=(n // tm,),
        in_specs=[pl.BlockSpec((tm, d), lambda i: (i, 0)), pl.BlockSpec((1, d), lambda i: (0, 0))],
        out_specs=pl.BlockSpec((tm, d), lambda i: (i, 0)),
        out_shape=jax.ShapeDtypeStruct((n, d), F32),
        compiler_params=_cparams(("parallel",)),
        name="final_norm",
    )(x, g.reshape(1, d))


def _intra_chunk(problems):
    c = problems[0][0].shape[0]
    nb = c // SUBLANES
    row = lax.broadcasted_iota(I32, (SUBLANES, 1), 0)
    qbs = [[q[SUBLANES * i:SUBLANES * (i + 1)] for i in range(nb)] for q, _, _, _ in problems]
    bbs = [[b[SUBLANES * i:SUBLANES * (i + 1)] for i in range(nb)] for _, _, b, _ in problems]
    outs = [[[None] * nb for _ in heads] for _, _, _, heads in problems]
    for s in range(c):
        rb0 = s // SUBLANES
        for pi, (_, k, b, heads) in enumerate(problems):
            ks = k[s:s + 1, :]
            bs = b[s:s + 1, :]
            for rb in range(rb0, nb):
                p = qbs[pi][rb] * (ks * jnp.exp(bbs[pi][rb] - bs))
                for hi, (mask, v) in enumerate(heads):
                    pm = p if mask is None else jnp.where(mask, p, 0.0)
                    col = jnp.sum(pm, axis=-1, keepdims=True)
                    if rb == rb0:
                        col = jnp.where(row + SUBLANES * rb >= s, col, 0.0)
                    term = col * v[s:s + 1, :]
                    prev = outs[pi][hi][rb]
                    outs[pi][hi][rb] = term if prev is None else prev + term
    return [[jnp.concatenate(o, axis=0) for o in po] for po in outs]


def _gated_chunk(problems, states):
    intra = _intra_chunk(problems)
    qes = [q * jnp.exp(b) for q, _, b, _ in problems]
    b_lasts = [b[-1:, :] for _, _, b, _ in problems]
    khs = [k * jnp.exp(bl - b) for (_, k, b, _), bl in zip(problems, b_lasts)]
    outs = []
    for (_, _, _, heads), qe, st, po in zip(problems, qes, states, intra):
        o = []
        for (mask, v), oi in zip(heads, po):
            qm = qe if mask is None else jnp.where(mask, qe, 0.0)
            o.append(oi + _dot_nt(qm, st, precision=HI))
        outs.append(o)
    new_states = []
    for (_, _, _, heads), kh, st, bl in zip(problems, khs, states, b_lasts):
        upd = _dot_tn(heads[0][1], kh, precision=HI)
        if len(heads) == 2:
            upd = jnp.where(heads[0][0], upd, _dot_tn(heads[1][1], kh, precision=HI))
        new_states.append(st * jnp.exp(bl) + upd)
    return outs, new_states


def _head_rms(o, g):
    ms = jnp.mean(o * o, axis=-1, keepdims=True)
    return o * lax.rsqrt(ms + NORM_EPS) * g


def _hgrn_kernel(zq_ref, zf_ref, zi_ref, zg_ref, lb_ref, ng_ref, s0_ref, o_ref, s_ref, st_scr, *, chunk):
    t = pl.program_id(2)

    @pl.when(t == 0)
    def _():
        st_scr[...] = s0_ref[:, 0]

    lb = lb_ref[0]
    nseq = zq_ref.shape[0]
    nchunks = zq_ref.shape[1] // chunk

    def body(ci, carry):
        sl = pl.ds(pl.multiple_of(ci * chunk, chunk), chunk)
        problems = []
        for i in range(nseq):
            f = lb + (1.0 - lb) * _sigmoid(zf_ref[i, sl, :])
            problems.append((_silu(zq_ref[i, sl, :]), 1.0 - f, _cumsum_rows(jnp.log(f)), [(None, zi_ref[i, sl, :])]))
        outs, states = _gated_chunk(problems, [st_scr[i] for i in range(nseq)])
        for i in range(nseq):
            st_scr[i] = states[i]
            o_ref[i, sl, :] = _head_rms(outs[i][0], ng_ref[...]) * _silu(zg_ref[i, sl, :])
        return carry

    lax.fori_loop(0, nchunks, body, 0)

    @pl.when(t == pl.num_programs(2) - 1)
    def _():
        s_ref[:, 0] = st_scr[...]


def hgrn_recurrence(z, lb, norm_g, s0_t, tb, chunk, nb):
    bsz, t, _ = z.shape
    zspec = lambda off: pl.BlockSpec((nb, tb, LANES), lambda b, h, i: (b, i, h + off))
    return pl.pallas_call(
        functools.partial(_hgrn_kernel, chunk=chunk),
        grid=(bsz // nb, A_HEADS, t // tb),
        in_specs=[zspec(0), zspec(4), zspec(8), zspec(12),
                  pl.BlockSpec((1, 1, LANES), lambda b, h, i: (h, 0, 0)),
                  pl.BlockSpec((1, LANES), lambda b, h, i: (0, 0)),
                  pl.BlockSpec((nb, 1, LANES, LANES), lambda b, h, i: (b, h, 0, 0))],
        out_specs=[pl.BlockSpec((nb, tb, LANES), lambda b, h, i: (b, i, h)),
                   pl.BlockSpec((nb, 1, LANES, LANES), lambda b, h, i: (b, h, 0, 0))],
        out_shape=[jax.ShapeDtypeStruct((bsz, t, A_WIDTH), F32),
                   jax.ShapeDtypeStruct((bsz, A_HEADS, LANES, LANES), F32)],
        scratch_shapes=[pltpu.VMEM((nb, LANES, LANES), F32)],
        compiler_params=_cparams(("parallel", "parallel", "arbitrary")),
        name="hgrn_recurrence",
    )(z, z, z, z, lb.reshape(A_HEADS, 1, LANES), norm_g.reshape(1, LANES), s0_t)


def _gla_kernel(zq_ref, zk_ref, zv_ref, zg_ref, zlr_ref, w2_ref, gb_ref, ng_ref, s0_ref, o_ref, s_ref, st_scr,
                *, chunk):
    t = pl.program_id(2)

    @pl.when(t == 0)
    def _():
        st_scr[...] = s0_ref[:, 0]

    nseq = zq_ref.shape[0]
    nchunks = zq_ref.shape[1] // chunk
    m0 = _lane_mask(LANES, 0, B_DK)
    m1 = _lane_mask(LANES, B_DK, LANES)

    def body(ci, carry):
        sl = pl.ds(pl.multiple_of(ci * chunk, chunk), chunk)
        problems = []
        for i in range(nseq):
            pre = _dot(zlr_ref[i, sl, :], w2_ref[...], precision=HI) + gb_ref[...]
            log_g = (jnp.minimum(pre, 0.0) - jnp.log(1.0 + jnp.exp(-jnp.abs(pre)))) * (1.0 / GLA_NORMALIZER)
            v = zv_ref[i, sl, :]
            problems.append((zq_ref[i, sl, :] * (B_DK ** -0.5), zk_ref[i, sl, :], _cumsum_rows(log_g),
                             [(m0, v[:, :LANES]), (m1, v[:, LANES:])]))
        outs, states = _gated_chunk(problems, [st_scr[i] for i in range(nseq)])
        for i in range(nseq):
            st_scr[i] = states[i]
            gate = _silu(zg_ref[i, sl, :])
            o_ref[i, sl, 0:LANES] = _head_rms(outs[i][0], ng_ref[...]) * gate[:, :LANES]
            o_ref[i, sl, LANES:2 * LANES] = _head_rms(outs[i][1], ng_ref[...]) * gate[:, LANES:]
        return carry

    lax.fori_loop(0, nchunks, body, 0)

    @pl.when(t == pl.num_programs(2) - 1)
    def _():
        s_ref[:, 0] = st_scr[...]


def gla_recurrence(z, w2pad, gate_b, norm_g, s0_t, tb, chunk, nb):
    bsz, t, _ = z.shape
    npairs = B_HEADS // 2
    return pl.pallas_call(
        functools.partial(_gla_kernel, chunk=chunk),
        grid=(bsz // nb, npairs, t // tb),
        in_specs=[pl.BlockSpec((nb, tb, LANES), lambda b, p, i: (b, i, 16 + p)),
                  pl.BlockSpec((nb, tb, LANES), lambda b, p, i: (b, i, 18 + p)),
                  pl.BlockSpec((nb, tb, 2 * LANES), lambda b, p, i: (b, i, 10 + p)),
                  pl.BlockSpec((nb, tb, 2 * LANES), lambda b, p, i: (b, i, 12 + p)),
                  pl.BlockSpec((nb, tb, LANES), lambda b, p, i: (b, i, 28)),
                  pl.BlockSpec((LANES, LANES), lambda b, p, i: (0, p)),
                  pl.BlockSpec((1, LANES), lambda b, p, i: (0, p)),
                  pl.BlockSpec((1, LANES), lambda b, p, i: (0, 0)),
                  pl.BlockSpec((nb, 1, LANES, LANES), lambda b, p, i: (b, p, 0, 0))],
        out_specs=[pl.BlockSpec((nb, tb, 2 * LANES), lambda b, p, i: (b, i, p)),
                   pl.BlockSpec((nb, 1, LANES, LANES), lambda b, p, i: (b, p, 0, 0))],
        out_shape=[jax.ShapeDtypeStruct((bsz, t, B_WIDTH), F32),
                   jax.ShapeDtypeStruct((bsz, npairs, LANES, LANES), F32)],
        scratch_shapes=[pltpu.VMEM((nb, LANES, LANES), F32)],
        compiler_params=_cparams(("parallel", "parallel", "arbitrary")),
        name="gla_recurrence",
    )(z, z, z, z, z, w2pad, gate_b.reshape(1, 2 * LANES), norm_g.reshape(1, LANES), s0_t)


def _rwkv_proj_kernel(x_ref, xp_ref, xl_ref, g1_ref, mu_ref, wr_ref, wk_ref, wv_ref, ww1_ref, ww2_ref, w0_ref,
                      aw1_ref, aw2_ref, a0_ref, gw1_ref, gw2_ref, kk_ref, ka_ref,
                      r_out, k_out, v_out, lw_out, a_out, kk_out, g_out, hl_out, *, tiles_per_seq):
    i = pl.program_id(0)
    g1 = g1_ref[...]
    hn = _rms(x_ref[...], g1)
    tm = hn.shape[0]
    prev = _rms(xp_ref[...], g1)[SUBLANES - 1:SUBLANES, :]
    prev = jnp.where(i % tiles_per_seq == 0, xl_ref[0], prev)
    row = lax.broadcasted_iota(I32, (tm, 1), 0)
    xprev = jnp.where(row == 0, prev, pltpu.roll(hn, 1, axis=0))
    dx = xprev - hn

    def mix(j):
        return hn + dx * mu_ref[j:j + 1, :]

    r = _dot(mix(0).astype(BF16), wr_ref[...])
    k = _dot(mix(1).astype(BF16), wk_ref[...])
    v = _dot(mix(2).astype(BF16), wv_ref[...])
    wl = _dot(jnp.tanh(_dot(mix(3), ww1_ref[...], precision=HI)), ww2_ref[...], precision=HI)
    z = w0_ref[...] + wl
    wpre = -(jnp.maximum(-z, 0.0) + jnp.log(1.0 + jnp.exp(-jnp.abs(z)))) - 0.5
    al = _dot(_dot(mix(4), aw1_ref[...], precision=HI), aw2_ref[...], precision=HI)
    a = _sigmoid(a0_ref[...] + al)
    gg = _dot(_sigmoid(_dot(mix(5).astype(BF16), gw1_ref[...])).astype(BF16), gw2_ref[...])
    r_out[...] = r
    k_out[...] = k * (1.0 + (a - 1.0) * ka_ref[...])
    v_out[...] = v
    lw_out[...] = -jnp.exp(wpre)
    a_out[...] = a
    kk_out[...] = k * kk_ref[...]
    g_out[...] = gg
    hl_out[0] = hn[tm - 1:tm, :]


def rwkv_proj(x, x_last, seq_len, g1, mu, wr, wk, wv, ww1, ww2, w0, aw1, aw2, a0, gw1, gw2, k_k, k_a, tm):
    n, d = x.shape
    tiles_per_seq = seq_len // tm
    row = lambda a: a.reshape(1, d)
    full = lambda a: pl.BlockSpec(a.shape, lambda i: (0,) * a.ndim)
    tile = pl.BlockSpec((tm, d), lambda i: (i, 0))
    blocks8 = tm // SUBLANES
    args = (x, x, x_last.reshape(-1, 1, d), row(g1), mu, wr, wk, wv, ww1, ww2, row(w0), aw1, aw2, row(a0),
            gw1, gw2, row(k_k), row(k_a))
    in_specs = [tile,
                pl.BlockSpec((SUBLANES, d), lambda i: (jnp.maximum(i * blocks8 - 1, 0), 0)),
                pl.BlockSpec((1, 1, d), lambda i: (i // tiles_per_seq, 0, 0))]
    in_specs += [full(a) for a in args[3:]]
    outs = pl.pallas_call(
        functools.partial(_rwkv_proj_kernel, tiles_per_seq=tiles_per_seq),
        grid=(n // tm,),
        in_specs=in_specs,
        out_specs=[tile] * 7 + [pl.BlockSpec((1, 1, d), lambda i: (i, 0, 0))],
        out_shape=[jax.ShapeDtypeStruct((n, d), F32)] * 7 + [jax.ShapeDtypeStruct((n // tm, 1, d), F32)],
        compiler_params=_cparams(("parallel",)),
        name="rwkv_proj",
    )(*args)
    return outs


_NN = ((1,), (0,))
_NT = ((1,), (1,))
_TN = ((0,), (0,))
RWKV_AB_PASSES = 1
RWKV_INV_PASSES = 1
RWKV_APPLY_PASSES = 1
RWKV_STATE_PASSES = 3
RWKV_SEQS_PER_STEP = 4


def _split_bf16(a):
    hi = a.astype(BF16)
    return hi, (a - hi.astype(F32)).astype(BF16)


def _mm(a, b, dims, passes):
    if passes == 6:
        return lax.dot_general(a, b, (dims, ((), ())), preferred_element_type=F32, precision=HI)
    dg = lambda x, y: lax.dot_general(x, y, (dims, ((), ())), preferred_element_type=F32)
    ah, al = _split_bf16(a)
    bh, bl = _split_bf16(b)
    if passes == 1:
        return dg(ah, bh)
    return dg(ah, bh) + (dg(al, bh) + dg(ah, bl))


def _cumsum_rows3(g):
    tri = _tri(g.shape[0], False).astype(BF16)
    h1 = g.astype(BF16)
    r1 = g - h1.astype(F32)
    h2 = r1.astype(BF16)
    h3 = (r1 - h2.astype(F32)).astype(BF16)
    return _dot(tri, h1) + (_dot(tri, h2) + _dot(tri, h3))


def _pair_sum(x, m0):
    s0 = jnp.sum(jnp.where(m0, x, 0.0), axis=-1, keepdims=True)
    s1 = jnp.sum(jnp.where(m0, 0.0, x), axis=-1, keepdims=True)
    return jnp.where(m0, s0, s1)


def _rwkv_kernel(r_ref, k_ref, v_ref, lw_ref, a_ref, kk_ref, g_ref, rk_ref, lng_ref, lnb_ref, s0_ref,
                 o_ref, s_ref, mt_scr, *, chunk):
    t = pl.program_id(2)

    @pl.when(t == 0)
    def _():
        mt_scr[...] = s0_ref[:, 0]

    nrows = r_ref.shape[0]
    nchunks = r_ref.shape[1] // chunk
    c2 = 2 * chunk
    m0 = _lane_mask(LANES, 0, C_HEAD)
    rowi = lax.broadcasted_iota(I32, (LANES, LANES), 0)
    coli = lax.broadcasted_iota(I32, (LANES, LANES), 1)
    blockdiag = (rowi < C_HEAD) == (coli < C_HEAD)
    ti = lax.broadcasted_iota(I32, (c2, c2), 0)
    si = lax.broadcasted_iota(I32, (c2, c2), 1)
    same_head = (ti < chunk) == (si < chunk)
    tm_ = jnp.where(ti < chunk, ti, ti - chunk)
    sm_ = jnp.where(si < chunk, si, si - chunk)
    strict = same_head & (sm_ < tm_)
    incl = same_head & (sm_ <= tm_)

    def stack_heads(x):
        return jnp.concatenate([jnp.where(m0, x, 0.0), jnp.where(m0, 0.0, x)], axis=0)

    def twice(x):
        return jnp.concatenate([x, x], axis=0)

    def unstack(x2):
        return jnp.where(m0, x2[:chunk], x2[chunk:])

    eye = (ti == si).astype(F32)
    seqs = range(nrows)

    def body(ci, carry):
        sl = pl.ds(pl.multiple_of(ci * chunk, chunk), chunk)
        r = [r_ref[i, sl, :] for i in seqs]
        k = [k_ref[i, sl, :] for i in seqs]
        v = [v_ref[i, sl, :] for i in seqs]
        lw = [lw_ref[i, sl, :] for i in seqs]
        kkr = [kk_ref[i, sl, :] for i in seqs]
        kk = [x * lax.rsqrt(_pair_sum(x * x, m0) + 1e-12) for x in kkr]
        al = [a_ref[i, sl, :] * kk[i] for i in seqs]
        gam = [_cumsum_rows3(x) for x in lw]
        e_neg = [jnp.exp(-x) for x in gam]
        xr = [jnp.concatenate([stack_heads(kk[i] * jnp.exp(gam[i] - lw[i])),
                               stack_heads(r[i] * jnp.exp(gam[i]))], axis=0) for i in seqs]
        alk = [jnp.concatenate([twice(al[i] * e_neg[i]), twice(k[i] * e_neg[i])], axis=0) for i in seqs]
        mt = [mt_scr[i] for i in seqs]
        ab = [_mm(xr[i], alk[i], _NT, RWKV_AB_PASSES) for i in seqs]
        xm = [_mm(xr[i], mt[i], _NT, RWKV_STATE_PASSES) for i in seqs]
        a_al = [jnp.where(strict, x[:c2, :c2], 0.0) for x in ab]
        a_k = [jnp.where(strict, x[:c2, c2:], 0.0) for x in ab]
        b_alk = [jnp.concatenate([jnp.where(incl, x[c2:, c2:], 0.0), jnp.where(incl, -x[c2:, :c2], 0.0)], axis=1)
                 for x in ab]
        v2 = [twice(x) for x in v]
        rhs = [xm[i][:c2] + _mm(a_k[i], v2[i], _NN, RWKV_APPLY_PASSES) for i in seqs]
        p = [-x for x in a_al]
        tinv = [eye + x for x in p]
        span = 2
        while span < chunk:
            p = [_mm(x, x, _NN, RWKV_INV_PASSES) for x in p]
            tinv = [tinv[i] + _mm(tinv[i], p[i], _NN, RWKV_INV_PASSES) for i in seqs]
            span *= 2
        u = [unstack(_mm(tinv[i], rhs[i], _NN, RWKV_APPLY_PASSES)) for i in seqs]
        o = [unstack(xm[i][c2:] + _mm(b_alk[i], jnp.concatenate([v2[i], twice(u[i])], axis=0), _NN,
                                      RWKV_APPLY_PASSES)) for i in seqs]
        g_last = [x[-1:, :] for x in gam]
        e_end = [jnp.exp(g_last[i] - gam[i]) for i in seqs]
        upd = [_mm(jnp.concatenate([v[i], u[i]], axis=0),
                   jnp.concatenate([k[i] * e_end[i], -(al[i] * e_end[i])], axis=0), _TN, RWKV_STATE_PASSES)
               for i in seqs]
        for i in seqs:
            mt_scr[i] = mt[i] * jnp.exp(g_last[i]) + jnp.where(blockdiag, upd[i], 0.0)
        for i in seqs:
            mean = _pair_sum(o[i], m0) * (1.0 / C_HEAD)
            cen = o[i] - mean
            var = _pair_sum(cen * cen, m0) * (1.0 / C_HEAD)
            on = cen * lax.rsqrt(var + C_GN_EPS) * lng_ref[...] + lnb_ref[...]
            bonus = _pair_sum(r[i] * k[i] * rk_ref[...], m0) * v[i]
            o_ref[i, sl, :] = (on + bonus) * g_ref[i, sl, :]
        return carry

    lax.fori_loop(0, nchunks, body, 0)

    @pl.when(t == pl.num_programs(2) - 1)
    def _():
        s_ref[:, 0] = mt_scr[...]


def rwkv_recurrence(r, k, v, lw, a, kk, g, r_k, ln_g, ln_b, s0_bd, tb, chunk, nb):
    bsz, t, d = r.shape
    npairs = C_HEADS // 2
    seq = pl.BlockSpec((nb, tb, LANES), lambda b, p, i: (b, i, p))
    vec = pl.BlockSpec((1, LANES), lambda b, p, i: (0, p))
    st = pl.BlockSpec((nb, 1, LANES, LANES), lambda b, p, i: (b, p, 0, 0))
    return pl.pallas_call(
        functools.partial(_rwkv_kernel, chunk=chunk),
        grid=(bsz // nb, npairs, t // tb),
        in_specs=[seq] * 7 + [vec, vec, vec, st],
        out_specs=[seq, st],
        out_shape=[jax.ShapeDtypeStruct((bsz, t, d), F32),
                   jax.ShapeDtypeStruct((bsz, npairs, LANES, LANES), F32)],
        scratch_shapes=[pltpu.VMEM((nb, LANES, LANES), F32)],
        compiler_params=_cparams(("parallel", "parallel", "arbitrary")),
        name="rwkv_recurrence",
    )(r, k, v, lw, a, kk, g, r_k.reshape(1, d), ln_g.reshape(1, d), ln_b.reshape(1, d), s0_bd)


NEG_INF = float("-inf")


def _top16_rows(s):
    n = s.shape[0]
    key = lax.broadcasted_iota(I32, s.shape, 0)
    vals, idxs = [], []
    for _ in range(P_TOPK):
        m = jnp.max(s, axis=0, keepdims=True)
        am = jnp.min(jnp.where(s == m, key, n), axis=0, keepdims=True)
        vals.append(m)
        idxs.append(am)
        s = jnp.where(key == am, NEG_INF, s)
    return vals, idxs


def _top16_pairs(v0, i0, v1, i1):
    a0 = jnp.concatenate(v0[0:8], axis=0)
    a1 = jnp.concatenate(v0[8:16], axis=0)
    b0 = jnp.concatenate(v1[0:8], axis=0)
    b1 = jnp.concatenate(v1[8:16], axis=0)
    ia0 = jnp.concatenate(i0[0:8], axis=0) * P_KEYS
    ia1 = jnp.concatenate(i0[8:16], axis=0) * P_KEYS
    ib0 = jnp.concatenate(i1[0:8], axis=0)
    ib1 = jnp.concatenate(i1[8:16], axis=0)
    row = lax.broadcasted_iota(I32, (SUBLANES, 1), 0)
    slabs = []

    def add(val, eid, keep):
        slabs.append((val if keep is None else jnp.where(keep, val, NEG_INF), eid))

    add(v0[0] + b0, ia0[0:1] + ib0, None)
    add(v0[0] + b1, ia0[0:1] + ib1, None)
    add(v0[1] + b0, ia0[1:2] + ib0, None)
    add(v0[2] + b0, ia0[2:3] + ib0, row < 5)
    add(v0[3] + b0, ia0[3:4] + ib0, row < 4)
    add(a0 + v1[0], ia0 + ib0[0:1], row >= 4)
    add(a1 + v1[0], ia1 + ib0[0:1], None)
    add(a0 + v1[1], ia0 + ib0[1:2], row >= 4)
    add(a0 + v1[2], ia0 + ib0[2:3], row == 4)

    big = P_KEYS * P_KEYS
    out_v, out_e = [], []
    for _ in range(P_TOPK):
        m = slabs[0][0]
        for val, _e in slabs[1:]:
            m = jnp.maximum(m, val)
        m = jnp.max(m, axis=0, keepdims=True)
        e = None
        for val, eid in slabs:
            c = jnp.where(val == m, eid, big)
            e = c if e is None else jnp.minimum(e, c)
        e = jnp.min(e, axis=0, keepdims=True)
        out_v.append(m)
        out_e.append(e)
        slabs = [(jnp.where(eid == e, NEG_INF, val), eid) for val, eid in slabs]
    return out_v, out_e


def _peer_select_kernel(x_ref, g_ref, wq_ref, keys_ref, xn_out, eid_out, gate_out):
    hn = _rms(x_ref[...], g_ref[...])
    xn_out[...] = hn
    q = _dot(hn.astype(BF16), wq_ref[...])
    tm = q.shape[0]
    for lt in range(tm // LANES):
        rows = slice(lt * LANES, (lt + 1) * LANES)
        e_rows, g_rows = [], []
        for h in range(P_HEADS):
            tops = []
            for p in range(2):
                hp = 2 * h + p
                s = _dot_nt(keys_ref[hp], q[rows, hp * LANES:(hp + 1) * LANES], precision=HI)
                tops.append(_top16_rows(s))
            cs, ce = _top16_pairs(tops[0][0], tops[0][1], tops[1][0], tops[1][1])
            ex = [jnp.exp(c - cs[0]) for c in cs]
            tot = ex[0]
            for e in ex[1:]:
                tot = tot + e
            inv = 1.0 / tot
            e_rows += ce
            g_rows += [e * inv for e in ex]
        eid_out[rows, :] = jnp.concatenate(e_rows, axis=0).T
        gate_out[rows, :] = jnp.concatenate(g_rows, axis=0).T


def peer_select(x, g, wq_bf16, keys, tm):
    n, d = x.shape
    return pl.pallas_call(
        _peer_select_kernel,
        grid=(n // tm,),
        in_specs=[pl.BlockSpec((tm, d), lambda i: (i, 0)),
                  pl.BlockSpec((1, d), lambda i: (0, 0)),
                  pl.BlockSpec(wq_bf16.shape, lambda i: (0, 0)),
                  pl.BlockSpec(keys.shape, lambda i: (0, 0, 0))],
        out_specs=[pl.BlockSpec((tm, d), lambda i: (i, 0)),
                   pl.BlockSpec((tm, P_PICKS), lambda i: (i, 0)),
                   pl.BlockSpec((tm, P_PICKS), lambda i: (i, 0))],
        out_shape=[jax.ShapeDtypeStruct((n, d), F32),
                   jax.ShapeDtypeStruct((n, P_PICKS), I32),
                   jax.ShapeDtypeStruct((n, P_PICKS), F32)],
        compiler_params=_cparams(("parallel",)),
        name="peer_select",
    )(x, g.reshape(1, d), wq_bf16, keys)


def _peer_act_kernel(h_ref, g_ref, o_ref):
    h = h_ref[...]
    o_ref[...] = 0.5 * h * (1.0 + lax.erf(h * (2.0 ** -0.5))) * g_ref[...]


def peer_act(hid, gate, tm):
    n, p = hid.shape
    spec = pl.BlockSpec((tm, p), lambda i: (i, 0))
    return pl.pallas_call(
        _peer_act_kernel, grid=(n // tm,), in_specs=[spec, spec], out_specs=spec,
        out_shape=jax.ShapeDtypeStruct((n, p), F32),
        compiler_params=_cparams(("parallel",)), name="peer_act",
    )(hid, gate)


_CHUNKS_PER_TOK = P_PICKS // SC_CHUNK
_WORD_VREGS = SC_WORDS // SC_LANES
U32 = jnp.uint32


def pack_rows_bf16(t):
    lo = lax.bitcast_convert_type(t[..., :SC_WORDS].astype(BF16), jnp.uint16).astype(U32)
    a = lax.bitcast_convert_type(t[..., SC_WORDS:], U32)
    sign = a & U32(0x80000000)
    mag = a & U32(0x7FFFFFFF)
    steps = (jnp.maximum(mag + U32(0x8000), lo) - lo) >> 16
    near = (steps << 16) + lo
    near = jnp.where(near >= U32(0x7F800000), near - U32(0x10000), near)
    return lax.bitcast_convert_type(sign | near, I32)


def _low_f32(w):
    return lax.bitcast_convert_type(w << 16, F32)


def _high_f32(w):
    return lax.bitcast_convert_type(w, F32)


def _sc_worker():
    return lax.axis_index("s") * SC_CORES + lax.axis_index("c")


def _sc_hid_body(tab_hbm, idx_hbm, x_hbm, hid_hbm, idx_v, x_v, hid_v, rows_a, rows_b, sem_a, sem_b, *, tok_per_w):
    wid = _sc_worker()
    nblk = tok_per_w // SC_TOK_BLOCK
    nch = SC_TOK_BLOCK * _CHUNKS_PER_TOK
    lane = lax.iota(I32, SC_LANES)
    quads = SC_LANES // SC_HID_ROWS

    def dots(rows, x_row, res_ref, col0):
        for grp in range(SC_CHUNK // SC_LANES):
            def some_rows(q, res):
                r0 = grp * SC_LANES + q * SC_HID_ROWS

                def some_words(jj, accs):
                    accs = list(accs)
                    for jw in range(SC_HID_WORDS):
                        off = (jj * SC_HID_WORDS + jw) * SC_LANES
                        xl = x_v[x_row, pl.ds(off, SC_LANES)]
                        xh = x_v[x_row, pl.ds(SC_WORDS + off, SC_LANES)]
                        for i in range(SC_HID_ROWS):
                            w = rows[r0 + i, pl.ds(off, SC_LANES)]
                            k = 2 * i + jw % 2
                            accs[k] = accs[k] + (_low_f32(w) * xl + _high_f32(w) * xh)
                    return tuple(accs)

                zero = jnp.zeros((SC_LANES,), F32)
                accs = lax.fori_loop(0, _WORD_VREGS // SC_HID_WORDS, some_words, (zero,) * (2 * SC_HID_ROWS))
                for i in range(SC_HID_ROWS):
                    res = jnp.where(lane == q * SC_HID_ROWS + i, jnp.sum(accs[2 * i] + accs[2 * i + 1]), res)
                return res
            res = lax.fori_loop(0, quads, some_rows, jnp.zeros((SC_LANES,), F32))
            res_ref[x_row, pl.ds(col0 + grp * SC_LANES, SC_LANES)] = res

    def block(bi, carry):
        tok0 = wid * tok_per_w + bi * SC_TOK_BLOCK
        pltpu.sync_copy(idx_hbm.at[pl.ds(tok0 * _CHUNKS_PER_TOK, nch)], idx_v)
        pltpu.sync_copy(x_hbm.at[pl.ds(tok0, SC_TOK_BLOCK)], x_v)
        pltpu.async_copy(tab_hbm.at[idx_v.at[0]], rows_a, sem_a)

        def pair(j, c):
            ca = 2 * j
            pltpu.async_copy(tab_hbm.at[idx_v.at[ca + 1]], rows_b, sem_b)
            pltpu.make_async_copy(tab_hbm.at[idx_v.at[0]], rows_a, sem_a).wait()
            dots(rows_a, ca // _CHUNKS_PER_TOK, hid_v, (ca % _CHUNKS_PER_TOK) * SC_CHUNK)

            @pl.when(j < nch // 2 - 1)
            def _():
                pltpu.async_copy(tab_hbm.at[idx_v.at[ca + 2]], rows_a, sem_a)
            pltpu.make_async_copy(tab_hbm.at[idx_v.at[0]], rows_b, sem_b).wait()
            cb = ca + 1
            dots(rows_b, cb // _CHUNKS_PER_TOK, hid_v, (cb % _CHUNKS_PER_TOK) * SC_CHUNK)
            return c
        lax.fori_loop(0, nch // 2, pair, 0)
        pltpu.sync_copy(hid_v, hid_hbm.at[pl.ds(tok0, SC_TOK_BLOCK)])
        return carry

    lax.fori_loop(0, nblk, block, 0)


def sc_expert_hidden(table_packed, eidx, xn):
    n = xn.shape[0]
    tok_per_w = n // SC_WORKERS
    mesh = plsc.VectorSubcoreMesh(core_axis_name="c", subcore_axis_name="s")
    nch = SC_TOK_BLOCK * _CHUNKS_PER_TOK
    k = pl.kernel(
        functools.partial(_sc_hid_body, tok_per_w=tok_per_w), mesh=mesh,
        out_type=jax.ShapeDtypeStruct((n, P_PICKS), F32),
        scratch_types=[pltpu.VMEM((nch, SC_CHUNK), I32),
                       pltpu.VMEM((SC_TOK_BLOCK, D_MODEL), F32),
                       pltpu.VMEM((SC_TOK_BLOCK, P_PICKS), F32),
                       pltpu.VMEM((SC_CHUNK, SC_WORDS), I32),
                       pltpu.VMEM((SC_CHUNK, SC_WORDS), I32),
                       pltpu.SemaphoreType.DMA, pltpu.SemaphoreType.DMA],
        compiler_params=pltpu.CompilerParams(needs_layout_passes=False),
        name="sc_expert_hidden",
    )
    return k(table_packed, eidx.reshape(n * _CHUNKS_PER_TOK, SC_CHUNK), xn)


def _sc_out_body(tab_hbm, idx_hbm, act_hbm, x_hbm, y_hbm, idx_v, act_v, y_v, rows_a, rows_b, sem_a, sem_b,
                 *, tok_per_w):
    wid = _sc_worker()
    nblk = tok_per_w // SC_TOK_BLOCK
    nch = SC_TOK_BLOCK * _CHUNKS_PER_TOK
    half_w = _WORD_VREGS // 2

    def accum(rows, tok, col0):
        for hv in range(2):
            base = hv * half_w * SC_LANES
            acc0 = (tuple(y_v[tok, pl.ds(base + j * SC_LANES, SC_LANES)] for j in range(half_w))
                    + tuple(y_v[tok, pl.ds(SC_WORDS + base + j * SC_LANES, SC_LANES)] for j in range(half_w)))

            def one(r, acc):
                aidx = jnp.full((SC_LANES,), col0, I32) + r
                wgt = plsc.load_gather(act_v, [jnp.full((SC_LANES,), tok, I32), aidx])
                lo, hi = [], []
                for j in range(half_w):
                    w = rows[r, pl.ds(base + j * SC_LANES, SC_LANES)]
                    lo.append(acc[j] + wgt * _low_f32(w))
                    hi.append(acc[half_w + j] + wgt * _high_f32(w))
                return tuple(lo + hi)
            acc = lax.fori_loop(0, SC_CHUNK, one, acc0)
            for j in range(half_w):
                y_v[tok, pl.ds(base + j * SC_LANES, SC_LANES)] = acc[j]
                y_v[tok, pl.ds(SC_WORDS + base + j * SC_LANES, SC_LANES)] = acc[half_w + j]

    def block(bi, carry):
        tok0 = wid * tok_per_w + bi * SC_TOK_BLOCK
        pltpu.sync_copy(idx_hbm.at[pl.ds(tok0 * _CHUNKS_PER_TOK, nch)], idx_v)
        pltpu.sync_copy(act_hbm.at[pl.ds(tok0, SC_TOK_BLOCK)], act_v)
        pltpu.sync_copy(x_hbm.at[pl.ds(tok0, SC_TOK_BLOCK)], y_v)
        pltpu.async_copy(tab_hbm.at[idx_v.at[0]], rows_a, sem_a)

        def pair(j, c):
            ca = 2 * j
            pltpu.async_copy(tab_hbm.at[idx_v.at[ca + 1]], rows_b, sem_b)
            pltpu.make_async_copy(tab_hbm.at[idx_v.at[0]], rows_a, sem_a).wait()
            accum(rows_a, ca // _CHUNKS_PER_TOK, (ca % _CHUNKS_PER_TOK) * SC_CHUNK)

            @pl.when(j < nch // 2 - 1)
            def _():
                pltpu.async_copy(tab_hbm.at[idx_v.at[ca + 2]], rows_a, sem_a)
            pltpu.make_async_copy(tab_hbm.at[idx_v.at[0]], rows_b, sem_b).wait()
            cb = ca + 1
            accum(rows_b, cb // _CHUNKS_PER_TOK, (cb % _CHUNKS_PER_TOK) * SC_CHUNK)
            return c
        lax.fori_loop(0, nch // 2, pair, 0)
        pltpu.sync_copy(y_v, y_hbm.at[pl.ds(tok0, SC_TOK_BLOCK)])
        return carry

    lax.fori_loop(0, nblk, block, 0)


def sc_expert_output(table_packed, eidx, act, x):
    n = x.shape[0]
    tok_per_w = n // SC_WORKERS
    mesh = plsc.VectorSubcoreMesh(core_axis_name="c", subcore_axis_name="s")
    nch = SC_TOK_BLOCK * _CHUNKS_PER_TOK
    k = pl.kernel(
        functools.partial(_sc_out_body, tok_per_w=tok_per_w), mesh=mesh,
        out_type=jax.ShapeDtypeStruct((n, D_MODEL), F32),
        scratch_types=[pltpu.VMEM((nch, SC_CHUNK), I32),
                       pltpu.VMEM((SC_TOK_BLOCK, P_PICKS), F32),
                       pltpu.VMEM((SC_TOK_BLOCK, D_MODEL), F32),
                       pltpu.VMEM((SC_CHUNK, SC_WORDS), I32),
                       pltpu.VMEM((SC_CHUNK, SC_WORDS), I32),
                       pltpu.SemaphoreType.DMA, pltpu.SemaphoreType.DMA],
        compiler_params=pltpu.CompilerParams(needs_layout_passes=False),
        name="sc_expert_output",
    )
    return k(table_packed, eidx.reshape(n * _CHUNKS_PER_TOK, SC_CHUNK), act, x)


def peer_ffn(x, g, wq_bf16, keys, u_packed, v_packed, tm):
    xn, eidx, gate = peer_select(x, g, wq_bf16, keys, tm)
    hid = sc_expert_hidden(u_packed, eidx, xn)
    act = peer_act(hid, gate, tm)
    return sc_expert_output(v_packed, eidx, act, x)


def _prep_w_in(w_in):
    main = jnp.concatenate([w_in[:, :3072], w_in[:, 3088:3600]], axis=1)
    lr = jnp.pad(w_in[:, 3072:3088], ((0, 0), (0, LANES - GLA_RANK)))
    return jnp.concatenate([main, lr], axis=1).astype(BF16)


def _trunk(x, st_h, st_g, st_r, st_s, w, seq_len, tm, tmp, tb, chunk):
    bsz = x.shape[0]
    n = bsz * seq_len
    x2 = x.reshape(n, D_MODEL)

    z = norm_proj(x2, w["norm1_g"][0], w["w_in"], tm).reshape(bsz, seq_len, Z_WIDTH)
    s0_h = jnp.swapaxes(st_h, -1, -2)
    s0_g = jnp.swapaxes(st_g.reshape(bsz, 2, 2 * B_DK, LANES), -1, -2)
    o_a, sh_t = hgrn_recurrence(z, w["lb0"], w["hgrn_norm_g"], s0_h, tb, chunk, GATED_SEQS_PER_STEP)
    o_b, sg_t = gla_recurrence(z, w["gla_w2"], w["gla_b"], w["gla_norm_g"], s0_g, tb, chunk, GATED_SEQS_PER_STEP)
    new_h = jnp.swapaxes(sh_t, -1, -2)
    new_g = jnp.swapaxes(sg_t, -1, -2).reshape(bsz, B_HEADS, B_DK, LANES)
    x2 = out_proj2(x2, o_a.reshape(n, A_WIDTH), o_b.reshape(n, B_WIDTH), w["w_out_a"], w["w_out_b"], tm)
    x2 = peer_ffn(x2, w["norm2_g"][0], w["peer_wq"][0], w["peer_keys"][0], w["peer_u"][0], w["peer_v"][0], tmp)

    r, k, v, lw, a, kk, g, hl = rwkv_proj(
        x2, st_s, seq_len, w["norm1_g"][1], w["mu"], w["wr"], w["wk"], w["wv"], w["w_w1"], w["w_w2"], w["w0"],
        w["a_w1"], w["a_w2"], w["a0"], w["g_w1"], w["g_w2"], w["k_k"], w["k_a"], tm)
    new_s = hl.reshape(bsz, seq_len // tm, D_MODEL)[:, -1]
    pr = st_r.reshape(bsz, C_HEADS // 2, 2, C_HEAD, C_HEAD)
    zero = jnp.zeros_like(pr[:, :, 0])
    s0_r = jnp.concatenate([jnp.concatenate([pr[:, :, 0], zero], axis=-1),
                            jnp.concatenate([zero, pr[:, :, 1]], axis=-1)], axis=-2)
    sh3 = lambda t: t.reshape(bsz, seq_len, D_MODEL)
    o_c, sr_bd = rwkv_recurrence(sh3(r), sh3(k), sh3(v), sh3(lw), sh3(a), sh3(kk), sh3(g),
                                 w["r_k"], w["ln_g"], w["ln_b"], s0_r, tb, chunk, min(RWKV_SEQS_PER_STEP, bsz))
    new_r = jnp.stack([sr_bd[:, :, :C_HEAD, :C_HEAD], sr_bd[:, :, C_HEAD:, C_HEAD:]], axis=2)
    new_r = new_r.reshape(bsz, C_HEADS, C_HEAD, C_HEAD)
    x2 = out_proj1(x2, o_c.reshape(n, D_MODEL), w["w_out_c"], tm)
    x2 = peer_ffn(x2, w["norm2_g"][1], w["peer_wq"][1], w["peer_keys"][1], w["peer_u"][1], w["peer_v"][1], tmp)

    y = final_norm(x2, w["final_g"], tm).reshape(bsz, seq_len, D_MODEL)
    return y, new_h[None], new_g[None], new_r[None], new_s[None]


def kernel(x_prompt, x_sample, state_hgrn, state_gla, state_rwkv, state_shift, w_in_ab, hgrn_lower_bounds, hgrn_norm_g, gla_gate_w2, gla_gate_b, gla_norm_g, w_out_ab, rwkv_mu, rwkv_w_rkv, rwkv_w_w1, rwkv_w_w2, rwkv_w0, rwkv_a_w1, rwkv_a_w2, rwkv_a0, rwkv_g_w1, rwkv_g_w2, rwkv_k_k, rwkv_k_a, rwkv_r_k, rwkv_ln_g, rwkv_ln_b, w_out_c, norm1_g, norm2_g, final_g, peer_w_q, peer_sub_keys, peer_u, peer_v):
    lbs = jnp.cumsum(jax.nn.softmax(hgrn_lower_bounds.astype(F32), axis=0), axis=0)
    w = dict(
        norm1_g=norm1_g, norm2_g=norm2_g, final_g=final_g,
        w_in=_prep_w_in(w_in_ab[0]), lb0=lbs[0], hgrn_norm_g=hgrn_norm_g[0],
        gla_w2=jnp.pad(gla_gate_w2[0], ((0, LANES - GLA_RANK), (0, 0))), gla_b=gla_gate_b[0],
        gla_norm_g=gla_norm_g[0],
        w_out_a=w_out_ab[0, :A_WIDTH].astype(BF16), w_out_b=w_out_ab[0, A_WIDTH:].astype(BF16),
        mu=rwkv_mu[0], wr=rwkv_w_rkv[0, 0].astype(BF16), wk=rwkv_w_rkv[0, 1].astype(BF16),
        wv=rwkv_w_rkv[0, 2].astype(BF16), w_w1=rwkv_w_w1[0], w_w2=rwkv_w_w2[0], w0=rwkv_w0[0],
        a_w1=rwkv_a_w1[0], a_w2=rwkv_a_w2[0], a0=rwkv_a0[0],
        g_w1=rwkv_g_w1[0].astype(BF16), g_w2=rwkv_g_w2[0].astype(BF16),
        k_k=rwkv_k_k[0], k_a=rwkv_k_a[0], r_k=rwkv_r_k[0], ln_g=rwkv_ln_g[0], ln_b=rwkv_ln_b[0],
        w_out_c=w_out_c[0].astype(BF16),
        peer_wq=peer_w_q.astype(BF16),
        peer_keys=peer_sub_keys.reshape(peer_sub_keys.shape[0], 2 * P_HEADS, P_KEYS, P_KEYS),
        peer_u=pack_rows_bf16(peer_u), peer_v=pack_rows_bf16(peer_v),
    )
    bp, tp, _ = x_prompt.shape
    bs, ts, _ = x_sample.shape
    assert sum(PROMPT_GROUPS) == bp
    groups, start = [], 0
    for gsz in PROMPT_GROUPS:
        zeros = lambda s: jnp.zeros((gsz,) + s.shape[2:], F32)
        groups.append(_trunk(x_prompt[start:start + gsz], zeros(state_hgrn), zeros(state_gla), zeros(state_rwkv),
                             zeros(state_shift), w, tp, tm=256, tmp=256, tb=512, chunk=64))
        start += gsz
    y_p = jnp.concatenate([g[0] for g in groups], axis=0)
    p_h, p_g, p_r, p_s = (jnp.concatenate([g[j] for g in groups], axis=1) for j in range(1, 5))
    y_s, s_h, s_g, s_r, s_s = _trunk(x_sample, state_hgrn[0], state_gla[0], state_rwkv[0], state_shift[0],
                                     w, ts, tm=32, tmp=128, tb=32, chunk=32)
    return (y_p, y_s, p_h, p_g, p_r, p_s, s_h, s_g, s_r, s_s)
```

```python
import functools

import jax
import jax.numpy as jnp
from jax import lax
from jax.experimental import pallas as pl
from jax.experimental.pallas import tpu as pltpu
from jax.experimental.pallas import tpu_sc as plsc

F32 = jnp.float32
BF16 = jnp.bfloat16
I32 = jnp.int32
HI = lax.Precision.HIGHEST

D_MODEL = 1024
NORM_EPS = 1e-6
LANES = 128
SUBLANES = 8
VMEM_LIMIT = 56 * 1024 * 1024

A_WIDTH = 512
A_HEADS = 4
B_WIDTH = 512
B_HEADS = 4
B_DK = 64
GLA_RANK = 16
GLA_NORMALIZER = 16.0
Z_WIDTH = 3712
C_HEAD = 64
C_HEADS = 16
C_GN_EPS = 64e-5
P_HEADS = 8
P_KEYS = 128
P_TOPK = 16
P_PICKS = P_HEADS * P_TOPK
SC_CORES = 2
SC_SUBCORES = 16
SC_WORKERS = SC_CORES * SC_SUBCORES
SC_LANES = 16
SC_CHUNK = 64
SC_TOK_BLOCK = 16
SC_WORDS = D_MODEL // 2
SC_HID_ROWS = 4
SC_HID_WORDS = 4
PROMPT_GROUPS = (2,) * 8
GATED_SEQS_PER_STEP = 2


def _cparams(sem):
    return pltpu.CompilerParams(dimension_semantics=sem, vmem_limit_bytes=VMEM_LIMIT)


def _rms(x, g):
    ms = jnp.mean(x * x, axis=-1, keepdims=True)
    return x * lax.rsqrt(ms + NORM_EPS) * g


def _dot(a, b, precision=None):
    return jnp.dot(a, b, preferred_element_type=F32, precision=precision)


def _dot_nt(a, b, precision=None):
    return lax.dot_general(a, b, (((1,), (1,)), ((), ())), preferred_element_type=F32, precision=precision)


def _dot_tn(a, b, precision=None):
    return lax.dot_general(a, b, (((0,), (0,)), ((), ())), preferred_element_type=F32, precision=precision)


def _tri(n, strict):
    r = lax.broadcasted_iota(I32, (n, n), 0)
    c = lax.broadcasted_iota(I32, (n, n), 1)
    return (c < r) if strict else (c <= r)


def _cumsum_rows(g):
    return _dot(_tri(g.shape[0], False).astype(F32), g, precision=HI)


def _lane_mask(width, lo, hi):
    l = lax.broadcasted_iota(I32, (1, width), 1)
    return (l >= lo) & (l < hi)


def _sigmoid(x):
    return 1.0 / (1.0 + jnp.exp(-x))


def _silu(x):
    return x * _sigmoid(x)


def _norm_proj_kernel(x_ref, g_ref, w_ref, o_ref):
    hn = _rms(x_ref[...], g_ref[...])
    o_ref[...] = _dot(hn.astype(BF16), w_ref[...])


def norm_proj(x, g, w_bf16, tm):
    n, d = x.shape
    f = w_bf16.shape[1]
    return pl.pallas_call(
        _norm_proj_kernel,
        grid=(n // tm,),
        in_specs=[pl.BlockSpec((tm, d), lambda i: (i, 0)),
                  pl.BlockSpec((1, d), lambda i: (0, 0)),
                  pl.BlockSpec((d, f), lambda i: (0, 0))],
        out_specs=pl.BlockSpec((tm, f), lambda i: (i, 0)),
        out_shape=jax.ShapeDtypeStruct((n, f), F32),
        compiler_params=_cparams(("parallel",)),
        name="norm_proj",
    )(x, g.reshape(1, d), w_bf16)


def _out_proj2_kernel(x_ref, a_ref, b_ref, wa_ref, wb_ref, o_ref):
    y = _dot(a_ref[...].astype(BF16), wa_ref[...]) + _dot(b_ref[...].astype(BF16), wb_ref[...])
    o_ref[...] = x_ref[...] + y


def out_proj2(x, a, b, wa, wb, tm):
    n, d = x.shape
    ka, kb = a.shape[1], b.shape[1]
    return pl.pallas_call(
        _out_proj2_kernel,
        grid=(n // tm,),
        in_specs=[pl.BlockSpec((tm, d), lambda i: (i, 0)),
                  pl.BlockSpec((tm, ka), lambda i: (i, 0)),
                  pl.BlockSpec((tm, kb), lambda i: (i, 0)),
                  pl.BlockSpec((ka, d), lambda i: (0, 0)),
                  pl.BlockSpec((kb, d), lambda i: (0, 0))],
        out_specs=pl.BlockSpec((tm, d), lambda i: (i, 0)),
        out_shape=jax.ShapeDtypeStruct((n, d), F32),
        compiler_params=_cparams(("parallel",)),
        name="out_proj2",
    )(x, a, b, wa, wb)


def _out_proj1_kernel(x_ref, a_ref, wa_ref, o_ref):
    o_ref[...] = x_ref[...] + _dot(a_ref[...].astype(BF16), wa_ref[...])


def out_proj1(x, a, wa, tm):
    n, d = x.shape
    ka = a.shape[1]
    return pl.pallas_call(
        _out_proj1_kernel,
        grid=(n // tm,),
        in_specs=[pl.BlockSpec((tm, d), lambda i: (i, 0)),
                  pl.BlockSpec((tm, ka), lambda i: (i, 0)),
                  pl.BlockSpec((ka, d), lambda i: (0, 0))],
        out_specs=pl.BlockSpec((tm, d), lambda i: (i, 0)),
        out_shape=jax.ShapeDtypeStruct((n, d), F32),
        compiler_params=_cparams(("parallel",)),
        name="out_proj1",
    )(x, a, wa)


def _final_norm_kernel(x_ref, g_ref, o_ref):
    o_ref[...] = _rms(x_ref[...], g_ref[...])


def final_norm(x, g, tm):
    n, d = x.shape
    return pl.pallas_call(
        _final_norm_kernel,
        grid=(n // tm,),
        in_specs=[pl.BlockSpec((tm, d), lambda i: (i, 0)), pl.BlockSpec((1, d), lambda i: (0, 0))],
        out_specs=pl.BlockSpec((tm, d), lambda i: (i, 0)),
        out_shape=jax.ShapeDtypeStruct((n, d), F32),
        compiler_params=_cparams(("parallel",)),
        name="final_norm",
    )(x, g.reshape(1, d))


def _intra_chunk(problems):
    c = problems[0][0].shape[0]
    nb = c // SUBLANES
    row = lax.broadcasted_iota(I32, (SUBLANES, 1), 0)
    qbs = [[q[SUBLANES * i:SUBLANES * (i + 1)] for i in range(nb)] for q, _, _, _ in problems]
    bbs = [[b[SUBLANES * i:SUBLANES * (i + 1)] for i in range(nb)] for _, _, b, _ in problems]
    outs = [[[None] * nb for _ in heads] for _, _, _, heads in problems]
    for s in range(c):
        rb0 = s // SUBLANES
        for pi, (_, k, b, heads) in enumerate(problems):
            ks = k[s:s + 1, :]
            bs = b[s:s + 1, :]
            for rb in range(rb0, nb):
                p = qbs[pi][rb] * (ks * jnp.exp(bbs[pi][rb] - bs))
                for hi, (mask, v) in enumerate(heads):
                    pm = p if mask is None else jnp.where(mask, p, 0.0)
                    col = jnp.sum(pm, axis=-1, keepdims=True)
                    if rb == rb0:
                        col = jnp.where(row + SUBLANES * rb >= s, col, 0.0)
                    term = col * v[s:s + 1, :]
                    prev = outs[pi][hi][rb]
                    outs[pi][hi][rb] = term if prev is None else prev + term
    return [[jnp.concatenate(o, axis=0) for o in po] for po in outs]


def _gated_chunk(problems, states):
    intra = _intra_chunk(problems)
    qes = [q * jnp.exp(b) for q, _, b, _ in problems]
    b_lasts = [b[-1:, :] for _, _, b, _ in problems]
    khs = [k * jnp.exp(bl - b) for (_, k, b, _), bl in zip(problems, b_lasts)]
    outs = []
    for (_, _, _, heads), qe, st, po in zip(problems, qes, states, intra):
        o = []
        for (mask, v), oi in zip(heads, po):
            qm = qe if mask is None else jnp.where(mask, qe, 0.0)
            o.append(oi + _dot_nt(qm, st, precision=HI))
        outs.append(o)
    new_states = []
    for (_, _, _, heads), kh, st, bl in zip(problems, khs, states, b_lasts):
        upd = _dot_tn(heads[0][1], kh, precision=HI)
        if len(heads) == 2:
            upd = jnp.where(heads[0][0], upd, _dot_tn(heads[1][1], kh, precision=HI))
        new_states.append(st * jnp.exp(bl) + upd)
    return outs, new_states


def _head_rms(o, g):
    ms = jnp.mean(o * o, axis=-1, keepdims=True)
    return o * lax.rsqrt(ms + NORM_EPS) * g


def _hgrn_kernel(zq_ref, zf_ref, zi_ref, zg_ref, lb_ref, ng_ref, s0_ref, o_ref, s_ref, st_scr, *, chunk):
    t = pl.program_id(2)

    @pl.when(t == 0)
    def _():
        st_scr[...] = s0_ref[:, 0]

    lb = lb_ref[0]
    nseq = zq_ref.shape[0]
    nchunks = zq_ref.shape[1] // chunk

    def body(ci, carry):
        sl = pl.ds(pl.multiple_of(ci * chunk, chunk), chunk)
        problems = []
        for i in range(nseq):
            f = lb + (1.0 - lb) * _sigmoid(zf_ref[i, sl, :])
            problems.append((_silu(zq_ref[i, sl, :]), 1.0 - f, _cumsum_rows(jnp.log(f)), [(None, zi_ref[i, sl, :])]))
        outs, states = _gated_chunk(problems, [st_scr[i] for i in range(nseq)])
        for i in range(nseq):
            st_scr[i] = states[i]
            o_ref[i, sl, :] = _head_rms(outs[i][0], ng_ref[...]) * _silu(zg_ref[i, sl, :])
        return carry

    lax.fori_loop(0, nchunks, body, 0)

    @pl.when(t == pl.num_programs(2) - 1)
    def _():
        s_ref[:, 0] = st_scr[...]


def hgrn_recurrence(z, lb, norm_g, s0_t, tb, chunk, nb):
    bsz, t, _ = z.shape
    zspec = lambda off: pl.BlockSpec((nb, tb, LANES), lambda b, h, i: (b, i, h + off))
    return pl.pallas_call(
        functools.partial(_hgrn_kernel, chunk=chunk),
        grid=(bsz // nb, A_HEADS, t // tb),
        in_specs=[zspec(0), zspec(4), zspec(8), zspec(12),
                  pl.BlockSpec((1, 1, LANES), lambda b, h, i: (h, 0, 0)),
                  pl.BlockSpec((1, LANES), lambda b, h, i: (0, 0)),
                  pl.BlockSpec((nb, 1, LANES, LANES), lambda b, h, i: (b, h, 0, 0))],
        out_specs=[pl.BlockSpec((nb, tb, LANES), lambda b, h, i: (b, i, h)),
                   pl.BlockSpec((nb, 1, LANES, LANES), lambda b, h, i: (b, h, 0, 0))],
        out_shape=[jax.ShapeDtypeStruct((bsz, t, A_WIDTH), F32),
                   jax.ShapeDtypeStruct((bsz, A_HEADS, LANES, LANES), F32)],
        scratch_shapes=[pltpu.VMEM((nb, LANES, LANES), F32)],
        compiler_params=_cparams(("parallel", "parallel", "arbitrary")),
        name="hgrn_recurrence",
    )(z, z, z, z, lb.reshape(A_HEADS, 1, LANES), norm_g.reshape(1, LANES), s0_t)


def _gla_kernel(zq_ref, zk_ref, zv_ref, zg_ref, zlr_ref, w2_ref, gb_ref, ng_ref, s0_ref, o_ref, s_ref, st_scr,
                *, chunk):
    t = pl.program_id(2)

    @pl.when(t == 0)
    def _():
        st_scr[...] = s0_ref[:, 0]

    nseq = zq_ref.shape[0]
    nchunks = zq_ref.shape[1] // chunk
    m0 = _lane_mask(LANES, 0, B_DK)
    m1 = _lane_mask(LANES, B_DK, LANES)

    def body(ci, carry):
        sl = pl.ds(pl.multiple_of(ci * chunk, chunk), chunk)
        problems = []
        for i in range(nseq):
            pre = _dot(zlr_ref[i, sl, :], w2_ref[...], precision=HI) + gb_ref[...]
            log_g = (jnp.minimum(pre, 0.0) - jnp.log(1.0 + jnp.exp(-jnp.abs(pre)))) * (1.0 / GLA_NORMALIZER)
            v = zv_ref[i, sl, :]
            problems.append((zq_ref[i, sl, :] * (B_DK ** -0.5), zk_ref[i, sl, :], _cumsum_rows(log_g),
                             [(m0, v[:, :LANES]), (m1, v[:, LANES:])]))
        outs, states = _gated_chunk(problems, [st_scr[i] for i in range(nseq)])
        for i in range(nseq):
            st_scr[i] = states[i]
            gate = _silu(zg_ref[i, sl, :])
            o_ref[i, sl, 0:LANES] = _head_rms(outs[i][0], ng_ref[...]) * gate[:, :LANES]
            o_ref[i, sl, LANES:2 * LANES] = _head_rms(outs[i][1], ng_ref[...]) * gate[:, LANES:]
        return carry

    lax.fori_loop(0, nchunks, body, 0)

    @pl.when(t == pl.num_programs(2) - 1)
    def _():
        s_ref[:, 0] = st_scr[...]


def gla_recurrence(z, w2pad, gate_b, norm_g, s0_t, tb, chunk, nb):
    bsz, t, _ = z.shape
    npairs = B_HEADS // 2
    return pl.pallas_call(
        functools.partial(_gla_kernel, chunk=chunk),
        grid=(bsz // nb, npairs, t // tb),
        in_specs=[pl.BlockSpec((nb, tb, LANES), lambda b, p, i: (b, i, 16 + p)),
                  pl.BlockSpec((nb, tb, LANES), lambda b, p, i: (b, i, 18 + p)),
                  pl.BlockSpec((nb, tb, 2 * LANES), lambda b, p, i: (b, i, 10 + p)),
                  pl.BlockSpec((nb, tb, 2 * LANES), lambda b, p, i: (b, i, 12 + p)),
                  pl.BlockSpec((nb, tb, LANES), lambda b, p, i: (b, i, 28)),
                  pl.BlockSpec((LANES, LANES), lambda b, p, i: (0, p)),
                  pl.BlockSpec((1, LANES), lambda b, p, i: (0, p)),
                  pl.BlockSpec((1, LANES), lambda b, p, i: (0, 0)),
                  pl.BlockSpec((nb, 1, LANES, LANES), lambda b, p, i: (b, p, 0, 0))],
        out_specs=[pl.BlockSpec((nb, tb, 2 * LANES), lambda b, p, i: (b, i, p)),
                   pl.BlockSpec((nb, 1, LANES, LANES), lambda b, p, i: (b, p, 0, 0))],
        out_shape=[jax.ShapeDtypeStruct((bsz, t, B_WIDTH), F32),
                   jax.ShapeDtypeStruct((bsz, npairs, LANES, LANES), F32)],
        scratch_shapes=[pltpu.VMEM((nb, LANES, LANES), F32)],
        compiler_params=_cparams(("parallel", "parallel", "arbitrary")),
        name="gla_recurrence",
    )(z, z, z, z, z, w2pad, gate_b.reshape(1, 2 * LANES), norm_g.reshape(1, LANES), s0_t)


def _rwkv_proj_kernel(x_ref, xp_ref, xl_ref, g1_ref, mu_ref, wr_ref, wk_ref, wv_ref, ww1_ref, ww2_ref, w0_ref,
                      aw1_ref, aw2_ref, a0_ref, gw1_ref, gw2_ref, kk_ref, ka_ref,
                      r_out, k_out, v_out, lw_out, a_out, kk_out, g_out, hl_out, *, tiles_per_seq):
    i = pl.program_id(0)
    g1 = g1_ref[...]
    hn = _rms(x_ref[...], g1)
    tm = hn.shape[0]
    prev = _rms(xp_ref[...], g1)[SUBLANES - 1:SUBLANES, :]
    prev = jnp.where(i % tiles_per_seq == 0, xl_ref[0], prev)
    row = lax.broadcasted_iota(I32, (tm, 1), 0)
    xprev = jnp.where(row == 0, prev, pltpu.roll(hn, 1, axis=0))
    dx = xprev - hn

    def mix(j):
        return hn + dx * mu_ref[j:j + 1, :]

    r = _dot(mix(0).astype(BF16), wr_ref[...])
    k = _dot(mix(1).astype(BF16), wk_ref[...])
    v = _dot(mix(2).astype(BF16), wv_ref[...])
    wl = _dot(jnp.tanh(_dot(mix(3), ww1_ref[...], precision=HI)), ww2_ref[...], precision=HI)
    z = w0_ref[...] + wl
    wpre = -(jnp.maximum(-z, 0.0) + jnp.log(1.0 + jnp.exp(-jnp.abs(z)))) - 0.5
    al = _dot(_dot(mix(4), aw1_ref[...], precision=HI), aw2_ref[...], precision=HI)
    a = _sigmoid(a0_ref[...] + al)
    gg = _dot(_sigmoid(_dot(mix(5).astype(BF16), gw1_ref[...])).astype(BF16), gw2_ref[...])
    r_out[...] = r
    k_out[...] = k * (1.0 + (a - 1.0) * ka_ref[...])
    v_out[...] = v
    lw_out[...] = -jnp.exp(wpre)
    a_out[...] = a
    kk_out[...] = k * kk_ref[...]
    g_out[...] = gg
    hl_out[0] = hn[tm - 1:tm, :]


def rwkv_proj(x, x_last, seq_len, g1, mu, wr, wk, wv, ww1, ww2, w0, aw1, aw2, a0, gw1, gw2, k_k, k_a, tm):
    n, d = x.shape
    tiles_per_seq = seq_len // tm
    row = lambda a: a.reshape(1, d)
    full = lambda a: pl.BlockSpec(a.shape, lambda i: (0,) * a.ndim)
    tile = pl.BlockSpec((tm, d), lambda i: (i, 0))
    blocks8 = tm // SUBLANES
    args = (x, x, x_last.reshape(-1, 1, d), row(g1), mu, wr, wk, wv, ww1, ww2, row(w0), aw1, aw2, row(a0),
            gw1, gw2, row(k_k), row(k_a))
    in_specs = [tile,
                pl.BlockSpec((SUBLANES, d), lambda i: (jnp.maximum(i * blocks8 - 1, 0), 0)),
                pl.BlockSpec((1, 1, d), lambda i: (i // tiles_per_seq, 0, 0))]
    in_specs += [full(a) for a in args[3:]]
    outs = pl.pallas_call(
        functools.partial(_rwkv_proj_kernel, tiles_per_seq=tiles_per_seq),
        grid=(n // tm,),
        in_specs=in_specs,
        out_specs=[tile] * 7 + [pl.BlockSpec((1, 1, d), lambda i: (i, 0, 0))],
        out_shape=[jax.ShapeDtypeStruct((n, d), F32)] * 7 + [jax.ShapeDtypeStruct((n // tm, 1, d), F32)],
        compiler_params=_cparams(("parallel",)),
        name="rwkv_proj",
    )(*args)
    return outs


_NN = ((1,), (0,))
_NT = ((1,), (1,))
_TN = ((0,), (0,))
RWKV_AB_PASSES = 1
RWKV_INV_PASSES = 1
RWKV_APPLY_PASSES = 1
RWKV_STATE_PASSES = 3
RWKV_SEQS_PER_STEP = 4


def _split_bf16(a):
    hi = a.astype(BF16)
    return hi, (a - hi.astype(F32)).astype(BF16)


def _mm(a, b, dims, passes):
    if passes == 6:
        return lax.dot_general(a, b, (dims, ((), ())), preferred_element_type=F32, precision=HI)
    dg = lambda x, y: lax.dot_general(x, y, (dims, ((), ())), preferred_element_type=F32)
    ah, al = _split_bf16(a)
    bh, bl = _split_bf16(b)
    if passes == 1:
        return dg(ah, bh)
    return dg(ah, bh) + (dg(al, bh) + dg(ah, bl))


def _cumsum_rows3(g):
    tri = _tri(g.shape[0], False).astype(BF16)
    h1 = g.astype(BF16)
    r1 = g - h1.astype(F32)
    h2 = r1.astype(BF16)
    h3 = (r1 - h2.astype(F32)).astype(BF16)
    return _dot(tri, h1) + (_dot(tri, h2) + _dot(tri, h3))


def _pair_sum(x, m0):
    s0 = jnp.sum(jnp.where(m0, x, 0.0), axis=-1, keepdims=True)
    s1 = jnp.sum(jnp.where(m0, 0.0, x), axis=-1, keepdims=True)
    return jnp.where(m0, s0, s1)


def _rwkv_kernel(r_ref, k_ref, v_ref, lw_ref, a_ref, kk_ref, g_ref, rk_ref, lng_ref, lnb_ref, s0_ref,
                 o_ref, s_ref, mt_scr, *, chunk):
    t = pl.program_id(2)

    @pl.when(t == 0)
    def _():
        mt_scr[...] = s0_ref[:, 0]

    nrows = r_ref.shape[0]
    nchunks = r_ref.shape[1] // chunk
    c2 = 2 * chunk
    m0 = _lane_mask(LANES, 0, C_HEAD)
    rowi = lax.broadcasted_iota(I32, (LANES, LANES), 0)
    coli = lax.broadcasted_iota(I32, (LANES, LANES), 1)
    blockdiag = (rowi < C_HEAD) == (coli < C_HEAD)
    ti = lax.broadcasted_iota(I32, (c2, c2), 0)
    si = lax.broadcasted_iota(I32, (c2, c2), 1)
    same_head = (ti < chunk) == (si < chunk)
    tm_ = jnp.where(ti < chunk, ti, ti - chunk)
    sm_ = jnp.where(si < chunk, si, si - chunk)
    strict = same_head & (sm_ < tm_)
    incl = same_head & (sm_ <= tm_)

    def stack_heads(x):
        return jnp.concatenate([jnp.where(m0, x, 0.0), jnp.where(m0, 0.0, x)], axis=0)

    def twice(x):
        return jnp.concatenate([x, x], axis=0)

    def unstack(x2):
        return jnp.where(m0, x2[:chunk], x2[chunk:])

    eye = (ti == si).astype(F32)
    seqs = range(nrows)

    def body(ci, carry):
        sl = pl.ds(pl.multiple_of(ci * chunk, chunk), chunk)
        r = [r_ref[i, sl, :] for i in seqs]
        k = [k_ref[i, sl, :] for i in seqs]
        v = [v_ref[i, sl, :] for i in seqs]
        lw = [lw_ref[i, sl, :] for i in seqs]
        kkr = [kk_ref[i, sl, :] for i in seqs]
        kk = [x * lax.rsqrt(_pair_sum(x * x, m0) + 1e-12) for x in kkr]
        al = [a_ref[i, sl, :] * kk[i] for i in seqs]
        gam = [_cumsum_rows3(x) for x in lw]
        e_neg = [jnp.exp(-x) for x in gam]
        xr = [jnp.concatenate([stack_heads(kk[i] * jnp.exp(gam[i] - lw[i])),
                               stack_heads(r[i] * jnp.exp(gam[i]))], axis=0) for i in seqs]
        alk = [jnp.concatenate([twice(al[i] * e_neg[i]), twice(k[i] * e_neg[i])], axis=0) for i in seqs]
        mt = [mt_scr[i] for i in seqs]
        ab = [_mm(xr[i], alk[i], _NT, RWKV_AB_PASSES) for i in seqs]
        xm = [_mm(xr[i], mt[i], _NT, RWKV_STATE_PASSES) for i in seqs]
        a_al = [jnp.where(strict, x[:c2, :c2], 0.0) for x in ab]
        a_k = [jnp.where(strict, x[:c2, c2:], 0.0) for x in ab]
        b_alk = [jnp.concatenate([jnp.where(incl, x[c2:, c2:], 0.0), jnp.where(incl, -x[c2:, :c2], 0.0)], axis=1)
                 for x in ab]
        v2 = [twice(x) for x in v]
        rhs = [xm[i][:c2] + _mm(a_k[i], v2[i], _NN, RWKV_APPLY_PASSES) for i in seqs]
        p = [-x for x in a_al]
        tinv = [eye + x for x in p]
        span = 2
        while span < chunk:
            p = [_mm(x, x, _NN, RWKV_INV_PASSES) for x in p]
            tinv = [tinv[i] + _mm(tinv[i], p[i], _NN, RWKV_INV_PASSES) for i in seqs]
            span *= 2
        u = [unstack(_mm(tinv[i], rhs[i], _NN, RWKV_APPLY_PASSES)) for i in seqs]
        o = [unstack(xm[i][c2:] + _mm(b_alk[i], jnp.concatenate([v2[i], twice(u[i])], axis=0), _NN,
                                      RWKV_APPLY_PASSES)) for i in seqs]
        g_last = [x[-1:, :] for x in gam]
        e_end = [jnp.exp(g_last[i] - gam[i]) for i in seqs]
        upd = [_mm(jnp.concatenate([v[i], u[i]], axis=0),
                   jnp.concatenate([k[i] * e_end[i], -(al[i] * e_end[i])], axis=0), _TN, RWKV_STATE_PASSES)
               for i in seqs]
        for i in seqs:
            mt_scr[i] = mt[i] * jnp.exp(g_last[i]) + jnp.where(blockdiag, upd[i], 0.0)
        for i in seqs:
            mean = _pair_sum(o[i], m0) * (1.0 / C_HEAD)
            cen = o[i] - mean
            var = _pair_sum(cen * cen, m0) * (1.0 / C_HEAD)
            on = cen * lax.rsqrt(var + C_GN_EPS) * lng_ref[...] + lnb_ref[...]
            bonus = _pair_sum(r[i] * k[i] * rk_ref[...], m0) * v[i]
            o_ref[i, sl, :] = (on + bonus) * g_ref[i, sl, :]
        return carry

    lax.fori_loop(0, nchunks, body, 0)

    @pl.when(t == pl.num_programs(2) - 1)
    def _():
        s_ref[:, 0] = mt_scr[...]


def rwkv_recurrence(r, k, v, lw, a, kk, g, r_k, ln_g, ln_b, s0_bd, tb, chunk, nb):
    bsz, t, d = r.shape
    npairs = C_HEADS // 2
    seq = pl.BlockSpec((nb, tb, LANES), lambda b, p, i: (b, i, p))
    vec = pl.BlockSpec((1, LANES), lambda b, p, i: (0, p))
    st = pl.BlockSpec((nb, 1, LANES, LANES), lambda b, p, i: (b, p, 0, 0))
    return pl.pallas_call(
        functools.partial(_rwkv_kernel, chunk=chunk),
        grid=(bsz // nb, npairs, t // tb),
        in_specs=[seq] * 7 + [vec, vec, vec, st],
        out_specs=[seq, st],
        out_shape=[jax.ShapeDtypeStruct((bsz, t, d), F32),
                   jax.ShapeDtypeStruct((bsz, npairs, LANES, LANES), F32)],
        scratch_shapes=[pltpu.VMEM((nb, LANES, LANES), F32)],
        compiler_params=_cparams(("parallel", "parallel", "arbitrary")),
        name="rwkv_recurrence",
    )(r, k, v, lw, a, kk, g, r_k.reshape(1, d), ln_g.reshape(1, d), ln_b.reshape(1, d), s0_bd)


NEG_INF = float("-inf")


def _top16_rows(s):
    n = s.shape[0]
    key = lax.broadcasted_iota(I32, s.shape, 0)
    vals, idxs = [], []
    for _ in range(P_TOPK):
        m = jnp.max(s, axis=0, keepdims=True)
        am = jnp.min(jnp.where(s == m, key, n), axis=0, keepdims=True)
        vals.append(m)
        idxs.append(am)
        s = jnp.where(key == am, NEG_INF, s)
    return vals, idxs


def _top16_pairs(v0, i0, v1, i1):
    a0 = jnp.concatenate(v0[0:8], axis=0)
    a1 = jnp.concatenate(v0[8:16], axis=0)
    b0 = jnp.concatenate(v1[0:8], axis=0)
    b1 = jnp.concatenate(v1[8:16], axis=0)
    ia0 = jnp.concatenate(i0[0:8], axis=0) * P_KEYS
    ia1 = jnp.concatenate(i0[8:16], axis=0) * P_KEYS
    ib0 = jnp.concatenate(i1[0:8], axis=0)
    ib1 = jnp.concatenate(i1[8:16], axis=0)
    row = lax.broadcasted_iota(I32, (SUBLANES, 1), 0)
    slabs = []

    def add(val, eid, keep):
        slabs.append((val if keep is None else jnp.where(keep, val, NEG_INF), eid))

    add(v0[0] + b0, ia0[0:1] + ib0, None)
    add(v0[0] + b1, ia0[0:1] + ib1, None)
    add(v0[1] + b0, ia0[1:2] + ib0, None)
    add(v0[2] + b0, ia0[2:3] + ib0, row < 5)
    add(v0[3] + b0, ia0[3:4] + ib0, row < 4)
    add(a0 + v1[0], ia0 + ib0[0:1], row >= 4)
    add(a1 + v1[0], ia1 + ib0[0:1], None)
    add(a0 + v1[1], ia0 + ib0[1:2], row >= 4)
    add(a0 + v1[2], ia0 + ib0[2:3], row == 4)

    big = P_KEYS * P_KEYS
    out_v, out_e = [], []
    for _ in range(P_TOPK):
        m = slabs[0][0]
        for val, _e in slabs[1:]:
            m = jnp.maximum(m, val)
        m = jnp.max(m, axis=0, keepdims=True)
        e = None
        for val, eid in slabs:
            c = jnp.where(val == m, eid, big)
            e = c if e is None else jnp.minimum(e, c)
        e = jnp.min(e, axis=0, keepdims=True)
        out_v.append(m)
        out_e.append(e)
        slabs = [(jnp.where(eid == e, NEG_INF, val), eid) for val, eid in slabs]
    return out_v, out_e


def _peer_select_kernel(x_ref, g_ref, wq_ref, keys_ref, xn_out, eid_out, gate_out):
    hn = _rms(x_ref[...], g_ref[...])
    xn_out[...] = hn
    q = _dot(hn.astype(BF16), wq_ref[...])
    tm = q.shape[0]
    for lt in range(tm // LANES):
        rows = slice(lt * LANES, (lt + 1) * LANES)
        e_rows, g_rows = [], []
        for h in range(P_HEADS):
            tops = []
            for p in range(2):
                hp = 2 * h + p
                s = _dot_nt(keys_ref[hp], q[rows, hp * LANES:(hp + 1) * LANES], precision=HI)
                tops.append(_top16_rows(s))
            cs, ce = _top16_pairs(tops[0][0], tops[0][1], tops[1][0], tops[1][1])
            ex = [jnp.exp(c - cs[0]) for c in cs]
            tot = ex[0]
            for e in ex[1:]:
                tot = tot + e
            inv = 1.0 / tot
            e_rows += ce
            g_rows += [e * inv for e in ex]
        eid_out[rows, :] = jnp.concatenate(e_rows, axis=0).T
        gate_out[rows, :] = jnp.concatenate(g_rows, axis=0).T


def peer_select(x, g, wq_bf16, keys, tm):
    n, d = x.shape
    return pl.pallas_call(
        _peer_select_kernel,
        grid=(n // tm,),
        in_specs=[pl.BlockSpec((tm, d), lambda i: (i, 0)),
                  pl.BlockSpec((1, d), lambda i: (0, 0)),
                  pl.BlockSpec(wq_bf16.shape, lambda i: (0, 0)),
                  pl.BlockSpec(keys.shape, lambda i: (0, 0, 0))],
        out_specs=[pl.BlockSpec((tm, d), lambda i: (i, 0)),
                   pl.BlockSpec((tm, P_PICKS), lambda i: (i, 0)),
                   pl.BlockSpec((tm, P_PICKS), lambda i: (i, 0))],
        out_shape=[jax.ShapeDtypeStruct((n, d), F32),
                   jax.ShapeDtypeStruct((n, P_PICKS), I32),
                   jax.ShapeDtypeStruct((n, P_PICKS), F32)],
        compiler_params=_cparams(("parallel",)),
        name="peer_select",
    )(x, g.reshape(1, d), wq_bf16, keys)


def _peer_act_kernel(h_ref, g_ref, o_ref):
    h = h_ref[...]
    o_ref[...] = 0.5 * h * (1.0 + lax.erf(h * (2.0 ** -0.5))) * g_ref[...]


def peer_act(hid, gate, tm):
    n, p = hid.shape
    spec = pl.BlockSpec((tm, p), lambda i: (i, 0))
    return pl.pallas_call(
        _peer_act_kernel, grid=(n // tm,), in_specs=[spec, spec], out_specs=spec,
        out_shape=jax.ShapeDtypeStruct((n, p), F32),
        compiler_params=_cparams(("parallel",)), name="peer_act",
    )(hid, gate)


_CHUNKS_PER_TOK = P_PICKS // SC_CHUNK
_WORD_VREGS = SC_WORDS // SC_LANES
U32 = jnp.uint32


def pack_rows_bf16(t):
    lo = lax.bitcast_convert_type(t[..., :SC_WORDS].astype(BF16), jnp.uint16).astype(U32)
    a = lax.bitcast_convert_type(t[..., SC_WORDS:], U32)
    sign = a & U32(0x80000000)
    mag = a & U32(0x7FFFFFFF)
    steps = (jnp.maximum(mag + U32(0x8000), lo) - lo) >> 16
    near = (steps << 16) + lo
    near = jnp.where(near >= U32(0x7F800000), near - U32(0x10000), near)
    return lax.bitcast_convert_type(sign | near, I32)


def _low_f32(w):
    return lax.bitcast_convert_type(w << 16, F32)


def _high_f32(w):
    return lax.bitcast_convert_type(w, F32)


def _sc_worker():
    return lax.axis_index("s") * SC_CORES + lax.axis_index("c")


def _sc_hid_body(tab_hbm, idx_hbm, x_hbm, hid_hbm, idx_v, x_v, hid_v, rows_a, rows_b, sem_a, sem_b, *, tok_per_w):
    wid = _sc_worker()
    nblk = tok_per_w // SC_TOK_BLOCK
    nch = SC_TOK_BLOCK * _CHUNKS_PER_TOK
    lane = lax.iota(I32, SC_LANES)
    quads = SC_LANES // SC_HID_ROWS

    def dots(rows, x_row, res_ref, col0):
        for grp in range(SC_CHUNK // SC_LANES):
            def some_rows(q, res):
                r0 = grp * SC_LANES + q * SC_HID_ROWS

                def some_words(jj, accs):
                    accs = list(accs)
                    for jw in range(SC_HID_WORDS):
                        off = (jj * SC_HID_WORDS + jw) * SC_LANES
                        xl = x_v[x_row, pl.ds(off, SC_LANES)]
                        xh = x_v[x_row, pl.ds(SC_WORDS + off, SC_LANES)]
                        for i in range(SC_HID_ROWS):
                            w = rows[r0 + i, pl.ds(off, SC_LANES)]
                            k = 2 * i + jw % 2
                            accs[k] = accs[k] + (_low_f32(w) * xl + _high_f32(w) * xh)
                    return tuple(accs)

                zero = jnp.zeros((SC_LANES,), F32)
                accs = lax.fori_loop(0, _WORD_VREGS // SC_HID_WORDS, some_words, (zero,) * (2 * SC_HID_ROWS))
                for i in range(SC_HID_ROWS):
                    res = jnp.where(lane == q * SC_HID_ROWS + i, jnp.sum(accs[2 * i] + accs[2 * i + 1]), res)
                return res
            res = lax.fori_loop(0, quads, some_rows, jnp.zeros((SC_LANES,), F32))
            res_ref[x_row, pl.ds(col0 + grp * SC_LANES, SC_LANES)] = res

    def block(bi, carry):
        tok0 = wid * tok_per_w + bi * SC_TOK_BLOCK
        pltpu.sync_copy(idx_hbm.at[pl.ds(tok0 * _CHUNKS_PER_TOK, nch)], idx_v)
        pltpu.async_copy(tab_hbm.at[idx_v.at[0]], rows_a, sem_a)
        pltpu.sync_copy(x_hbm.at[pl.ds(tok0, SC_TOK_BLOCK)], x_v)

        def pair(j, c):
            ca = 2 * j
            pltpu.async_copy(tab_hbm.at[idx_v.at[ca + 1]], rows_b, sem_b)
            pltpu.make_async_copy(tab_hbm.at[idx_v.at[0]], rows_a, sem_a).wait()
            dots(rows_a, ca // _CHUNKS_PER_TOK, hid_v, (ca % _CHUNKS_PER_TOK) * SC_CHUNK)

            @pl.when(j < nch // 2 - 1)
            def _():
                pltpu.async_copy(tab_hbm.at[idx_v.at[ca + 2]], rows_a, sem_a)
            pltpu.make_async_copy(tab_hbm.at[idx_v.at[0]], rows_b, sem_b).wait()
            cb = ca + 1
            dots(rows_b, cb // _CHUNKS_PER_TOK, hid_v, (cb % _CHUNKS_PER_TOK) * SC_CHUNK)
            return c
        lax.fori_loop(0, nch // 2, pair, 0)
        pltpu.sync_copy(hid_v, hid_hbm.at[pl.ds(tok0, SC_TOK_BLOCK)])
        return carry

    lax.fori_loop(0, nblk, block, 0)


def sc_expert_hidden(table_packed, eidx, xn):
    n = xn.shape[0]
    tok_per_w = n // SC_WORKERS
    mesh = plsc.VectorSubcoreMesh(core_axis_name="c", subcore_axis_name="s")
    nch = SC_TOK_BLOCK * _CHUNKS_PER_TOK
    k = pl.kernel(
        functools.partial(_sc_hid_body, tok_per_w=tok_per_w), mesh=mesh,
        out_type=jax.ShapeDtypeStruct((n, P_PICKS), F32),
        scratch_types=[pltpu.VMEM((nch, SC_CHUNK), I32),
                       pltpu.VMEM((SC_TOK_BLOCK, D_MODEL), F32),
                       pltpu.VMEM((SC_TOK_BLOCK, P_PICKS), F32),
                       pltpu.VMEM((SC_CHUNK, SC_WORDS), I32),
                       pltpu.VMEM((SC_CHUNK, SC_WORDS), I32),
                       pltpu.SemaphoreType.DMA, pltpu.SemaphoreType.DMA],
        compiler_params=pltpu.CompilerParams(needs_layout_passes=False),
        name="sc_expert_hidden",
    )
    return k(table_packed, eidx.reshape(n * _CHUNKS_PER_TOK, SC_CHUNK), xn)


def _sc_out_body(tab_hbm, idx_hbm, act_hbm, x_hbm, y_hbm, idx_v, act_v, y_v, rows_a, rows_b, sem_a, sem_b,
                 *, tok_per_w):
    wid = _sc_worker()
    nblk = tok_per_w // SC_TOK_BLOCK
    nch = SC_TOK_BLOCK * _CHUNKS_PER_TOK
    half_w = _WORD_VREGS // 2

    def accum(rows, tok, col0):
        for hv in range(2):
            base = hv * half_w * SC_LANES
            acc0 = (tuple(y_v[tok, pl.ds(base + j * SC_LANES, SC_LANES)] for j in range(half_w))
                    + tuple(y_v[tok, pl.ds(SC_WORDS + base + j * SC_LANES, SC_LANES)] for j in range(half_w)))

            def one(r, acc):
                aidx = jnp.full((SC_LANES,), col0, I32) + r
                wgt = plsc.load_gather(act_v, [jnp.full((SC_LANES,), tok, I32), aidx])
                lo, hi = [], []
                for j in range(half_w):
                    w = rows[r, pl.ds(base + j * SC_LANES, SC_LANES)]
                    lo.append(acc[j] + wgt * _low_f32(w))
                    hi.append(acc[half_w + j] + wgt * _high_f32(w))
                return tuple(lo + hi)
            acc = lax.fori_loop(0, SC_CHUNK, one, acc0)
            for j in range(half_w):
                y_v[tok, pl.ds(base + j * SC_LANES, SC_LANES)] = acc[j]
                y_v[tok, pl.ds(SC_WORDS + base + j * SC_LANES, SC_LANES)] = acc[half_w + j]

    def block(bi, carry):
        tok0 = wid * tok_per_w + bi * SC_TOK_BLOCK
        pltpu.sync_copy(idx_hbm.at[pl.ds(tok0 * _CHUNKS_PER_TOK, nch)], idx_v)
        pltpu.async_copy(tab_hbm.at[idx_v.at[0]], rows_a, sem_a)
        pltpu.sync_copy(act_hbm.at[pl.ds(tok0, SC_TOK_BLOCK)], act_v)
        pltpu.sync_copy(x_hbm.at[pl.ds(tok0, SC_TOK_BLOCK)], y_v)

        def pair(j, c):
            ca = 2 * j
            pltpu.async_copy(tab_hbm.at[idx_v.at[ca + 1]], rows_b, sem_b)
            pltpu.make_async_copy(tab_hbm.at[idx_v.at[0]], rows_a, sem_a).wait()
            accum(rows_a, ca // _CHUNKS_PER_TOK, (ca % _CHUNKS_PER_TOK) * SC_CHUNK)

            @pl.when(j < nch // 2 - 1)
            def _():
                pltpu.async_copy(tab_hbm.at[idx_v.at[ca + 2]], rows_a, sem_a)
            pltpu.make_async_copy(tab_hbm.at[idx_v.at[0]], rows_b, sem_b).wait()
            cb = ca + 1
            accum(rows_b, cb // _CHUNKS_PER_TOK, (cb % _CHUNKS_PER_TOK) * SC_CHUNK)
            return c
        lax.fori_loop(0, nch // 2, pair, 0)
        pltpu.sync_copy(y_v, y_hbm.at[pl.ds(tok0, SC_TOK_BLOCK)])
        return carry

    lax.fori_loop(0, nblk, block, 0)


def sc_expert_output(table_packed, eidx, act, x):
    n = x.shape[0]
    tok_per_w = n // SC_WORKERS
    mesh = plsc.VectorSubcoreMesh(core_axis_name="c", subcore_axis_name="s")
    nch = SC_TOK_BLOCK * _CHUNKS_PER_TOK
    k = pl.kernel(
        functools.partial(_sc_out_body, tok_per_w=tok_per_w), mesh=mesh,
        out_type=jax.ShapeDtypeStruct((n, D_MODEL), F32),
        scratch_types=[pltpu.VMEM((nch, SC_CHUNK), I32),
                       pltpu.VMEM((SC_TOK_BLOCK, P_PICKS), F32),
                       pltpu.VMEM((SC_TOK_BLOCK, D_MODEL), F32),
                       pltpu.VMEM((SC_CHUNK, SC_WORDS), I32),
                       pltpu.VMEM((SC_CHUNK, SC_WORDS), I32),
                       pltpu.SemaphoreType.DMA, pltpu.SemaphoreType.DMA],
        compiler_params=pltpu.CompilerParams(needs_layout_passes=False),
        name="sc_expert_output",
    )
    return k(table_packed, eidx.reshape(n * _CHUNKS_PER_TOK, SC_CHUNK), act, x)


def peer_ffn(x, g, wq_bf16, keys, u_packed, v_packed, tm):
    xn, eidx, gate = peer_select(x, g, wq_bf16, keys, tm)
    hid = sc_expert_hidden(u_packed, eidx, xn)
    act = peer_act(hid, gate, tm)
    return sc_expert_output(v_packed, eidx, act, x)


def _prep_w_in(w_in):
    main = jnp.concatenate([w_in[:, :3072], w_in[:, 3088:3600]], axis=1)
    lr = jnp.pad(w_in[:, 3072:3088], ((0, 0), (0, LANES - GLA_RANK)))
    return jnp.concatenate([main, lr], axis=1).astype(BF16)


def _trunk(x, st_h, st_g, st_r, st_s, w, seq_len, tm, tmp, tb, chunk):
    bsz = x.shape[0]
    n = bsz * seq_len
    x2 = x.reshape(n, D_MODEL)

    z = norm_proj(x2, w["norm1_g"][0], w["w_in"], tm).reshape(bsz, seq_len, Z_WIDTH)
    s0_h = jnp.swapaxes(st_h, -1, -2)
    s0_g = jnp.swapaxes(st_g.reshape(bsz, 2, 2 * B_DK, LANES), -1, -2)
    o_a, sh_t = hgrn_recurrence(z, w["lb0"], w["hgrn_norm_g"], s0_h, tb, chunk, GATED_SEQS_PER_STEP)
    o_b, sg_t = gla_recurrence(z, w["gla_w2"], w["gla_b"], w["gla_norm_g"], s0_g, tb, chunk, GATED_SEQS_PER_STEP)
    new_h = jnp.swapaxes(sh_t, -1, -2)
    new_g = jnp.swapaxes(sg_t, -1, -2).reshape(bsz, B_HEADS, B_DK, LANES)
    x2 = out_proj2(x2, o_a.reshape(n, A_WIDTH), o_b.reshape(n, B_WIDTH), w["w_out_a"], w["w_out_b"], tm)
    x2 = peer_ffn(x2, w["norm2_g"][0], w["peer_wq"][0], w["peer_keys"][0], w["peer_u"][0], w["peer_v"][0], tmp)

    r, k, v, lw, a, kk, g, hl = rwkv_proj(
        x2, st_s, seq_len, w["norm1_g"][1], w["mu"], w["wr"], w["wk"], w["wv"], w["w_w1"], w["w_w2"], w["w0"],
        w["a_w1"], w["a_w2"], w["a0"], w["g_w1"], w["g_w2"], w["k_k"], w["k_a"], tm)
    new_s = hl.reshape(bsz, seq_len // tm, D_MODEL)[:, -1]
    pr = st_r.reshape(bsz, C_HEADS // 2, 2, C_HEAD, C_HEAD)
    zero = jnp.zeros_like(pr[:, :, 0])
    s0_r = jnp.concatenate([jnp.concatenate([pr[:, :, 0], zero], axis=-1),
                            jnp.concatenate([zero, pr[:, :, 1]], axis=-1)], axis=-2)
    sh3 = lambda t: t.reshape(bsz, seq_len, D_MODEL)
    o_c, sr_bd = rwkv_recurrence(sh3(r), sh3(k), sh3(v), sh3(lw), sh3(a), sh3(kk), sh3(g),
                                 w["r_k"], w["ln_g"], w["ln_b"], s0_r, tb, chunk, min(RWKV_SEQS_PER_STEP, bsz))
    new_r = jnp.stack([sr_bd[:, :, :C_HEAD, :C_HEAD], sr_bd[:, :, C_HEAD:, C_HEAD:]], axis=2)
    new_r = new_r.reshape(bsz, C_HEADS, C_HEAD, C_HEAD)
    x2 = out_proj1(x2, o_c.reshape(n, D_MODEL), w["w_out_c"], tm)
    x2 = peer_ffn(x2, w["norm2_g"][1], w["peer_wq"][1], w["peer_keys"][1], w["peer_u"][1], w["peer_v"][1], tmp)

    y = final_norm(x2, w["final_g"], tm).reshape(bsz, seq_len, D_MODEL)
    return y, new_h[None], new_g[None], new_r[None], new_s[None]


def kernel(x_prompt, x_sample, state_hgrn, state_gla, state_rwkv, state_shift, w_in_ab, hgrn_lower_bounds, hgrn_norm_g, gla_gate_w2, gla_gate_b, gla_norm_g, w_out_ab, rwkv_mu, rwkv_w_rkv, rwkv_w_w1, rwkv_w_w2, rwkv_w0, rwkv_a_w1, rwkv_a_w2, rwkv_a0, rwkv_g_w1, rwkv_g_w2, rwkv_k_k, rwkv_k_a, rwkv_r_k, rwkv_ln_g, rwkv_ln_b, w_out_c, norm1_g, norm2_g, final_g, peer_w_q, peer_sub_keys, peer_u, peer_v):
    lbs = jnp.cumsum(jax.nn.softmax(hgrn_lower_bounds.astype(F32), axis=0), axis=0)
    w = dict(
        norm1_g=norm1_g, norm2_g=norm2_g, final_g=final_g,
        w_in=_prep_w_in(w_in_ab[0]), lb0=lbs[0], hgrn_norm_g=hgrn_norm_g[0],
        gla_w2=jnp.pad(gla_gate_w2[0], ((0, LANES - GLA_RANK), (0, 0))), gla_b=gla_gate_b[0],
        gla_norm_g=gla_norm_g[0],
        w_out_a=w_out_ab[0, :A_WIDTH].astype(BF16), w_out_b=w_out_ab[0, A_WIDTH:].astype(BF16),
        mu=rwkv_mu[0], wr=rwkv_w_rkv[0, 0].astype(BF16), wk=rwkv_w_rkv[0, 1].astype(BF16),
        wv=rwkv_w_rkv[0, 2].astype(BF16), w_w1=rwkv_w_w1[0], w_w2=rwkv_w_w2[0], w0=rwkv_w0[0],
        a_w1=rwkv_a_w1[0], a_w2=rwkv_a_w2[0], a0=rwkv_a0[0],
        g_w1=rwkv_g_w1[0].astype(BF16), g_w2=rwkv_g_w2[0].astype(BF16),
        k_k=rwkv_k_k[0], k_a=rwkv_k_a[0], r_k=rwkv_r_k[0], ln_g=rwkv_ln_g[0], ln_b=rwkv_ln_b[0],
        w_out_c=w_out_c[0].astype(BF16),
        peer_wq=peer_w_q.astype(BF16),
        peer_keys=peer_sub_keys.reshape(peer_sub_keys.shape[0], 2 * P_HEADS, P_KEYS, P_KEYS),
        peer_u=pack_rows_bf16(peer_u), peer_v=pack_rows_bf16(peer_v),
    )
    bp, tp, _ = x_prompt.shape
    bs, ts, _ = x_sample.shape
    assert sum(PROMPT_GROUPS) == bp
    groups, start = [], 0
    for gsz in PROMPT_GROUPS:
        zeros = lambda s: jnp.zeros((gsz,) + s.shape[2:], F32)
        groups.append(_trunk(x_prompt[start:start + gsz], zeros(state_hgrn), zeros(state_gla), zeros(state_rwkv),
                             zeros(state_shift), w, tp, tm=256, tmp=256, tb=512, chunk=64))
        start += gsz
    y_p = jnp.concatenate([g[0] for g in groups], axis=0)
    p_h, p_g, p_r, p_s = (jnp.concatenate([g[j] for g in groups], axis=1) for j in range(1, 5))
    y_s, s_h, s_g, s_r, s_s = _trunk(x_sample, state_hgrn[0], state_gla[0], state_rwkv[0], state_shift[0],
                                     w, ts, tm=32, tmp=128, tb=32, chunk=32)
    return (y_p, y_s, p_h, p_g, p_r, p_s, s_h, s_g, s_r, s_s)
```

```python
import functools

import jax
import jax.numpy as jnp
from jax import lax
from jax.experimental import pallas as pl
from jax.experimental.pallas import tpu as pltpu
from jax.experimental.pallas import tpu_sc as plsc

F32 = jnp.float32
BF16 = jnp.bfloat16
I32 = jnp.int32
HI = lax.Precision.HIGHEST

D_MODEL = 1024
NORM_EPS = 1e-6
LANES = 128
SUBLANES = 8
VMEM_LIMIT = 56 * 1024 * 1024

A_WIDTH = 512
A_HEADS = 4
B_WIDTH = 512
B_HEADS = 4
B_DK = 64
GLA_RANK = 16
GLA_NORMALIZER = 16.0
Z_WIDTH = 3712
C_HEAD = 64
C_HEADS = 16
C_GN_EPS = 64e-5
P_HEADS = 8
P_KEYS = 128
P_TOPK = 16
P_PICKS = P_HEADS * P_TOPK
SC_CORES = 2
SC_SUBCORES = 16
SC_WORKERS = SC_CORES * SC_SUBCORES
SC_LANES = 16
SC_CHUNK = 64
SC_TOK_BLOCKS = (32, 16)
SC_WORDS = D_MODEL // 2
SC_HID_ROWS = 4
SC_HID_WORDS = 4
PROMPT_GROUPS = (2,) * 8
GATED_SEQS_PER_STEP = 2


def _cparams(sem):
    return pltpu.CompilerParams(dimension_semantics=sem, vmem_limit_bytes=VMEM_LIMIT)


def _rms(x, g):
    ms = jnp.mean(x * x, axis=-1, keepdims=True)
    return x * lax.rsqrt(ms + NORM_EPS) * g


def _dot(a, b, precision=None):
    return jnp.dot(a, b, preferred_element_type=F32, precision=precision)


def _dot_nt(a, b, precision=None):
    return lax.dot_general(a, b, (((1,), (1,)), ((), ())), preferred_element_type=F32, precision=precision)


def _dot_tn(a, b, precision=None):
    return lax.dot_general(a, b, (((0,), (0,)), ((), ())), preferred_element_type=F32, precision=precision)


def _tri(n, strict):
    r = lax.broadcasted_iota(I32, (n, n), 0)
    c = lax.broadcasted_iota(I32, (n, n), 1)
    return (c < r) if strict else (c <= r)


def _cumsum_rows(g):
    return _dot(_tri(g.shape[0], False).astype(F32), g, precision=HI)


def _lane_mask(width, lo, hi):
    l = lax.broadcasted_iota(I32, (1, width), 1)
    return (l >= lo) & (l < hi)


def _sigmoid(x):
    return 1.0 / (1.0 + jnp.exp(-x))


def _silu(x):
    return x * _sigmoid(x)


def _norm_proj_kernel(x_ref, g_ref, w_ref, o_ref):
    hn = _rms(x_ref[...], g_ref[...])
    o_ref[...] = _dot(hn.astype(BF16), w_ref[...])


def norm_proj(x, g, w_bf16, tm):
    n, d = x.shape
    f = w_bf16.shape[1]
    return pl.pallas_call(
        _norm_proj_kernel,
        grid=(n // tm,),
        in_specs=[pl.BlockSpec((tm, d), lambda i: (i, 0)),
                  pl.BlockSpec((1, d), lambda i: (0, 0)),
                  pl.BlockSpec((d, f), lambda i: (0, 0))],
        out_specs=pl.BlockSpec((tm, f), lambda i: (i, 0)),
        out_shape=jax.ShapeDtypeStruct((n, f), F32),
        compiler_params=_cparams(("parallel",)),
        name="norm_proj",
    )(x, g.reshape(1, d), w_bf16)


def _out_proj2_kernel(x_ref, a_ref, b_ref, wa_ref, wb_ref, o_ref):
    y = _dot(a_ref[...].astype(BF16), wa_ref[...]) + _dot(b_ref[...].astype(BF16), wb_ref[...])
    o_ref[...] = x_ref[...] + y


def out_proj2(x, a, b, wa, wb, tm):
    n, d = x.shape
    ka, kb = a.shape[1], b.shape[1]
    return pl.pallas_call(
        _out_proj2_kernel,
        grid=(n // tm,),
        in_specs=[pl.BlockSpec((tm, d), lambda i: (i, 0)),
                  pl.BlockSpec((tm, ka), lambda i: (i, 0)),
                  pl.BlockSpec((tm, kb), lambda i: (i, 0)),
                  pl.BlockSpec((ka, d), lambda i: (0, 0)),
                  pl.BlockSpec((kb, d), lambda i: (0, 0))],
        out_specs=pl.BlockSpec((tm, d), lambda i: (i, 0)),
        out_shape=jax.ShapeDtypeStruct((n, d), F32),
        compiler_params=_cparams(("parallel",)),
        name="out_proj2",
    )(x, a, b, wa, wb)


def _out_proj1_kernel(x_ref, a_ref, wa_ref, o_ref):
    o_ref[...] = x_ref[...] + _dot(a_ref[...].astype(BF16), wa_ref[...])


def out_proj1(x, a, wa, tm):
    n, d = x.shape
    ka = a.shape[1]
    return pl.pallas_call(
        _out_proj1_kernel,
        grid=(n // tm,),
        in_specs=[pl.BlockSpec((tm, d), lambda i: (i, 0)),
                  pl.BlockSpec((tm, ka), lambda i: (i, 0)),
                  pl.BlockSpec((ka, d), lambda i: (0, 0))],
        out_specs=pl.BlockSpec((tm, d), lambda i: (i, 0)),
        out_shape=jax.ShapeDtypeStruct((n, d), F32),
        compiler_params=_cparams(("parallel",)),
        name="out_proj1",
    )(x, a, wa)


def _final_norm_kernel(x_ref, g_ref, o_ref):
    o_ref[...] = _rms(x_ref[...], g_ref[...])


def final_norm(x, g, tm):
    n, d = x.shape
    return pl.pallas_call(
        _final_norm_kernel,
        grid=(n // tm,),
        in_specs=[pl.BlockSpec((tm, d), lambda i: (i, 0)), pl.BlockSpec((1, d), lambda i: (0, 0))],
        out_specs=pl.BlockSpec((tm, d), lambda i: (i, 0)),
        out_shape=jax.ShapeDtypeStruct((n, d), F32),
        compiler_params=_cparams(("parallel",)),
        name="final_norm",
    )(x, g.reshape(1, d))


def _intra_chunk(problems):
    c = problems[0][0].shape[0]
    nb = c // SUBLANES
    row = lax.broadcasted_iota(I32, (SUBLANES, 1), 0)
    qbs = [[q[SUBLANES * i:SUBLANES * (i + 1)] for i in range(nb)] for q, _, _, _ in problems]
    bbs = [[b[SUBLANES * i:SUBLANES * (i + 1)] for i in range(nb)] for _, _, b, _ in problems]
    outs = [[[None] * nb for _ in heads] for _, _, _, heads in problems]
    for s in range(c):
        rb0 = s // SUBLANES
        for pi, (_, k, b, heads) in enumerate(problems):
            ks = k[s:s + 1, :]
            bs = b[s:s + 1, :]
            for rb in range(rb0, nb):
                p = qbs[pi][rb] * (ks * jnp.exp(bbs[pi][rb] - bs))
                for hi, (mask, v) in enumerate(heads):
                    pm = p if mask is None else jnp.where(mask, p, 0.0)
                    col = jnp.sum(pm, axis=-1, keepdims=True)
                    if rb == rb0:
                        col = jnp.where(row + SUBLANES * rb >= s, col, 0.0)
                    term = col * v[s:s + 1, :]
                    prev = outs[pi][hi][rb]
                    outs[pi][hi][rb] = term if prev is None else prev + term
    return [[jnp.concatenate(o, axis=0) for o in po] for po in outs]


def _gated_chunk(problems, states):
    intra = _intra_chunk(problems)
    qes = [q * jnp.exp(b) for q, _, b, _ in problems]
    b_lasts = [b[-1:, :] for _, _, b, _ in problems]
    khs = [k * jnp.exp(bl - b) for (_, k, b, _), bl in zip(problems, b_lasts)]
    outs = []
    for (_, _, _, heads), qe, st, po in zip(problems, qes, states, intra):
        o = []
        for (mask, v), oi in zip(heads, po):
            qm = qe if mask is None else jnp.where(mask, qe, 0.0)
            o.append(oi + _dot_nt(qm, st, precision=HI))
        outs.append(o)
    new_states = []
    for (_, _, _, heads), kh, st, bl in zip(problems, khs, states, b_lasts):
        upd = _dot_tn(heads[0][1], kh, precision=HI)
        if len(heads) == 2:
            upd = jnp.where(heads[0][0], upd, _dot_tn(heads[1][1], kh, precision=HI))
        new_states.append(st * jnp.exp(bl) + upd)
    return outs, new_states


def _head_rms(o, g):
    ms = jnp.mean(o * o, axis=-1, keepdims=True)
    return o * lax.rsqrt(ms + NORM_EPS) * g


def _hgrn_kernel(zq_ref, zf_ref, zi_ref, zg_ref, lb_ref, ng_ref, s0_ref, o_ref, s_ref, st_scr, *, chunk):
    t = pl.program_id(2)

    @pl.when(t == 0)
    def _():
        st_scr[...] = s0_ref[:, 0]

    lb = lb_ref[0]
    nseq = zq_ref.shape[0]
    nchunks = zq_ref.shape[1] // chunk

    def body(ci, carry):
        sl = pl.ds(pl.multiple_of(ci * chunk, chunk), chunk)
        problems = []
        for i in range(nseq):
            f = lb + (1.0 - lb) * _sigmoid(zf_ref[i, sl, :])
            problems.append((_silu(zq_ref[i, sl, :]), 1.0 - f, _cumsum_rows(jnp.log(f)), [(None, zi_ref[i, sl, :])]))
        outs, states = _gated_chunk(problems, [st_scr[i] for i in range(nseq)])
        for i in range(nseq):
            st_scr[i] = states[i]
            o_ref[i, sl, :] = _head_rms(outs[i][0], ng_ref[...]) * _silu(zg_ref[i, sl, :])
        return carry

    lax.fori_loop(0, nchunks, body, 0)

    @pl.when(t == pl.num_programs(2) - 1)
    def _():
        s_ref[:, 0] = st_scr[...]


def hgrn_recurrence(z, lb, norm_g, s0_t, tb, chunk, nb):
    bsz, t, _ = z.shape
    zspec = lambda off: pl.BlockSpec((nb, tb, LANES), lambda b, h, i: (b, i, h + off))
    return pl.pallas_call(
        functools.partial(_hgrn_kernel, chunk=chunk),
        grid=(bsz // nb, A_HEADS, t // tb),
        in_specs=[zspec(0), zspec(4), zspec(8), zspec(12),
                  pl.BlockSpec((1, 1, LANES), lambda b, h, i: (h, 0, 0)),
                  pl.BlockSpec((1, LANES), lambda b, h, i: (0, 0)),
                  pl.BlockSpec((nb, 1, LANES, LANES), lambda b, h, i: (b, h, 0, 0))],
        out_specs=[pl.BlockSpec((nb, tb, LANES), lambda b, h, i: (b, i, h)),
                   pl.BlockSpec((nb, 1, LANES, LANES), lambda b, h, i: (b, h, 0, 0))],
        out_shape=[jax.ShapeDtypeStruct((bsz, t, A_WIDTH), F32),
                   jax.ShapeDtypeStruct((bsz, A_HEADS, LANES, LANES), F32)],
        scratch_shapes=[pltpu.VMEM((nb, LANES, LANES), F32)],
        compiler_params=_cparams(("parallel", "parallel", "arbitrary")),
        name="hgrn_recurrence",
    )(z, z, z, z, lb.reshape(A_HEADS, 1, LANES), norm_g.reshape(1, LANES), s0_t)


def _gla_kernel(zq_ref, zk_ref, zv_ref, zg_ref, zlr_ref, w2_ref, gb_ref, ng_ref, s0_ref, o_ref, s_ref, st_scr,
                *, chunk):
    t = pl.program_id(2)

    @pl.when(t == 0)
    def _():
        st_scr[...] = s0_ref[:, 0]

    nseq = zq_ref.shape[0]
    nchunks = zq_ref.shape[1] // chunk
    m0 = _lane_mask(LANES, 0, B_DK)
    m1 = _lane_mask(LANES, B_DK, LANES)

    def body(ci, carry):
        sl = pl.ds(pl.multiple_of(ci * chunk, chunk), chunk)
        problems = []
        for i in range(nseq):
            pre = _dot(zlr_ref[i, sl, :], w2_ref[...], precision=HI) + gb_ref[...]
            log_g = (jnp.minimum(pre, 0.0) - jnp.log(1.0 + jnp.exp(-jnp.abs(pre)))) * (1.0 / GLA_NORMALIZER)
            v = zv_ref[i, sl, :]
            problems.append((zq_ref[i, sl, :] * (B_DK ** -0.5), zk_ref[i, sl, :], _cumsum_rows(log_g),
                             [(m0, v[:, :LANES]), (m1, v[:, LANES:])]))
        outs, states = _gated_chunk(problems, [st_scr[i] for i in range(nseq)])
        for i in range(nseq):
            st_scr[i] = states[i]
            gate = _silu(zg_ref[i, sl, :])
            o_ref[i, sl, 0:LANES] = _head_rms(outs[i][0], ng_ref[...]) * gate[:, :LANES]
            o_ref[i, sl, LANES:2 * LANES] = _head_rms(outs[i][1], ng_ref[...]) * gate[:, LANES:]
        return carry

    lax.fori_loop(0, nchunks, body, 0)

    @pl.when(t == pl.num_programs(2) - 1)
    def _():
        s_ref[:, 0] = st_scr[...]


def gla_recurrence(z, w2pad, gate_b, norm_g, s0_t, tb, chunk, nb):
    bsz, t, _ = z.shape
    npairs = B_HEADS // 2
    return pl.pallas_call(
        functools.partial(_gla_kernel, chunk=chunk),
        grid=(bsz // nb, npairs, t // tb),
        in_specs=[pl.BlockSpec((nb, tb, LANES), lambda b, p, i: (b, i, 16 + p)),
                  pl.BlockSpec((nb, tb, LANES), lambda b, p, i: (b, i, 18 + p)),
                  pl.BlockSpec((nb, tb, 2 * LANES), lambda b, p, i: (b, i, 10 + p)),
                  pl.BlockSpec((nb, tb, 2 * LANES), lambda b, p, i: (b, i, 12 + p)),
                  pl.BlockSpec((nb, tb, LANES), lambda b, p, i: (b, i, 28)),
                  pl.BlockSpec((LANES, LANES), lambda b, p, i: (0, p)),
                  pl.BlockSpec((1, LANES), lambda b, p, i: (0, p)),
                  pl.BlockSpec((1, LANES), lambda b, p, i: (0, 0)),
                  pl.BlockSpec((nb, 1, LANES, LANES), lambda b, p, i: (b, p, 0, 0))],
        out_specs=[pl.BlockSpec((nb, tb, 2 * LANES), lambda b, p, i: (b, i, p)),
                   pl.BlockSpec((nb, 1, LANES, LANES), lambda b, p, i: (b, p, 0, 0))],
        out_shape=[jax.ShapeDtypeStruct((bsz, t, B_WIDTH), F32),
                   jax.ShapeDtypeStruct((bsz, npairs, LANES, LANES), F32)],
        scratch_shapes=[pltpu.VMEM((nb, LANES, LANES), F32)],
        compiler_params=_cparams(("parallel", "parallel", "arbitrary")),
        name="gla_recurrence",
    )(z, z, z, z, z, w2pad, gate_b.reshape(1, 2 * LANES), norm_g.reshape(1, LANES), s0_t)


def _rwkv_proj_kernel(x_ref, xp_ref, xl_ref, g1_ref, mu_ref, wr_ref, wk_ref, wv_ref, ww1_ref, ww2_ref, w0_ref,
                      aw1_ref, aw2_ref, a0_ref, gw1_ref, gw2_ref, kk_ref, ka_ref,
                      r_out, k_out, v_out, lw_out, a_out, kk_out, g_out, hl_out, *, tiles_per_seq):
    i = pl.program_id(0)
    g1 = g1_ref[...]
    hn = _rms(x_ref[...], g1)
    tm = hn.shape[0]
    prev = _rms(xp_ref[...], g1)[SUBLANES - 1:SUBLANES, :]
    prev = jnp.where(i % tiles_per_seq == 0, xl_ref[0], prev)
    row = lax.broadcasted_iota(I32, (tm, 1), 0)
    xprev = jnp.where(row == 0, prev, pltpu.roll(hn, 1, axis=0))
    dx = xprev - hn

    def mix(j):
        return hn + dx * mu_ref[j:j + 1, :]

    r = _dot(mix(0).astype(BF16), wr_ref[...])
    k = _dot(mix(1).astype(BF16), wk_ref[...])
    v = _dot(mix(2).astype(BF16), wv_ref[...])
    wl = _dot(jnp.tanh(_dot(mix(3), ww1_ref[...], precision=HI)), ww2_ref[...], precision=HI)
    z = w0_ref[...] + wl
    wpre = -(jnp.maximum(-z, 0.0) + jnp.log(1.0 + jnp.exp(-jnp.abs(z)))) - 0.5
    al = _dot(_dot(mix(4), aw1_ref[...], precision=HI), aw2_ref[...], precision=HI)
    a = _sigmoid(a0_ref[...] + al)
    gg = _dot(_sigmoid(_dot(mix(5).astype(BF16), gw1_ref[...])).astype(BF16), gw2_ref[...])
    r_out[...] = r
    k_out[...] = k * (1.0 + (a - 1.0) * ka_ref[...])
    v_out[...] = v
    lw_out[...] = -jnp.exp(wpre)
    a_out[...] = a
    kk_out[...] = k * kk_ref[...]
    g_out[...] = gg
    hl_out[0] = hn[tm - 1:tm, :]


def rwkv_proj(x, x_last, seq_len, g1, mu, wr, wk, wv, ww1, ww2, w0, aw1, aw2, a0, gw1, gw2, k_k, k_a, tm):
    n, d = x.shape
    tiles_per_seq = seq_len // tm
    row = lambda a: a.reshape(1, d)
    full = lambda a: pl.BlockSpec(a.shape, lambda i: (0,) * a.ndim)
    tile = pl.BlockSpec((tm, d), lambda i: (i, 0))
    blocks8 = tm // SUBLANES
    args = (x, x, x_last.reshape(-1, 1, d), row(g1), mu, wr, wk, wv, ww1, ww2, row(w0), aw1, aw2, row(a0),
            gw1, gw2, row(k_k), row(k_a))
    in_specs = [tile,
                pl.BlockSpec((SUBLANES, d), lambda i: (jnp.maximum(i * blocks8 - 1, 0), 0)),
                pl.BlockSpec((1, 1, d), lambda i: (i // tiles_per_seq, 0, 0))]
    in_specs += [full(a) for a in args[3:]]
    outs = pl.pallas_call(
        functools.partial(_rwkv_proj_kernel, tiles_per_seq=tiles_per_seq),
        grid=(n // tm,),
        in_specs=in_specs,
        out_specs=[tile] * 7 + [pl.BlockSpec((1, 1, d), lambda i: (i, 0, 0))],
        out_shape=[jax.ShapeDtypeStruct((n, d), F32)] * 7 + [jax.ShapeDtypeStruct((n // tm, 1, d), F32)],
        compiler_params=_cparams(("parallel",)),
        name="rwkv_proj",
    )(*args)
    return outs


_NN = ((1,), (0,))
_NT = ((1,), (1,))
_TN = ((0,), (0,))
RWKV_AB_PASSES = 1
RWKV_INV_PASSES = 1
RWKV_APPLY_PASSES = 1
RWKV_STATE_PASSES = 3
RWKV_SEQS_PER_STEP = 4


def _split_bf16(a):
    hi = a.astype(BF16)
    return hi, (a - hi.astype(F32)).astype(BF16)


def _mm(a, b, dims, passes):
    if passes == 6:
        return lax.dot_general(a, b, (dims, ((), ())), preferred_element_type=F32, precision=HI)
    dg = lambda x, y: lax.dot_general(x, y, (dims, ((), ())), preferred_element_type=F32)
    ah, al = _split_bf16(a)
    bh, bl = _split_bf16(b)
    if passes == 1:
        return dg(ah, bh)
    return dg(ah, bh) + (dg(al, bh) + dg(ah, bl))


def _cumsum_rows3(g):
    tri = _tri(g.shape[0], False).astype(BF16)
    h1 = g.astype(BF16)
    r1 = g - h1.astype(F32)
    h2 = r1.astype(BF16)
    h3 = (r1 - h2.astype(F32)).astype(BF16)
    return _dot(tri, h1) + (_dot(tri, h2) + _dot(tri, h3))


def _pair_sum(x, m0):
    s0 = jnp.sum(jnp.where(m0, x, 0.0), axis=-1, keepdims=True)
    s1 = jnp.sum(jnp.where(m0, 0.0, x), axis=-1, keepdims=True)
    return jnp.where(m0, s0, s1)


def _rwkv_kernel(r_ref, k_ref, v_ref, lw_ref, a_ref, kk_ref, g_ref, rk_ref, lng_ref, lnb_ref, s0_ref,
                 o_ref, s_ref, mt_scr, *, chunk):
    t = pl.program_id(2)

    @pl.when(t == 0)
    def _():
        mt_scr[...] = s0_ref[:, 0]

    nrows = r_ref.shape[0]
    nchunks = r_ref.shape[1] // chunk
    c2 = 2 * chunk
    m0 = _lane_mask(LANES, 0, C_HEAD)
    rowi = lax.broadcasted_iota(I32, (LANES, LANES), 0)
    coli = lax.broadcasted_iota(I32, (LANES, LANES), 1)
    blockdiag = (rowi < C_HEAD) == (coli < C_HEAD)
    ti = lax.broadcasted_iota(I32, (c2, c2), 0)
    si = lax.broadcasted_iota(I32, (c2, c2), 1)
    same_head = (ti < chunk) == (si < chunk)
    tm_ = jnp.where(ti < chunk, ti, ti - chunk)
    sm_ = jnp.where(si < chunk, si, si - chunk)
    strict = same_head & (sm_ < tm_)
    incl = same_head & (sm_ <= tm_)

    def stack_heads(x):
        return jnp.concatenate([jnp.where(m0, x, 0.0), jnp.where(m0, 0.0, x)], axis=0)

    def twice(x):
        return jnp.concatenate([x, x], axis=0)

    def unstack(x2):
        return jnp.where(m0, x2[:chunk], x2[chunk:])

    eye = (ti == si).astype(F32)
    seqs = range(nrows)

    def body(ci, carry):
        sl = pl.ds(pl.multiple_of(ci * chunk, chunk), chunk)
        r = [r_ref[i, sl, :] for i in seqs]
        k = [k_ref[i, sl, :] for i in seqs]
        v = [v_ref[i, sl, :] for i in seqs]
        lw = [lw_ref[i, sl, :] for i in seqs]
        kkr = [kk_ref[i, sl, :] for i in seqs]
        kk = [x * lax.rsqrt(_pair_sum(x * x, m0) + 1e-12) for x in kkr]
        al = [a_ref[i, sl, :] * kk[i] for i in seqs]
        gam = [_cumsum_rows3(x) for x in lw]
        e_neg = [jnp.exp(-x) for x in gam]
        xr = [jnp.concatenate([stack_heads(kk[i] * jnp.exp(gam[i] - lw[i])),
                               stack_heads(r[i] * jnp.exp(gam[i]))], axis=0) for i in seqs]
        alk = [jnp.concatenate([twice(al[i] * e_neg[i]), twice(k[i] * e_neg[i])], axis=0) for i in seqs]
        mt = [mt_scr[i] for i in seqs]
        ab = [_mm(xr[i], alk[i], _NT, RWKV_AB_PASSES) for i in seqs]
        xm = [_mm(xr[i], mt[i], _NT, RWKV_STATE_PASSES) for i in seqs]
        a_al = [jnp.where(strict, x[:c2, :c2], 0.0) for x in ab]
        a_k = [jnp.where(strict, x[:c2, c2:], 0.0) for x in ab]
        b_alk = [jnp.concatenate([jnp.where(incl, x[c2:, c2:], 0.0), jnp.where(incl, -x[c2:, :c2], 0.0)], axis=1)
                 for x in ab]
        v2 = [twice(x) for x in v]
        rhs = [xm[i][:c2] + _mm(a_k[i], v2[i], _NN, RWKV_APPLY_PASSES) for i in seqs]
        p = [-x for x in a_al]
        tinv = [eye + x for x in p]
        span = 2
        while span < chunk:
            p = [_mm(x, x, _NN, RWKV_INV_PASSES) for x in p]
            tinv = [tinv[i] + _mm(tinv[i], p[i], _NN, RWKV_INV_PASSES) for i in seqs]
            span *= 2
        u = [unstack(_mm(tinv[i], rhs[i], _NN, RWKV_APPLY_PASSES)) for i in seqs]
        o = [unstack(xm[i][c2:] + _mm(b_alk[i], jnp.concatenate([v2[i], twice(u[i])], axis=0), _NN,
                                      RWKV_APPLY_PASSES)) for i in seqs]
        g_last = [x[-1:, :] for x in gam]
        e_end = [jnp.exp(g_last[i] - gam[i]) for i in seqs]
        upd = [_mm(jnp.concatenate([v[i], u[i]], axis=0),
                   jnp.concatenate([k[i] * e_end[i], -(al[i] * e_end[i])], axis=0), _TN, RWKV_STATE_PASSES)
               for i in seqs]
        for i in seqs:
            mt_scr[i] = mt[i] * jnp.exp(g_last[i]) + jnp.where(blockdiag, upd[i], 0.0)
        for i in seqs:
            mean = _pair_sum(o[i], m0) * (1.0 / C_HEAD)
            cen = o[i] - mean
            var = _pair_sum(cen * cen, m0) * (1.0 / C_HEAD)
            on = cen * lax.rsqrt(var + C_GN_EPS) * lng_ref[...] + lnb_ref[...]
            bonus = _pair_sum(r[i] * k[i] * rk_ref[...], m0) * v[i]
            o_ref[i, sl, :] = (on + bonus) * g_ref[i, sl, :]
        return carry

    lax.fori_loop(0, nchunks, body, 0)

    @pl.when(t == pl.num_programs(2) - 1)
    def _():
        s_ref[:, 0] = mt_scr[...]


def rwkv_recurrence(r, k, v, lw, a, kk, g, r_k, ln_g, ln_b, s0_bd, tb, chunk, nb):
    bsz, t, d = r.shape
    npairs = C_HEADS // 2
    seq = pl.BlockSpec((nb, tb, LANES), lambda b, p, i: (b, i, p))
    vec = pl.BlockSpec((1, LANES), lambda b, p, i: (0, p))
    st = pl.BlockSpec((nb, 1, LANES, LANES), lambda b, p, i: (b, p, 0, 0))
    return pl.pallas_call(
        functools.partial(_rwkv_kernel, chunk=chunk),
        grid=(bsz // nb, npairs, t // tb),
        in_specs=[seq] * 7 + [vec, vec, vec, st],
        out_specs=[seq, st],
        out_shape=[jax.ShapeDtypeStruct((bsz, t, d), F32),
                   jax.ShapeDtypeStruct((bsz, npairs, LANES, LANES), F32)],
        scratch_shapes=[pltpu.VMEM((nb, LANES, LANES), F32)],
        compiler_params=_cparams(("parallel", "parallel", "arbitrary")),
        name="rwkv_recurrence",
    )(r, k, v, lw, a, kk, g, r_k.reshape(1, d), ln_g.reshape(1, d), ln_b.reshape(1, d), s0_bd)


NEG_INF = float("-inf")


def _top16_rows(s):
    n = s.shape[0]
    key = lax.broadcasted_iota(I32, s.shape, 0)
    vals, idxs = [], []
    for _ in range(P_TOPK):
        m = jnp.max(s, axis=0, keepdims=True)
        am = jnp.min(jnp.where(s == m, key, n), axis=0, keepdims=True)
        vals.append(m)
        idxs.append(am)
        s = jnp.where(key == am, NEG_INF, s)
    return vals, idxs


def _top16_pairs(v0, i0, v1, i1):
    a0 = jnp.concatenate(v0[0:8], axis=0)
    a1 = jnp.concatenate(v0[8:16], axis=0)
    b0 = jnp.concatenate(v1[0:8], axis=0)
    b1 = jnp.concatenate(v1[8:16], axis=0)
    ia0 = jnp.concatenate(i0[0:8], axis=0) * P_KEYS
    ia1 = jnp.concatenate(i0[8:16], axis=0) * P_KEYS
    ib0 = jnp.concatenate(i1[0:8], axis=0)
    ib1 = jnp.concatenate(i1[8:16], axis=0)
    row = lax.broadcasted_iota(I32, (SUBLANES, 1), 0)
    slabs = []

    def add(val, eid, keep):
        slabs.append((val if keep is None else jnp.where(keep, val, NEG_INF), eid))

    add(v0[0] + b0, ia0[0:1] + ib0, None)
    add(v0[0] + b1, ia0[0:1] + ib1, None)
    add(v0[1] + b0, ia0[1:2] + ib0, None)
    add(v0[2] + b0, ia0[2:3] + ib0, row < 5)
    add(v0[3] + b0, ia0[3:4] + ib0, row < 4)
    add(a0 + v1[0], ia0 + ib0[0:1], row >= 4)
    add(a1 + v1[0], ia1 + ib0[0:1], None)
    add(a0 + v1[1], ia0 + ib0[1:2], row >= 4)
    add(a0 + v1[2], ia0 + ib0[2:3], row == 4)

    big = P_KEYS * P_KEYS
    out_v, out_e = [], []
    for _ in range(P_TOPK):
        m = slabs[0][0]
        for val, _e in slabs[1:]:
            m = jnp.maximum(m, val)
        m = jnp.max(m, axis=0, keepdims=True)
        e = None
        for val, eid in slabs:
            c = jnp.where(val == m, eid, big)
            e = c if e is None else jnp.minimum(e, c)
        e = jnp.min(e, axis=0, keepdims=True)
        out_v.append(m)
        out_e.append(e)
        slabs = [(jnp.where(eid == e, NEG_INF, val), eid) for val, eid in slabs]
    return out_v, out_e


def _peer_select_kernel(x_ref, g_ref, wq_ref, keys_ref, xn_out, eid_out, gate_out):
    hn = _rms(x_ref[...], g_ref[...])
    xn_out[...] = hn
    q = _dot(hn.astype(BF16), wq_ref[...])
    tm = q.shape[0]
    for lt in range(tm // LANES):
        rows = slice(lt * LANES, (lt + 1) * LANES)
        e_rows, g_rows = [], []
        for h in range(P_HEADS):
            tops = []
            for p in range(2):
                hp = 2 * h + p
                s = _dot_nt(keys_ref[hp], q[rows, hp * LANES:(hp + 1) * LANES], precision=HI)
                tops.append(_top16_rows(s))
            cs, ce = _top16_pairs(tops[0][0], tops[0][1], tops[1][0], tops[1][1])
            ex = [jnp.exp(c - cs[0]) for c in cs]
            tot = ex[0]
            for e in ex[1:]:
                tot = tot + e
            inv = 1.0 / tot
            e_rows += ce
            g_rows += [e * inv for e in ex]
        eid_out[rows, :] = jnp.concatenate(e_rows, axis=0).T
        gate_out[rows, :] = jnp.concatenate(g_rows, axis=0).T


def peer_select(x, g, wq_bf16, keys, tm):
    n, d = x.shape
    return pl.pallas_call(
        _peer_select_kernel,
        grid=(n // tm,),
        in_specs=[pl.BlockSpec((tm, d), lambda i: (i, 0)),
                  pl.BlockSpec((1, d), lambda i: (0, 0)),
                  pl.BlockSpec(wq_bf16.shape, lambda i: (0, 0)),
                  pl.BlockSpec(keys.shape, lambda i: (0, 0, 0))],
        out_specs=[pl.BlockSpec((tm, d), lambda i: (i, 0)),
                   pl.BlockSpec((tm, P_PICKS), lambda i: (i, 0)),
                   pl.BlockSpec((tm, P_PICKS), lambda i: (i, 0))],
        out_shape=[jax.ShapeDtypeStruct((n, d), F32),
                   jax.ShapeDtypeStruct((n, P_PICKS), I32),
                   jax.ShapeDtypeStruct((n, P_PICKS), F32)],
        compiler_params=_cparams(("parallel",)),
        name="peer_select",
    )(x, g.reshape(1, d), wq_bf16, keys)


def _peer_act_kernel(h_ref, g_ref, o_ref):
    h = h_ref[...]
    o_ref[...] = 0.5 * h * (1.0 + lax.erf(h * (2.0 ** -0.5))) * g_ref[...]


def peer_act(hid, gate, tm):
    n, p = hid.shape
    spec = pl.BlockSpec((tm, p), lambda i: (i, 0))
    return pl.pallas_call(
        _peer_act_kernel, grid=(n // tm,), in_specs=[spec, spec], out_specs=spec,
        out_shape=jax.ShapeDtypeStruct((n, p), F32),
        compiler_params=_cparams(("parallel",)), name="peer_act",
    )(hid, gate)


_CHUNKS_PER_TOK = P_PICKS // SC_CHUNK
_WORD_VREGS = SC_WORDS // SC_LANES
U32 = jnp.uint32


def pack_rows_bf16(t):
    lo = lax.bitcast_convert_type(t[..., :SC_WORDS].astype(BF16), jnp.uint16).astype(U32)
    a = lax.bitcast_convert_type(t[..., SC_WORDS:], U32)
    sign = a & U32(0x80000000)
    mag = a & U32(0x7FFFFFFF)
    steps = (jnp.maximum(mag + U32(0x8000), lo) - lo) >> 16
    near = (steps << 16) + lo
    near = jnp.where(near >= U32(0x7F800000), near - U32(0x10000), near)
    return lax.bitcast_convert_type(sign | near, I32)


def _low_f32(w):
    return lax.bitcast_convert_type(w << 16, F32)


def _high_f32(w):
    return lax.bitcast_convert_type(w, F32)


def _sc_worker():
    return lax.axis_index("s") * SC_CORES + lax.axis_index("c")


def _sc_tok_block(tok_per_w):
    return next(b for b in SC_TOK_BLOCKS if tok_per_w % b == 0)


def _sc_hid_body(tab_hbm, idx_hbm, x_hbm, hid_hbm, idx_v, x_v, hid_v, rows_a, rows_b, sem_a, sem_b, *, tok_per_w):
    wid = _sc_worker()
    blk = x_v.shape[0]
    nblk = tok_per_w // blk
    nch = blk * _CHUNKS_PER_TOK
    lane = lax.iota(I32, SC_LANES)
    quads = SC_LANES // SC_HID_ROWS

    def dots(rows, x_row, res_ref, col0):
        for grp in range(SC_CHUNK // SC_LANES):
            def some_rows(q, res):
                r0 = grp * SC_LANES + q * SC_HID_ROWS

                def some_words(jj, accs):
                    accs = list(accs)
                    for jw in range(SC_HID_WORDS):
                        off = (jj * SC_HID_WORDS + jw) * SC_LANES
                        xl = x_v[x_row, pl.ds(off, SC_LANES)]
                        xh = x_v[x_row, pl.ds(SC_WORDS + off, SC_LANES)]
                        for i in range(SC_HID_ROWS):
                            w = rows[r0 + i, pl.ds(off, SC_LANES)]
                            k = 2 * i + jw % 2
                            accs[k] = accs[k] + (_low_f32(w) * xl + _high_f32(w) * xh)
                    return tuple(accs)

                zero = jnp.zeros((SC_LANES,), F32)
                accs = lax.fori_loop(0, _WORD_VREGS // SC_HID_WORDS, some_words, (zero,) * (2 * SC_HID_ROWS))
                for i in range(SC_HID_ROWS):
                    res = jnp.where(lane == q * SC_HID_ROWS + i, jnp.sum(accs[2 * i] + accs[2 * i + 1]), res)
                return res
            res = lax.fori_loop(0, quads, some_rows, jnp.zeros((SC_LANES,), F32))
            res_ref[x_row, pl.ds(col0 + grp * SC_LANES, SC_LANES)] = res

    def block(bi, carry):
        tok0 = wid * tok_per_w + bi * blk
        pltpu.sync_copy(idx_hbm.at[pl.ds(tok0 * _CHUNKS_PER_TOK, nch)], idx_v)
        pltpu.async_copy(tab_hbm.at[idx_v.at[0]], rows_a, sem_a)
        pltpu.sync_copy(x_hbm.at[pl.ds(tok0, blk)], x_v)

        def pair(j, c):
            ca = 2 * j
            pltpu.async_copy(tab_hbm.at[idx_v.at[ca + 1]], rows_b, sem_b)
            pltpu.make_async_copy(tab_hbm.at[idx_v.at[0]], rows_a, sem_a).wait()
            dots(rows_a, ca // _CHUNKS_PER_TOK, hid_v, (ca % _CHUNKS_PER_TOK) * SC_CHUNK)

            @pl.when(j < nch // 2 - 1)
            def _():
                pltpu.async_copy(tab_hbm.at[idx_v.at[ca + 2]], rows_a, sem_a)
            pltpu.make_async_copy(tab_hbm.at[idx_v.at[0]], rows_b, sem_b).wait()
            cb = ca + 1
            dots(rows_b, cb // _CHUNKS_PER_TOK, hid_v, (cb % _CHUNKS_PER_TOK) * SC_CHUNK)
            return c
        lax.fori_loop(0, nch // 2, pair, 0)
        pltpu.sync_copy(hid_v, hid_hbm.at[pl.ds(tok0, blk)])
        return carry

    lax.fori_loop(0, nblk, block, 0)


def sc_expert_hidden(table_packed, eidx, xn):
    n = xn.shape[0]
    tok_per_w = n // SC_WORKERS
    blk = _sc_tok_block(tok_per_w)
    mesh = plsc.VectorSubcoreMesh(core_axis_name="c", subcore_axis_name="s")
    nch = blk * _CHUNKS_PER_TOK
    k = pl.kernel(
        functools.partial(_sc_hid_body, tok_per_w=tok_per_w), mesh=mesh,
        out_type=jax.ShapeDtypeStruct((n, P_PICKS), F32),
        scratch_types=[pltpu.VMEM((nch, SC_CHUNK), I32),
                       pltpu.VMEM((blk, D_MODEL), F32),
                       pltpu.VMEM((blk, P_PICKS), F32),
                       pltpu.VMEM((SC_CHUNK, SC_WORDS), I32),
                       pltpu.VMEM((SC_CHUNK, SC_WORDS), I32),
                       pltpu.SemaphoreType.DMA, pltpu.SemaphoreType.DMA],
        compiler_params=pltpu.CompilerParams(needs_layout_passes=False),
        name="sc_expert_hidden",
    )
    return k(table_packed, eidx.reshape(n * _CHUNKS_PER_TOK, SC_CHUNK), xn)


def _sc_out_body(tab_hbm, idx_hbm, act_hbm, x_hbm, y_hbm, idx_v, act_v, y_v, rows_a, rows_b, sem_a, sem_b,
                 *, tok_per_w):
    wid = _sc_worker()
    blk = y_v.shape[0]
    nblk = tok_per_w // blk
    nch = blk * _CHUNKS_PER_TOK
    half_w = _WORD_VREGS // 2

    def accum(rows, tok, col0):
        for hv in range(2):
            base = hv * half_w * SC_LANES
            acc0 = (tuple(y_v[tok, pl.ds(base + j * SC_LANES, SC_LANES)] for j in range(half_w))
                    + tuple(y_v[tok, pl.ds(SC_WORDS + base + j * SC_LANES, SC_LANES)] for j in range(half_w)))

            def one(r, acc):
                aidx = jnp.full((SC_LANES,), col0, I32) + r
                wgt = plsc.load_gather(act_v, [jnp.full((SC_LANES,), tok, I32), aidx])
                lo, hi = [], []
                for j in range(half_w):
                    w = rows[r, pl.ds(base + j * SC_LANES, SC_LANES)]
                    lo.append(acc[j] + wgt * _low_f32(w))
                    hi.append(acc[half_w + j] + wgt * _high_f32(w))
                return tuple(lo + hi)
            acc = lax.fori_loop(0, SC_CHUNK, one, acc0)
            for j in range(half_w):
                y_v[tok, pl.ds(base + j * SC_LANES, SC_LANES)] = acc[j]
                y_v[tok, pl.ds(SC_WORDS + base + j * SC_LANES, SC_LANES)] = acc[half_w + j]

    def block(bi, carry):
        tok0 = wid * tok_per_w + bi * blk
        pltpu.sync_copy(idx_hbm.at[pl.ds(tok0 * _CHUNKS_PER_TOK, nch)], idx_v)
        pltpu.async_copy(tab_hbm.at[idx_v.at[0]], rows_a, sem_a)
        pltpu.sync_copy(act_hbm.at[pl.ds(tok0, blk)], act_v)
        pltpu.sync_copy(x_hbm.at[pl.ds(tok0, blk)], y_v)

        def pair(j, c):
            ca = 2 * j
            pltpu.async_copy(tab_hbm.at[idx_v.at[ca + 1]], rows_b, sem_b)
            pltpu.make_async_copy(tab_hbm.at[idx_v.at[0]], rows_a, sem_a).wait()
            accum(rows_a, ca // _CHUNKS_PER_TOK, (ca % _CHUNKS_PER_TOK) * SC_CHUNK)

            @pl.when(j < nch // 2 - 1)
            def _():
                pltpu.async_copy(tab_hbm.at[idx_v.at[ca + 2]], rows_a, sem_a)
            pltpu.make_async_copy(tab_hbm.at[idx_v.at[0]], rows_b, sem_b).wait()
            cb = ca + 1
            accum(rows_b, cb // _CHUNKS_PER_TOK, (cb % _CHUNKS_PER_TOK) * SC_CHUNK)
            return c
        lax.fori_loop(0, nch // 2, pair, 0)
        pltpu.sync_copy(y_v, y_hbm.at[pl.ds(tok0, blk)])
        return carry

    lax.fori_loop(0, nblk, block, 0)


def sc_expert_output(table_packed, eidx, act, x):
    n = x.shape[0]
    tok_per_w = n // SC_WORKERS
    blk = _sc_tok_block(tok_per_w)
    mesh = plsc.VectorSubcoreMesh(core_axis_name="c", subcore_axis_name="s")
    nch = blk * _CHUNKS_PER_TOK
    k = pl.kernel(
        functools.partial(_sc_out_body, tok_per_w=tok_per_w), mesh=mesh,
        out_type=jax.ShapeDtypeStruct((n, D_MODEL), F32),
        scratch_types=[pltpu.VMEM((nch, SC_CHUNK), I32),
                       pltpu.VMEM((blk, P_PICKS), F32),
                       pltpu.VMEM((blk, D_MODEL), F32),
                       pltpu.VMEM((SC_CHUNK, SC_WORDS), I32),
                       pltpu.VMEM((SC_CHUNK, SC_WORDS), I32),
                       pltpu.SemaphoreType.DMA, pltpu.SemaphoreType.DMA],
        compiler_params=pltpu.CompilerParams(needs_layout_passes=False),
        name="sc_expert_output",
    )
    return k(table_packed, eidx.reshape(n * _CHUNKS_PER_TOK, SC_CHUNK), act, x)


def peer_ffn(x, g, wq_bf16, keys, u_packed, v_packed, tm):
    xn, eidx, gate = peer_select(x, g, wq_bf16, keys, tm)
    hid = sc_expert_hidden(u_packed, eidx, xn)
    act = peer_act(hid, gate, tm)
    return sc_expert_output(v_packed, eidx, act, x)


def _prep_w_in(w_in):
    main = jnp.concatenate([w_in[:, :3072], w_in[:, 3088:3600]], axis=1)
    lr = jnp.pad(w_in[:, 3072:3088], ((0, 0), (0, LANES - GLA_RANK)))
    return jnp.concatenate([main, lr], axis=1).astype(BF16)


def _trunk(x, st_h, st_g, st_r, st_s, w, seq_len, tm, tmp, tb, chunk):
    bsz = x.shape[0]
    n = bsz * seq_len
    x2 = x.reshape(n, D_MODEL)

    z = norm_proj(x2, w["norm1_g"][0], w["w_in"], tm).reshape(bsz, seq_len, Z_WIDTH)
    s0_h = jnp.swapaxes(st_h, -1, -2)
    s0_g = jnp.swapaxes(st_g.reshape(bsz, 2, 2 * B_DK, LANES), -1, -2)
    o_a, sh_t = hgrn_recurrence(z, w["lb0"], w["hgrn_norm_g"], s0_h, tb, chunk, GATED_SEQS_PER_STEP)
    o_b, sg_t = gla_recurrence(z, w["gla_w2"], w["gla_b"], w["gla_norm_g"], s0_g, tb, chunk, GATED_SEQS_PER_STEP)
    new_h = jnp.swapaxes(sh_t, -1, -2)
    new_g = jnp.swapaxes(sg_t, -1, -2).reshape(bsz, B_HEADS, B_DK, LANES)
    x2 = out_proj2(x2, o_a.reshape(n, A_WIDTH), o_b.reshape(n, B_WIDTH), w["w_out_a"], w["w_out_b"], tm)
    x2 = peer_ffn(x2, w["norm2_g"][0], w["peer_wq"][0], w["peer_keys"][0], w["peer_u"][0], w["peer_v"][0], tmp)

    r, k, v, lw, a, kk, g, hl = rwkv_proj(
        x2, st_s, seq_len, w["norm1_g"][1], w["mu"], w["wr"], w["wk"], w["wv"], w["w_w1"], w["w_w2"], w["w0"],
        w["a_w1"], w["a_w2"], w["a0"], w["g_w1"], w["g_w2"], w["k_k"], w["k_a"], tm)
    new_s = hl.reshape(bsz, seq_len // tm, D_MODEL)[:, -1]
    pr = st_r.reshape(bsz, C_HEADS // 2, 2, C_HEAD, C_HEAD)
    zero = jnp.zeros_like(pr[:, :, 0])
    s0_r = jnp.concatenate([jnp.concatenate([pr[:, :, 0], zero], axis=-1),
                            jnp.concatenate([zero, pr[:, :, 1]], axis=-1)], axis=-2)
    sh3 = lambda t: t.reshape(bsz, seq_len, D_MODEL)
    o_c, sr_bd = rwkv_recurrence(sh3(r), sh3(k), sh3(v), sh3(lw), sh3(a), sh3(kk), sh3(g),
                                 w["r_k"], w["ln_g"], w["ln_b"], s0_r, tb, chunk, min(RWKV_SEQS_PER_STEP, bsz))
    new_r = jnp.stack([sr_bd[:, :, :C_HEAD, :C_HEAD], sr_bd[:, :, C_HEAD:, C_HEAD:]], axis=2)
    new_r = new_r.reshape(bsz, C_HEADS, C_HEAD, C_HEAD)
    x2 = out_proj1(x2, o_c.reshape(n, D_MODEL), w["w_out_c"], tm)
    x2 = peer_ffn(x2, w["norm2_g"][1], w["peer_wq"][1], w["peer_keys"][1], w["peer_u"][1], w["peer_v"][1], tmp)

    y = final_norm(x2, w["final_g"], tm).reshape(bsz, seq_len, D_MODEL)
    return y, new_h[None], new_g[None], new_r[None], new_s[None]


def kernel(x_prompt, x_sample, state_hgrn, state_gla, state_rwkv, state_shift, w_in_ab, hgrn_lower_bounds, hgrn_norm_g, gla_gate_w2, gla_gate_b, gla_norm_g, w_out_ab, rwkv_mu, rwkv_w_rkv, rwkv_w_w1, rwkv_w_w2, rwkv_w0, rwkv_a_w1, rwkv_a_w2, rwkv_a0, rwkv_g_w1, rwkv_g_w2, rwkv_k_k, rwkv_k_a, rwkv_r_k, rwkv_ln_g, rwkv_ln_b, w_out_c, norm1_g, norm2_g, final_g, peer_w_q, peer_sub_keys, peer_u, peer_v):
    lbs = jnp.cumsum(jax.nn.softmax(hgrn_lower_bounds.astype(F32), axis=0), axis=0)
    w = dict(
        norm1_g=norm1_g, norm2_g=norm2_g, final_g=final_g,
        w_in=_prep_w_in(w_in_ab[0]), lb0=lbs[0], hgrn_norm_g=hgrn_norm_g[0],
        gla_w2=jnp.pad(gla_gate_w2[0], ((0, LANES - GLA_RANK), (0, 0))), gla_b=gla_gate_b[0],
        gla_norm_g=gla_norm_g[0],
        w_out_a=w_out_ab[0, :A_WIDTH].astype(BF16), w_out_b=w_out_ab[0, A_WIDTH:].astype(BF16),
        mu=rwkv_mu[0], wr=rwkv_w_rkv[0, 0].astype(BF16), wk=rwkv_w_rkv[0, 1].astype(BF16),
        wv=rwkv_w_rkv[0, 2].astype(BF16), w_w1=rwkv_w_w1[0], w_w2=rwkv_w_w2[0], w0=rwkv_w0[0],
        a_w1=rwkv_a_w1[0], a_w2=rwkv_a_w2[0], a0=rwkv_a0[0],
        g_w1=rwkv_g_w1[0].astype(BF16), g_w2=rwkv_g_w2[0].astype(BF16),
        k_k=rwkv_k_k[0], k_a=rwkv_k_a[0], r_k=rwkv_r_k[0], ln_g=rwkv_ln_g[0], ln_b=rwkv_ln_b[0],
        w_out_c=w_out_c[0].astype(BF16),
        peer_wq=peer_w_q.astype(BF16),
        peer_keys=peer_sub_keys.reshape(peer_sub_keys.shape[0], 2 * P_HEADS, P_KEYS, P_KEYS),
        peer_u=pack_rows_bf16(peer_u), peer_v=pack_rows_bf16(peer_v),
    )
    bp, tp, _ = x_prompt.shape
    bs, ts, _ = x_sample.shape
    assert sum(PROMPT_GROUPS) == bp
    groups, start = [], 0
    for gsz in PROMPT_GROUPS:
        zeros = lambda s: jnp.zeros((gsz,) + s.shape[2:], F32)
        groups.append(_trunk(x_prompt[start:start + gsz], zeros(state_hgrn), zeros(state_gla), zeros(state_rwkv),
                             zeros(state_shift), w, tp, tm=256, tmp=256, tb=512, chunk=64))
        start += gsz
    y_p = jnp.concatenate([g[0] for g in groups], axis=0)
    p_h, p_g, p_r, p_s = (jnp.concatenate([g[j] for g in groups], axis=1) for j in range(1, 5))
    y_s, s_h, s_g, s_r, s_s = _trunk(x_sample, state_hgrn[0], state_gla[0], state_rwkv[0], state_shift[0],
                                     w, ts, tm=32, tmp=128, tb=32, chunk=32)
    return (y_p, y_s, p_h, p_g, p_r, p_s, s_h, s_g, s_r, s_s)
```

```python
import functools

import jax
import jax.numpy as jnp
from jax import lax
from jax.experimental import pallas as pl
from jax.experimental.pallas import tpu as pltpu
from jax.experimental.pallas import tpu_sc as plsc

F32 = jnp.float32
BF16 = jnp.bfloat16
I32 = jnp.int32
HI = lax.Precision.HIGHEST

D_MODEL = 1024
NORM_EPS = 1e-6
LANES = 128
SUBLANES = 8
VMEM_LIMIT = 56 * 1024 * 1024

A_WIDTH = 512
A_HEADS = 4
B_WIDTH = 512
B_HEADS = 4
B_DK = 64
GLA_RANK = 16
GLA_NORMALIZER = 16.0
Z_WIDTH = 3712
C_HEAD = 64
C_HEADS = 16
C_GN_EPS = 64e-5
P_HEADS = 8
P_KEYS = 128
P_TOPK = 16
P_PICKS = P_HEADS * P_TOPK
SC_CORES = 2
SC_SUBCORES = 16
SC_WORKERS = SC_CORES * SC_SUBCORES
SC_LANES = 16
SC_CHUNK = 64
SC_TOK_BLOCKS = (32, 16)
SC_WORDS = D_MODEL // 2
SC_HID_ROWS = 8
SC_HID_WORDS = 2
PROMPT_GROUPS = (2,) * 8
GATED_SEQS_PER_STEP = 2


def _cparams(sem):
    return pltpu.CompilerParams(dimension_semantics=sem, vmem_limit_bytes=VMEM_LIMIT)


def _rms(x, g):
    ms = jnp.mean(x * x, axis=-1, keepdims=True)
    return x * lax.rsqrt(ms + NORM_EPS) * g


def _dot(a, b, precision=None):
    return jnp.dot(a, b, preferred_element_type=F32, precision=precision)


def _dot_nt(a, b, precision=None):
    return lax.dot_general(a, b, (((1,), (1,)), ((), ())), preferred_element_type=F32, precision=precision)


def _dot_tn(a, b, precision=None):
    return lax.dot_general(a, b, (((0,), (0,)), ((), ())), preferred_element_type=F32, precision=precision)


def _tri(n, strict):
    r = lax.broadcasted_iota(I32, (n, n), 0)
    c = lax.broadcasted_iota(I32, (n, n), 1)
    return (c < r) if strict else (c <= r)


def _cumsum_rows(g):
    return _dot(_tri(g.shape[0], False).astype(F32), g, precision=HI)


def _lane_mask(width, lo, hi):
    l = lax.broadcasted_iota(I32, (1, width), 1)
    return (l >= lo) & (l < hi)


def _sigmoid(x):
    return 1.0 / (1.0 + jnp.exp(-x))


def _silu(x):
    return x * _sigmoid(x)


def _norm_proj_kernel(x_ref, g_ref, w_ref, o_ref):
    hn = _rms(x_ref[...], g_ref[...])
    o_ref[...] = _dot(hn.astype(BF16), w_ref[...])


def norm_proj(x, g, w_bf16, tm):
    n, d = x.shape
    f = w_bf16.shape[1]
    return pl.pallas_call(
        _norm_proj_kernel,
        grid=(n // tm,),
        in_specs=[pl.BlockSpec((tm, d), lambda i: (i, 0)),
                  pl.BlockSpec((1, d), lambda i: (0, 0)),
                  pl.BlockSpec((d, f), lambda i: (0, 0))],
        out_specs=pl.BlockSpec((tm, f), lambda i: (i, 0)),
        out_shape=jax.ShapeDtypeStruct((n, f), F32),
        compiler_params=_cparams(("parallel",)),
        name="norm_proj",
    )(x, g.reshape(1, d), w_bf16)


def _out_proj2_kernel(x_ref, a_ref, b_ref, wa_ref, wb_ref, o_ref):
    y = _dot(a_ref[...].astype(BF16), wa_ref[...]) + _dot(b_ref[...].astype(BF16), wb_ref[...])
    o_ref[...] = x_ref[...] + y


def out_proj2(x, a, b, wa, wb, tm):
    n, d = x.shape
    ka, kb = a.shape[1], b.shape[1]
    return pl.pallas_call(
        _out_proj2_kernel,
        grid=(n // tm,),
        in_specs=[pl.BlockSpec((tm, d), lambda i: (i, 0)),
                  pl.BlockSpec((tm, ka), lambda i: (i, 0)),
                  pl.BlockSpec((tm, kb), lambda i: (i, 0)),
                  pl.BlockSpec((ka, d), lambda i: (0, 0)),
                  pl.BlockSpec((kb, d), lambda i: (0, 0))],
        out_specs=pl.BlockSpec((tm, d), lambda i: (i, 0)),
        out_shape=jax.ShapeDtypeStruct((n, d), F32),
        compiler_params=_cparams(("parallel",)),
        name="out_proj2",
    )(x, a, b, wa, wb)


def _out_proj1_kernel(x_ref, a_ref, wa_ref, o_ref):
    o_ref[...] = x_ref[...] + _dot(a_ref[...].astype(BF16), wa_ref[...])


def out_proj1(x, a, wa, tm):
    n, d = x.shape
    ka = a.shape[1]
    return pl.pallas_call(
        _out_proj1_kernel,
        grid=(n // tm,),
        in_specs=[pl.BlockSpec((tm, d), lambda i: (i, 0)),
                  pl.BlockSpec((tm, ka), lambda i: (i, 0)),
                  pl.BlockSpec((ka, d), lambda i: (0, 0))],
        out_specs=pl.BlockSpec((tm, d), lambda i: (i, 0)),
        out_shape=jax.ShapeDtypeStruct((n, d), F32),
        compiler_params=_cparams(("parallel",)),
        name="out_proj1",
    )(x, a, wa)


def _final_norm_kernel(x_ref, g_ref, o_ref):
    o_ref[...] = _rms(x_ref[...], g_ref[...])


def final_norm(x, g, tm):
    n, d = x.shape
    return pl.pallas_call(
        _final_norm_kernel,
        grid=(n // tm,),
        in_specs=[pl.BlockSpec((tm, d), lambda i: (i, 0)), pl.BlockSpec((1, d), lambda i: (0, 0))],
        out_specs=pl.BlockSpec((tm, d), lambda i: (i, 0)),
        out_shape=jax.ShapeDtypeStruct((n, d), F32),
        compiler_params=_cparams(("parallel",)),
        name="final_norm",
    )(x, g.reshape(1, d))


def _intra_chunk(problems):
    c = problems[0][0].shape[0]
    nb = c // SUBLANES
    row = lax.broadcasted_iota(I32, (SUBLANES, 1), 0)
    qbs = [[q[SUBLANES * i:SUBLANES * (i + 1)] for i in range(nb)] for q, _, _, _ in problems]
    bbs = [[b[SUBLANES * i:SUBLANES * (i + 1)] for i in range(nb)] for _, _, b, _ in problems]
    outs = [[[None] * nb for _ in heads] for _, _, _, heads in problems]
    for s in range(c):
        rb0 = s // SUBLANES
        for pi, (_, k, b, heads) in enumerate(problems):
            ks = k[s:s + 1, :]
            bs = b[s:s + 1, :]
            for rb in range(rb0, nb):
                p = qbs[pi][rb] * (ks * jnp.exp(bbs[pi][rb] - bs))
                for hi, (mask, v) in enumerate(heads):
                    pm = p if mask is None else jnp.where(mask, p, 0.0)
                    col = jnp.sum(pm, axis=-1, keepdims=True)
                    if rb == rb0:
                        col = jnp.where(row + SUBLANES * rb >= s, col, 0.0)
                    term = col * v[s:s + 1, :]
                    prev = outs[pi][hi][rb]
                    outs[pi][hi][rb] = term if prev is None else prev + term
    return [[jnp.concatenate(o, axis=0) for o in po] for po in outs]


def _gated_chunk(problems, states):
    intra = _intra_chunk(problems)
    qes = [q * jnp.exp(b) for q, _, b, _ in problems]
    b_lasts = [b[-1:, :] for _, _, b, _ in problems]
    khs = [k * jnp.exp(bl - b) for (_, k, b, _), bl in zip(problems, b_lasts)]
    outs = []
    for (_, _, _, heads), qe, st, po in zip(problems, qes, states, intra):
        o = []
        for (mask, v), oi in zip(heads, po):
            qm = qe if mask is None else jnp.where(mask, qe, 0.0)
            o.append(oi + _dot_nt(qm, st, precision=HI))
        outs.append(o)
    new_states = []
    for (_, _, _, heads), kh, st, bl in zip(problems, khs, states, b_lasts):
        upd = _dot_tn(heads[0][1], kh, precision=HI)
        if len(heads) == 2:
            upd = jnp.where(heads[0][0], upd, _dot_tn(heads[1][1], kh, precision=HI))
        new_states.append(st * jnp.exp(bl) + upd)
    return outs, new_states


def _head_rms(o, g):
    ms = jnp.mean(o * o, axis=-1, keepdims=True)
    return o * lax.rsqrt(ms + NORM_EPS) * g


def _hgrn_kernel(zq_ref, zf_ref, zi_ref, zg_ref, lb_ref, ng_ref, s0_ref, o_ref, s_ref, st_scr, *, chunk):
    t = pl.program_id(2)

    @pl.when(t == 0)
    def _():
        st_scr[...] = s0_ref[:, 0]

    lb = lb_ref[0]
    nseq = zq_ref.shape[0]
    nchunks = zq_ref.shape[1] // chunk

    def body(ci, carry):
        sl = pl.ds(pl.multiple_of(ci * chunk, chunk), chunk)
        problems = []
        for i in range(nseq):
            f = lb + (1.0 - lb) * _sigmoid(zf_ref[i, sl, :])
            problems.append((_silu(zq_ref[i, sl, :]), 1.0 - f, _cumsum_rows(jnp.log(f)), [(None, zi_ref[i, sl, :])]))
        outs, states = _gated_chunk(problems, [st_scr[i] for i in range(nseq)])
        for i in range(nseq):
            st_scr[i] = states[i]
            o_ref[i, sl, :] = _head_rms(outs[i][0], ng_ref[...]) * _silu(zg_ref[i, sl, :])
        return carry

    lax.fori_loop(0, nchunks, body, 0)

    @pl.when(t == pl.num_programs(2) - 1)
    def _():
        s_ref[:, 0] = st_scr[...]


def hgrn_recurrence(z, lb, norm_g, s0_t, tb, chunk, nb):
    bsz, t, _ = z.shape
    zspec = lambda off: pl.BlockSpec((nb, tb, LANES), lambda b, h, i: (b, i, h + off))
    return pl.pallas_call(
        functools.partial(_hgrn_kernel, chunk=chunk),
        grid=(bsz // nb, A_HEADS, t // tb),
        in_specs=[zspec(0), zspec(4), zspec(8), zspec(12),
                  pl.BlockSpec((1, 1, LANES), lambda b, h, i: (h, 0, 0)),
                  pl.BlockSpec((1, LANES), lambda b, h, i: (0, 0)),
                  pl.BlockSpec((nb, 1, LANES, LANES), lambda b, h, i: (b, h, 0, 0))],
        out_specs=[pl.BlockSpec((nb, tb, LANES), lambda b, h, i: (b, i, h)),
                   pl.BlockSpec((nb, 1, LANES, LANES), lambda b, h, i: (b, h, 0, 0))],
        out_shape=[jax.ShapeDtypeStruct((bsz, t, A_WIDTH), F32),
                   jax.ShapeDtypeStruct((bsz, A_HEADS, LANES, LANES), F32)],
        scratch_shapes=[pltpu.VMEM((nb, LANES, LANES), F32)],
        compiler_params=_cparams(("parallel", "parallel", "arbitrary")),
        name="hgrn_recurrence",
    )(z, z, z, z, lb.reshape(A_HEADS, 1, LANES), norm_g.reshape(1, LANES), s0_t)


def _gla_kernel(zq_ref, zk_ref, zv_ref, zg_ref, zlr_ref, w2_ref, gb_ref, ng_ref, s0_ref, o_ref, s_ref, st_scr,
                *, chunk):
    t = pl.program_id(2)

    @pl.when(t == 0)
    def _():
        st_scr[...] = s0_ref[:, 0]

    nseq = zq_ref.shape[0]
    nchunks = zq_ref.shape[1] // chunk
    m0 = _lane_mask(LANES, 0, B_DK)
    m1 = _lane_mask(LANES, B_DK, LANES)

    def body(ci, carry):
        sl = pl.ds(pl.multiple_of(ci * chunk, chunk), chunk)
        problems = []
        for i in range(nseq):
            pre = _dot(zlr_ref[i, sl, :], w2_ref[...], precision=HI) + gb_ref[...]
            log_g = (jnp.minimum(pre, 0.0) - jnp.log(1.0 + jnp.exp(-jnp.abs(pre)))) * (1.0 / GLA_NORMALIZER)
            v = zv_ref[i, sl, :]
            problems.append((zq_ref[i, sl, :] * (B_DK ** -0.5), zk_ref[i, sl, :], _cumsum_rows(log_g),
                             [(m0, v[:, :LANES]), (m1, v[:, LANES:])]))
        outs, states = _gated_chunk(problems, [st_scr[i] for i in range(nseq)])
        for i in range(nseq):
            st_scr[i] = states[i]
            gate = _silu(zg_ref[i, sl, :])
            o_ref[i, sl, 0:LANES] = _head_rms(outs[i][0], ng_ref[...]) * gate[:, :LANES]
            o_ref[i, sl, LANES:2 * LANES] = _head_rms(outs[i][1], ng_ref[...]) * gate[:, LANES:]
        return carry

    lax.fori_loop(0, nchunks, body, 0)

    @pl.when(t == pl.num_programs(2) - 1)
    def _():
        s_ref[:, 0] = st_scr[...]


def gla_recurrence(z, w2pad, gate_b, norm_g, s0_t, tb, chunk, nb):
    bsz, t, _ = z.shape
    npairs = B_HEADS // 2
    return pl.pallas_call(
        functools.partial(_gla_kernel, chunk=chunk),
        grid=(bsz // nb, npairs, t // tb),
        in_specs=[pl.BlockSpec((nb, tb, LANES), lambda b, p, i: (b, i, 16 + p)),
                  pl.BlockSpec((nb, tb, LANES), lambda b, p, i: (b, i, 18 + p)),
                  pl.BlockSpec((nb, tb, 2 * LANES), lambda b, p, i: (b, i, 10 + p)),
                  pl.BlockSpec((nb, tb, 2 * LANES), lambda b, p, i: (b, i, 12 + p)),
                  pl.BlockSpec((nb, tb, LANES), lambda b, p, i: (b, i, 28)),
                  pl.BlockSpec((LANES, LANES), lambda b, p, i: (0, p)),
                  pl.BlockSpec((1, LANES), lambda b, p, i: (0, p)),
                  pl.BlockSpec((1, LANES), lambda b, p, i: (0, 0)),
                  pl.BlockSpec((nb, 1, LANES, LANES), lambda b, p, i: (b, p, 0, 0))],
        out_specs=[pl.BlockSpec((nb, tb, 2 * LANES), lambda b, p, i: (b, i, p)),
                   pl.BlockSpec((nb, 1, LANES, LANES), lambda b, p, i: (b, p, 0, 0))],
        out_shape=[jax.ShapeDtypeStruct((bsz, t, B_WIDTH), F32),
                   jax.ShapeDtypeStruct((bsz, npairs, LANES, LANES), F32)],
        scratch_shapes=[pltpu.VMEM((nb, LANES, LANES), F32)],
        compiler_params=_cparams(("parallel", "parallel", "arbitrary")),
        name="gla_recurrence",
    )(z, z, z, z, z, w2pad, gate_b.reshape(1, 2 * LANES), norm_g.reshape(1, LANES), s0_t)


def _rwkv_proj_kernel(x_ref, xp_ref, xl_ref, g1_ref, mu_ref, wr_ref, wk_ref, wv_ref, ww1_ref, ww2_ref, w0_ref,
                      aw1_ref, aw2_ref, a0_ref, gw1_ref, gw2_ref, kk_ref, ka_ref,
                      r_out, k_out, v_out, lw_out, a_out, kk_out, g_out, hl_out, *, tiles_per_seq):
    i = pl.program_id(0)
    g1 = g1_ref[...]
    hn = _rms(x_ref[...], g1)
    tm = hn.shape[0]
    prev = _rms(xp_ref[...], g1)[SUBLANES - 1:SUBLANES, :]
    prev = jnp.where(i % tiles_per_seq == 0, xl_ref[0], prev)
    row = lax.broadcasted_iota(I32, (tm, 1), 0)
    xprev = jnp.where(row == 0, prev, pltpu.roll(hn, 1, axis=0))
    dx = xprev - hn

    def mix(j):
        return hn + dx * mu_ref[j:j + 1, :]

    r = _dot(mix(0).astype(BF16), wr_ref[...])
    k = _dot(mix(1).astype(BF16), wk_ref[...])
    v = _dot(mix(2).astype(BF16), wv_ref[...])
    wl = _dot(jnp.tanh(_dot(mix(3), ww1_ref[...], precision=HI)), ww2_ref[...], precision=HI)
    z = w0_ref[...] + wl
    wpre = -(jnp.maximum(-z, 0.0) + jnp.log(1.0 + jnp.exp(-jnp.abs(z)))) - 0.5
    al = _dot(_dot(mix(4), aw1_ref[...], precision=HI), aw2_ref[...], precision=HI)
    a = _sigmoid(a0_ref[...] + al)
    gg = _dot(_sigmoid(_dot(mix(5).astype(BF16), gw1_ref[...])).astype(BF16), gw2_ref[...])
    r_out[...] = r
    k_out[...] = k * (1.0 + (a - 1.0) * ka_ref[...])
    v_out[...] = v
    lw_out[...] = -jnp.exp(wpre)
    a_out[...] = a
    kk_out[...] = k * kk_ref[...]
    g_out[...] = gg
    hl_out[0] = hn[tm - 1:tm, :]


def rwkv_proj(x, x_last, seq_len, g1, mu, wr, wk, wv, ww1, ww2, w0, aw1, aw2, a0, gw1, gw2, k_k, k_a, tm):
    n, d = x.shape
    tiles_per_seq = seq_len // tm
    row = lambda a: a.reshape(1, d)
    full = lambda a: pl.BlockSpec(a.shape, lambda i: (0,) * a.ndim)
    tile = pl.BlockSpec((tm, d), lambda i: (i, 0))
    blocks8 = tm // SUBLANES
    args = (x, x, x_last.reshape(-1, 1, d), row(g1), mu, wr, wk, wv, ww1, ww2, row(w0), aw1, aw2, row(a0),
            gw1, gw2, row(k_k), row(k_a))
    in_specs = [tile,
                pl.BlockSpec((SUBLANES, d), lambda i: (jnp.maximum(i * blocks8 - 1, 0), 0)),
                pl.BlockSpec((1, 1, d), lambda i: (i // tiles_per_seq, 0, 0))]
    in_specs += [full(a) for a in args[3:]]
    outs = pl.pallas_call(
        functools.partial(_rwkv_proj_kernel, tiles_per_seq=tiles_per_seq),
        grid=(n // tm,),
        in_specs=in_specs,
        out_specs=[tile] * 7 + [pl.BlockSpec((1, 1, d), lambda i: (i, 0, 0))],
        out_shape=[jax.ShapeDtypeStruct((n, d), F32)] * 7 + [jax.ShapeDtypeStruct((n // tm, 1, d), F32)],
        compiler_params=_cparams(("parallel",)),
        name="rwkv_proj",
    )(*args)
    return outs


_NN = ((1,), (0,))
_NT = ((1,), (1,))
_TN = ((0,), (0,))
RWKV_AB_PASSES = 1
RWKV_INV_PASSES = 1
RWKV_APPLY_PASSES = 1
RWKV_STATE_PASSES = 3
RWKV_SEQS_PER_STEP = 4


def _split_bf16(a):
    hi = a.astype(BF16)
    return hi, (a - hi.astype(F32)).astype(BF16)


def _mm(a, b, dims, passes):
    if passes == 6:
        return lax.dot_general(a, b, (dims, ((), ())), preferred_element_type=F32, precision=HI)
    dg = lambda x, y: lax.dot_general(x, y, (dims, ((), ())), preferred_element_type=F32)
    ah, al = _split_bf16(a)
    bh, bl = _split_bf16(b)
    if passes == 1:
        return dg(ah, bh)
    return dg(ah, bh) + (dg(al, bh) + dg(ah, bl))


def _cumsum_rows3(g):
    tri = _tri(g.shape[0], False).astype(BF16)
    h1 = g.astype(BF16)
    r1 = g - h1.astype(F32)
    h2 = r1.astype(BF16)
    h3 = (r1 - h2.astype(F32)).astype(BF16)
    return _dot(tri, h1) + (_dot(tri, h2) + _dot(tri, h3))


def _pair_sum(x, m0):
    s0 = jnp.sum(jnp.where(m0, x, 0.0), axis=-1, keepdims=True)
    s1 = jnp.sum(jnp.where(m0, 0.0, x), axis=-1, keepdims=True)
    return jnp.where(m0, s0, s1)


def _rwkv_kernel(r_ref, k_ref, v_ref, lw_ref, a_ref, kk_ref, g_ref, rk_ref, lng_ref, lnb_ref, s0_ref,
                 o_ref, s_ref, mt_scr, *, chunk):
    t = pl.program_id(2)

    @pl.when(t == 0)
    def _():
        mt_scr[...] = s0_ref[:, 0]

    nrows = r_ref.shape[0]
    nchunks = r_ref.shape[1] // chunk
    c2 = 2 * chunk
    m0 = _lane_mask(LANES, 0, C_HEAD)
    rowi = lax.broadcasted_iota(I32, (LANES, LANES), 0)
    coli = lax.broadcasted_iota(I32, (LANES, LANES), 1)
    blockdiag = (rowi < C_HEAD) == (coli < C_HEAD)
    ti = lax.broadcasted_iota(I32, (c2, c2), 0)
    si = lax.broadcasted_iota(I32, (c2, c2), 1)
    same_head = (ti < chunk) == (si < chunk)
    tm_ = jnp.where(ti < chunk, ti, ti - chunk)
    sm_ = jnp.where(si < chunk, si, si - chunk)
    strict = same_head & (sm_ < tm_)
    incl = same_head & (sm_ <= tm_)

    def stack_heads(x):
        return jnp.concatenate([jnp.where(m0, x, 0.0), jnp.where(m0, 0.0, x)], axis=0)

    def twice(x):
        return jnp.concatenate([x, x], axis=0)

    def unstack(x2):
        return jnp.where(m0, x2[:chunk], x2[chunk:])

    eye = (ti == si).astype(F32)
    seqs = range(nrows)

    def body(ci, carry):
        sl = pl.ds(pl.multiple_of(ci * chunk, chunk), chunk)
        r = [r_ref[i, sl, :] for i in seqs]
        k = [k_ref[i, sl, :] for i in seqs]
        v = [v_ref[i, sl, :] for i in seqs]
        lw = [lw_ref[i, sl, :] for i in seqs]
        kkr = [kk_ref[i, sl, :] for i in seqs]
        kk = [x * lax.rsqrt(_pair_sum(x * x, m0) + 1e-12) for x in kkr]
        al = [a_ref[i, sl, :] * kk[i] for i in seqs]
        gam = [_cumsum_rows3(x) for x in lw]
        e_neg = [jnp.exp(-x) for x in gam]
        xr = [jnp.concatenate([stack_heads(kk[i] * jnp.exp(gam[i] - lw[i])),
                               stack_heads(r[i] * jnp.exp(gam[i]))], axis=0) for i in seqs]
        alk = [jnp.concatenate([twice(al[i] * e_neg[i]), twice(k[i] * e_neg[i])], axis=0) for i in seqs]
        mt = [mt_scr[i] for i in seqs]
        ab = [_mm(xr[i], alk[i], _NT, RWKV_AB_PASSES) for i in seqs]
        xm = [_mm(xr[i], mt[i], _NT, RWKV_STATE_PASSES) for i in seqs]
        a_al = [jnp.where(strict, x[:c2, :c2], 0.0) for x in ab]
        a_k = [jnp.where(strict, x[:c2, c2:], 0.0) for x in ab]
        b_alk = [jnp.concatenate([jnp.where(incl, x[c2:, c2:], 0.0), jnp.where(incl, -x[c2:, :c2], 0.0)], axis=1)
                 for x in ab]
        v2 = [twice(x) for x in v]
        rhs = [xm[i][:c2] + _mm(a_k[i], v2[i], _NN, RWKV_APPLY_PASSES) for i in seqs]
        p = [-x for x in a_al]
        tinv = [eye + x for x in p]
        span = 2
        while span < chunk:
            p = [_mm(x, x, _NN, RWKV_INV_PASSES) for x in p]
            tinv = [tinv[i] + _mm(tinv[i], p[i], _NN, RWKV_INV_PASSES) for i in seqs]
            span *= 2
        u = [unstack(_mm(tinv[i], rhs[i], _NN, RWKV_APPLY_PASSES)) for i in seqs]
        o = [unstack(xm[i][c2:] + _mm(b_alk[i], jnp.concatenate([v2[i], twice(u[i])], axis=0), _NN,
                                      RWKV_APPLY_PASSES)) for i in seqs]
        g_last = [x[-1:, :] for x in gam]
        e_end = [jnp.exp(g_last[i] - gam[i]) for i in seqs]
        upd = [_mm(jnp.concatenate([v[i], u[i]], axis=0),
                   jnp.concatenate([k[i] * e_end[i], -(al[i] * e_end[i])], axis=0), _TN, RWKV_STATE_PASSES)
               for i in seqs]
        for i in seqs:
            mt_scr[i] = mt[i] * jnp.exp(g_last[i]) + jnp.where(blockdiag, upd[i], 0.0)
        for i in seqs:
            mean = _pair_sum(o[i], m0) * (1.0 / C_HEAD)
            cen = o[i] - mean
            var = _pair_sum(cen * cen, m0) * (1.0 / C_HEAD)
            on = cen * lax.rsqrt(var + C_GN_EPS) * lng_ref[...] + lnb_ref[...]
            bonus = _pair_sum(r[i] * k[i] * rk_ref[...], m0) * v[i]
            o_ref[i, sl, :] = (on + bonus) * g_ref[i, sl, :]
        return carry

    lax.fori_loop(0, nchunks, body, 0)

    @pl.when(t == pl.num_programs(2) - 1)
    def _():
        s_ref[:, 0] = mt_scr[...]


def rwkv_recurrence(r, k, v, lw, a, kk, g, r_k, ln_g, ln_b, s0_bd, tb, chunk, nb):
    bsz, t, d = r.shape
    npairs = C_HEADS // 2
    seq = pl.BlockSpec((nb, tb, LANES), lambda b, p, i: (b, i, p))
    vec = pl.BlockSpec((1, LANES), lambda b, p, i: (0, p))
    st = pl.BlockSpec((nb, 1, LANES, LANES), lambda b, p, i: (b, p, 0, 0))
    return pl.pallas_call(
        functools.partial(_rwkv_kernel, chunk=chunk),
        grid=(bsz // nb, npairs, t // tb),
        in_specs=[seq] * 7 + [vec, vec, vec, st],
        out_specs=[seq, st],
        out_shape=[jax.ShapeDtypeStruct((bsz, t, d), F32),
                   jax.ShapeDtypeStruct((bsz, npairs, LANES, LANES), F32)],
        scratch_shapes=[pltpu.VMEM((nb, LANES, LANES), F32)],
        compiler_params=_cparams(("parallel", "parallel", "arbitrary")),
        name="rwkv_recurrence",
    )(r, k, v, lw, a, kk, g, r_k.reshape(1, d), ln_g.reshape(1, d), ln_b.reshape(1, d), s0_bd)


NEG_INF = float("-inf")


def _top16_rows(s):
    n = s.shape[0]
    key = lax.broadcasted_iota(I32, s.shape, 0)
    vals, idxs = [], []
    for _ in range(P_TOPK):
        m = jnp.max(s, axis=0, keepdims=True)
        am = jnp.min(jnp.where(s == m, key, n), axis=0, keepdims=True)
        vals.append(m)
        idxs.append(am)
        s = jnp.where(key == am, NEG_INF, s)
    return vals, idxs


def _top16_pairs(v0, i0, v1, i1):
    a0 = jnp.concatenate(v0[0:8], axis=0)
    a1 = jnp.concatenate(v0[8:16], axis=0)
    b0 = jnp.concatenate(v1[0:8], axis=0)
    b1 = jnp.concatenate(v1[8:16], axis=0)
    ia0 = jnp.concatenate(i0[0:8], axis=0) * P_KEYS
    ia1 = jnp.concatenate(i0[8:16], axis=0) * P_KEYS
    ib0 = jnp.concatenate(i1[0:8], axis=0)
    ib1 = jnp.concatenate(i1[8:16], axis=0)
    row = lax.broadcasted_iota(I32, (SUBLANES, 1), 0)
    slabs = []

    def add(val, eid, keep):
        slabs.append((val if keep is None else jnp.where(keep, val, NEG_INF), eid))

    add(v0[0] + b0, ia0[0:1] + ib0, None)
    add(v0[0] + b1, ia0[0:1] + ib1, None)
    add(v0[1] + b0, ia0[1:2] + ib0, None)
    add(v0[2] + b0, ia0[2:3] + ib0, row < 5)
    add(v0[3] + b0, ia0[3:4] + ib0, row < 4)
    add(a0 + v1[0], ia0 + ib0[0:1], row >= 4)
    add(a1 + v1[0], ia1 + ib0[0:1], None)
    add(a0 + v1[1], ia0 + ib0[1:2], row >= 4)
    add(a0 + v1[2], ia0 + ib0[2:3], row == 4)

    big = P_KEYS * P_KEYS
    out_v, out_e = [], []
    for _ in range(P_TOPK):
        m = slabs[0][0]
        for val, _e in slabs[1:]:
            m = jnp.maximum(m, val)
        m = jnp.max(m, axis=0, keepdims=True)
        e = None
        for val, eid in slabs:
            c = jnp.where(val == m, eid, big)
            e = c if e is None else jnp.minimum(e, c)
        e = jnp.min(e, axis=0, keepdims=True)
        out_v.append(m)
        out_e.append(e)
        slabs = [(jnp.where(eid == e, NEG_INF, val), eid) for val, eid in slabs]
    return out_v, out_e


def _peer_select_kernel(x_ref, g_ref, wq_ref, keys_ref, xn_out, eid_out, gate_out):
    hn = _rms(x_ref[...], g_ref[...])
    xn_out[...] = hn
    q = _dot(hn.astype(BF16), wq_ref[...])
    tm = q.shape[0]
    for lt in range(tm // LANES):
        rows = slice(lt * LANES, (lt + 1) * LANES)
        e_rows, g_rows = [], []
        for h in range(P_HEADS):
            tops = []
            for p in range(2):
                hp = 2 * h + p
                s = _dot_nt(keys_ref[hp], q[rows, hp * LANES:(hp + 1) * LANES], precision=HI)
                tops.append(_top16_rows(s))
            cs, ce = _top16_pairs(tops[0][0], tops[0][1], tops[1][0], tops[1][1])
            ex = [jnp.exp(c - cs[0]) for c in cs]
            tot = ex[0]
            for e in ex[1:]:
                tot = tot + e
            inv = 1.0 / tot
            e_rows += ce
            g_rows += [e * inv for e in ex]
        eid_out[rows, :] = jnp.concatenate(e_rows, axis=0).T
        gate_out[rows, :] = jnp.concatenate(g_rows, axis=0).T


def peer_select(x, g, wq_bf16, keys, tm):
    n, d = x.shape
    return pl.pallas_call(
        _peer_select_kernel,
        grid=(n // tm,),
        in_specs=[pl.BlockSpec((tm, d), lambda i: (i, 0)),
                  pl.BlockSpec((1, d), lambda i: (0, 0)),
                  pl.BlockSpec(wq_bf16.shape, lambda i: (0, 0)),
                  pl.BlockSpec(keys.shape, lambda i: (0, 0, 0))],
        out_specs=[pl.BlockSpec((tm, d), lambda i: (i, 0)),
                   pl.BlockSpec((tm, P_PICKS), lambda i: (i, 0)),
                   pl.BlockSpec((tm, P_PICKS), lambda i: (i, 0))],
        out_shape=[jax.ShapeDtypeStruct((n, d), F32),
                   jax.ShapeDtypeStruct((n, P_PICKS), I32),
                   jax.ShapeDtypeStruct((n, P_PICKS), F32)],
        compiler_params=_cparams(("parallel",)),
        name="peer_select",
    )(x, g.reshape(1, d), wq_bf16, keys)


def _peer_act_kernel(h_ref, g_ref, o_ref):
    h = h_ref[...]
    o_ref[...] = 0.5 * h * (1.0 + lax.erf(h * (2.0 ** -0.5))) * g_ref[...]


def peer_act(hid, gate, tm):
    n, p = hid.shape
    spec = pl.BlockSpec((tm, p), lambda i: (i, 0))
    return pl.pallas_call(
        _peer_act_kernel, grid=(n // tm,), in_specs=[spec, spec], out_specs=spec,
        out_shape=jax.ShapeDtypeStruct((n, p), F32),
        compiler_params=_cparams(("parallel",)), name="peer_act",
    )(hid, gate)


_CHUNKS_PER_TOK = P_PICKS // SC_CHUNK
_WORD_VREGS = SC_WORDS // SC_LANES
U32 = jnp.uint32


def pack_rows_bf16(t):
    lo = lax.bitcast_convert_type(t[..., :SC_WORDS].astype(BF16), jnp.uint16).astype(U32)
    a = lax.bitcast_convert_type(t[..., SC_WORDS:], U32)
    sign = a & U32(0x80000000)
    mag = a & U32(0x7FFFFFFF)
    steps = (jnp.maximum(mag + U32(0x8000), lo) - lo) >> 16
    near = (steps << 16) + lo
    near = jnp.where(near >= U32(0x7F800000), near - U32(0x10000), near)
    return lax.bitcast_convert_type(sign | near, I32)


def _low_f32(w):
    return lax.bitcast_convert_type(w << 16, F32)


def _high_f32(w):
    return lax.bitcast_convert_type(w, F32)


def _sc_worker():
    return lax.axis_index("s") * SC_CORES + lax.axis_index("c")


def _sc_tok_block(tok_per_w):
    return next(b for b in SC_TOK_BLOCKS if tok_per_w % b == 0)


def _sc_hid_body(tab_hbm, idx_hbm, x_hbm, hid_hbm, idx_v, x_v, hid_v, rows_a, rows_b, sem_a, sem_b, *, tok_per_w):
    wid = _sc_worker()
    blk = x_v.shape[0]
    nblk = tok_per_w // blk
    nch = blk * _CHUNKS_PER_TOK
    lane = lax.iota(I32, SC_LANES)
    quads = SC_LANES // SC_HID_ROWS

    def dots(rows, x_row, res_ref, col0):
        for grp in range(SC_CHUNK // SC_LANES):
            def some_rows(q, res):
                r0 = grp * SC_LANES + q * SC_HID_ROWS

                def some_words(jj, accs):
                    accs = list(accs)
                    for jw in range(SC_HID_WORDS):
                        off = (jj * SC_HID_WORDS + jw) * SC_LANES
                        xl = x_v[x_row, pl.ds(off, SC_LANES)]
                        xh = x_v[x_row, pl.ds(SC_WORDS + off, SC_LANES)]
                        for i in range(SC_HID_ROWS):
                            w = rows[r0 + i, pl.ds(off, SC_LANES)]
                            k = 2 * i + jw % 2
                            accs[k] = accs[k] + (_low_f32(w) * xl + _high_f32(w) * xh)
                    return tuple(accs)

                zero = jnp.zeros((SC_LANES,), F32)
                accs = lax.fori_loop(0, _WORD_VREGS // SC_HID_WORDS, some_words, (zero,) * (2 * SC_HID_ROWS))
                for i in range(SC_HID_ROWS):
                    res = jnp.where(lane == q * SC_HID_ROWS + i, jnp.sum(accs[2 * i] + accs[2 * i + 1]), res)
                return res
            res = lax.fori_loop(0, quads, some_rows, jnp.zeros((SC_LANES,), F32))
            res_ref[x_row, pl.ds(col0 + grp * SC_LANES, SC_LANES)] = res

    def block(bi, carry):
        tok0 = wid * tok_per_w + bi * blk
        pltpu.sync_copy(idx_hbm.at[pl.ds(tok0 * _CHUNKS_PER_TOK, nch)], idx_v)
        pltpu.async_copy(tab_hbm.at[idx_v.at[0]], rows_a, sem_a)
        pltpu.sync_copy(x_hbm.at[pl.ds(tok0, blk)], x_v)

        def pair(j, c):
            ca = 2 * j
            pltpu.async_copy(tab_hbm.at[idx_v.at[ca + 1]], rows_b, sem_b)
            pltpu.make_async_copy(tab_hbm.at[idx_v.at[0]], rows_a, sem_a).wait()
            dots(rows_a, ca // _CHUNKS_PER_TOK, hid_v, (ca % _CHUNKS_PER_TOK) * SC_CHUNK)

            @pl.when(j < nch // 2 - 1)
            def _():
                pltpu.async_copy(tab_hbm.at[idx_v.at[ca + 2]], rows_a, sem_a)
            pltpu.make_async_copy(tab_hbm.at[idx_v.at[0]], rows_b, sem_b).wait()
            cb = ca + 1
            dots(rows_b, cb // _CHUNKS_PER_TOK, hid_v, (cb % _CHUNKS_PER_TOK) * SC_CHUNK)
            return c
        lax.fori_loop(0, nch // 2, pair, 0)
        pltpu.sync_copy(hid_v, hid_hbm.at[pl.ds(tok0, blk)])
        return carry

    lax.fori_loop(0, nblk, block, 0)


def sc_expert_hidden(table_packed, eidx, xn):
    n = xn.shape[0]
    tok_per_w = n // SC_WORKERS
    blk = _sc_tok_block(tok_per_w)
    mesh = plsc.VectorSubcoreMesh(core_axis_name="c", subcore_axis_name="s")
    nch = blk * _CHUNKS_PER_TOK
    k = pl.kernel(
        functools.partial(_sc_hid_body, tok_per_w=tok_per_w), mesh=mesh,
        out_type=jax.ShapeDtypeStruct((n, P_PICKS), F32),
        scratch_types=[pltpu.VMEM((nch, SC_CHUNK), I32),
                       pltpu.VMEM((blk, D_MODEL), F32),
                       pltpu.VMEM((blk, P_PICKS), F32),
                       pltpu.VMEM((SC_CHUNK, SC_WORDS), I32),
                       pltpu.VMEM((SC_CHUNK, SC_WORDS), I32),
                       pltpu.SemaphoreType.DMA, pltpu.SemaphoreType.DMA],
        compiler_params=pltpu.CompilerParams(needs_layout_passes=False),
        name="sc_expert_hidden",
    )
    return k(table_packed, eidx.reshape(n * _CHUNKS_PER_TOK, SC_CHUNK), xn)


def _sc_out_body(tab_hbm, idx_hbm, act_hbm, x_hbm, y_hbm, idx_v, act_v, y_v, rows_a, rows_b, sem_a, sem_b,
                 *, tok_per_w):
    wid = _sc_worker()
    blk = y_v.shape[0]
    nblk = tok_per_w // blk
    nch = blk * _CHUNKS_PER_TOK
    half_w = _WORD_VREGS // 2

    def accum(rows, tok, col0):
        for hv in range(2):
            base = hv * half_w * SC_LANES
            acc0 = (tuple(y_v[tok, pl.ds(base + j * SC_LANES, SC_LANES)] for j in range(half_w))
                    + tuple(y_v[tok, pl.ds(SC_WORDS + base + j * SC_LANES, SC_LANES)] for j in range(half_w)))

            def one(r, acc):
                aidx = jnp.full((SC_LANES,), col0, I32) + r
                wgt = plsc.load_gather(act_v, [jnp.full((SC_LANES,), tok, I32), aidx])
                lo, hi = [], []
                for j in range(half_w):
                    w = rows[r, pl.ds(base + j * SC_LANES, SC_LANES)]
                    lo.append(acc[j] + wgt * _low_f32(w))
                    hi.append(acc[half_w + j] + wgt * _high_f32(w))
                return tuple(lo + hi)
            acc = lax.fori_loop(0, SC_CHUNK, one, acc0)
            for j in range(half_w):
                y_v[tok, pl.ds(base + j * SC_LANES, SC_LANES)] = acc[j]
                y_v[tok, pl.ds(SC_WORDS + base + j * SC_LANES, SC_LANES)] = acc[half_w + j]

    def block(bi, carry):
        tok0 = wid * tok_per_w + bi * blk
        pltpu.sync_copy(idx_hbm.at[pl.ds(tok0 * _CHUNKS_PER_TOK, nch)], idx_v)
        pltpu.async_copy(tab_hbm.at[idx_v.at[0]], rows_a, sem_a)
        pltpu.sync_copy(act_hbm.at[pl.ds(tok0, blk)], act_v)
        pltpu.sync_copy(x_hbm.at[pl.ds(tok0, blk)], y_v)

        def pair(j, c):
            ca = 2 * j
            pltpu.async_copy(tab_hbm.at[idx_v.at[ca + 1]], rows_b, sem_b)
            pltpu.make_async_copy(tab_hbm.at[idx_v.at[0]], rows_a, sem_a).wait()
            accum(rows_a, ca // _CHUNKS_PER_TOK, (ca % _CHUNKS_PER_TOK) * SC_CHUNK)

            @pl.when(j < nch // 2 - 1)
            def _():
                pltpu.async_copy(tab_hbm.at[idx_v.at[ca + 2]], rows_a, sem_a)
            pltpu.make_async_copy(tab_hbm.at[idx_v.at[0]], rows_b, sem_b).wait()
            cb = ca + 1
            accum(rows_b, cb // _CHUNKS_PER_TOK, (cb % _CHUNKS_PER_TOK) * SC_CHUNK)
            return c
        lax.fori_loop(0, nch // 2, pair, 0)
        pltpu.sync_copy(y_v, y_hbm.at[pl.ds(tok0, blk)])
        return carry

    lax.fori_loop(0, nblk, block, 0)


def sc_expert_output(table_packed, eidx, act, x):
    n = x.shape[0]
    tok_per_w = n // SC_WORKERS
    blk = _sc_tok_block(tok_per_w)
    mesh = plsc.VectorSubcoreMesh(core_axis_name="c", subcore_axis_name="s")
    nch = blk * _CHUNKS_PER_TOK
    k = pl.kernel(
        functools.partial(_sc_out_body, tok_per_w=tok_per_w), mesh=mesh,
        out_type=jax.ShapeDtypeStruct((n, D_MODEL), F32),
        scratch_types=[pltpu.VMEM((nch, SC_CHUNK), I32),
                       pltpu.VMEM((blk, P_PICKS), F32),
                       pltpu.VMEM((blk, D_MODEL), F32),
                       pltpu.VMEM((SC_CHUNK, SC_WORDS), I32),
                       pltpu.VMEM((SC_CHUNK, SC_WORDS), I32),
                       pltpu.SemaphoreType.DMA, pltpu.SemaphoreType.DMA],
        compiler_params=pltpu.CompilerParams(needs_layout_passes=False),
        name="sc_expert_output",
    )
    return k(table_packed, eidx.reshape(n * _CHUNKS_PER_TOK, SC_CHUNK), act, x)


def peer_ffn(x, g, wq_bf16, keys, u_packed, v_packed, tm):
    xn, eidx, gate = peer_select(x, g, wq_bf16, keys, tm)
    hid = sc_expert_hidden(u_packed, eidx, xn)
    act = peer_act(hid, gate, tm)
    return sc_expert_output(v_packed, eidx, act, x)


def _prep_w_in(w_in):
    main = jnp.concatenate([w_in[:, :3072], w_in[:, 3088:3600]], axis=1)
    lr = jnp.pad(w_in[:, 3072:3088], ((0, 0), (0, LANES - GLA_RANK)))
    return jnp.concatenate([main, lr], axis=1).astype(BF16)


def _trunk(x, st_h, st_g, st_r, st_s, w, seq_len, tm, tmp, tb, chunk):
    bsz = x.shape[0]
    n = bsz * seq_len
    x2 = x.reshape(n, D_MODEL)

    z = norm_proj(x2, w["norm1_g"][0], w["w_in"], tm).reshape(bsz, seq_len, Z_WIDTH)
    s0_h = jnp.swapaxes(st_h, -1, -2)
    s0_g = jnp.swapaxes(st_g.reshape(bsz, 2, 2 * B_DK, LANES), -1, -2)
    o_a, sh_t = hgrn_recurrence(z, w["lb0"], w["hgrn_norm_g"], s0_h, tb, chunk, GATED_SEQS_PER_STEP)
    o_b, sg_t = gla_recurrence(z, w["gla_w2"], w["gla_b"], w["gla_norm_g"], s0_g, tb, chunk, GATED_SEQS_PER_STEP)
    new_h = jnp.swapaxes(sh_t, -1, -2)
    new_g = jnp.swapaxes(sg_t, -1, -2).reshape(bsz, B_HEADS, B_DK, LANES)
    x2 = out_proj2(x2, o_a.reshape(n, A_WIDTH), o_b.reshape(n, B_WIDTH), w["w_out_a"], w["w_out_b"], tm)
    x2 = peer_ffn(x2, w["norm2_g"][0], w["peer_wq"][0], w["peer_keys"][0], w["peer_u"][0], w["peer_v"][0], tmp)

    r, k, v, lw, a, kk, g, hl = rwkv_proj(
        x2, st_s, seq_len, w["norm1_g"][1], w["mu"], w["wr"], w["wk"], w["wv"], w["w_w1"], w["w_w2"], w["w0"],
        w["a_w1"], w["a_w2"], w["a0"], w["g_w1"], w["g_w2"], w["k_k"], w["k_a"], tm)
    new_s = hl.reshape(bsz, seq_len // tm, D_MODEL)[:, -1]
    pr = st_r.reshape(bsz, C_HEADS // 2, 2, C_HEAD, C_HEAD)
    zero = jnp.zeros_like(pr[:, :, 0])
    s0_r = jnp.concatenate([jnp.concatenate([pr[:, :, 0], zero], axis=-1),
                            jnp.concatenate([zero, pr[:, :, 1]], axis=-1)], axis=-2)
    sh3 = lambda t: t.reshape(bsz, seq_len, D_MODEL)
    o_c, sr_bd = rwkv_recurrence(sh3(r), sh3(k), sh3(v), sh3(lw), sh3(a), sh3(kk), sh3(g),
                                 w["r_k"], w["ln_g"], w["ln_b"], s0_r, tb, chunk, min(RWKV_SEQS_PER_STEP, bsz))
    new_r = jnp.stack([sr_bd[:, :, :C_HEAD, :C_HEAD], sr_bd[:, :, C_HEAD:, C_HEAD:]], axis=2)
    new_r = new_r.reshape(bsz, C_HEADS, C_HEAD, C_HEAD)
    x2 = out_proj1(x2, o_c.reshape(n, D_MODEL), w["w_out_c"], tm)
    x2 = peer_ffn(x2, w["norm2_g"][1], w["peer_wq"][1], w["peer_keys"][1], w["peer_u"][1], w["peer_v"][1], tmp)

    y = final_norm(x2, w["final_g"], tm).reshape(bsz, seq_len, D_MODEL)
    return y, new_h[None], new_g[None], new_r[None], new_s[None]


def kernel(x_prompt, x_sample, state_hgrn, state_gla, state_rwkv, state_shift, w_in_ab, hgrn_lower_bounds, hgrn_norm_g, gla_gate_w2, gla_gate_b, gla_norm_g, w_out_ab, rwkv_mu, rwkv_w_rkv, rwkv_w_w1, rwkv_w_w2, rwkv_w0, rwkv_a_w1, rwkv_a_w2, rwkv_a0, rwkv_g_w1, rwkv_g_w2, rwkv_k_k, rwkv_k_a, rwkv_r_k, rwkv_ln_g, rwkv_ln_b, w_out_c, norm1_g, norm2_g, final_g, peer_w_q, peer_sub_keys, peer_u, peer_v):
    lbs = jnp.cumsum(jax.nn.softmax(hgrn_lower_bounds.astype(F32), axis=0), axis=0)
    w = dict(
        norm1_g=norm1_g, norm2_g=norm2_g, final_g=final_g,
        w_in=_prep_w_in(w_in_ab[0]), lb0=lbs[0], hgrn_norm_g=hgrn_norm_g[0],
        gla_w2=jnp.pad(gla_gate_w2[0], ((0, LANES - GLA_RANK), (0, 0))), gla_b=gla_gate_b[0],
        gla_norm_g=gla_norm_g[0],
        w_out_a=w_out_ab[0, :A_WIDTH].astype(BF16), w_out_b=w_out_ab[0, A_WIDTH:].astype(BF16),
        mu=rwkv_mu[0], wr=rwkv_w_rkv[0, 0].astype(BF16), wk=rwkv_w_rkv[0, 1].astype(BF16),
        wv=rwkv_w_rkv[0, 2].astype(BF16), w_w1=rwkv_w_w1[0], w_w2=rwkv_w_w2[0], w0=rwkv_w0[0],
        a_w1=rwkv_a_w1[0], a_w2=rwkv_a_w2[0], a0=rwkv_a0[0],
        g_w1=rwkv_g_w1[0].astype(BF16), g_w2=rwkv_g_w2[0].astype(BF16),
        k_k=rwkv_k_k[0], k_a=rwkv_k_a[0], r_k=rwkv_r_k[0], ln_g=rwkv_ln_g[0], ln_b=rwkv_ln_b[0],
        w_out_c=w_out_c[0].astype(BF16),
        peer_wq=peer_w_q.astype(BF16),
        peer_keys=peer_sub_keys.reshape(peer_sub_keys.shape[0], 2 * P_HEADS, P_KEYS, P_KEYS),
        peer_u=pack_rows_bf16(peer_u), peer_v=pack_rows_bf16(peer_v),
    )
    bp, tp, _ = x_prompt.shape
    bs, ts, _ = x_sample.shape
    assert sum(PROMPT_GROUPS) == bp
    groups, start = [], 0
    for gsz in PROMPT_GROUPS:
        zeros = lambda s: jnp.zeros((gsz,) + s.shape[2:], F32)
        groups.append(_trunk(x_prompt[start:start + gsz], zeros(state_hgrn), zeros(state_gla), zeros(state_rwkv),
                             zeros(state_shift), w, tp, tm=256, tmp=256, tb=512, chunk=64))
        start += gsz
    y_p = jnp.concatenate([g[0] for g in groups], axis=0)
    p_h, p_g, p_r, p_s = (jnp.concatenate([g[j] for g in groups], axis=1) for j in range(1, 5))
    y_s, s_h, s_g, s_r, s_s = _trunk(x_sample, state_hgrn[0], state_gla[0], state_rwkv[0], state_shift[0],
                                     w, ts, tm=32, tmp=128, tb=32, chunk=32)
    return (y_p, y_s, p_h, p_g, p_r, p_s, s_h, s_g, s_r, s_s)
```

```python
import functools

import jax
import jax.numpy as jnp
from jax import lax
from jax.experimental import pallas as pl
from jax.experimental.pallas import tpu as pltpu
from jax.experimental.pallas import tpu_sc as plsc

F32 = jnp.float32
BF16 = jnp.bfloat16
I32 = jnp.int32
HI = lax.Precision.HIGHEST

D_MODEL = 1024
NORM_EPS = 1e-6
LANES = 128
SUBLANES = 8
VMEM_LIMIT = 56 * 1024 * 1024

A_WIDTH = 512
A_HEADS = 4
B_WIDTH = 512
B_HEADS = 4
B_DK = 64
GLA_RANK = 16
GLA_NORMALIZER = 16.0
Z_WIDTH = 3712
C_HEAD = 64
C_HEADS = 16
C_GN_EPS = 64e-5
P_HEADS = 8
P_KEYS = 128
P_TOPK = 16
P_PICKS = P_HEADS * P_TOPK
SC_CORES = 2
SC_SUBCORES = 16
SC_WORKERS = SC_CORES * SC_SUBCORES
SC_LANES = 16
SC_CHUNK = 64
SC_TOK_BLOCKS = (32, 16)
SC_WORDS = D_MODEL // 2
SC_HID_ROWS = 8
SC_HID_WORDS = 2
PROMPT_GROUPS = (2,) * 8
GATED_SEQS_PER_STEP = 2


def _cparams(sem):
    return pltpu.CompilerParams(dimension_semantics=sem, vmem_limit_bytes=VMEM_LIMIT)


def _rms(x, g):
    ms = jnp.mean(x * x, axis=-1, keepdims=True)
    return x * lax.rsqrt(ms + NORM_EPS) * g


def _dot(a, b, precision=None):
    return jnp.dot(a, b, preferred_element_type=F32, precision=precision)


def _dot_nt(a, b, precision=None):
    return lax.dot_general(a, b, (((1,), (1,)), ((), ())), preferred_element_type=F32, precision=precision)


def _dot_tn(a, b, precision=None):
    return lax.dot_general(a, b, (((0,), (0,)), ((), ())), preferred_element_type=F32, precision=precision)


def _tri(n, strict):
    r = lax.broadcasted_iota(I32, (n, n), 0)
    c = lax.broadcasted_iota(I32, (n, n), 1)
    return (c < r) if strict else (c <= r)


def _cumsum_rows(g):
    return _dot(_tri(g.shape[0], False).astype(F32), g, precision=HI)


def _lane_mask(width, lo, hi):
    l = lax.broadcasted_iota(I32, (1, width), 1)
    return (l >= lo) & (l < hi)


def _sigmoid(x):
    return 1.0 / (1.0 + jnp.exp(-x))


def _silu(x):
    return x * _sigmoid(x)


def _norm_proj_kernel(x_ref, g_ref, w_ref, o_ref):
    hn = _rms(x_ref[...], g_ref[...])
    o_ref[...] = _dot(hn.astype(BF16), w_ref[...])


def norm_proj(x, g, w_bf16, tm):
    n, d = x.shape
    f = w_bf16.shape[1]
    return pl.pallas_call(
        _norm_proj_kernel,
        grid=(n // tm,),
        in_specs=[pl.BlockSpec((tm, d), lambda i: (i, 0)),
                  pl.BlockSpec((1, d), lambda i: (0, 0)),
                  pl.BlockSpec((d, f), lambda i: (0, 0))],
        out_specs=pl.BlockSpec((tm, f), lambda i: (i, 0)),
        out_shape=jax.ShapeDtypeStruct((n, f), F32),
        compiler_params=_cparams(("parallel",)),
        name="norm_proj",
    )(x, g.reshape(1, d), w_bf16)


def _out_proj2_kernel(x_ref, a_ref, b_ref, wa_ref, wb_ref, o_ref):
    y = _dot(a_ref[...].astype(BF16), wa_ref[...]) + _dot(b_ref[...].astype(BF16), wb_ref[...])
    o_ref[...] = x_ref[...] + y


def out_proj2(x, a, b, wa, wb, tm):
    n, d = x.shape
    ka, kb = a.shape[1], b.shape[1]
    return pl.pallas_call(
        _out_proj2_kernel,
        grid=(n // tm,),
        in_specs=[pl.BlockSpec((tm, d), lambda i: (i, 0)),
                  pl.BlockSpec((tm, ka), lambda i: (i, 0)),
                  pl.BlockSpec((tm, kb), lambda i: (i, 0)),
                  pl.BlockSpec((ka, d), lambda i: (0, 0)),
                  pl.BlockSpec((kb, d), lambda i: (0, 0))],
        out_specs=pl.BlockSpec((tm, d), lambda i: (i, 0)),
        out_shape=jax.ShapeDtypeStruct((n, d), F32),
        compiler_params=_cparams(("parallel",)),
        name="out_proj2",
    )(x, a, b, wa, wb)


def _out_proj1_kernel(x_ref, a_ref, wa_ref, o_ref):
    o_ref[...] = x_ref[...] + _dot(a_ref[...].astype(BF16), wa_ref[...])


def out_proj1(x, a, wa, tm):
    n, d = x.shape
    ka = a.shape[1]
    return pl.pallas_call(
        _out_proj1_kernel,
        grid=(n // tm,),
        in_specs=[pl.BlockSpec((tm, d), lambda i: (i, 0)),
                  pl.BlockSpec((tm, ka), lambda i: (i, 0)),
                  pl.BlockSpec((ka, d), lambda i: (0, 0))],
        out_specs=pl.BlockSpec((tm, d), lambda i: (i, 0)),
        out_shape=jax.ShapeDtypeStruct((n, d), F32),
        compiler_params=_cparams(("parallel",)),
        name="out_proj1",
    )(x, a, wa)


def _final_norm_kernel(x_ref, g_ref, o_ref):
    o_ref[...] = _rms(x_ref[...], g_ref[...])


def final_norm(x, g, tm):
    n, d = x.shape
    return pl.pallas_call(
        _final_norm_kernel,
        grid=(n // tm,),
        in_specs=[pl.BlockSpec((tm, d), lambda i: (i, 0)), pl.BlockSpec((1, d), lambda i: (0, 0))],
        out_specs=pl.BlockSpec((tm, d), lambda i: (i, 0)),
        out_shape=jax.ShapeDtypeStruct((n, d), F32),
        compiler_params=_cparams(("parallel",)),
        name="final_norm",
    )(x, g.reshape(1, d))


def _intra_chunk(problems):
    c = problems[0][0].shape[0]
    nb = c // SUBLANES
    row = lax.broadcasted_iota(I32, (SUBLANES, 1), 0)
    qbs = [[q[SUBLANES * i:SUBLANES * (i + 1)] for i in range(nb)] for q, _, _, _ in problems]
    bbs = [[b[SUBLANES * i:SUBLANES * (i + 1)] for i in range(nb)] for _, _, b, _ in problems]
    outs = [[[None] * nb for _ in heads] for _, _, _, heads in problems]
    for s in range(c):
        rb0 = s // SUBLANES
        for pi, (_, k, b, heads) in enumerate(problems):
            ks = k[s:s + 1, :]
            bs = b[s:s + 1, :]
            for rb in range(rb0, nb):
                p = qbs[pi][rb] * (ks * jnp.exp(bbs[pi][rb] - bs))
                for hi, (mask, v) in enumerate(heads):
                    pm = p if mask is None else jnp.where(mask, p, 0.0)
                    col = jnp.sum(pm, axis=-1, keepdims=True)
                    if rb == rb0:
                        col = jnp.where(row + SUBLANES * rb >= s, col, 0.0)
                    term = col * v[s:s + 1, :]
                    prev = outs[pi][hi][rb]
                    outs[pi][hi][rb] = term if prev is None else prev + term
    return [[jnp.concatenate(o, axis=0) for o in po] for po in outs]


def _gated_chunk(problems, states):
    intra = _intra_chunk(problems)
    qes = [q * jnp.exp(b) for q, _, b, _ in problems]
    b_lasts = [b[-1:, :] for _, _, b, _ in problems]
    khs = [k * jnp.exp(bl - b) for (_, k, b, _), bl in zip(problems, b_lasts)]
    outs = []
    for (_, _, _, heads), qe, st, po in zip(problems, qes, states, intra):
        o = []
        for (mask, v), oi in zip(heads, po):
            qm = qe if mask is None else jnp.where(mask, qe, 0.0)
            o.append(oi + _dot_nt(qm, st, precision=HI))
        outs.append(o)
    new_states = []
    for (_, _, _, heads), kh, st, bl in zip(problems, khs, states, b_lasts):
        upd = _dot_tn(heads[0][1], kh, precision=HI)
        if len(heads) == 2:
            upd = jnp.where(heads[0][0], upd, _dot_tn(heads[1][1], kh, precision=HI))
        new_states.append(st * jnp.exp(bl) + upd)
    return outs, new_states


def _head_rms(o, g):
    ms = jnp.mean(o * o, axis=-1, keepdims=True)
    return o * lax.rsqrt(ms + NORM_EPS) * g


def _hgrn_kernel(zq_ref, zf_ref, zi_ref, zg_ref, lb_ref, ng_ref, s0_ref, o_ref, s_ref, st_scr, *, chunk):
    t = pl.program_id(2)

    @pl.when(t == 0)
    def _():
        st_scr[...] = s0_ref[:, 0]

    lb = lb_ref[0]
    nseq = zq_ref.shape[0]
    nchunks = zq_ref.shape[1] // chunk

    def body(ci, carry):
        sl = pl.ds(pl.multiple_of(ci * chunk, chunk), chunk)
        problems = []
        for i in range(nseq):
            f = lb + (1.0 - lb) * _sigmoid(zf_ref[i, sl, :])
            problems.append((_silu(zq_ref[i, sl, :]), 1.0 - f, _cumsum_rows(jnp.log(f)), [(None, zi_ref[i, sl, :])]))
        outs, states = _gated_chunk(problems, [st_scr[i] for i in range(nseq)])
        for i in range(nseq):
            st_scr[i] = states[i]
            o_ref[i, sl, :] = _head_rms(outs[i][0], ng_ref[...]) * _silu(zg_ref[i, sl, :])
        return carry

    lax.fori_loop(0, nchunks, body, 0)

    @pl.when(t == pl.num_programs(2) - 1)
    def _():
        s_ref[:, 0] = st_scr[...]


def hgrn_recurrence(z, lb, norm_g, s0_t, tb, chunk, nb):
    bsz, t, _ = z.shape
    zspec = lambda off: pl.BlockSpec((nb, tb, LANES), lambda b, h, i: (b, i, h + off))
    return pl.pallas_call(
        functools.partial(_hgrn_kernel, chunk=chunk),
        grid=(bsz // nb, A_HEADS, t // tb),
        in_specs=[zspec(0), zspec(4), zspec(8), zspec(12),
                  pl.BlockSpec((1, 1, LANES), lambda b, h, i: (h, 0, 0)),
                  pl.BlockSpec((1, LANES), lambda b, h, i: (0, 0)),
                  pl.BlockSpec((nb, 1, LANES, LANES), lambda b, h, i: (b, h, 0, 0))],
        out_specs=[pl.BlockSpec((nb, tb, LANES), lambda b, h, i: (b, i, h)),
                   pl.BlockSpec((nb, 1, LANES, LANES), lambda b, h, i: (b, h, 0, 0))],
        out_shape=[jax.ShapeDtypeStruct((bsz, t, A_WIDTH), F32),
                   jax.ShapeDtypeStruct((bsz, A_HEADS, LANES, LANES), F32)],
        scratch_shapes=[pltpu.VMEM((nb, LANES, LANES), F32)],
        compiler_params=_cparams(("parallel", "parallel", "arbitrary")),
        name="hgrn_recurrence",
    )(z, z, z, z, lb.reshape(A_HEADS, 1, LANES), norm_g.reshape(1, LANES), s0_t)


def _gla_kernel(zq_ref, zk_ref, zv_ref, zg_ref, zlr_ref, w2_ref, gb_ref, ng_ref, s0_ref, o_ref, s_ref, st_scr,
                *, chunk):
    t = pl.program_id(2)

    @pl.when(t == 0)
    def _():
        st_scr[...] = s0_ref[:, 0]

    nseq = zq_ref.shape[0]
    nchunks = zq_ref.shape[1] // chunk
    m0 = _lane_mask(LANES, 0, B_DK)
    m1 = _lane_mask(LANES, B_DK, LANES)

    def body(ci, carry):
        sl = pl.ds(pl.multiple_of(ci * chunk, chunk), chunk)
        problems = []
        for i in range(nseq):
            pre = _dot(zlr_ref[i, sl, :], w2_ref[...], precision=HI) + gb_ref[...]
            log_g = (jnp.minimum(pre, 0.0) - jnp.log(1.0 + jnp.exp(-jnp.abs(pre)))) * (1.0 / GLA_NORMALIZER)
            v = zv_ref[i, sl, :]
            problems.append((zq_ref[i, sl, :] * (B_DK ** -0.5), zk_ref[i, sl, :], _cumsum_rows(log_g),
                             [(m0, v[:, :LANES]), (m1, v[:, LANES:])]))
        outs, states = _gated_chunk(problems, [st_scr[i] for i in range(nseq)])
        for i in range(nseq):
            st_scr[i] = states[i]
            gate = _silu(zg_ref[i, sl, :])
            o_ref[i, sl, 0:LANES] = _head_rms(outs[i][0], ng_ref[...]) * gate[:, :LANES]
            o_ref[i, sl, LANES:2 * LANES] = _head_rms(outs[i][1], ng_ref[...]) * gate[:, LANES:]
        return carry

    lax.fori_loop(0, nchunks, body, 0)

    @pl.when(t == pl.num_programs(2) - 1)
    def _():
        s_ref[:, 0] = st_scr[...]


def gla_recurrence(z, w2pad, gate_b, norm_g, s0_t, tb, chunk, nb):
    bsz, t, _ = z.shape
    npairs = B_HEADS // 2
    return pl.pallas_call(
        functools.partial(_gla_kernel, chunk=chunk),
        grid=(bsz // nb, npairs, t // tb),
        in_specs=[pl.BlockSpec((nb, tb, LANES), lambda b, p, i: (b, i, 16 + p)),
                  pl.BlockSpec((nb, tb, LANES), lambda b, p, i: (b, i, 18 + p)),
                  pl.BlockSpec((nb, tb, 2 * LANES), lambda b, p, i: (b, i, 10 + p)),
                  pl.BlockSpec((nb, tb, 2 * LANES), lambda b, p, i: (b, i, 12 + p)),
                  pl.BlockSpec((nb, tb, LANES), lambda b, p, i: (b, i, 28)),
                  pl.BlockSpec((LANES, LANES), lambda b, p, i: (0, p)),
                  pl.BlockSpec((1, LANES), lambda b, p, i: (0, p)),
                  pl.BlockSpec((1, LANES), lambda b, p, i: (0, 0)),
                  pl.BlockSpec((nb, 1, LANES, LANES), lambda b, p, i: (b, p, 0, 0))],
        out_specs=[pl.BlockSpec((nb, tb, 2 * LANES), lambda b, p, i: (b, i, p)),
                   pl.BlockSpec((nb, 1, LANES, LANES), lambda b, p, i: (b, p, 0, 0))],
        out_shape=[jax.ShapeDtypeStruct((bsz, t, B_WIDTH), F32),
                   jax.ShapeDtypeStruct((bsz, npairs, LANES, LANES), F32)],
        scratch_shapes=[pltpu.VMEM((nb, LANES, LANES), F32)],
        compiler_params=_cparams(("parallel", "parallel", "arbitrary")),
        name="gla_recurrence",
    )(z, z, z, z, z, w2pad, gate_b.reshape(1, 2 * LANES), norm_g.reshape(1, LANES), s0_t)


def _rwkv_proj_kernel(x_ref, xp_ref, xl_ref, g1_ref, mu_ref, wr_ref, wk_ref, wv_ref, ww1_ref, ww2_ref, w0_ref,
                      aw1_ref, aw2_ref, a0_ref, gw1_ref, gw2_ref, kk_ref, ka_ref,
                      r_out, k_out, v_out, lw_out, a_out, kk_out, g_out, hl_out, *, tiles_per_seq):
    i = pl.program_id(0)
    g1 = g1_ref[...]
    hn = _rms(x_ref[...], g1)
    tm = hn.shape[0]
    prev = _rms(xp_ref[...], g1)[SUBLANES - 1:SUBLANES, :]
    prev = jnp.where(i % tiles_per_seq == 0, xl_ref[0], prev)
    row = lax.broadcasted_iota(I32, (tm, 1), 0)
    xprev = jnp.where(row == 0, prev, pltpu.roll(hn, 1, axis=0))
    dx = xprev - hn

    def mix(j):
        return hn + dx * mu_ref[j:j + 1, :]

    r = _dot(mix(0).astype(BF16), wr_ref[...])
    k = _dot(mix(1).astype(BF16), wk_ref[...])
    v = _dot(mix(2).astype(BF16), wv_ref[...])
    wl = _dot(jnp.tanh(_dot(mix(3), ww1_ref[...], precision=HI)), ww2_ref[...], precision=HI)
    z = w0_ref[...] + wl
    wpre = -(jnp.maximum(-z, 0.0) + jnp.log(1.0 + jnp.exp(-jnp.abs(z)))) - 0.5
    al = _dot(_dot(mix(4), aw1_ref[...], precision=HI), aw2_ref[...], precision=HI)
    a = _sigmoid(a0_ref[...] + al)
    gg = _dot(_sigmoid(_dot(mix(5).astype(BF16), gw1_ref[...])).astype(BF16), gw2_ref[...])
    r_out[...] = r
    k_out[...] = k * (1.0 + (a - 1.0) * ka_ref[...])
    v_out[...] = v
    lw_out[...] = -jnp.exp(wpre)
    a_out[...] = a
    kk_out[...] = k * kk_ref[...]
    g_out[...] = gg
    hl_out[0] = hn[tm - 1:tm, :]


def rwkv_proj(x, x_last, seq_len, g1, mu, wr, wk, wv, ww1, ww2, w0, aw1, aw2, a0, gw1, gw2, k_k, k_a, tm):
    n, d = x.shape
    tiles_per_seq = seq_len // tm
    row = lambda a: a.reshape(1, d)
    full = lambda a: pl.BlockSpec(a.shape, lambda i: (0,) * a.ndim)
    tile = pl.BlockSpec((tm, d), lambda i: (i, 0))
    blocks8 = tm // SUBLANES
    args = (x, x, x_last.reshape(-1, 1, d), row(g1), mu, wr, wk, wv, ww1, ww2, row(w0), aw1, aw2, row(a0),
            gw1, gw2, row(k_k), row(k_a))
    in_specs = [tile,
                pl.BlockSpec((SUBLANES, d), lambda i: (jnp.maximum(i * blocks8 - 1, 0), 0)),
                pl.BlockSpec((1, 1, d), lambda i: (i // tiles_per_seq, 0, 0))]
    in_specs += [full(a) for a in args[3:]]
    outs = pl.pallas_call(
        functools.partial(_rwkv_proj_kernel, tiles_per_seq=tiles_per_seq),
        grid=(n // tm,),
        in_specs=in_specs,
        out_specs=[tile] * 7 + [pl.BlockSpec((1, 1, d), lambda i: (i, 0, 0))],
        out_shape=[jax.ShapeDtypeStruct((n, d), F32)] * 7 + [jax.ShapeDtypeStruct((n // tm, 1, d), F32)],
        compiler_params=_cparams(("parallel",)),
        name="rwkv_proj",
    )(*args)
    return outs


_NN = ((1,), (0,))
_NT = ((1,), (1,))
_TN = ((0,), (0,))
RWKV_AB_PASSES = 1
RWKV_INV_PASSES = 1
RWKV_APPLY_PASSES = 1
RWKV_STATE_PASSES = 3
RWKV_SEQS_PER_STEP = 4


def _split_bf16(a):
    hi = a.astype(BF16)
    return hi, (a - hi.astype(F32)).astype(BF16)


def _mm(a, b, dims, passes):
    if passes == 6:
        return lax.dot_general(a, b, (dims, ((), ())), preferred_element_type=F32, precision=HI)
    dg = lambda x, y: lax.dot_general(x, y, (dims, ((), ())), preferred_element_type=F32)
    ah, al = _split_bf16(a)
    bh, bl = _split_bf16(b)
    if passes == 1:
        return dg(ah, bh)
    return dg(ah, bh) + (dg(al, bh) + dg(ah, bl))


def _cumsum_rows3(g):
    tri = _tri(g.shape[0], False).astype(BF16)
    h1 = g.astype(BF16)
    r1 = g - h1.astype(F32)
    h2 = r1.astype(BF16)
    h3 = (r1 - h2.astype(F32)).astype(BF16)
    return _dot(tri, h1) + (_dot(tri, h2) + _dot(tri, h3))


def _pair_sum(x, m0):
    s0 = jnp.sum(jnp.where(m0, x, 0.0), axis=-1, keepdims=True)
    s1 = jnp.sum(jnp.where(m0, 0.0, x), axis=-1, keepdims=True)
    return jnp.where(m0, s0, s1)


def _rwkv_kernel(r_ref, k_ref, v_ref, lw_ref, a_ref, kk_ref, g_ref, rk_ref, lng_ref, lnb_ref, s0_ref,
                 o_ref, s_ref, mt_scr, *, chunk):
    t = pl.program_id(2)

    @pl.when(t == 0)
    def _():
        mt_scr[...] = s0_ref[:, 0]

    nrows = r_ref.shape[0]
    nchunks = r_ref.shape[1] // chunk
    c2 = 2 * chunk
    m0 = _lane_mask(LANES, 0, C_HEAD)
    rowi = lax.broadcasted_iota(I32, (LANES, LANES), 0)
    coli = lax.broadcasted_iota(I32, (LANES, LANES), 1)
    blockdiag = (rowi < C_HEAD) == (coli < C_HEAD)
    ti = lax.broadcasted_iota(I32, (c2, c2), 0)
    si = lax.broadcasted_iota(I32, (c2, c2), 1)
    same_head = (ti < chunk) == (si < chunk)
    tm_ = jnp.where(ti < chunk, ti, ti - chunk)
    sm_ = jnp.where(si < chunk, si, si - chunk)
    strict = same_head & (sm_ < tm_)
    incl = same_head & (sm_ <= tm_)

    def stack_heads(x):
        return jnp.concatenate([jnp.where(m0, x, 0.0), jnp.where(m0, 0.0, x)], axis=0)

    def twice(x):
        return jnp.concatenate([x, x], axis=0)

    def unstack(x2):
        return jnp.where(m0, x2[:chunk], x2[chunk:])

    eye = (ti == si).astype(F32)
    seqs = range(nrows)

    def body(ci, carry):
        sl = pl.ds(pl.multiple_of(ci * chunk, chunk), chunk)
        r = [r_ref[i, sl, :] for i in seqs]
        k = [k_ref[i, sl, :] for i in seqs]
        v = [v_ref[i, sl, :] for i in seqs]
        lw = [lw_ref[i, sl, :] for i in seqs]
        kkr = [kk_ref[i, sl, :] for i in seqs]
        kk = [x * lax.rsqrt(_pair_sum(x * x, m0) + 1e-12) for x in kkr]
        al = [a_ref[i, sl, :] * kk[i] for i in seqs]
        gam = [_cumsum_rows3(x) for x in lw]
        e_neg = [jnp.exp(-x) for x in gam]
        xr = [jnp.concatenate([stack_heads(kk[i] * jnp.exp(gam[i] - lw[i])),
                               stack_heads(r[i] * jnp.exp(gam[i]))], axis=0) for i in seqs]
        alk = [jnp.concatenate([twice(al[i] * e_neg[i]), twice(k[i] * e_neg[i])], axis=0) for i in seqs]
        mt = [mt_scr[i] for i in seqs]
        ab = [_mm(xr[i], alk[i], _NT, RWKV_AB_PASSES) for i in seqs]
        xm = [_mm(xr[i], mt[i], _NT, RWKV_STATE_PASSES) for i in seqs]
        a_al = [jnp.where(strict, x[:c2, :c2], 0.0) for x in ab]
        a_k = [jnp.where(strict, x[:c2, c2:], 0.0) for x in ab]
        b_alk = [jnp.concatenate([jnp.where(incl, x[c2:, c2:], 0.0), jnp.where(incl, -x[c2:, :c2], 0.0)], axis=1)
                 for x in ab]
        v2 = [twice(x) for x in v]
        rhs = [xm[i][:c2] + _mm(a_k[i], v2[i], _NN, RWKV_APPLY_PASSES) for i in seqs]
        p = [-x for x in a_al]
        tinv = [eye + x for x in p]
        span = 2
        while span < chunk:
            p = [_mm(x, x, _NN, RWKV_INV_PASSES) for x in p]
            tinv = [tinv[i] + _mm(tinv[i], p[i], _NN, RWKV_INV_PASSES) for i in seqs]
            span *= 2
        u = [unstack(_mm(tinv[i], rhs[i], _NN, RWKV_APPLY_PASSES)) for i in seqs]
        o = [unstack(xm[i][c2:] + _mm(b_alk[i], jnp.concatenate([v2[i], twice(u[i])], axis=0), _NN,
                                      RWKV_APPLY_PASSES)) for i in seqs]
        g_last = [x[-1:, :] for x in gam]
        e_end = [jnp.exp(g_last[i] - gam[i]) for i in seqs]
        upd = [_mm(jnp.concatenate([v[i], u[i]], axis=0),
                   jnp.concatenate([k[i] * e_end[i], -(al[i] * e_end[i])], axis=0), _TN, RWKV_STATE_PASSES)
               for i in seqs]
        for i in seqs:
            mt_scr[i] = mt[i] * jnp.exp(g_last[i]) + jnp.where(blockdiag, upd[i], 0.0)
        for i in seqs:
            mean = _pair_sum(o[i], m0) * (1.0 / C_HEAD)
            cen = o[i] - mean
            var = _pair_sum(cen * cen, m0) * (1.0 / C_HEAD)
            on = cen * lax.rsqrt(var + C_GN_EPS) * lng_ref[...] + lnb_ref[...]
            bonus = _pair_sum(r[i] * k[i] * rk_ref[...], m0) * v[i]
            o_ref[i, sl, :] = (on + bonus) * g_ref[i, sl, :]
        return carry

    lax.fori_loop(0, nchunks, body, 0)

    @pl.when(t == pl.num_programs(2) - 1)
    def _():
        s_ref[:, 0] = mt_scr[...]


def rwkv_recurrence(r, k, v, lw, a, kk, g, r_k, ln_g, ln_b, s0_bd, tb, chunk, nb):
    bsz, t, d = r.shape
    npairs = C_HEADS // 2
    seq = pl.BlockSpec((nb, tb, LANES), lambda b, p, i: (b, i, p))
    vec = pl.BlockSpec((1, LANES), lambda b, p, i: (0, p))
    st = pl.BlockSpec((nb, 1, LANES, LANES), lambda b, p, i: (b, p, 0, 0))
    return pl.pallas_call(
        functools.partial(_rwkv_kernel, chunk=chunk),
        grid=(bsz // nb, npairs, t // tb),
        in_specs=[seq] * 7 + [vec, vec, vec, st],
        out_specs=[seq, st],
        out_shape=[jax.ShapeDtypeStruct((bsz, t, d), F32),
                   jax.ShapeDtypeStruct((bsz, npairs, LANES, LANES), F32)],
        scratch_shapes=[pltpu.VMEM((nb, LANES, LANES), F32)],
        compiler_params=_cparams(("parallel", "parallel", "arbitrary")),
        name="rwkv_recurrence",
    )(r, k, v, lw, a, kk, g, r_k.reshape(1, d), ln_g.reshape(1, d), ln_b.reshape(1, d), s0_bd)


NEG_INF = float("-inf")


def _top16_rows(s):
    n = s.shape[0]
    key = lax.broadcasted_iota(I32, s.shape, 0)
    vals, idxs = [], []
    for _ in range(P_TOPK):
        m = jnp.max(s, axis=0, keepdims=True)
        am = jnp.min(jnp.where(s == m, key, n), axis=0, keepdims=True)
        vals.append(m)
        idxs.append(am)
        s = jnp.where(key == am, NEG_INF, s)
    return vals, idxs


def _top16_pairs(v0, i0, v1, i1):
    a0 = jnp.concatenate(v0[0:8], axis=0)
    a1 = jnp.concatenate(v0[8:16], axis=0)
    b0 = jnp.concatenate(v1[0:8], axis=0)
    b1 = jnp.concatenate(v1[8:16], axis=0)
    ia0 = jnp.concatenate(i0[0:8], axis=0) * P_KEYS
    ia1 = jnp.concatenate(i0[8:16], axis=0) * P_KEYS
    ib0 = jnp.concatenate(i1[0:8], axis=0)
    ib1 = jnp.concatenate(i1[8:16], axis=0)
    row = lax.broadcasted_iota(I32, (SUBLANES, 1), 0)
    slabs = []

    def add(val, eid, keep):
        slabs.append((val if keep is None else jnp.where(keep, val, NEG_INF), eid))

    add(v0[0] + b0, ia0[0:1] + ib0, None)
    add(v0[0] + b1, ia0[0:1] + ib1, None)
    add(v0[1] + b0, ia0[1:2] + ib0, None)
    add(v0[2] + b0, ia0[2:3] + ib0, row < 5)
    add(v0[3] + b0, ia0[3:4] + ib0, row < 4)
    add(a0 + v1[0], ia0 + ib0[0:1], row >= 4)
    add(a1 + v1[0], ia1 + ib0[0:1], None)
    add(a0 + v1[1], ia0 + ib0[1:2], row >= 4)
    add(a0 + v1[2], ia0 + ib0[2:3], row == 4)

    big = P_KEYS * P_KEYS
    out_v, out_e = [], []
    for _ in range(P_TOPK):
        m = slabs[0][0]
        for val, _e in slabs[1:]:
            m = jnp.maximum(m, val)
        m = jnp.max(m, axis=0, keepdims=True)
        e = None
        for val, eid in slabs:
            c = jnp.where(val == m, eid, big)
            e = c if e is None else jnp.minimum(e, c)
        e = jnp.min(e, axis=0, keepdims=True)
        out_v.append(m)
        out_e.append(e)
        slabs = [(jnp.where(eid == e, NEG_INF, val), eid) for val, eid in slabs]
    return out_v, out_e


def _peer_select_kernel(x_ref, g_ref, wq_ref, keys_ref, xn_out, eid_out, gate_out):
    hn = _rms(x_ref[...], g_ref[...])
    xn_out[...] = hn
    q = _dot(hn.astype(BF16), wq_ref[...])
    tm = q.shape[0]
    for lt in range(tm // LANES):
        rows = slice(lt * LANES, (lt + 1) * LANES)
        e_rows, g_rows = [], []
        for h in range(P_HEADS):
            tops = []
            for p in range(2):
                hp = 2 * h + p
                s = _dot_nt(keys_ref[hp], q[rows, hp * LANES:(hp + 1) * LANES], precision=HI)
                tops.append(_top16_rows(s))
            cs, ce = _top16_pairs(tops[0][0], tops[0][1], tops[1][0], tops[1][1])
            ex = [jnp.exp(c - cs[0]) for c in cs]
            tot = ex[0]
            for e in ex[1:]:
                tot = tot + e
            inv = 1.0 / tot
            e_rows += ce
            g_rows += [e * inv for e in ex]
        eid_out[rows, :] = jnp.concatenate(e_rows, axis=0).T
        gate_out[rows, :] = jnp.concatenate(g_rows, axis=0).T


def peer_select(x, g, wq_bf16, keys, tm):
    n, d = x.shape
    return pl.pallas_call(
        _peer_select_kernel,
        grid=(n // tm,),
        in_specs=[pl.BlockSpec((tm, d), lambda i: (i, 0)),
                  pl.BlockSpec((1, d), lambda i: (0, 0)),
                  pl.BlockSpec(wq_bf16.shape, lambda i: (0, 0)),
                  pl.BlockSpec(keys.shape, lambda i: (0, 0, 0))],
        out_specs=[pl.BlockSpec((tm, d), lambda i: (i, 0)),
                   pl.BlockSpec((tm, P_PICKS), lambda i: (i, 0)),
                   pl.BlockSpec((tm, P_PICKS), lambda i: (i, 0))],
        out_shape=[jax.ShapeDtypeStruct((n, d), F32),
                   jax.ShapeDtypeStruct((n, P_PICKS), I32),
                   jax.ShapeDtypeStruct((n, P_PICKS), F32)],
        compiler_params=_cparams(("parallel",)),
        name="peer_select",
    )(x, g.reshape(1, d), wq_bf16, keys)


def _peer_act_kernel(h_ref, g_ref, o_ref):
    h = h_ref[...]
    o_ref[...] = 0.5 * h * (1.0 + lax.erf(h * (2.0 ** -0.5))) * g_ref[...]


def peer_act(hid, gate, tm):
    n, p = hid.shape
    spec = pl.BlockSpec((tm, p), lambda i: (i, 0))
    return pl.pallas_call(
        _peer_act_kernel, grid=(n // tm,), in_specs=[spec, spec], out_specs=spec,
        out_shape=jax.ShapeDtypeStruct((n, p), F32),
        compiler_params=_cparams(("parallel",)), name="peer_act",
    )(hid, gate)


_CHUNKS_PER_TOK = P_PICKS // SC_CHUNK
_WORD_VREGS = SC_WORDS // SC_LANES
U32 = jnp.uint32


def pack_rows_bf16(t):
    lo = lax.bitcast_convert_type(t[..., :SC_WORDS].astype(BF16), jnp.uint16).astype(U32)
    a = lax.bitcast_convert_type(t[..., SC_WORDS:], U32)
    sign = a & U32(0x80000000)
    mag = a & U32(0x7FFFFFFF)
    steps = (jnp.maximum(mag + U32(0x8000), lo) - lo) >> 16
    near = (steps << 16) + lo
    near = jnp.where(near >= U32(0x7F800000), near - U32(0x10000), near)
    return lax.bitcast_convert_type(sign | near, I32)


def _low_f32(w):
    return lax.bitcast_convert_type(w << 16, F32)


def _high_f32(w):
    return lax.bitcast_convert_type(w, F32)


def _sc_worker():
    return lax.axis_index("s") * SC_CORES + lax.axis_index("c")


def _sc_tok_block(tok_per_w):
    return next(b for b in SC_TOK_BLOCKS if tok_per_w % b == 0)


def _sc_hid_body(tab_hbm, idx_hbm, x_hbm, hid_hbm, idx_v, x_v, hid_v, rows_a, rows_b, sem_a, sem_b, sem_i, sem_x,
                 *, tok_per_w):
    wid = _sc_worker()
    blk = hid_v.shape[0]
    nblk = tok_per_w // blk
    nch = blk * _CHUNKS_PER_TOK
    lane = lax.iota(I32, SC_LANES)
    quads = SC_LANES // SC_HID_ROWS

    def dots(rows, x_row, out_row, col0):
        for grp in range(SC_CHUNK // SC_LANES):
            def some_rows(q, res):
                r0 = grp * SC_LANES + q * SC_HID_ROWS

                def some_words(jj, accs):
                    accs = list(accs)
                    for jw in range(SC_HID_WORDS):
                        off = (jj * SC_HID_WORDS + jw) * SC_LANES
                        xl = x_v[x_row, pl.ds(off, SC_LANES)]
                        xh = x_v[x_row, pl.ds(SC_WORDS + off, SC_LANES)]
                        for i in range(SC_HID_ROWS):
                            w = rows[r0 + i, pl.ds(off, SC_LANES)]
                            k = 2 * i + jw % 2
                            accs[k] = accs[k] + (_low_f32(w) * xl + _high_f32(w) * xh)
                    return tuple(accs)

                zero = jnp.zeros((SC_LANES,), F32)
                accs = lax.fori_loop(0, _WORD_VREGS // SC_HID_WORDS, some_words, (zero,) * (2 * SC_HID_ROWS))
                for i in range(SC_HID_ROWS):
                    res = jnp.where(lane == q * SC_HID_ROWS + i, jnp.sum(accs[2 * i] + accs[2 * i + 1]), res)
                return res
            res = lax.fori_loop(0, quads, some_rows, jnp.zeros((SC_LANES,), F32))
            hid_v[out_row, pl.ds(col0 + grp * SC_LANES, SC_LANES)] = res

    def stage(bi, slot):
        tok0 = wid * tok_per_w + bi * blk
        pltpu.async_copy(idx_hbm.at[pl.ds(tok0 * _CHUNKS_PER_TOK, nch)], idx_v.at[pl.ds(slot * nch, nch)], sem_i)
        pltpu.async_copy(x_hbm.at[pl.ds(tok0, blk)], x_v.at[pl.ds(slot * blk, blk)], sem_x)

    def wait_idx():
        pltpu.make_async_copy(idx_hbm.at[pl.ds(0, nch)], idx_v.at[pl.ds(0, nch)], sem_i).wait()

    def wait_x():
        pltpu.make_async_copy(x_hbm.at[pl.ds(0, blk)], x_v.at[pl.ds(0, blk)], sem_x).wait()

    stage(0, 0)
    wait_idx()
    pltpu.async_copy(tab_hbm.at[idx_v.at[0]], rows_a, sem_a)

    def block(bi, carry):
        slot = bi % 2
        tok0 = wid * tok_per_w + bi * blk
        has_next = bi + 1 < nblk
        i0 = slot * nch
        x0 = slot * blk
        wait_x()

        @pl.when(has_next)
        def _():
            stage(bi + 1, 1 - slot)

        def pair(j, c):
            ca = 2 * j
            pltpu.async_copy(tab_hbm.at[idx_v.at[i0 + ca + 1]], rows_b, sem_b)
            pltpu.make_async_copy(tab_hbm.at[idx_v.at[0]], rows_a, sem_a).wait()
            dots(rows_a, x0 + ca // _CHUNKS_PER_TOK, ca // _CHUNKS_PER_TOK, (ca % _CHUNKS_PER_TOK) * SC_CHUNK)
            last = j == nch // 2 - 1

            @pl.when(jnp.logical_not(last))
            def _():
                pltpu.async_copy(tab_hbm.at[idx_v.at[i0 + ca + 2]], rows_a, sem_a)

            @pl.when(jnp.logical_and(last, has_next))
            def _():
                wait_idx()
                pltpu.async_copy(tab_hbm.at[idx_v.at[(1 - slot) * nch]], rows_a, sem_a)
            pltpu.make_async_copy(tab_hbm.at[idx_v.at[0]], rows_b, sem_b).wait()
            cb = ca + 1
            dots(rows_b, x0 + cb // _CHUNKS_PER_TOK, cb // _CHUNKS_PER_TOK, (cb % _CHUNKS_PER_TOK) * SC_CHUNK)
            return c
        lax.fori_loop(0, nch // 2, pair, 0)
        pltpu.sync_copy(hid_v, hid_hbm.at[pl.ds(tok0, blk)])
        return carry

    lax.fori_loop(0, nblk, block, 0)


def sc_expert_hidden(table_packed, eidx, xn):
    n = xn.shape[0]
    tok_per_w = n // SC_WORKERS
    blk = SC_TOK_BLOCKS[-1]
    mesh = plsc.VectorSubcoreMesh(core_axis_name="c", subcore_axis_name="s")
    nch = blk * _CHUNKS_PER_TOK
    k = pl.kernel(
        functools.partial(_sc_hid_body, tok_per_w=tok_per_w), mesh=mesh,
        out_type=jax.ShapeDtypeStruct((n, P_PICKS), F32),
        scratch_types=[pltpu.VMEM((2 * nch, SC_CHUNK), I32),
                       pltpu.VMEM((2 * blk, D_MODEL), F32),
                       pltpu.VMEM((blk, P_PICKS), F32),
                       pltpu.VMEM((SC_CHUNK, SC_WORDS), I32),
                       pltpu.VMEM((SC_CHUNK, SC_WORDS), I32),
                       pltpu.SemaphoreType.DMA, pltpu.SemaphoreType.DMA,
                       pltpu.SemaphoreType.DMA, pltpu.SemaphoreType.DMA],
        compiler_params=pltpu.CompilerParams(needs_layout_passes=False),
        name="sc_expert_hidden",
    )
    return k(table_packed, eidx.reshape(n * _CHUNKS_PER_TOK, SC_CHUNK), xn)


def _sc_out_body(tab_hbm, idx_hbm, act_hbm, x_hbm, y_hbm, idx_v, act_v, y_v, rows_a, rows_b, sem_a, sem_b,
                 *, tok_per_w):
    wid = _sc_worker()
    blk = y_v.shape[0]
    nblk = tok_per_w // blk
    nch = blk * _CHUNKS_PER_TOK
    half_w = _WORD_VREGS // 2

    def accum(rows, tok, col0):
        for hv in range(2):
            base = hv * half_w * SC_LANES
            acc0 = (tuple(y_v[tok, pl.ds(base + j * SC_LANES, SC_LANES)] for j in range(half_w))
                    + tuple(y_v[tok, pl.ds(SC_WORDS + base + j * SC_LANES, SC_LANES)] for j in range(half_w)))

            def one(r, acc):
                aidx = jnp.full((SC_LANES,), col0, I32) + r
                wgt = plsc.load_gather(act_v, [jnp.full((SC_LANES,), tok, I32), aidx])
                lo, hi = [], []
                for j in range(half_w):
                    w = rows[r, pl.ds(base + j * SC_LANES, SC_LANES)]
                    lo.append(acc[j] + wgt * _low_f32(w))
                    hi.append(acc[half_w + j] + wgt * _high_f32(w))
                return tuple(lo + hi)
            acc = lax.fori_loop(0, SC_CHUNK, one, acc0)
            for j in range(half_w):
                y_v[tok, pl.ds(base + j * SC_LANES, SC_LANES)] = acc[j]
                y_v[tok, pl.ds(SC_WORDS + base + j * SC_LANES, SC_LANES)] = acc[half_w + j]

    def block(bi, carry):
        tok0 = wid * tok_per_w + bi * blk
        pltpu.sync_copy(idx_hbm.at[pl.ds(tok0 * _CHUNKS_PER_TOK, nch)], idx_v)
        pltpu.async_copy(tab_hbm.at[idx_v.at[0]], rows_a, sem_a)
        pltpu.sync_copy(act_hbm.at[pl.ds(tok0, blk)], act_v)
        pltpu.sync_copy(x_hbm.at[pl.ds(tok0, blk)], y_v)

        def pair(j, c):
            ca = 2 * j
            pltpu.async_copy(tab_hbm.at[idx_v.at[ca + 1]], rows_b, sem_b)
            pltpu.make_async_copy(tab_hbm.at[idx_v.at[0]], rows_a, sem_a).wait()
            accum(rows_a, ca // _CHUNKS_PER_TOK, (ca % _CHUNKS_PER_TOK) * SC_CHUNK)

            @pl.when(j < nch // 2 - 1)
            def _():
                pltpu.async_copy(tab_hbm.at[idx_v.at[ca + 2]], rows_a, sem_a)
            pltpu.make_async_copy(tab_hbm.at[idx_v.at[0]], rows_b, sem_b).wait()
            cb = ca + 1
            accum(rows_b, cb // _CHUNKS_PER_TOK, (cb % _CHUNKS_PER_TOK) * SC_CHUNK)
            return c
        lax.fori_loop(0, nch // 2, pair, 0)
        pltpu.sync_copy(y_v, y_hbm.at[pl.ds(tok0, blk)])
        return carry

    lax.fori_loop(0, nblk, block, 0)


def sc_expert_output(table_packed, eidx, act, x):
    n = x.shape[0]
    tok_per_w = n // SC_WORKERS
    blk = _sc_tok_block(tok_per_w)
    mesh = plsc.VectorSubcoreMesh(core_axis_name="c", subcore_axis_name="s")
    nch = blk * _CHUNKS_PER_TOK
    k = pl.kernel(
        functools.partial(_sc_out_body, tok_per_w=tok_per_w), mesh=mesh,
        out_type=jax.ShapeDtypeStruct((n, D_MODEL), F32),
        scratch_types=[pltpu.VMEM((nch, SC_CHUNK), I32),
                       pltpu.VMEM((blk, P_PICKS), F32),
                       pltpu.VMEM((blk, D_MODEL), F32),
                       pltpu.VMEM((SC_CHUNK, SC_WORDS), I32),
                       pltpu.VMEM((SC_CHUNK, SC_WORDS), I32),
                       pltpu.SemaphoreType.DMA, pltpu.SemaphoreType.DMA],
        compiler_params=pltpu.CompilerParams(needs_layout_passes=False),
        name="sc_expert_output",
    )
    return k(table_packed, eidx.reshape(n * _CHUNKS_PER_TOK, SC_CHUNK), act, x)


def peer_ffn(x, g, wq_bf16, keys, u_packed, v_packed, tm):
    xn, eidx, gate = peer_select(x, g, wq_bf16, keys, tm)
    hid = sc_expert_hidden(u_packed, eidx, xn)
    act = peer_act(hid, gate, tm)
    return sc_expert_output(v_packed, eidx, act, x)


def _prep_w_in(w_in):
    main = jnp.concatenate([w_in[:, :3072], w_in[:, 3088:3600]], axis=1)
    lr = jnp.pad(w_in[:, 3072:3088], ((0, 0), (0, LANES - GLA_RANK)))
    return jnp.concatenate([main, lr], axis=1).astype(BF16)


def _trunk(x, st_h, st_g, st_r, st_s, w, seq_len, tm, tmp, tb, chunk):
    bsz = x.shape[0]
    n = bsz * seq_len
    x2 = x.reshape(n, D_MODEL)

    z = norm_proj(x2, w["norm1_g"][0], w["w_in"], tm).reshape(bsz, seq_len, Z_WIDTH)
    s0_h = jnp.swapaxes(st_h, -1, -2)
    s0_g = jnp.swapaxes(st_g.reshape(bsz, 2, 2 * B_DK, LANES), -1, -2)
    o_a, sh_t = hgrn_recurrence(z, w["lb0"], w["hgrn_norm_g"], s0_h, tb, chunk, GATED_SEQS_PER_STEP)
    o_b, sg_t = gla_recurrence(z, w["gla_w2"], w["gla_b"], w["gla_norm_g"], s0_g, tb, chunk, GATED_SEQS_PER_STEP)
    new_h = jnp.swapaxes(sh_t, -1, -2)
    new_g = jnp.swapaxes(sg_t, -1, -2).reshape(bsz, B_HEADS, B_DK, LANES)
    x2 = out_proj2(x2, o_a.reshape(n, A_WIDTH), o_b.reshape(n, B_WIDTH), w["w_out_a"], w["w_out_b"], tm)
    x2 = peer_ffn(x2, w["norm2_g"][0], w["peer_wq"][0], w["peer_keys"][0], w["peer_u"][0], w["peer_v"][0], tmp)

    r, k, v, lw, a, kk, g, hl = rwkv_proj(
        x2, st_s, seq_len, w["norm1_g"][1], w["mu"], w["wr"], w["wk"], w["wv"], w["w_w1"], w["w_w2"], w["w0"],
        w["a_w1"], w["a_w2"], w["a0"], w["g_w1"], w["g_w2"], w["k_k"], w["k_a"], tm)
    new_s = hl.reshape(bsz, seq_len // tm, D_MODEL)[:, -1]
    pr = st_r.reshape(bsz, C_HEADS // 2, 2, C_HEAD, C_HEAD)
    zero = jnp.zeros_like(pr[:, :, 0])
    s0_r = jnp.concatenate([jnp.concatenate([pr[:, :, 0], zero], axis=-1),
                            jnp.concatenate([zero, pr[:, :, 1]], axis=-1)], axis=-2)
    sh3 = lambda t: t.reshape(bsz, seq_len, D_MODEL)
    o_c, sr_bd = rwkv_recurrence(sh3(r), sh3(k), sh3(v), sh3(lw), sh3(a), sh3(kk), sh3(g),
                                 w["r_k"], w["ln_g"], w["ln_b"], s0_r, tb, chunk, min(RWKV_SEQS_PER_STEP, bsz))
    new_r = jnp.stack([sr_bd[:, :, :C_HEAD, :C_HEAD], sr_bd[:, :, C_HEAD:, C_HEAD:]], axis=2)
    new_r = new_r.reshape(bsz, C_HEADS, C_HEAD, C_HEAD)
    x2 = out_proj1(x2, o_c.reshape(n, D_MODEL), w["w_out_c"], tm)
    x2 = peer_ffn(x2, w["norm2_g"][1], w["peer_wq"][1], w["peer_keys"][1], w["peer_u"][1], w["peer_v"][1], tmp)

    y = final_norm(x2, w["final_g"], tm).reshape(bsz, seq_len, D_MODEL)
    return y, new_h[None], new_g[None], new_r[None], new_s[None]


def kernel(x_prompt, x_sample, state_hgrn, state_gla, state_rwkv, state_shift, w_in_ab, hgrn_lower_bounds, hgrn_norm_g, gla_gate_w2, gla_gate_b, gla_norm_g, w_out_ab, rwkv_mu, rwkv_w_rkv, rwkv_w_w1, rwkv_w_w2, rwkv_w0, rwkv_a_w1, rwkv_a_w2, rwkv_a0, rwkv_g_w1, rwkv_g_w2, rwkv_k_k, rwkv_k_a, rwkv_r_k, rwkv_ln_g, rwkv_ln_b, w_out_c, norm1_g, norm2_g, final_g, peer_w_q, peer_sub_keys, peer_u, peer_v):
    lbs = jnp.cumsum(jax.nn.softmax(hgrn_lower_bounds.astype(F32), axis=0), axis=0)
    w = dict(
        norm1_g=norm1_g, norm2_g=norm2_g, final_g=final_g,
        w_in=_prep_w_in(w_in_ab[0]), lb0=lbs[0], hgrn_norm_g=hgrn_norm_g[0],
        gla_w2=jnp.pad(gla_gate_w2[0], ((0, LANES - GLA_RANK), (0, 0))), gla_b=gla_gate_b[0],
        gla_norm_g=gla_norm_g[0],
        w_out_a=w_out_ab[0, :A_WIDTH].astype(BF16), w_out_b=w_out_ab[0, A_WIDTH:].astype(BF16),
        mu=rwkv_mu[0], wr=rwkv_w_rkv[0, 0].astype(BF16), wk=rwkv_w_rkv[0, 1].astype(BF16),
        wv=rwkv_w_rkv[0, 2].astype(BF16), w_w1=rwkv_w_w1[0], w_w2=rwkv_w_w2[0], w0=rwkv_w0[0],
        a_w1=rwkv_a_w1[0], a_w2=rwkv_a_w2[0], a0=rwkv_a0[0],
        g_w1=rwkv_g_w1[0].astype(BF16), g_w2=rwkv_g_w2[0].astype(BF16),
        k_k=rwkv_k_k[0], k_a=rwkv_k_a[0], r_k=rwkv_r_k[0], ln_g=rwkv_ln_g[0], ln_b=rwkv_ln_b[0],
        w_out_c=w_out_c[0].astype(BF16),
        peer_wq=peer_w_q.astype(BF16),
        peer_keys=peer_sub_keys.reshape(peer_sub_keys.shape[0], 2 * P_HEADS, P_KEYS, P_KEYS),
        peer_u=pack_rows_bf16(peer_u), peer_v=pack_rows_bf16(peer_v),
    )
    bp, tp, _ = x_prompt.shape
    bs, ts, _ = x_sample.shape
    assert sum(PROMPT_GROUPS) == bp
    groups, start = [], 0
    for gsz in PROMPT_GROUPS:
        zeros = lambda s: jnp.zeros((gsz,) + s.shape[2:], F32)
        groups.append(_trunk(x_prompt[start:start + gsz], zeros(state_hgrn), zeros(state_gla), zeros(state_rwkv),
                             zeros(state_shift), w, tp, tm=256, tmp=256, tb=512, chunk=64))
        start += gsz
    y_p = jnp.concatenate([g[0] for g in groups], axis=0)
    p_h, p_g, p_r, p_s = (jnp.concatenate([g[j] for g in groups], axis=1) for j in range(1, 5))
    y_s, s_h, s_g, s_r, s_s = _trunk(x_sample, state_hgrn[0], state_gla[0], state_rwkv[0], state_shift[0],
                                     w, ts, tm=32, tmp=128, tb=32, chunk=32)
    return (y_p, y_s, p_h, p_g, p_r, p_s, s_h, s_g, s_r, s_s)
```

```python
import functools

import jax
import jax.numpy as jnp
from jax import lax
from jax.experimental import pallas as pl
from jax.experimental.pallas import tpu as pltpu
from jax.experimental.pallas import tpu_sc as plsc

F32 = jnp.float32
BF16 = jnp.bfloat16
I32 = jnp.int32
HI = lax.Precision.HIGHEST

D_MODEL = 1024
NORM_EPS = 1e-6
LANES = 128
SUBLANES = 8
VMEM_LIMIT = 56 * 1024 * 1024

A_WIDTH = 512
A_HEADS = 4
B_WIDTH = 512
B_HEADS = 4
B_DK = 64
GLA_RANK = 16
GLA_NORMALIZER = 16.0
Z_WIDTH = 3712
C_HEAD = 64
C_HEADS = 16
C_GN_EPS = 64e-5
P_HEADS = 8
P_KEYS = 128
P_TOPK = 16
P_PICKS = P_HEADS * P_TOPK
SC_CORES = 2
SC_SUBCORES = 16
SC_WORKERS = SC_CORES * SC_SUBCORES
SC_LANES = 16
SC_CHUNK = 64
SC_TOK_BLOCKS = (32, 16)
SC_WORDS = D_MODEL // 2
SC_HID_ROWS = 8
SC_HID_WORDS = 2
PROMPT_GROUPS = (2,) * 8
GATED_SEQS_PER_STEP = 2


def _cparams(sem):
    return pltpu.CompilerParams(dimension_semantics=sem, vmem_limit_bytes=VMEM_LIMIT)


def _rms(x, g):
    ms = jnp.mean(x * x, axis=-1, keepdims=True)
    return x * lax.rsqrt(ms + NORM_EPS) * g


def _dot(a, b, precision=None):
    return jnp.dot(a, b, preferred_element_type=F32, precision=precision)


def _dot_nt(a, b, precision=None):
    return lax.dot_general(a, b, (((1,), (1,)), ((), ())), preferred_element_type=F32, precision=precision)


def _dot_tn(a, b, precision=None):
    return lax.dot_general(a, b, (((0,), (0,)), ((), ())), preferred_element_type=F32, precision=precision)


def _tri(n, strict):
    r = lax.broadcasted_iota(I32, (n, n), 0)
    c = lax.broadcasted_iota(I32, (n, n), 1)
    return (c < r) if strict else (c <= r)


def _cumsum_rows(g):
    return _dot(_tri(g.shape[0], False).astype(F32), g, precision=HI)


def _lane_mask(width, lo, hi):
    l = lax.broadcasted_iota(I32, (1, width), 1)
    return (l >= lo) & (l < hi)


def _sigmoid(x):
    return 1.0 / (1.0 + jnp.exp(-x))


def _silu(x):
    return x * _sigmoid(x)


def _norm_proj_kernel(x_ref, g_ref, w_ref, o_ref):
    hn = _rms(x_ref[...], g_ref[...])
    o_ref[...] = _dot(hn.astype(BF16), w_ref[...])


def norm_proj(x, g, w_bf16, tm):
    n, d = x.shape
    f = w_bf16.shape[1]
    return pl.pallas_call(
        _norm_proj_kernel,
        grid=(n // tm,),
        in_specs=[pl.BlockSpec((tm, d), lambda i: (i, 0)),
                  pl.BlockSpec((1, d), lambda i: (0, 0)),
                  pl.BlockSpec((d, f), lambda i: (0, 0))],
        out_specs=pl.BlockSpec((tm, f), lambda i: (i, 0)),
        out_shape=jax.ShapeDtypeStruct((n, f), F32),
        compiler_params=_cparams(("parallel",)),
        name="norm_proj",
    )(x, g.reshape(1, d), w_bf16)


def _out_proj2_kernel(x_ref, a_ref, b_ref, wa_ref, wb_ref, o_ref):
    y = _dot(a_ref[...].astype(BF16), wa_ref[...]) + _dot(b_ref[...].astype(BF16), wb_ref[...])
    o_ref[...] = x_ref[...] + y


def out_proj2(x, a, b, wa, wb, tm):
    n, d = x.shape
    ka, kb = a.shape[1], b.shape[1]
    return pl.pallas_call(
        _out_proj2_kernel,
        grid=(n // tm,),
        in_specs=[pl.BlockSpec((tm, d), lambda i: (i, 0)),
                  pl.BlockSpec((tm, ka), lambda i: (i, 0)),
                  pl.BlockSpec((tm, kb), lambda i: (i, 0)),
                  pl.BlockSpec((ka, d), lambda i: (0, 0)),
                  pl.BlockSpec((kb, d), lambda i: (0, 0))],
        out_specs=pl.BlockSpec((tm, d), lambda i: (i, 0)),
        out_shape=jax.ShapeDtypeStruct((n, d), F32),
        compiler_params=_cparams(("parallel",)),
        name="out_proj2",
    )(x, a, b, wa, wb)


def _out_proj1_kernel(x_ref, a_ref, wa_ref, o_ref):
    o_ref[...] = x_ref[...] + _dot(a_ref[...].astype(BF16), wa_ref[...])


def out_proj1(x, a, wa, tm):
    n, d = x.shape
    ka = a.shape[1]
    return pl.pallas_call(
        _out_proj1_kernel,
        grid=(n // tm,),
        in_specs=[pl.BlockSpec((tm, d), lambda i: (i, 0)),
                  pl.BlockSpec((tm, ka), lambda i: (i, 0)),
                  pl.BlockSpec((ka, d), lambda i: (0, 0))],
        out_specs=pl.BlockSpec((tm, d), lambda i: (i, 0)),
        out_shape=jax.ShapeDtypeStruct((n, d), F32),
        compiler_params=_cparams(("parallel",)),
        name="out_proj1",
    )(x, a, wa)


def _final_norm_kernel(x_ref, g_ref, o_ref):
    o_ref[...] = _rms(x_ref[...], g_ref[...])


def final_norm(x, g, tm):
    n, d = x.shape
    return pl.pallas_call(
        _final_norm_kernel,
        grid=(n // tm,),
        in_specs=[pl.BlockSpec((tm, d), lambda i: (i, 0)), pl.BlockSpec((1, d), lambda i: (0, 0))],
        out_specs=pl.BlockSpec((tm, d), lambda i: (i, 0)),
        out_shape=jax.ShapeDtypeStruct((n, d), F32),
        compiler_params=_cparams(("parallel",)),
        name="final_norm",
    )(x, g.reshape(1, d))


def _intra_chunk(problems):
    c = problems[0][0].shape[0]
    nb = c // SUBLANES
    row = lax.broadcasted_iota(I32, (SUBLANES, 1), 0)
    qbs = [[q[SUBLANES * i:SUBLANES * (i + 1)] for i in range(nb)] for q, _, _, _ in problems]
    bbs = [[b[SUBLANES * i:SUBLANES * (i + 1)] for i in range(nb)] for _, _, b, _ in problems]
    outs = [[[None] * nb for _ in heads] for _, _, _, heads in problems]
    for s in range(c):
        rb0 = s // SUBLANES
        for pi, (_, k, b, heads) in enumerate(problems):
            ks = k[s:s + 1, :]
            bs = b[s:s + 1, :]
            for rb in range(rb0, nb):
                p = qbs[pi][rb] * (ks * jnp.exp(bbs[pi][rb] - bs))
                for hi, (mask, v) in enumerate(heads):
                    pm = p if mask is None else jnp.where(mask, p, 0.0)
                    col = jnp.sum(pm, axis=-1, keepdims=True)
                    if rb == rb0:
                        col = jnp.where(row + SUBLANES * rb >= s, col, 0.0)
                    term = col * v[s:s + 1, :]
                    prev = outs[pi][hi][rb]
                    outs[pi][hi][rb] = term if prev is None else prev + term
    return [[jnp.concatenate(o, axis=0) for o in po] for po in outs]


def _gated_chunk(problems, states):
    intra = _intra_chunk(problems)
    qes = [q * jnp.exp(b) for q, _, b, _ in problems]
    b_lasts = [b[-1:, :] for _, _, b, _ in problems]
    khs = [k * jnp.exp(bl - b) for (_, k, b, _), bl in zip(problems, b_lasts)]
    outs = []
    for (_, _, _, heads), qe, st, po in zip(problems, qes, states, intra):
        o = []
        for (mask, v), oi in zip(heads, po):
            qm = qe if mask is None else jnp.where(mask, qe, 0.0)
            o.append(oi + _dot_nt(qm, st, precision=HI))
        outs.append(o)
    new_states = []
    for (_, _, _, heads), kh, st, bl in zip(problems, khs, states, b_lasts):
        upd = _dot_tn(heads[0][1], kh, precision=HI)
        if len(heads) == 2:
            upd = jnp.where(heads[0][0], upd, _dot_tn(heads[1][1], kh, precision=HI))
        new_states.append(st * jnp.exp(bl) + upd)
    return outs, new_states


def _head_rms(o, g):
    ms = jnp.mean(o * o, axis=-1, keepdims=True)
    return o * lax.rsqrt(ms + NORM_EPS) * g


def _hgrn_kernel(zq_ref, zf_ref, zi_ref, zg_ref, lb_ref, ng_ref, s0_ref, o_ref, s_ref, st_scr, *, chunk):
    t = pl.program_id(2)

    @pl.when(t == 0)
    def _():
        st_scr[...] = s0_ref[:, 0]

    lb = lb_ref[0]
    nseq = zq_ref.shape[0]
    nchunks = zq_ref.shape[1] // chunk

    def body(ci, carry):
        sl = pl.ds(pl.multiple_of(ci * chunk, chunk), chunk)
        problems = []
        for i in range(nseq):
            f = lb + (1.0 - lb) * _sigmoid(zf_ref[i, sl, :])
            problems.append((_silu(zq_ref[i, sl, :]), 1.0 - f, _cumsum_rows(jnp.log(f)), [(None, zi_ref[i, sl, :])]))
        outs, states = _gated_chunk(problems, [st_scr[i] for i in range(nseq)])
        for i in range(nseq):
            st_scr[i] = states[i]
            o_ref[i, sl, :] = _head_rms(outs[i][0], ng_ref[...]) * _silu(zg_ref[i, sl, :])
        return carry

    lax.fori_loop(0, nchunks, body, 0)

    @pl.when(t == pl.num_programs(2) - 1)
    def _():
        s_ref[:, 0] = st_scr[...]


def hgrn_recurrence(z, lb, norm_g, s0_t, tb, chunk, nb):
    bsz, t, _ = z.shape
    zspec = lambda off: pl.BlockSpec((nb, tb, LANES), lambda b, h, i: (b, i, h + off))
    return pl.pallas_call(
        functools.partial(_hgrn_kernel, chunk=chunk),
        grid=(bsz // nb, A_HEADS, t // tb),
        in_specs=[zspec(0), zspec(4), zspec(8), zspec(12),
                  pl.BlockSpec((1, 1, LANES), lambda b, h, i: (h, 0, 0)),
                  pl.BlockSpec((1, LANES), lambda b, h, i: (0, 0)),
                  pl.BlockSpec((nb, 1, LANES, LANES), lambda b, h, i: (b, h, 0, 0))],
        out_specs=[pl.BlockSpec((nb, tb, LANES), lambda b, h, i: (b, i, h)),
                   pl.BlockSpec((nb, 1, LANES, LANES), lambda b, h, i: (b, h, 0, 0))],
        out_shape=[jax.ShapeDtypeStruct((bsz, t, A_WIDTH), F32),
                   jax.ShapeDtypeStruct((bsz, A_HEADS, LANES, LANES), F32)],
        scratch_shapes=[pltpu.VMEM((nb, LANES, LANES), F32)],
        compiler_params=_cparams(("parallel", "parallel", "arbitrary")),
        name="hgrn_recurrence",
    )(z, z, z, z, lb.reshape(A_HEADS, 1, LANES), norm_g.reshape(1, LANES), s0_t)


def _gla_kernel(zq_ref, zk_ref, zv_ref, zg_ref, zlr_ref, w2_ref, gb_ref, ng_ref, s0_ref, o_ref, s_ref, st_scr,
                *, chunk):
    t = pl.program_id(2)

    @pl.when(t == 0)
    def _():
        st_scr[...] = s0_ref[:, 0]

    nseq = zq_ref.shape[0]
    nchunks = zq_ref.shape[1] // chunk
    m0 = _lane_mask(LANES, 0, B_DK)
    m1 = _lane_mask(LANES, B_DK, LANES)

    def body(ci, carry):
        sl = pl.ds(pl.multiple_of(ci * chunk, chunk), chunk)
        problems = []
        for i in range(nseq):
            pre = _dot(zlr_ref[i, sl, :], w2_ref[...], precision=HI) + gb_ref[...]
            log_g = (jnp.minimum(pre, 0.0) - jnp.log(1.0 + jnp.exp(-jnp.abs(pre)))) * (1.0 / GLA_NORMALIZER)
            v = zv_ref[i, sl, :]
            problems.append((zq_ref[i, sl, :] * (B_DK ** -0.5), zk_ref[i, sl, :], _cumsum_rows(log_g),
                             [(m0, v[:, :LANES]), (m1, v[:, LANES:])]))
        outs, states = _gated_chunk(problems, [st_scr[i] for i in range(nseq)])
        for i in range(nseq):
            st_scr[i] = states[i]
            gate = _silu(zg_ref[i, sl, :])
            o_ref[i, sl, 0:LANES] = _head_rms(outs[i][0], ng_ref[...]) * gate[:, :LANES]
            o_ref[i, sl, LANES:2 * LANES] = _head_rms(outs[i][1], ng_ref[...]) * gate[:, LANES:]
        return carry

    lax.fori_loop(0, nchunks, body, 0)

    @pl.when(t == pl.num_programs(2) - 1)
    def _():
        s_ref[:, 0] = st_scr[...]


def gla_recurrence(z, w2pad, gate_b, norm_g, s0_t, tb, chunk, nb):
    bsz, t, _ = z.shape
    npairs = B_HEADS // 2
    return pl.pallas_call(
        functools.partial(_gla_kernel, chunk=chunk),
        grid=(bsz // nb, npairs, t // tb),
        in_specs=[pl.BlockSpec((nb, tb, LANES), lambda b, p, i: (b, i, 16 + p)),
                  pl.BlockSpec((nb, tb, LANES), lambda b, p, i: (b, i, 18 + p)),
                  pl.BlockSpec((nb, tb, 2 * LANES), lambda b, p, i: (b, i, 10 + p)),
                  pl.BlockSpec((nb, tb, 2 * LANES), lambda b, p, i: (b, i, 12 + p)),
                  pl.BlockSpec((nb, tb, LANES), lambda b, p, i: (b, i, 28)),
                  pl.BlockSpec((LANES, LANES), lambda b, p, i: (0, p)),
                  pl.BlockSpec((1, LANES), lambda b, p, i: (0, p)),
                  pl.BlockSpec((1, LANES), lambda b, p, i: (0, 0)),
                  pl.BlockSpec((nb, 1, LANES, LANES), lambda b, p, i: (b, p, 0, 0))],
        out_specs=[pl.BlockSpec((nb, tb, 2 * LANES), lambda b, p, i: (b, i, p)),
                   pl.BlockSpec((nb, 1, LANES, LANES), lambda b, p, i: (b, p, 0, 0))],
        out_shape=[jax.ShapeDtypeStruct((bsz, t, B_WIDTH), F32),
                   jax.ShapeDtypeStruct((bsz, npairs, LANES, LANES), F32)],
        scratch_shapes=[pltpu.VMEM((nb, LANES, LANES), F32)],
        compiler_params=_cparams(("parallel", "parallel", "arbitrary")),
        name="gla_recurrence",
    )(z, z, z, z, z, w2pad, gate_b.reshape(1, 2 * LANES), norm_g.reshape(1, LANES), s0_t)


def _rwkv_proj_kernel(x_ref, xp_ref, xl_ref, g1_ref, mu_ref, wr_ref, wk_ref, wv_ref, ww1_ref, ww2_ref, w0_ref,
                      aw1_ref, aw2_ref, a0_ref, gw1_ref, gw2_ref, kk_ref, ka_ref,
                      r_out, k_out, v_out, lw_out, a_out, kk_out, g_out, hl_out, *, tiles_per_seq):
    i = pl.program_id(0)
    g1 = g1_ref[...]
    hn = _rms(x_ref[...], g1)
    tm = hn.shape[0]
    prev = _rms(xp_ref[...], g1)[SUBLANES - 1:SUBLANES, :]
    prev = jnp.where(i % tiles_per_seq == 0, xl_ref[0], prev)
    row = lax.broadcasted_iota(I32, (tm, 1), 0)
    xprev = jnp.where(row == 0, prev, pltpu.roll(hn, 1, axis=0))
    dx = xprev - hn

    def mix(j):
        return hn + dx * mu_ref[j:j + 1, :]

    r = _dot(mix(0).astype(BF16), wr_ref[...])
    k = _dot(mix(1).astype(BF16), wk_ref[...])
    v = _dot(mix(2).astype(BF16), wv_ref[...])
    wl = _dot(jnp.tanh(_dot(mix(3), ww1_ref[...], precision=HI)), ww2_ref[...], precision=HI)
    z = w0_ref[...] + wl
    wpre = -(jnp.maximum(-z, 0.0) + jnp.log(1.0 + jnp.exp(-jnp.abs(z)))) - 0.5
    al = _dot(_dot(mix(4), aw1_ref[...], precision=HI), aw2_ref[...], precision=HI)
    a = _sigmoid(a0_ref[...] + al)
    gg = _dot(_sigmoid(_dot(mix(5).astype(BF16), gw1_ref[...])).astype(BF16), gw2_ref[...])
    r_out[...] = r
    k_out[...] = k * (1.0 + (a - 1.0) * ka_ref[...])
    v_out[...] = v
    lw_out[...] = -jnp.exp(wpre)
    a_out[...] = a
    kk_out[...] = k * kk_ref[...]
    g_out[...] = gg
    hl_out[0] = hn[tm - 1:tm, :]


def rwkv_proj(x, x_last, seq_len, g1, mu, wr, wk, wv, ww1, ww2, w0, aw1, aw2, a0, gw1, gw2, k_k, k_a, tm):
    n, d = x.shape
    tiles_per_seq = seq_len // tm
    row = lambda a: a.reshape(1, d)
    full = lambda a: pl.BlockSpec(a.shape, lambda i: (0,) * a.ndim)
    tile = pl.BlockSpec((tm, d), lambda i: (i, 0))
    blocks8 = tm // SUBLANES
    args = (x, x, x_last.reshape(-1, 1, d), row(g1), mu, wr, wk, wv, ww1, ww2, row(w0), aw1, aw2, row(a0),
            gw1, gw2, row(k_k), row(k_a))
    in_specs = [tile,
                pl.BlockSpec((SUBLANES, d), lambda i: (jnp.maximum(i * blocks8 - 1, 0), 0)),
                pl.BlockSpec((1, 1, d), lambda i: (i // tiles_per_seq, 0, 0))]
    in_specs += [full(a) for a in args[3:]]
    outs = pl.pallas_call(
        functools.partial(_rwkv_proj_kernel, tiles_per_seq=tiles_per_seq),
        grid=(n // tm,),
        in_specs=in_specs,
        out_specs=[tile] * 7 + [pl.BlockSpec((1, 1, d), lambda i: (i, 0, 0))],
        out_shape=[jax.ShapeDtypeStruct((n, d), F32)] * 7 + [jax.ShapeDtypeStruct((n // tm, 1, d), F32)],
        compiler_params=_cparams(("parallel",)),
        name="rwkv_proj",
    )(*args)
    return outs


_NN = ((1,), (0,))
_NT = ((1,), (1,))
_TN = ((0,), (0,))
RWKV_AB_PASSES = 1
RWKV_INV_PASSES = 1
RWKV_APPLY_PASSES = 1
RWKV_STATE_PASSES = 3
RWKV_SEQS_PER_STEP = 4


def _split_bf16(a):
    hi = a.astype(BF16)
    return hi, (a - hi.astype(F32)).astype(BF16)


def _mm(a, b, dims, passes):
    if passes == 6:
        return lax.dot_general(a, b, (dims, ((), ())), preferred_element_type=F32, precision=HI)
    dg = lambda x, y: lax.dot_general(x, y, (dims, ((), ())), preferred_element_type=F32)
    ah, al = _split_bf16(a)
    bh, bl = _split_bf16(b)
    if passes == 1:
        return dg(ah, bh)
    return dg(ah, bh) + (dg(al, bh) + dg(ah, bl))


def _cumsum_rows3(g):
    tri = _tri(g.shape[0], False).astype(BF16)
    h1 = g.astype(BF16)
    r1 = g - h1.astype(F32)
    h2 = r1.astype(BF16)
    h3 = (r1 - h2.astype(F32)).astype(BF16)
    return _dot(tri, h1) + (_dot(tri, h2) + _dot(tri, h3))


def _pair_sum(x, m0):
    s0 = jnp.sum(jnp.where(m0, x, 0.0), axis=-1, keepdims=True)
    s1 = jnp.sum(jnp.where(m0, 0.0, x), axis=-1, keepdims=True)
    return jnp.where(m0, s0, s1)


def _rwkv_kernel(r_ref, k_ref, v_ref, lw_ref, a_ref, kk_ref, g_ref, rk_ref, lng_ref, lnb_ref, s0_ref,
                 o_ref, s_ref, mt_scr, *, chunk):
    t = pl.program_id(2)

    @pl.when(t == 0)
    def _():
        mt_scr[...] = s0_ref[:, 0]

    nrows = r_ref.shape[0]
    nchunks = r_ref.shape[1] // chunk
    c2 = 2 * chunk
    m0 = _lane_mask(LANES, 0, C_HEAD)
    rowi = lax.broadcasted_iota(I32, (LANES, LANES), 0)
    coli = lax.broadcasted_iota(I32, (LANES, LANES), 1)
    blockdiag = (rowi < C_HEAD) == (coli < C_HEAD)
    ti = lax.broadcasted_iota(I32, (c2, c2), 0)
    si = lax.broadcasted_iota(I32, (c2, c2), 1)
    same_head = (ti < chunk) == (si < chunk)
    tm_ = jnp.where(ti < chunk, ti, ti - chunk)
    sm_ = jnp.where(si < chunk, si, si - chunk)
    strict = same_head & (sm_ < tm_)
    incl = same_head & (sm_ <= tm_)

    def stack_heads(x):
        return jnp.concatenate([jnp.where(m0, x, 0.0), jnp.where(m0, 0.0, x)], axis=0)

    def twice(x):
        return jnp.concatenate([x, x], axis=0)

    def unstack(x2):
        return jnp.where(m0, x2[:chunk], x2[chunk:])

    eye = (ti == si).astype(F32)
    seqs = range(nrows)

    def body(ci, carry):
        sl = pl.ds(pl.multiple_of(ci * chunk, chunk), chunk)
        r = [r_ref[i, sl, :] for i in seqs]
        k = [k_ref[i, sl, :] for i in seqs]
        v = [v_ref[i, sl, :] for i in seqs]
        lw = [lw_ref[i, sl, :] for i in seqs]
        kkr = [kk_ref[i, sl, :] for i in seqs]
        kk = [x * lax.rsqrt(_pair_sum(x * x, m0) + 1e-12) for x in kkr]
        al = [a_ref[i, sl, :] * kk[i] for i in seqs]
        gam = [_cumsum_rows3(x) for x in lw]
        e_neg = [jnp.exp(-x) for x in gam]
        xr = [jnp.concatenate([stack_heads(kk[i] * jnp.exp(gam[i] - lw[i])),
                               stack_heads(r[i] * jnp.exp(gam[i]))], axis=0) for i in seqs]
        alk = [jnp.concatenate([twice(al[i] * e_neg[i]), twice(k[i] * e_neg[i])], axis=0) for i in seqs]
        mt = [mt_scr[i] for i in seqs]
        ab = [_mm(xr[i], alk[i], _NT, RWKV_AB_PASSES) for i in seqs]
        xm = [_mm(xr[i], mt[i], _NT, RWKV_STATE_PASSES) for i in seqs]
        a_al = [jnp.where(strict, x[:c2, :c2], 0.0) for x in ab]
        a_k = [jnp.where(strict, x[:c2, c2:], 0.0) for x in ab]
        b_alk = [jnp.concatenate([jnp.where(incl, x[c2:, c2:], 0.0), jnp.where(incl, -x[c2:, :c2], 0.0)], axis=1)
                 for x in ab]
        v2 = [twice(x) for x in v]
        rhs = [xm[i][:c2] + _mm(a_k[i], v2[i], _NN, RWKV_APPLY_PASSES) for i in seqs]
        p = [-x for x in a_al]
        tinv = [eye + x for x in p]
        span = 2
        while span < chunk:
            p = [_mm(x, x, _NN, RWKV_INV_PASSES) for x in p]
            tinv = [tinv[i] + _mm(tinv[i], p[i], _NN, RWKV_INV_PASSES) for i in seqs]
            span *= 2
        u = [unstack(_mm(tinv[i], rhs[i], _NN, RWKV_APPLY_PASSES)) for i in seqs]
        o = [unstack(xm[i][c2:] + _mm(b_alk[i], jnp.concatenate([v2[i], twice(u[i])], axis=0), _NN,
                                      RWKV_APPLY_PASSES)) for i in seqs]
        g_last = [x[-1:, :] for x in gam]
        e_end = [jnp.exp(g_last[i] - gam[i]) for i in seqs]
        upd = [_mm(jnp.concatenate([v[i], u[i]], axis=0),
                   jnp.concatenate([k[i] * e_end[i], -(al[i] * e_end[i])], axis=0), _TN, RWKV_STATE_PASSES)
               for i in seqs]
        for i in seqs:
            mt_scr[i] = mt[i] * jnp.exp(g_last[i]) + jnp.where(blockdiag, upd[i], 0.0)
        for i in seqs:
            mean = _pair_sum(o[i], m0) * (1.0 / C_HEAD)
            cen = o[i] - mean
            var = _pair_sum(cen * cen, m0) * (1.0 / C_HEAD)
            on = cen * lax.rsqrt(var + C_GN_EPS) * lng_ref[...] + lnb_ref[...]
            bonus = _pair_sum(r[i] * k[i] * rk_ref[...], m0) * v[i]
            o_ref[i, sl, :] = (on + bonus) * g_ref[i, sl, :]
        return carry

    lax.fori_loop(0, nchunks, body, 0)

    @pl.when(t == pl.num_programs(2) - 1)
    def _():
        s_ref[:, 0] = mt_scr[...]


def rwkv_recurrence(r, k, v, lw, a, kk, g, r_k, ln_g, ln_b, s0_bd, tb, chunk, nb):
    bsz, t, d = r.shape
    npairs = C_HEADS // 2
    seq = pl.BlockSpec((nb, tb, LANES), lambda b, p, i: (b, i, p))
    vec = pl.BlockSpec((1, LANES), lambda b, p, i: (0, p))
    st = pl.BlockSpec((nb, 1, LANES, LANES), lambda b, p, i: (b, p, 0, 0))
    return pl.pallas_call(
        functools.partial(_rwkv_kernel, chunk=chunk),
        grid=(bsz // nb, npairs, t // tb),
        in_specs=[seq] * 7 + [vec, vec, vec, st],
        out_specs=[seq, st],
        out_shape=[jax.ShapeDtypeStruct((bsz, t, d), F32),
                   jax.ShapeDtypeStruct((bsz, npairs, LANES, LANES), F32)],
        scratch_shapes=[pltpu.VMEM((nb, LANES, LANES), F32)],
        compiler_params=_cparams(("parallel", "parallel", "arbitrary")),
        name="rwkv_recurrence",
    )(r, k, v, lw, a, kk, g, r_k.reshape(1, d), ln_g.reshape(1, d), ln_b.reshape(1, d), s0_bd)


NEG_INF = float("-inf")


def _top16_rows(s):
    n = s.shape[0]
    key = lax.broadcasted_iota(I32, s.shape, 0)
    vals, idxs = [], []
    for _ in range(P_TOPK):
        m = jnp.max(s, axis=0, keepdims=True)
        am = jnp.min(jnp.where(s == m, key, n), axis=0, keepdims=True)
        vals.append(m)
        idxs.append(am)
        s = jnp.where(key == am, NEG_INF, s)
    return vals, idxs


def _top16_pairs(v0, i0, v1, i1):
    a0 = jnp.concatenate(v0[0:8], axis=0)
    a1 = jnp.concatenate(v0[8:16], axis=0)
    b0 = jnp.concatenate(v1[0:8], axis=0)
    b1 = jnp.concatenate(v1[8:16], axis=0)
    ia0 = jnp.concatenate(i0[0:8], axis=0) * P_KEYS
    ia1 = jnp.concatenate(i0[8:16], axis=0) * P_KEYS
    ib0 = jnp.concatenate(i1[0:8], axis=0)
    ib1 = jnp.concatenate(i1[8:16], axis=0)
    row = lax.broadcasted_iota(I32, (SUBLANES, 1), 0)
    slabs = []

    def add(val, eid, keep):
        slabs.append((val if keep is None else jnp.where(keep, val, NEG_INF), eid))

    add(v0[0] + b0, ia0[0:1] + ib0, None)
    add(v0[0] + b1, ia0[0:1] + ib1, None)
    add(v0[1] + b0, ia0[1:2] + ib0, None)
    add(v0[2] + b0, ia0[2:3] + ib0, row < 5)
    add(v0[3] + b0, ia0[3:4] + ib0, row < 4)
    add(a0 + v1[0], ia0 + ib0[0:1], row >= 4)
    add(a1 + v1[0], ia1 + ib0[0:1], None)
    add(a0 + v1[1], ia0 + ib0[1:2], row >= 4)
    add(a0 + v1[2], ia0 + ib0[2:3], row == 4)

    big = P_KEYS * P_KEYS
    out_v, out_e = [], []
    for _ in range(P_TOPK):
        m = slabs[0][0]
        for val, _e in slabs[1:]:
            m = jnp.maximum(m, val)
        m = jnp.max(m, axis=0, keepdims=True)
        e = None
        for val, eid in slabs:
            c = jnp.where(val == m, eid, big)
            e = c if e is None else jnp.minimum(e, c)
        e = jnp.min(e, axis=0, keepdims=True)
        out_v.append(m)
        out_e.append(e)
        slabs = [(jnp.where(eid == e, NEG_INF, val), eid) for val, eid in slabs]
    return out_v, out_e


def _peer_select_kernel(x_ref, g_ref, wq_ref, keys_ref, xn_out, eid_out, gate_out):
    hn = _rms(x_ref[...], g_ref[...])
    xn_out[...] = hn
    q = _dot(hn.astype(BF16), wq_ref[...])
    tm = q.shape[0]
    for lt in range(tm // LANES):
        rows = slice(lt * LANES, (lt + 1) * LANES)
        e_rows, g_rows = [], []
        for h in range(P_HEADS):
            tops = []
            for p in range(2):
                hp = 2 * h + p
                s = _dot_nt(keys_ref[hp], q[rows, hp * LANES:(hp + 1) * LANES], precision=HI)
                tops.append(_top16_rows(s))
            cs, ce = _top16_pairs(tops[0][0], tops[0][1], tops[1][0], tops[1][1])
            ex = [jnp.exp(c - cs[0]) for c in cs]
            tot = ex[0]
            for e in ex[1:]:
                tot = tot + e
            inv = 1.0 / tot
            e_rows += ce
            g_rows += [e * inv for e in ex]
        eid_out[rows, :] = jnp.concatenate(e_rows, axis=0).T
        gate_out[rows, :] = jnp.concatenate(g_rows, axis=0).T


def peer_select(x, g, wq_bf16, keys, tm):
    n, d = x.shape
    return pl.pallas_call(
        _peer_select_kernel,
        grid=(n // tm,),
        in_specs=[pl.BlockSpec((tm, d), lambda i: (i, 0)),
                  pl.BlockSpec((1, d), lambda i: (0, 0)),
                  pl.BlockSpec(wq_bf16.shape, lambda i: (0, 0)),
                  pl.BlockSpec(keys.shape, lambda i: (0, 0, 0))],
        out_specs=[pl.BlockSpec((tm, d), lambda i: (i, 0)),
                   pl.BlockSpec((tm, P_PICKS), lambda i: (i, 0)),
                   pl.BlockSpec((tm, P_PICKS), lambda i: (i, 0))],
        out_shape=[jax.ShapeDtypeStruct((n, d), F32),
                   jax.ShapeDtypeStruct((n, P_PICKS), I32),
                   jax.ShapeDtypeStruct((n, P_PICKS), F32)],
        compiler_params=_cparams(("parallel",)),
        name="peer_select",
    )(x, g.reshape(1, d), wq_bf16, keys)


def _peer_act_kernel(h_ref, g_ref, o_ref):
    h = h_ref[...]
    o_ref[...] = 0.5 * h * (1.0 + lax.erf(h * (2.0 ** -0.5))) * g_ref[...]


def peer_act(hid, gate, tm):
    n, p = hid.shape
    spec = pl.BlockSpec((tm, p), lambda i: (i, 0))
    return pl.pallas_call(
        _peer_act_kernel, grid=(n // tm,), in_specs=[spec, spec], out_specs=spec,
        out_shape=jax.ShapeDtypeStruct((n, p), F32),
        compiler_params=_cparams(("parallel",)), name="peer_act",
    )(hid, gate)


_CHUNKS_PER_TOK = P_PICKS // SC_CHUNK
_WORD_VREGS = SC_WORDS // SC_LANES
U32 = jnp.uint32


def pack_rows_bf16(t):
    lo = lax.bitcast_convert_type(t[..., :SC_WORDS].astype(BF16), jnp.uint16).astype(U32)
    a = lax.bitcast_convert_type(t[..., SC_WORDS:], U32)
    sign = a & U32(0x80000000)
    mag = a & U32(0x7FFFFFFF)
    steps = (jnp.maximum(mag + U32(0x8000), lo) - lo) >> 16
    near = (steps << 16) + lo
    near = jnp.where(near >= U32(0x7F800000), near - U32(0x10000), near)
    return lax.bitcast_convert_type(sign | near, I32)


def _low_f32(w):
    return lax.bitcast_convert_type(w << 16, F32)


def _high_f32(w):
    return lax.bitcast_convert_type(w, F32)


def _sc_worker():
    return lax.axis_index("s") * SC_CORES + lax.axis_index("c")


def _sc_tok_block(tok_per_w):
    return next(b for b in SC_TOK_BLOCKS if tok_per_w % b == 0)


def _sc_hid_body(tab_hbm, idx_hbm, x_hbm, hid_hbm, idx_v, x_v, hid_v, rows_a, rows_b, sem_a, sem_b, sem_i, sem_x,
                 *, tok_per_w):
    wid = _sc_worker()
    blk = hid_v.shape[0]
    nblk = tok_per_w // blk
    nch = blk * _CHUNKS_PER_TOK
    lane = lax.iota(I32, SC_LANES)
    quads = SC_LANES // SC_HID_ROWS

    def dots(rows, x_row, out_row, col0):
        for grp in range(SC_CHUNK // SC_LANES):
            def some_rows(q, res):
                r0 = grp * SC_LANES + q * SC_HID_ROWS

                def some_words(jj, accs):
                    accs = list(accs)
                    for jw in range(SC_HID_WORDS):
                        off = (jj * SC_HID_WORDS + jw) * SC_LANES
                        xl = x_v[x_row, pl.ds(off, SC_LANES)]
                        xh = x_v[x_row, pl.ds(SC_WORDS + off, SC_LANES)]
                        for i in range(SC_HID_ROWS):
                            w = rows[r0 + i, pl.ds(off, SC_LANES)]
                            k = 2 * i + jw % 2
                            accs[k] = accs[k] + (_low_f32(w) * xl + _high_f32(w) * xh)
                    return tuple(accs)

                zero = jnp.zeros((SC_LANES,), F32)
                accs = lax.fori_loop(0, _WORD_VREGS // SC_HID_WORDS, some_words, (zero,) * (2 * SC_HID_ROWS))
                for i in range(SC_HID_ROWS):
                    res = jnp.where(lane == q * SC_HID_ROWS + i, jnp.sum(accs[2 * i] + accs[2 * i + 1]), res)
                return res
            res = lax.fori_loop(0, quads, some_rows, jnp.zeros((SC_LANES,), F32))
            hid_v[out_row, pl.ds(col0 + grp * SC_LANES, SC_LANES)] = res

    def stage(bi, slot):
        tok0 = wid * tok_per_w + bi * blk
        pltpu.async_copy(idx_hbm.at[pl.ds(tok0 * _CHUNKS_PER_TOK, nch)], idx_v.at[pl.ds(slot * nch, nch)], sem_i)
        pltpu.async_copy(x_hbm.at[pl.ds(tok0, blk)], x_v.at[pl.ds(slot * blk, blk)], sem_x)

    def wait_idx():
        pltpu.make_async_copy(idx_hbm.at[pl.ds(0, nch)], idx_v.at[pl.ds(0, nch)], sem_i).wait()

    def wait_x():
        pltpu.make_async_copy(x_hbm.at[pl.ds(0, blk)], x_v.at[pl.ds(0, blk)], sem_x).wait()

    stage(0, 0)
    wait_idx()
    pltpu.async_copy(tab_hbm.at[idx_v.at[0]], rows_a, sem_a)

    def block(bi, carry):
        slot = bi % 2
        tok0 = wid * tok_per_w + bi * blk
        has_next = bi + 1 < nblk
        i0 = slot * nch
        x0 = slot * blk
        wait_x()

        @pl.when(has_next)
        def _():
            stage(bi + 1, 1 - slot)

        def pair(j, c):
            ca = 2 * j
            pltpu.async_copy(tab_hbm.at[idx_v.at[i0 + ca + 1]], rows_b, sem_b)
            pltpu.make_async_copy(tab_hbm.at[idx_v.at[0]], rows_a, sem_a).wait()
            dots(rows_a, x0 + ca // _CHUNKS_PER_TOK, ca // _CHUNKS_PER_TOK, (ca % _CHUNKS_PER_TOK) * SC_CHUNK)
            last = j == nch // 2 - 1

            @pl.when(jnp.logical_not(last))
            def _():
                pltpu.async_copy(tab_hbm.at[idx_v.at[i0 + ca + 2]], rows_a, sem_a)

            @pl.when(jnp.logical_and(last, has_next))
            def _():
                wait_idx()
                pltpu.async_copy(tab_hbm.at[idx_v.at[(1 - slot) * nch]], rows_a, sem_a)
            pltpu.make_async_copy(tab_hbm.at[idx_v.at[0]], rows_b, sem_b).wait()
            cb = ca + 1
            dots(rows_b, x0 + cb // _CHUNKS_PER_TOK, cb // _CHUNKS_PER_TOK, (cb % _CHUNKS_PER_TOK) * SC_CHUNK)
            return c
        lax.fori_loop(0, nch // 2, pair, 0)
        pltpu.sync_copy(hid_v, hid_hbm.at[pl.ds(tok0, blk)])
        return carry

    lax.fori_loop(0, nblk, block, 0)


def sc_expert_hidden(table_packed, eidx, xn):
    n = xn.shape[0]
    tok_per_w = n // SC_WORKERS
    blk = SC_TOK_BLOCKS[-1]
    mesh = plsc.VectorSubcoreMesh(core_axis_name="c", subcore_axis_name="s")
    nch = blk * _CHUNKS_PER_TOK
    k = pl.kernel(
        functools.partial(_sc_hid_body, tok_per_w=tok_per_w), mesh=mesh,
        out_type=jax.ShapeDtypeStruct((n, P_PICKS), F32),
        scratch_types=[pltpu.VMEM((2 * nch, SC_CHUNK), I32),
                       pltpu.VMEM((2 * blk, D_MODEL), F32),
                       pltpu.VMEM((blk, P_PICKS), F32),
                       pltpu.VMEM((SC_CHUNK, SC_WORDS), I32),
                       pltpu.VMEM((SC_CHUNK, SC_WORDS), I32),
                       pltpu.SemaphoreType.DMA, pltpu.SemaphoreType.DMA,
                       pltpu.SemaphoreType.DMA, pltpu.SemaphoreType.DMA],
        compiler_params=pltpu.CompilerParams(needs_layout_passes=False),
        name="sc_expert_hidden",
    )
    return k(table_packed, eidx.reshape(n * _CHUNKS_PER_TOK, SC_CHUNK), xn)


def _sc_out_body(tab_hbm, idx_hbm, act_hbm, x_hbm, y_hbm, idx_v, act_v, y_v, rows_a, rows_b, sem_a, sem_b,
                 sem_i, sem_g, sem_x, *, tok_per_w):
    wid = _sc_worker()
    blk = y_v.shape[0] // 2
    nblk = tok_per_w // blk
    nch = blk * _CHUNKS_PER_TOK
    half_w = _WORD_VREGS // 2

    def accum(rows, tok, col0):
        for hv in range(2):
            base = hv * half_w * SC_LANES
            acc0 = (tuple(y_v[tok, pl.ds(base + j * SC_LANES, SC_LANES)] for j in range(half_w))
                    + tuple(y_v[tok, pl.ds(SC_WORDS + base + j * SC_LANES, SC_LANES)] for j in range(half_w)))

            def one(r, acc):
                aidx = jnp.full((SC_LANES,), col0, I32) + r
                wgt = plsc.load_gather(act_v, [jnp.full((SC_LANES,), tok, I32), aidx])
                lo, hi = [], []
                for j in range(half_w):
                    w = rows[r, pl.ds(base + j * SC_LANES, SC_LANES)]
                    lo.append(acc[j] + wgt * _low_f32(w))
                    hi.append(acc[half_w + j] + wgt * _high_f32(w))
                return tuple(lo + hi)
            acc = lax.fori_loop(0, SC_CHUNK, one, acc0)
            for j in range(half_w):
                y_v[tok, pl.ds(base + j * SC_LANES, SC_LANES)] = acc[j]
                y_v[tok, pl.ds(SC_WORDS + base + j * SC_LANES, SC_LANES)] = acc[half_w + j]

    def stage(bi, slot):
        tok0 = wid * tok_per_w + bi * blk
        pltpu.async_copy(idx_hbm.at[pl.ds(tok0 * _CHUNKS_PER_TOK, nch)], idx_v.at[pl.ds(slot * nch, nch)], sem_i)
        pltpu.async_copy(act_hbm.at[pl.ds(tok0, blk)], act_v.at[pl.ds(slot * blk, blk)], sem_g)
        pltpu.async_copy(x_hbm.at[pl.ds(tok0, blk)], y_v.at[pl.ds(slot * blk, blk)], sem_x)

    def wait_idx():
        pltpu.make_async_copy(idx_hbm.at[pl.ds(0, nch)], idx_v.at[pl.ds(0, nch)], sem_i).wait()

    def wait_act_x():
        pltpu.make_async_copy(act_hbm.at[pl.ds(0, blk)], act_v.at[pl.ds(0, blk)], sem_g).wait()
        pltpu.make_async_copy(x_hbm.at[pl.ds(0, blk)], y_v.at[pl.ds(0, blk)], sem_x).wait()

    stage(0, 0)
    wait_idx()
    pltpu.async_copy(tab_hbm.at[idx_v.at[0]], rows_a, sem_a)

    def block(bi, carry):
        slot = bi % 2
        tok0 = wid * tok_per_w + bi * blk
        has_next = bi + 1 < nblk
        i0 = slot * nch
        t0 = slot * blk
        wait_act_x()

        @pl.when(has_next)
        def _():
            stage(bi + 1, 1 - slot)

        def pair(j, c):
            ca = 2 * j
            pltpu.async_copy(tab_hbm.at[idx_v.at[i0 + ca + 1]], rows_b, sem_b)
            pltpu.make_async_copy(tab_hbm.at[idx_v.at[0]], rows_a, sem_a).wait()
            accum(rows_a, t0 + ca // _CHUNKS_PER_TOK, (ca % _CHUNKS_PER_TOK) * SC_CHUNK)
            last = j == nch // 2 - 1

            @pl.when(jnp.logical_not(last))
            def _():
                pltpu.async_copy(tab_hbm.at[idx_v.at[i0 + ca + 2]], rows_a, sem_a)

            @pl.when(jnp.logical_and(last, has_next))
            def _():
                wait_idx()
                pltpu.async_copy(tab_hbm.at[idx_v.at[(1 - slot) * nch]], rows_a, sem_a)
            pltpu.make_async_copy(tab_hbm.at[idx_v.at[0]], rows_b, sem_b).wait()
            cb = ca + 1
            accum(rows_b, t0 + cb // _CHUNKS_PER_TOK, (cb % _CHUNKS_PER_TOK) * SC_CHUNK)
            return c
        lax.fori_loop(0, nch // 2, pair, 0)
        pltpu.sync_copy(y_v.at[pl.ds(t0, blk)], y_hbm.at[pl.ds(tok0, blk)])
        return carry

    lax.fori_loop(0, nblk, block, 0)


def sc_expert_output(table_packed, eidx, act, x):
    n = x.shape[0]
    tok_per_w = n // SC_WORKERS
    blk = SC_TOK_BLOCKS[-1]
    mesh = plsc.VectorSubcoreMesh(core_axis_name="c", subcore_axis_name="s")
    nch = blk * _CHUNKS_PER_TOK
    k = pl.kernel(
        functools.partial(_sc_out_body, tok_per_w=tok_per_w), mesh=mesh,
        out_type=jax.ShapeDtypeStruct((n, D_MODEL), F32),
        scratch_types=[pltpu.VMEM((2 * nch, SC_CHUNK), I32),
                       pltpu.VMEM((2 * blk, P_PICKS), F32),
                       pltpu.VMEM((2 * blk, D_MODEL), F32),
                       pltpu.VMEM((SC_CHUNK, SC_WORDS), I32),
                       pltpu.VMEM((SC_CHUNK, SC_WORDS), I32),
                       pltpu.SemaphoreType.DMA, pltpu.SemaphoreType.DMA,
                       pltpu.SemaphoreType.DMA, pltpu.SemaphoreType.DMA, pltpu.SemaphoreType.DMA],
        compiler_params=pltpu.CompilerParams(needs_layout_passes=False),
        name="sc_expert_output",
    )
    return k(table_packed, eidx.reshape(n * _CHUNKS_PER_TOK, SC_CHUNK), act, x)


def peer_ffn(x, g, wq_bf16, keys, u_packed, v_packed, tm):
    xn, eidx, gate = peer_select(x, g, wq_bf16, keys, tm)
    hid = sc_expert_hidden(u_packed, eidx, xn)
    act = peer_act(hid, gate, tm)
    return sc_expert_output(v_packed, eidx, act, x)


def _prep_w_in(w_in):
    main = jnp.concatenate([w_in[:, :3072], w_in[:, 3088:3600]], axis=1)
    lr = jnp.pad(w_in[:, 3072:3088], ((0, 0), (0, LANES - GLA_RANK)))
    return jnp.concatenate([main, lr], axis=1).astype(BF16)


def _trunk(x, st_h, st_g, st_r, st_s, w, seq_len, tm, tmp, tb, chunk):
    bsz = x.shape[0]
    n = bsz * seq_len
    x2 = x.reshape(n, D_MODEL)

    z = norm_proj(x2, w["norm1_g"][0], w["w_in"], tm).reshape(bsz, seq_len, Z_WIDTH)
    s0_h = jnp.swapaxes(st_h, -1, -2)
    s0_g = jnp.swapaxes(st_g.reshape(bsz, 2, 2 * B_DK, LANES), -1, -2)
    o_a, sh_t = hgrn_recurrence(z, w["lb0"], w["hgrn_norm_g"], s0_h, tb, chunk, GATED_SEQS_PER_STEP)
    o_b, sg_t = gla_recurrence(z, w["gla_w2"], w["gla_b"], w["gla_norm_g"], s0_g, tb, chunk, GATED_SEQS_PER_STEP)
    new_h = jnp.swapaxes(sh_t, -1, -2)
    new_g = jnp.swapaxes(sg_t, -1, -2).reshape(bsz, B_HEADS, B_DK, LANES)
    x2 = out_proj2(x2, o_a.reshape(n, A_WIDTH), o_b.reshape(n, B_WIDTH), w["w_out_a"], w["w_out_b"], tm)
    x2 = peer_ffn(x2, w["norm2_g"][0], w["peer_wq"][0], w["peer_keys"][0], w["peer_u"][0], w["peer_v"][0], tmp)

    r, k, v, lw, a, kk, g, hl = rwkv_proj(
        x2, st_s, seq_len, w["norm1_g"][1], w["mu"], w["wr"], w["wk"], w["wv"], w["w_w1"], w["w_w2"], w["w0"],
        w["a_w1"], w["a_w2"], w["a0"], w["g_w1"], w["g_w2"], w["k_k"], w["k_a"], tm)
    new_s = hl.reshape(bsz, seq_len // tm, D_MODEL)[:, -1]
    pr = st_r.reshape(bsz, C_HEADS // 2, 2, C_HEAD, C_HEAD)
    zero = jnp.zeros_like(pr[:, :, 0])
    s0_r = jnp.concatenate([jnp.concatenate([pr[:, :, 0], zero], axis=-1),
                            jnp.concatenate([zero, pr[:, :, 1]], axis=-1)], axis=-2)
    sh3 = lambda t: t.reshape(bsz, seq_len, D_MODEL)
    o_c, sr_bd = rwkv_recurrence(sh3(r), sh3(k), sh3(v), sh3(lw), sh3(a), sh3(kk), sh3(g),
                                 w["r_k"], w["ln_g"], w["ln_b"], s0_r, tb, chunk, min(RWKV_SEQS_PER_STEP, bsz))
    new_r = jnp.stack([sr_bd[:, :, :C_HEAD, :C_HEAD], sr_bd[:, :, C_HEAD:, C_HEAD:]], axis=2)
    new_r = new_r.reshape(bsz, C_HEADS, C_HEAD, C_HEAD)
    x2 = out_proj1(x2, o_c.reshape(n, D_MODEL), w["w_out_c"], tm)
    x2 = peer_ffn(x2, w["norm2_g"][1], w["peer_wq"][1], w["peer_keys"][1], w["peer_u"][1], w["peer_v"][1], tmp)

    y = final_norm(x2, w["final_g"], tm).reshape(bsz, seq_len, D_MODEL)
    return y, new_h[None], new_g[None], new_r[None], new_s[None]


def kernel(x_prompt, x_sample, state_hgrn, state_gla, state_rwkv, state_shift, w_in_ab, hgrn_lower_bounds, hgrn_norm_g, gla_gate_w2, gla_gate_b, gla_norm_g, w_out_ab, rwkv_mu, rwkv_w_rkv, rwkv_w_w1, rwkv_w_w2, rwkv_w0, rwkv_a_w1, rwkv_a_w2, rwkv_a0, rwkv_g_w1, rwkv_g_w2, rwkv_k_k, rwkv_k_a, rwkv_r_k, rwkv_ln_g, rwkv_ln_b, w_out_c, norm1_g, norm2_g, final_g, peer_w_q, peer_sub_keys, peer_u, peer_v):
    lbs = jnp.cumsum(jax.nn.softmax(hgrn_lower_bounds.astype(F32), axis=0), axis=0)
    w = dict(
        norm1_g=norm1_g, norm2_g=norm2_g, final_g=final_g,
        w_in=_prep_w_in(w_in_ab[0]), lb0=lbs[0], hgrn_norm_g=hgrn_norm_g[0],
        gla_w2=jnp.pad(gla_gate_w2[0], ((0, LANES - GLA_RANK), (0, 0))), gla_b=gla_gate_b[0],
        gla_norm_g=gla_norm_g[0],
        w_out_a=w_out_ab[0, :A_WIDTH].astype(BF16), w_out_b=w_out_ab[0, A_WIDTH:].astype(BF16),
        mu=rwkv_mu[0], wr=rwkv_w_rkv[0, 0].astype(BF16), wk=rwkv_w_rkv[0, 1].astype(BF16),
        wv=rwkv_w_rkv[0, 2].astype(BF16), w_w1=rwkv_w_w1[0], w_w2=rwkv_w_w2[0], w0=rwkv_w0[0],
        a_w1=rwkv_a_w1[0], a_w2=rwkv_a_w2[0], a0=rwkv_a0[0],
        g_w1=rwkv_g_w1[0].astype(BF16), g_w2=rwkv_g_w2[0].astype(BF16),
        k_k=rwkv_k_k[0], k_a=rwkv_k_a[0], r_k=rwkv_r_k[0], ln_g=rwkv_ln_g[0], ln_b=rwkv_ln_b[0],
        w_out_c=w_out_c[0].astype(BF16),
        peer_wq=peer_w_q.astype(BF16),
        peer_keys=peer_sub_keys.reshape(peer_sub_keys.shape[0], 2 * P_HEADS, P_KEYS, P_KEYS),
        peer_u=pack_rows_bf16(peer_u), peer_v=pack_rows_bf16(peer_v),
    )
    bp, tp, _ = x_prompt.shape
    bs, ts, _ = x_sample.shape
    assert sum(PROMPT_GROUPS) == bp
    groups, start = [], 0
    for gsz in PROMPT_GROUPS:
        zeros = lambda s: jnp.zeros((gsz,) + s.shape[2:], F32)
        groups.append(_trunk(x_prompt[start:start + gsz], zeros(state_hgrn), zeros(state_gla), zeros(state_rwkv),
                             zeros(state_shift), w, tp, tm=256, tmp=256, tb=512, chunk=64))
        start += gsz
    y_p = jnp.concatenate([g[0] for g in groups], axis=0)
    p_h, p_g, p_r, p_s = (jnp.concatenate([g[j] for g in groups], axis=1) for j in range(1, 5))
    y_s, s_h, s_g, s_r, s_s = _trunk(x_sample, state_hgrn[0], state_gla[0], state_rwkv[0], state_shift[0],
                                     w, ts, tm=32, tmp=128, tb=32, chunk=32)
    return (y_p, y_s, p_h, p_g, p_r, p_s, s_h, s_g, s_r, s_s)
```
